```python
import jax
import jax.numpy as jnp
from jax import lax
import numpy as np

D_MODEL = 1024
BATCH = 8
SEQ = 2048
DEPTH = 4
DEC_BATCH = 8
DEC_SEQ = 32
PAST_LEN = 4096

CHUNK = 64
N_MIXERS = 2
N_SB_LAYERS = (DEPTH + N_MIXERS - 1) // N_MIXERS
N_RW_LAYERS = DEPTH // N_MIXERS
SB_HEADS = 16
SB_HEAD_DIM = D_MODEL // SB_HEADS
Q_BLOCK = 128
RW_HEAD_DIM = 64
RW_HEADS = D_MODEL // RW_HEAD_DIM
RW_DECAY_LORA = max(32, int(round(1.8 * D_MODEL ** 0.5 / 32)) * 32)
RW_AAA_LORA = max(32, int(round(1.8 * D_MODEL ** 0.5 / 32)) * 32)
RW_MV_LORA = max(32, int(round(1.3 * D_MODEL ** 0.5 / 32)) * 32)
RW_GATE_LORA = max(32, int(round(0.6 * D_MODEL ** 0.8 / 32)) * 32)
RW_GN_EPS = 64e-5
N_MEM = 256
XA_HEADS = 4
XA_HEAD_DIM = D_MODEL // XA_HEADS
FFN_HIDDEN = ((8 * D_MODEL // 3 + 255) // 256) * 256
NORM_EPS = 1e-6

kernel_name = 'hybrid_stickbreak_rwkv7_stream'


def rms_norm(x, gain):
    xf = x.astype(jnp.float32)
    y = xf * lax.rsqrt(jnp.mean(xf * xf, axis=-1, keepdims=True) + NORM_EPS)
    return (y * gain.astype(jnp.float32)).astype(x.dtype)


def swiglu_ffn(h, w_gate_up, w_down):
    gate, up = jnp.split(h @ w_gate_up, 2, axis=-1)
    return (jax.nn.silu(gate) * up) @ w_down


def memory_kv(mem, gain, w_kv):
    m = rms_norm(mem, gain)
    k, v = jnp.split(m @ w_kv, 2, axis=-1)
    shp = mem.shape[:2] + (XA_HEADS, XA_HEAD_DIM)
    return k.reshape(shp), v.reshape(shp)


def memory_cross_attention(h, mem_k, mem_v, w_q, w_o):
    B, L, _ = h.shape
    q = (h @ w_q).reshape(B, L, XA_HEADS, XA_HEAD_DIM)
    s = jnp.einsum('blhd,bmhd->bhlm', q, mem_k).astype(jnp.float32) * (XA_HEAD_DIM ** -0.5)
    p = jax.nn.softmax(s, axis=-1).astype(mem_v.dtype)
    o = jnp.einsum('bhlm,bmhd->blhd', p, mem_v)
    return o.reshape(B, L, D_MODEL) @ w_o


def stick_breaking_attention(q, k, v, q_offset):
    B, Lq, H, Dh = q.shape
    Lk = k.shape[1]
    blk = min(Q_BLOCK, Lq)
    n_blk = -(-Lq // blk)
    pad = n_blk * blk - Lq
    q_blocks = jnp.pad(q, ((0, 0), (0, pad), (0, 0), (0, 0)))
    q_blocks = q_blocks.reshape(B, n_blk, blk, H, Dh).transpose(1, 0, 2, 3, 4)
    key_pos = jnp.arange(Lk)
    scale = Dh ** -0.5

    def one_block(args):
        q_blk, start = args
        z = jnp.einsum('bqhd,bkhd->bhqk', q_blk, k).astype(jnp.float32) * scale
        q_pos = q_offset + start + jnp.arange(blk)
        before = key_pos[None, :] < q_pos[:, None]
        log_1m_beta = jnp.where(before, jax.nn.log_sigmoid(-z), 0.0)
        log_after = lax.cumsum(log_1m_beta, axis=3, reverse=True) - log_1m_beta
        weights = jnp.where(before, jnp.exp(jax.nn.log_sigmoid(z) + log_after), 0.0)
        return jnp.einsum('bhqk,bkhd->bqhd', weights.astype(v.dtype), v)

    starts = jnp.arange(n_blk, dtype=jnp.int32) * blk
    out = lax.map(one_block, (q_blocks, starts))
    return out.transpose(1, 0, 2, 3, 4).reshape(B, n_blk * blk, H, Dh)[:, :Lq]


def _wkv7_step(S, inp):
    r, decay, k, v, a_vec, b_vec = inp
    sa = jnp.einsum('bhij,bhj->bhi', S, a_vec)
    S = S * decay[:, :, None, :] + sa[..., None] * b_vec[:, :, None, :] + v[..., None] * k[:, :, None, :]
    return S, jnp.einsum('bhij,bhj->bhi', S, r)


def rwkv7_time_mix(h, shift_prev, state0, v_first, mu, w_r, w_k, w_v, w_o, w0, w1, w2,
                   a0, a1, a2, g1, g2, k_k, k_a, r_k, lnx_w, lnx_b, v_res):
    B, L, D = h.shape
    f32 = jnp.float32
    h_prev = jnp.concatenate([shift_prev[:, None, :].astype(h.dtype), h[:, :-1]], axis=1)
    xx = h_prev - h
    x_r, x_w, x_k, x_v, x_a, x_g = (h + xx * mu[i] for i in range(6))
    r = x_r @ w_r
    k = x_k @ w_k
    v = x_v @ w_v
    w_log = -jax.nn.softplus(-(w0 + jnp.tanh(x_w @ w1) @ w2)) - 0.5
    if v_res is None:
        v_first = v
    else:
        v0, v1, v2 = v_res
        v = v + (v_first - v) * jax.nn.sigmoid(v0 + (x_v @ v1) @ v2)
    a = jax.nn.sigmoid(a0 + (x_a @ a1) @ a2)
    g = jax.nn.sigmoid(x_g @ g1) @ g2

    def heads(t):
        return t.astype(f32).reshape(B, L, RW_HEADS, RW_HEAD_DIM)

    kk = heads(k * k_k)
    kk = kk * lax.rsqrt(jnp.maximum(jnp.sum(kk * kk, axis=-1, keepdims=True), 1e-24))
    k = k * (1 + (a - 1) * k_a)
    r_h, k_h, v_h, a_h = heads(r), heads(k), heads(v), heads(a)
    decay = jnp.exp(-jnp.exp(heads(w_log)))
    seq = tuple(jnp.moveaxis(t, 1, 0) for t in (r_h, decay, k_h, v_h, -kk, kk * a_h))
    state, y = lax.scan(_wkv7_step, state0.astype(f32), seq)
    y = jnp.moveaxis(y, 0, 1)
    mean = jnp.mean(y, axis=-1, keepdims=True)
    var = jnp.mean(jnp.square(y - mean), axis=-1, keepdims=True)
    o = ((y - mean) * lax.rsqrt(var + RW_GN_EPS)).reshape(B, L, D) * lnx_w.astype(f32) + lnx_b.astype(f32)
    bonus = jnp.sum(r_h * k_h * r_k.astype(f32), axis=-1, keepdims=True) * v_h
    o = (o + bonus.reshape(B, L, D)) * g.astype(f32)
    out = o.astype(h.dtype) @ w_o
    return out, state.astype(state0.dtype), h[:, -1], v_first


def trunk(x, mem_k, mem_v, sb_past_k, sb_past_v, rw_state0, rw_shift0, p):
    B = x.shape[0]
    past_len = 0 if sb_past_k is None else sb_past_k.shape[2]
    sb_k_rows, sb_v_rows, rw_states, rw_shifts = [], [], [], []
    v_first = None
    for layer in range(DEPTH):
        j = layer // N_MIXERS
        h = rms_norm(x, p['norm_mix'][layer])
        if layer % N_MIXERS == 0:
            q, k, v = jnp.split(h @ p['sb_w_qkv'][j], 3, axis=-1)
            shp = h.shape[:2] + (SB_HEADS, SB_HEAD_DIM)
            q, k, v = q.reshape(shp), k.reshape(shp), v.reshape(shp)
            sb_k_rows.append(k)
            sb_v_rows.append(v)
            if sb_past_k is not None:
                k = jnp.concatenate([sb_past_k[j], k], axis=1)
                v = jnp.concatenate([sb_past_v[j], v], axis=1)
            o = stick_breaking_attention(q, k, v, past_len)
            x = x + o.reshape(h.shape) @ p['sb_w_o'][j]
        else:
            if rw_state0 is None:
                state0 = jnp.zeros((B, RW_HEADS, RW_HEAD_DIM, RW_HEAD_DIM), x.dtype)
                shift0 = jnp.zeros((B, D_MODEL), x.dtype)
            else:
                state0, shift0 = rw_state0[j], rw_shift0[j]
            v_res = None if j == 0 else (p['rw_v0'][j - 1], p['rw_v1'][j - 1], p['rw_v2'][j - 1])
            out, state, shift, v_first = rwkv7_time_mix(
                h, shift0, state0, v_first, p['rw_mu'][j], p['rw_w_r'][j], p['rw_w_k'][j],
                p['rw_w_v'][j], p['rw_w_o'][j], p['rw_w0'][j], p['rw_w1'][j], p['rw_w2'][j],
                p['rw_a0'][j], p['rw_a1'][j], p['rw_a2'][j], p['rw_g1'][j], p['rw_g2'][j],
                p['rw_k_k'][j], p['rw_k_a'][j], p['rw_r_k'][j], p['rw_lnx_w'][j], p['rw_lnx_b'][j], v_res)
            rw_states.append(state)
            rw_shifts.append(shift)
            x = x + out
        h = rms_norm(x, p['norm_xattn'][layer])
        x = x + memory_cross_attention(h, mem_k[layer], mem_v[layer], p['xa_w_q'][layer], p['xa_w_o'][layer])
        h = rms_norm(x, p['norm_ffn'][layer])
        x = x + swiglu_ffn(h, p['ffn_w_gate_up'][layer], p['ffn_w_down'][layer])
    y = rms_norm(x, p['norm_final'])
    return y, jnp.stack(sb_k_rows), jnp.stack(sb_v_rows), jnp.stack(rw_states), jnp.stack(rw_shifts)


def setup_inputs(seed: int = 0) -> dict:
    key = jax.random.key(seed)
    ks = iter(jax.random.split(key, 64))
    f32 = jnp.float32
    D = D_MODEL

    def nrm(shape, scale=1.0):
        return jax.random.normal(next(ks), shape, f32) * scale

    def uni(shape, lo, hi):
        return jax.random.uniform(next(ks), shape, f32, lo, hi)

    return {
        'x_prompt': nrm((BATCH, SEQ, D)),
        'x_sample': nrm((DEC_BATCH, DEC_SEQ, D)),
        'mem_prompt': nrm((BATCH, N_MEM, D)),
        'cache_sb_k': nrm((N_SB_LAYERS, DEC_BATCH, PAST_LEN, SB_HEADS, SB_HEAD_DIM)),
        'cache_sb_v': nrm((N_SB_LAYERS, DEC_BATCH, PAST_LEN, SB_HEADS, SB_HEAD_DIM)),
        'state_rwkv_wkv': nrm((N_RW_LAYERS, DEC_BATCH, RW_HEADS, RW_HEAD_DIM, RW_HEAD_DIM), 0.3),
        'state_rwkv_shift': nrm((N_RW_LAYERS, DEC_BATCH, D)),
        'cache_mem_k': nrm((DEPTH, DEC_BATCH, N_MEM, XA_HEADS, XA_HEAD_DIM)),
        'cache_mem_v': nrm((DEPTH, DEC_BATCH, N_MEM, XA_HEADS, XA_HEAD_DIM)),
        'norm_mix': 1.0 + nrm((DEPTH, D), 0.05),
        'norm_xattn': 1.0 + nrm((DEPTH, D), 0.05),
        'norm_ffn': 1.0 + nrm((DEPTH, D), 0.05),
        'norm_mem': 1.0 + nrm((DEPTH, D), 0.05),
        'norm_final': 1.0 + nrm((D,), 0.05),
        'sb_w_qkv': nrm((N_SB_LAYERS, D, 3 * D), D ** -0.5),
        'sb_w_o': nrm((N_SB_LAYERS, D, D), D ** -0.5),
        'rw_mu': uni((N_RW_LAYERS, 6, D), 0.0, 1.0),
        'rw_w_r': nrm((N_RW_LAYERS, D, D), D ** -0.5),
        'rw_w_k': nrm((N_RW_LAYERS, D, D), D ** -0.5),
        'rw_w_v': nrm((N_RW_LAYERS, D, D), D ** -0.5),
        'rw_w_o': nrm((N_RW_LAYERS, D, D), D ** -0.5),
        'rw_w0': uni((N_RW_LAYERS, D), -5.0, -1.0),
        'rw_w1': nrm((N_RW_LAYERS, D, RW_DECAY_LORA), D ** -0.5),
        'rw_w2': nrm((N_RW_LAYERS, RW_DECAY_LORA, D), 0.1),
        'rw_a0': nrm((N_RW_LAYERS, D), 0.1),
        'rw_a1': nrm((N_RW_LAYERS, D, RW_AAA_LORA), D ** -0.5),
        'rw_a2': nrm((N_RW_LAYERS, RW_AAA_LORA, D), 0.5 * RW_AAA_LORA ** -0.5),
        'rw_v0': nrm((N_RW_LAYERS - 1, D), 0.1),
        'rw_v1': nrm((N_RW_LAYERS - 1, D, RW_MV_LORA), D ** -0.5),
        'rw_v2': nrm((N_RW_LAYERS - 1, RW_MV_LORA, D), 0.5 * RW_MV_LORA ** -0.5),
        'rw_g1': nrm((N_RW_LAYERS, D, RW_GATE_LORA), D ** -0.5),
        'rw_g2': nrm((N_RW_LAYERS, RW_GATE_LORA, D), RW_GATE_LORA ** -0.5),
        'rw_k_k': 0.85 + nrm((N_RW_LAYERS, D), 0.05),
        'rw_k_a': 1.0 + nrm((N_RW_LAYERS, D), 0.05),
        'rw_r_k': nrm((N_RW_LAYERS, RW_HEADS, RW_HEAD_DIM), 0.1),
        'rw_lnx_w': 1.0 + nrm((N_RW_LAYERS, D), 0.05),
        'rw_lnx_b': nrm((N_RW_LAYERS, D), 0.02),
        'xa_w_q': nrm((DEPTH, D, D), D ** -0.5),
        'xa_w_kv': nrm((DEPTH, D, 2 * D), D ** -0.5),
        'xa_w_o': nrm((DEPTH, D, D), D ** -0.5),
        'ffn_w_gate_up': nrm((DEPTH, D, 2 * FFN_HIDDEN), D ** -0.5),
        'ffn_w_down': nrm((DEPTH, FFN_HIDDEN, D), FFN_HIDDEN ** -0.5),
    }


def reference(x_prompt, x_sample, mem_prompt, cache_sb_k, cache_sb_v, state_rwkv_wkv, state_rwkv_shift,
              cache_mem_k, cache_mem_v, norm_mix, norm_xattn, norm_ffn, norm_mem, norm_final,
              sb_w_qkv, sb_w_o, rw_mu, rw_w_r, rw_w_k, rw_w_v, rw_w_o, rw_w0, rw_w1, rw_w2,
              rw_a0, rw_a1, rw_a2, rw_v0, rw_v1, rw_v2, rw_g1, rw_g2, rw_k_k, rw_k_a, rw_r_k,
              rw_lnx_w, rw_lnx_b, xa_w_q, xa_w_kv, xa_w_o, ffn_w_gate_up, ffn_w_down):
    p = dict(norm_mix=norm_mix, norm_xattn=norm_xattn, norm_ffn=norm_ffn, norm_final=norm_final,
             sb_w_qkv=sb_w_qkv, sb_w_o=sb_w_o, rw_mu=rw_mu, rw_w_r=rw_w_r, rw_w_k=rw_w_k,
             rw_w_v=rw_w_v, rw_w_o=rw_w_o, rw_w0=rw_w0, rw_w1=rw_w1, rw_w2=rw_w2, rw_a0=rw_a0,
             rw_a1=rw_a1, rw_a2=rw_a2, rw_v0=rw_v0, rw_v1=rw_v1, rw_v2=rw_v2, rw_g1=rw_g1,
             rw_g2=rw_g2, rw_k_k=rw_k_k, rw_k_a=rw_k_a, rw_r_k=rw_r_k, rw_lnx_w=rw_lnx_w,
             rw_lnx_b=rw_lnx_b, xa_w_q=xa_w_q, xa_w_o=xa_w_o, ffn_w_gate_up=ffn_w_gate_up,
             ffn_w_down=ffn_w_down)

    mem_kv = [memory_kv(mem_prompt, norm_mem[layer], xa_w_kv[layer]) for layer in range(DEPTH)]
    prompt_mem_k = jnp.stack([kv[0] for kv in mem_kv])
    prompt_mem_v = jnp.stack([kv[1] for kv in mem_kv])
    y_prompt, prompt_sb_k, prompt_sb_v, prompt_rwkv_wkv, prompt_rwkv_shift = trunk(
        x_prompt, prompt_mem_k, prompt_mem_v, None, None, None, None, p)

    y_sample, sample_sb_k, sample_sb_v, sample_rwkv_wkv, sample_rwkv_shift = trunk(
        x_sample, cache_mem_k, cache_mem_v, cache_sb_k, cache_sb_v, state_rwkv_wkv, state_rwkv_shift, p)

    return (y_prompt, y_sample, prompt_sb_k, prompt_sb_v, prompt_rwkv_wkv, prompt_rwkv_shift,
            prompt_mem_k, prompt_mem_v, sample_sb_k, sample_sb_v, sample_rwkv_wkv, sample_rwkv_shift)
```

```python
import functools

import jax
import jax.numpy as jnp
from jax import lax
from jax.experimental import pallas as pl
from jax.experimental.pallas import tpu as pltpu

F32 = jnp.float32
BF16 = jnp.bfloat16

NORM_EPS = 1e-6
RW_GN_EPS = 64e-5
SB_HEADS = 16
XA_HEADS = 4
RW_HEAD_DIM = 64
LANES = 128
ROW_TILE = 256
SB_BLOCK = 128
WKV_CHUNK = 64
FFN_CHUNK = 256
VMEM_LIMIT_BYTES = 56 * 1024 * 1024


def _params(*sem):
    return pltpu.CompilerParams(dimension_semantics=sem, vmem_limit_bytes=VMEM_LIMIT_BYTES)


def _dot(a, b):
    return jnp.dot(a, b, preferred_element_type=F32)


def _dot_nt(a, b):
    return lax.dot_general(a, b, (((1,), (1,)), ((), ())), preferred_element_type=F32)


def _split(x, n):
    parts = []
    for i in range(n):
        p = x.astype(BF16)
        parts.append(p)
        if i + 1 < n:
            x = x - p.astype(F32)
    return parts


def _mm(a, b, passes, nt=False):
    dot = _dot_nt if nt else _dot
    if passes == 1:
        return dot(a.astype(BF16), b.astype(BF16))
    a_hi, a_lo = _split(a, 2)
    b_hi, b_lo = _split(b, 2)
    return dot(a_hi, b_hi) + (dot(a_hi, b_lo) + dot(a_lo, b_hi))


def _mm_exact_rhs(a, b_bf16, n):
    out = None
    for p in _split(a, n):
        t = _dot(p, b_bf16)
        out = t if out is None else out + t
    return out


def _mm_exact_lhs(a_bf16, b, n):
    out = None
    for p in _split(b, n):
        t = _dot(a_bf16, p)
        out = t if out is None else out + t
    return out


def _rms(x, gain):
    ms = jnp.mean(x * x, axis=-1, keepdims=True)
    return x * lax.rsqrt(ms + NORM_EPS) * gain


def _group_ones(n, group):
    r = lax.broadcasted_iota(jnp.int32, (n, n), 0) // group
    c = lax.broadcasted_iota(jnp.int32, (n, n), 1) // group
    return jnp.where(r == c, 1.0, 0.0).astype(BF16)


def _mem_kv_kernel(x_ref, g_ref, w_ref, k_ref, v_ref):
    d = x_ref.shape[-1]
    h = _rms(x_ref[...], g_ref[0]).astype(BF16)
    kv = _dot(h, w_ref[0])
    k_ref[0] = kv[:, :d]
    v_ref[0] = kv[:, d:]


def _mem_kv(mem, gains, w_kv):
    depth = w_kv.shape[0]
    rows, d = mem.shape
    tm = min(ROW_TILE, rows)
    out = jax.ShapeDtypeStruct((depth, rows, d), F32)
    return pl.pallas_call(
        _mem_kv_kernel,
        grid=(depth, rows // tm),
        in_specs=[pl.BlockSpec((tm, d), lambda l, i: (i, 0)),
                  pl.BlockSpec((1, 1, d), lambda l, i: (l, 0, 0)),
                  pl.BlockSpec((1, d, 2 * d), lambda l, i: (l, 0, 0))],
        out_specs=[pl.BlockSpec((1, tm, d), lambda l, i: (l, i, 0)),
                   pl.BlockSpec((1, tm, d), lambda l, i: (l, i, 0))],
        out_shape=[out, out],
        compiler_params=_params("arbitrary", "arbitrary"),
        name="mem_kv",
    )(mem, gains.reshape(depth, 1, d), w_kv)


def _sb_qkv_kernel(x_ref, g_ref, w_ref, k_ref, v_ref, qkv_ref, *, q_scale):
    d = x_ref.shape[-1]
    h = _rms(x_ref[...], g_ref[...]).astype(BF16)
    qkv = _dot(h, w_ref[...])
    k_ref[...] = qkv[:, d:2 * d]
    v_ref[...] = qkv[:, 2 * d:]
    qkv_ref[:, :d] = (qkv[:, :d] * q_scale).astype(BF16)
    qkv_ref[:, d:] = qkv[:, d:].astype(BF16)


def _sb_qkv(x, gain, w):
    rows, d = x.shape
    tm = min(ROW_TILE, rows)
    q_scale = float((d // SB_HEADS) ** -0.5)
    return pl.pallas_call(
        functools.partial(_sb_qkv_kernel, q_scale=q_scale),
        grid=(rows // tm,),
        in_specs=[pl.BlockSpec((tm, d), lambda i: (i, 0)),
                  pl.BlockSpec((1, d), lambda i: (0, 0)),
                  pl.BlockSpec((d, 3 * d), lambda i: (0, 0))],
        out_specs=[pl.BlockSpec((tm, d), lambda i: (i, 0)),
                   pl.BlockSpec((tm, d), lambda i: (i, 0)),
                   pl.BlockSpec((tm, 3 * d), lambda i: (i, 0))],
        out_shape=[jax.ShapeDtypeStruct((rows, d), F32),
                   jax.ShapeDtypeStruct((rows, d), F32),
                   jax.ShapeDtypeStruct((rows, 3 * d), BF16)],
        compiler_params=_params("arbitrary"),
        name="sb_qkv",
    )(x, gain.reshape(1, d), w)


def _sb_block(q_heads, kb, vb, carry_ref, acc_ref, upper, ones, mask):
    for h, qh in enumerate(q_heads):
        z = _dot_nt(qh, kb)
        sp = jnp.maximum(z, 0.0) + jnp.log(1.0 + jnp.exp(-jnp.abs(z)))
        log_1m_beta = -sp
        if mask is not None:
            log_1m_beta = jnp.where(mask, log_1m_beta, 0.0)
        hi, lo = _split(log_1m_beta, 2)
        after = _dot(hi, upper) + _dot(lo, upper)
        total = _dot(hi, ones) + _dot(lo, ones)
        w = jnp.exp((z - sp) + after + carry_ref[h])
        if mask is not None:
            w = jnp.where(mask, w, 0.0)
        acc_ref[h] += _dot(w.astype(BF16), vb)
        carry_ref[h] += total


def _sb_consts(bk):
    r = lax.broadcasted_iota(jnp.int32, (bk, bk), 0)
    c = lax.broadcasted_iota(jnp.int32, (bk, bk), 1)
    upper = jnp.where(r > c, 1.0, 0.0).astype(BF16)
    ones = jnp.ones((bk, LANES), BF16)
    return upper, ones


def _sb_q_heads(q):
    half = LANES // 2
    lane = lax.broadcasted_iota(jnp.int32, q.shape, 1)
    zero = jnp.zeros_like(q)
    return [jnp.where(lane < half, q, zero), jnp.where(lane >= half, q, zero)], lane


def _sb_finish(o_ref, acc_ref, lane):
    o_ref[...] = jnp.where(lane < LANES // 2, acc_ref[0], acc_ref[1]).astype(o_ref.dtype)


def _sb_prompt_kernel(q_ref, k_ref, v_ref, o_ref, carry_ref, acc_ref):
    blk = q_ref.shape[0]
    i = pl.program_id(2)
    upper, ones = _sb_consts(blk)
    q_heads, lane = _sb_q_heads(q_ref[...])
    carry_ref[...] = jnp.zeros_like(carry_ref)
    acc_ref[...] = jnp.zeros_like(acc_ref)
    t = lax.broadcasted_iota(jnp.int32, (blk, blk), 0)
    s = lax.broadcasted_iota(jnp.int32, (blk, blk), 1)
    d0 = pl.multiple_of(i * blk, blk)
    _sb_block(q_heads, k_ref[pl.ds(d0, blk), :], v_ref[pl.ds(d0, blk), :],
              carry_ref, acc_ref, upper, ones, s < t)

    def body(it, c):
        j0 = pl.multiple_of((i - 1 - it) * blk, blk)
        _sb_block(q_heads, k_ref[pl.ds(j0, blk), :], v_ref[pl.ds(j0, blk), :],
                  carry_ref, acc_ref, upper, ones, None)
        return c

    lax.fori_loop(0, i, body, 0)
    _sb_finish(o_ref, acc_ref, lane)


def _sb_attention_prompt(qkv, batch):
    rows, d3 = qkv.shape
    d = d3 // 3
    seq = rows // batch
    blk = min(SB_BLOCK, seq)
    nq = seq // blk
    pairs = d // LANES
    return pl.pallas_call(
        _sb_prompt_kernel,
        grid=(batch, pairs, nq),
        in_specs=[pl.BlockSpec((blk, LANES), lambda b, p, i: (b * nq + i, p)),
                  pl.BlockSpec((seq, LANES), lambda b, p, i: (b, pairs + p)),
                  pl.BlockSpec((seq, LANES), lambda b, p, i: (b, 2 * pairs + p))],
        out_specs=pl.BlockSpec((blk, LANES), lambda b, p, i: (b * nq + i, p)),
        out_shape=jax.ShapeDtypeStruct((rows, d), BF16),
        scratch_shapes=[pltpu.VMEM((2, blk, LANES), F32), pltpu.VMEM((2, blk, LANES), F32)],
        compiler_params=_params("arbitrary", "arbitrary", "arbitrary"),
        name="sb_attention_prompt",
    )(qkv, qkv, qkv)


def _sb_sample_kernel(q_ref, kn_ref, vn_ref, kp_ref, vp_ref, o_ref, carry_ref, acc_ref, *, blk):
    tq = q_ref.shape[0]
    n_past = kp_ref.shape[2] // blk
    upper, ones = _sb_consts(blk)
    q_heads, lane = _sb_q_heads(q_ref[...])
    carry_ref[...] = jnp.zeros_like(carry_ref)
    acc_ref[...] = jnp.zeros_like(acc_ref)
    t = lax.broadcasted_iota(jnp.int32, (tq, blk), 0)
    s = lax.broadcasted_iota(jnp.int32, (tq, blk), 1)
    _sb_block(q_heads, kn_ref[0], vn_ref[0], carry_ref, acc_ref, upper, ones, s < t)

    def body(it, c):
        j0 = pl.multiple_of((n_past - 1 - it) * blk, blk)
        kb = kp_ref[0, 0, pl.ds(j0, blk), :].astype(BF16)
        vb = vp_ref[0, 0, pl.ds(j0, blk), :].astype(BF16)
        _sb_block(q_heads, kb, vb, carry_ref, acc_ref, upper, ones, None)
        return c

    lax.fori_loop(0, n_past, body, 0)
    _sb_finish(o_ref, acc_ref, lane)


def _sb_attention_sample(qkv, batch, past_k, past_v, layer):
    rows, d3 = qkv.shape
    d = d3 // 3
    seq = rows // batch
    blk = SB_BLOCK
    pairs = d // LANES
    past_len = past_k.shape[2]
    assert seq <= blk and past_len % blk == 0
    new = qkv[:, d:].reshape(batch, seq, 2 * d)
    new = jnp.pad(new, ((0, 0), (0, blk - seq), (0, 0)))
    return pl.pallas_call(
        functools.partial(_sb_sample_kernel, blk=blk),
        grid=(batch, pairs),
        in_specs=[pl.BlockSpec((seq, LANES), lambda b, p: (b, p)),
                  pl.BlockSpec((1, blk, LANES), lambda b, p: (b, 0, p)),
                  pl.BlockSpec((1, blk, LANES), lambda b, p: (b, 0, pairs + p)),
                  pl.BlockSpec((1, 1, past_len, LANES), lambda b, p: (layer, b, 0, p)),
                  pl.BlockSpec((1, 1, past_len, LANES), lambda b, p: (layer, b, 0, p))],
        out_specs=pl.BlockSpec((seq, LANES), lambda b, p: (b, p)),
        out_shape=jax.ShapeDtypeStruct((rows, d), BF16),
        scratch_shapes=[pltpu.VMEM((2, seq, LANES), F32), pltpu.VMEM((2, seq, LANES), F32)],
        compiler_params=_params("arbitrary", "arbitrary"),
        name="sb_attention_sample",
    )(qkv, new, new, past_k, past_v)


def _linear_residual_kernel(a_ref, w_ref, x_ref, o_ref):
    o_ref[...] = x_ref[...] + _dot(a_ref[...], w_ref[...])


def _linear_residual(a, w, x):
    rows, d = x.shape
    tm = min(ROW_TILE, rows)
    return pl.pallas_call(
        _linear_residual_kernel,
        grid=(rows // tm,),
        in_specs=[pl.BlockSpec((tm, a.shape[1]), lambda i: (i, 0)),
                  pl.BlockSpec(w.shape, lambda i: (0, 0)),
                  pl.BlockSpec((tm, d), lambda i: (i, 0))],
        out_specs=pl.BlockSpec((tm, d), lambda i: (i, 0)),
        out_shape=jax.ShapeDtypeStruct((rows, d), F32),
        input_output_aliases={2: 0},
        compiler_params=_params("arbitrary"),
        name="linear_residual",
    )(a, w, x)


def _xattn_kernel(x_ref, g_ref, wq_ref, wo_ref, mk_ref, mv_ref, o_ref, *, heads):
    x = x_ref[...]
    d = x.shape[-1]
    dh = d // heads
    h = _rms(x, g_ref[...]).astype(BF16)
    q = _dot(h, wq_ref[...]).astype(BF16)
    outs = []
    for hd in range(heads):
        sl = slice(hd * dh, (hd + 1) * dh)
        s = _dot_nt(q[:, sl], mk_ref[0, :, sl]) * (dh ** -0.5)
        e = jnp.exp(s - jnp.max(s, axis=-1, keepdims=True))
        p = e / jnp.sum(e, axis=-1, keepdims=True)
        outs.append(_dot(p.astype(BF16), mv_ref[0, :, sl]).astype(BF16))
    o = jnp.concatenate(outs, axis=-1)
    o_ref[...] = x + _dot(o, wo_ref[...])


def _xattn(x, gain, wq, wo, mem_k, mem_v):
    rows, d = x.shape
    batch, n_mem, _ = mem_k.shape
    seq = rows // batch
    tm = min(ROW_TILE, seq)
    nt = seq // tm
    return pl.pallas_call(
        functools.partial(_xattn_kernel, heads=XA_HEADS),
        grid=(batch, nt),
        in_specs=[pl.BlockSpec((tm, d), lambda b, i: (b * nt + i, 0)),
                  pl.BlockSpec((1, d), lambda b, i: (0, 0)),
                  pl.BlockSpec((d, d), lambda b, i: (0, 0)),
                  pl.BlockSpec((d, d), lambda b, i: (0, 0)),
                  pl.BlockSpec((1, n_mem, d), lambda b, i: (b, 0, 0)),
                  pl.BlockSpec((1, n_mem, d), lambda b, i: (b, 0, 0))],
        out_specs=pl.BlockSpec((tm, d), lambda b, i: (b * nt + i, 0)),
        out_shape=jax.ShapeDtypeStruct((rows, d), F32),
        input_output_aliases={0: 0},
        compiler_params=_params("arbitrary", "arbitrary"),
        name="xattn",
    )(x, gain.reshape(1, d), wq, wo, mem_k, mem_v)


def _ffn_kernel(x_ref, g_ref, wgu_ref, wd_ref, o_ref, *, final_gain):
    x = x_ref[...]
    hidden = wd_ref.shape[0]
    h = _rms(x, g_ref[0:1, :]).astype(BF16)
    acc = x
    for c0 in range(0, hidden, FFN_CHUNK):
        c1 = min(c0 + FFN_CHUNK, hidden)
        gate = _dot(h, wgu_ref[:, c0:c1])
        up = _dot(h, wgu_ref[:, hidden + c0:hidden + c1])
        act = (gate * jax.nn.sigmoid(gate) * up).astype(BF16)
        acc = acc + _dot(act, wd_ref[c0:c1, :])
    if final_gain:
        acc = _rms(acc, g_ref[1:2, :])
    o_ref[...] = acc


def _ffn(x, gain, w_gate_up, w_down, final_gain=None):
    rows, d = x.shape
    tm = min(ROW_TILE, rows)
    gains = jnp.stack([gain, gain if final_gain is None else final_gain])
    return pl.pallas_call(
        functools.partial(_ffn_kernel, final_gain=final_gain is not None),
        grid=(rows // tm,),
        in_specs=[pl.BlockSpec((tm, d), lambda i: (i, 0)),
                  pl.BlockSpec((2, d), lambda i: (0, 0)),
                  pl.BlockSpec(w_gate_up.shape, lambda i: (0, 0)),
                  pl.BlockSpec(w_down.shape, lambda i: (0, 0))],
        out_specs=pl.BlockSpec((tm, d), lambda i: (i, 0)),
        out_shape=jax.ShapeDtypeStruct((rows, d), F32),
        input_output_aliases={0: 0},
        compiler_params=_params("arbitrary"),
        name="ffn",
    )(x, gains, w_gate_up, w_down)


def _rw_proj_kernel(*refs, has_v_res):
    if has_v_res:
        (x_ref, gain_ref, shift0_ref, mu_ref, wr_ref, wk_ref, wv_ref, vec_ref,
         w1_ref, w2_ref, a1_ref, a2_ref, g1_ref, g2_ref, v1_ref, v2_ref, vfirst_ref,
         r_ref, lw_ref, k_ref, v_ref, kk_ref, a_ref, g_ref, shift_ref, prev_ref) = refs
    else:
        (x_ref, gain_ref, shift0_ref, mu_ref, wr_ref, wk_ref, wv_ref, vec_ref,
         w1_ref, w2_ref, a1_ref, a2_ref, g1_ref, g2_ref,
         r_ref, lw_ref, k_ref, v_ref, kk_ref, a_ref, g_ref, shift_ref, prev_ref) = refs
    i = pl.program_id(1)
    h = _rms(x_ref[...], gain_ref[...])
    tm = h.shape[0]

    @pl.when(i == 0)
    def _():
        prev_ref[...] = shift0_ref[0]

    row = lax.broadcasted_iota(jnp.int32, h.shape, 0)
    h_prev = jnp.where(row == 0, prev_ref[...], pltpu.roll(h, 1, 0))
    last = h[tm - 1:tm, :]
    prev_ref[...] = last
    shift_ref[0] = last

    xx = h_prev - h
    x_r, x_w, x_k, x_v, x_a, x_g = ((h + xx * mu_ref[j:j + 1, :]).astype(BF16) for j in range(6))
    w0, a0, k_k, k_a, v0 = (vec_ref[j:j + 1, :] for j in range(5))

    r = _dot(x_r, wr_ref[...])
    k = _dot(x_k, wk_ref[...])
    v = _dot(x_v, wv_ref[...])
    dec = w0 + _dot(jnp.tanh(_dot(x_w, w1_ref[...])).astype(BF16), w2_ref[...])
    w_log = -jax.nn.softplus(-dec) - 0.5
    if has_v_res:
        mix = jax.nn.sigmoid(v0 + _dot(_dot(x_v, v1_ref[...]).astype(BF16), v2_ref[...]))
        v = v + (vfirst_ref[...] - v) * mix
    a = jax.nn.sigmoid(a0 + _dot(_dot(x_a, a1_ref[...]).astype(BF16), a2_ref[...]))
    g = _dot(jax.nn.sigmoid(_dot(x_g, g1_ref[...])).astype(BF16), g2_ref[...])

    r_ref[...] = r
    lw_ref[...] = -jnp.exp(w_log)
    kk_ref[...] = k * k_k
    k_ref[...] = k * (1.0 + (a - 1.0) * k_a)
    v_ref[...] = v
    a_ref[...] = a
    g_ref[...] = g


def _pad_cols(w, mult=LANES):
    pad = -w.shape[1] % mult
    return jnp.pad(w, ((0, 0), (0, pad))) if pad else w


def _pad_rows(w, mult=LANES):
    pad = -w.shape[0] % mult
    return jnp.pad(w, ((0, pad), (0, 0))) if pad else w


def _rw_proj(x, batch, gain, shift0, p, v_first):
    rows, d = x.shape
    seq = rows // batch
    tm = min(ROW_TILE, seq)
    nt = seq // tm
    has_v_res = v_first is not None
    zeros = jnp.zeros((d,), F32)
    vec = jnp.stack([p["w0"], p["a0"], p["k_k"], p["k_a"], p["v0"] if has_v_res else zeros,
                     zeros, zeros, zeros])
    row_spec = pl.BlockSpec((tm, d), lambda b, i: (b * nt + i, 0))

    def full(arr):
        return pl.BlockSpec(arr.shape, lambda b, i: (0,) * arr.ndim)

    ins = [x, gain.reshape(1, d), shift0.reshape(batch, 1, d), p["mu"], p["w_r"], p["w_k"], p["w_v"], vec,
           p["w1"], p["w2"], p["a1"], p["a2"], p["g1"], p["g2"]]
    in_specs = [row_spec, full(ins[1]), pl.BlockSpec((1, 1, d), lambda b, i: (b, 0, 0))]
    in_specs += [full(a) for a in ins[3:]]
    if has_v_res:
        ins += [p["v1"], p["v2"], v_first]
        in_specs += [full(p["v1"]), full(p["v2"]), row_spec]
    tok = jax.ShapeDtypeStruct((rows, d), F32)
    outs = pl.pallas_call(
        functools.partial(_rw_proj_kernel, has_v_res=has_v_res),
        grid=(batch, nt),
        in_specs=in_specs,
        out_specs=[row_spec] * 7 + [pl.BlockSpec((1, 1, d), lambda b, i: (b, 0, 0))],
        out_shape=[tok] * 7 + [jax.ShapeDtypeStruct((batch, 1, d), F32)],
        scratch_shapes=[pltpu.VMEM((1, d), F32)],
        compiler_params=_params("arbitrary", "arbitrary"),
        name="rw_proj",
    )(*ins)
    return outs[:7], outs[7].reshape(batch, d)


WKV_PASSES = 3


def _inv_unit_lower(low, eye):
    n = low.shape[0]
    t = eye + low
    pw = low
    k = 1
    while 2 * k < n:
        pw = _mm(pw, pw, WKV_PASSES)
        t = t + _mm(t, pw, WKV_PASSES)
        k *= 2
    return t


def _wkv_kernel(*refs, has_state0):
    if has_state0:
        (r_ref, lw_ref, k_ref, v_ref, kk_ref, a_ref, g_ref, vec_ref, s0_ref,
         o_ref, sout_ref, state_ref) = refs
    else:
        (r_ref, lw_ref, k_ref, v_ref, kk_ref, a_ref, g_ref, vec_ref,
         o_ref, sout_ref, state_ref) = refs
    c = pl.program_id(2)
    n = r_ref.shape[0]
    hd = RW_HEAD_DIM

    @pl.when(c == 0)
    def _():
        if has_state0:
            state_ref[...] = s0_ref[0, 0]
        else:
            state_ref[...] = jnp.zeros_like(state_ref)

    gsum = _group_ones(LANES, hd)
    tr = lax.broadcasted_iota(jnp.int32, (n, n), 0)
    tc = lax.broadcasted_iota(jnp.int32, (n, n), 1)
    tri_incl = jnp.where(tc <= tr, 1.0, 0.0).astype(BF16)
    strict = tc < tr
    incl = tc <= tr
    eye = jnp.where(tc == tr, 1.0, 0.0).astype(F32)
    lane = lax.broadcasted_iota(jnp.int32, (n, LANES), 1)
    head_masks = [lane < hd, lane >= hd]
    sr = lax.broadcasted_iota(jnp.int32, (LANES, LANES), 0) // hd
    sc = lax.broadcasted_iota(jnp.int32, (LANES, LANES), 1) // hd
    same_head = sr == sc

    r = r_ref[...]
    lw = lw_ref[...]
    k = k_ref[...]
    v = v_ref[...]
    kk = kk_ref[...]
    a = a_ref[...]
    state = state_ref[...]

    cs = _mm_exact_lhs(tri_incl, lw, 3)
    cs_last = cs[n - 1:n, :]
    d_incl = jnp.exp(cs)
    d_excl = jnp.exp(cs - lw)
    d_inv = jnp.exp(-cs)
    d_rest = jnp.exp(cs_last - cs)
    kkn = kk * lax.rsqrt(jnp.maximum(_mm_exact_rhs(kk * kk, gsum, 3), 1e-24))
    b = kkn * a
    a_t = -kkn * d_excl
    r_t = r * d_incl
    b_t = b * d_inv
    k_t = k * d_inv

    from_state = _mm(jnp.concatenate([a_t, r_t], axis=0), state, WKV_PASSES, nt=True)
    a_s, r_s = from_state[:n], from_state[n:]
    rhs_bk = jnp.concatenate([b_t, k_t], axis=0)
    zero = jnp.zeros_like(a_t)
    sig_heads, y_heads = [], []
    for m in head_masks:
        lhs = jnp.concatenate([jnp.where(m, a_t, zero), jnp.where(m, r_t, zero)], axis=0)
        gram = _mm(lhs, rhs_bk, WKV_PASSES, nt=True)
        l_ab = jnp.where(strict, gram[:n, :n], 0.0)
        m_ak = jnp.where(strict, gram[:n, n:], 0.0)
        m_rb = jnp.where(incl, gram[n:, :n], 0.0)
        m_rk = jnp.where(incl, gram[n:, n:], 0.0)
        rhs = a_s + _mm(m_ak, v, WKV_PASSES)
        sig = _mm(_inv_unit_lower(l_ab, eye), rhs, WKV_PASSES)
        y = r_s + _mm(jnp.concatenate([m_rb, m_rk], axis=1),
                      jnp.concatenate([sig, v], axis=0), WKV_PASSES)
        sig_heads.append(sig)
        y_heads.append(y)
    sig = jnp.where(head_masks[0], sig_heads[0], sig_heads[1])
    y = jnp.where(head_masks[0], y_heads[0], y_heads[1])

    upd = _mm(jnp.concatenate([sig, v], axis=0).T,
              jnp.concatenate([b * d_rest, k * d_rest], axis=0), WKV_PASSES)
    new_state = state * jnp.exp(cs_last) + jnp.where(same_head, upd, 0.0)
    state_ref[...] = new_state
    sout_ref[0, 0] = new_state

    r_k, lnx_w, lnx_b = (vec_ref[j:j + 1, :] for j in range(3))
    inv_hd = 1.0 / hd
    mean = _mm_exact_rhs(y, gsum, 3) * inv_hd
    dev = y - mean
    var = _mm_exact_rhs(dev * dev, gsum, 3) * inv_hd
    o = dev * lax.rsqrt(var + RW_GN_EPS) * lnx_w + lnx_b
    bonus = _mm_exact_rhs(r * k * r_k, gsum, 3) * v
    o_ref[...] = ((o + bonus) * g_ref[...]).astype(o_ref.dtype)


def _wkv(tok, batch, vec, state0):
    rows, d = tok[0].shape
    seq = rows // batch
    n = min(WKV_CHUNK, seq)
    nc = seq // n
    pairs = d // LANES
    has_state0 = state0 is not None
    tok_spec = pl.BlockSpec((n, LANES), lambda b, p, c: (b * nc + c, p))
    state_spec = pl.BlockSpec((1, 1, LANES, LANES), lambda b, p, c: (b, p, 0, 0))
    ins = list(tok) + [vec]
    in_specs = [tok_spec] * 7 + [pl.BlockSpec((8, LANES), lambda b, p, c: (0, p))]
    if has_state0:
        ins.append(state0)
        in_specs.append(state_spec)
    return pl.pallas_call(
        functools.partial(_wkv_kernel, has_state0=has_state0),
        grid=(batch, pairs, nc),
        in_specs=in_specs,
        out_specs=[tok_spec, state_spec],
        out_shape=[jax.ShapeDtypeStruct((rows, d), BF16),
                   jax.ShapeDtypeStruct((batch, pairs, LANES, LANES), F32)],
        scratch_shapes=[pltpu.VMEM((LANES, LANES), F32)],
        compiler_params=_params("arbitrary", "arbitrary", "arbitrary"),
        name="wkv",
    )(*ins)


def _state_to_blockdiag(state):
    b, h, n, _ = state.shape
    s = state.reshape(b, h // 2, 2, n, n)
    z = jnp.zeros_like(s[:, :, 0])
    top = jnp.concatenate([s[:, :, 0], z], axis=-1)
    bot = jnp.concatenate([z, s[:, :, 1]], axis=-1)
    return jnp.concatenate([top, bot], axis=-2)


def _state_from_blockdiag(bd):
    b, p, n2, _ = bd.shape
    n = n2 // 2
    return jnp.stack([bd[:, :, :n, :n], bd[:, :, n:, n:]], axis=2).reshape(b, 2 * p, n, n)


def _trunk(x, batch, mem_k, mem_v, past_k, past_v, rw_state0, rw_shift0, w):
    depth = len(w["ffn_gu"])
    d = x.shape[-1]
    sb_k, sb_v, rw_states, rw_shifts = [], [], [], []
    v_first = None
    for layer in range(depth):
        j = layer // 2
        if layer % 2 == 0:
            k, v, qkv = _sb_qkv(x, w["norm_mix"][layer], w["sb_qkv"][j])
            sb_k.append(k)
            sb_v.append(v)
            if past_k is None:
                o = _sb_attention_prompt(qkv, batch)
            else:
                o = _sb_attention_sample(qkv, batch, past_k, past_v, j)
            x = _linear_residual(o, w["sb_o"][j], x)
        else:
            if rw_state0 is None:
                state0 = None
                shift0 = jnp.zeros((batch, d), F32)
            else:
                state0 = _state_to_blockdiag(rw_state0[j])
                shift0 = rw_shift0[j]
            tok, shift = _rw_proj(x, batch, w["norm_mix"][layer], shift0, w["rw"][j], v_first)
            if v_first is None:
                v_first = tok[3]
            o, state = _wkv(tok, batch, w["rw"][j]["out_vec"], state0)
            rw_states.append(_state_from_blockdiag(state))
            rw_shifts.append(shift)
            x = _linear_residual(o, w["rw"][j]["w_o"], x)
        x = _xattn(x, w["norm_xattn"][layer], w["xa_q"][layer], w["xa_o"][layer], mem_k[layer], mem_v[layer])
        final = w["norm_final"] if layer == depth - 1 else None
        x = _ffn(x, w["norm_ffn"][layer], w["ffn_gu"][layer], w["ffn_down"][layer], final)
    return x, jnp.stack(sb_k), jnp.stack(sb_v), jnp.stack(rw_states), jnp.stack(rw_shifts)


def kernel(x_prompt, x_sample, mem_prompt, cache_sb_k, cache_sb_v, state_rwkv_wkv, state_rwkv_shift,
           cache_mem_k, cache_mem_v, norm_mix, norm_xattn, norm_ffn, norm_mem, norm_final,
           sb_w_qkv, sb_w_o, rw_mu, rw_w_r, rw_w_k, rw_w_v, rw_w_o, rw_w0, rw_w1, rw_w2,
           rw_a0, rw_a1, rw_a2, rw_v0, rw_v1, rw_v2, rw_g1, rw_g2, rw_k_k, rw_k_a, rw_r_k,
           rw_lnx_w, rw_lnx_b, xa_w_q, xa_w_kv, xa_w_o, ffn_w_gate_up, ffn_w_down):
    batch, seq, d = x_prompt.shape
    dec_batch, dec_seq, _ = x_sample.shape
    depth = norm_mix.shape[0]
    n_rw = rw_w_r.shape[0]
    n_sb = sb_w_qkv.shape[0]
    n_mem = mem_prompt.shape[1]
    bf = lambda t: t.astype(BF16)

    rw = []
    for j in range(n_rw):
        pj = dict(mu=rw_mu[j], w_r=bf(rw_w_r[j]), w_k=bf(rw_w_k[j]), w_v=bf(rw_w_v[j]), w_o=bf(rw_w_o[j]),
                  w0=rw_w0[j], a0=rw_a0[j], k_k=rw_k_k[j], k_a=rw_k_a[j],
                  w1=bf(_pad_cols(rw_w1[j])), w2=bf(_pad_rows(rw_w2[j])),
                  a1=bf(_pad_cols(rw_a1[j])), a2=bf(_pad_rows(rw_a2[j])),
                  g1=bf(_pad_cols(rw_g1[j])), g2=bf(_pad_rows(rw_g2[j])))
        if j > 0:
            pj.update(v0=rw_v0[j - 1], v1=bf(_pad_cols(rw_v1[j - 1])), v2=bf(_pad_rows(rw_v2[j - 1])))
        zeros = jnp.zeros((d,), F32)
        pj["out_vec"] = jnp.stack([rw_r_k[j].reshape(d), rw_lnx_w[j], rw_lnx_b[j]] + [zeros] * 5)
        rw.append(pj)
    w = dict(norm_mix=norm_mix, norm_xattn=norm_xattn, norm_ffn=norm_ffn, norm_final=norm_final,
             sb_qkv=bf(sb_w_qkv), sb_o=bf(sb_w_o), rw=rw, xa_q=bf(xa_w_q), xa_o=bf(xa_w_o),
             ffn_gu=bf(ffn_w_gate_up), ffn_down=bf(ffn_w_down))

    mem_k, mem_v = _mem_kv(mem_prompt.reshape(batch * n_mem, d), norm_mem, bf(xa_w_kv))
    mem_shape = (depth, batch, n_mem, XA_HEADS, d // XA_HEADS)
    y_p, p_sb_k, p_sb_v, p_wkv, p_shift = _trunk(
        x_prompt.reshape(batch * seq, d), batch,
        bf(mem_k).reshape(depth, batch, n_mem, d), bf(mem_v).reshape(depth, batch, n_mem, d),
        None, None, None, None, w)

    past_len = cache_sb_k.shape[2]
    y_s, s_sb_k, s_sb_v, s_wkv, s_shift = _trunk(
        x_sample.reshape(dec_batch * dec_seq, d), dec_batch,
        bf(cache_mem_k).reshape(depth, dec_batch, n_mem, d), bf(cache_mem_v).reshape(depth, dec_batch, n_mem, d),
        cache_sb_k.reshape(n_sb, dec_batch, past_len, d), cache_sb_v.reshape(n_sb, dec_batch, past_len, d),
        state_rwkv_wkv, state_rwkv_shift, w)

    hd = d // SB_HEADS
    return (y_p.reshape(batch, seq, d), y_s.reshape(dec_batch, dec_seq, d),
            p_sb_k.reshape(n_sb, batch, seq, SB_HEADS, hd), p_sb_v.reshape(n_sb, batch, seq, SB_HEADS, hd),
            p_wkv, p_shift,
            mem_k.reshape(mem_shape), mem_v.reshape(mem_shape),
            s_sb_k.reshape(n_sb, dec_batch, dec_seq, SB_HEADS, hd),
            s_sb_v.reshape(n_sb, dec_batch, dec_seq, SB_HEADS, hd),
            s_wkv, s_shift)
```

```python
import functools

import jax
import jax.numpy as jnp
from jax import lax
from jax.experimental import pallas as pl
from jax.experimental.pallas import tpu as pltpu

F32 = jnp.float32
BF16 = jnp.bfloat16

NORM_EPS = 1e-6
RW_GN_EPS = 64e-5
SB_HEADS = 16
XA_HEADS = 4
RW_HEAD_DIM = 64
LANES = 128
ROW_TILE = 256
SB_BLOCK = 256
SB_PAIRS = 4
SB_PAST_SPAN = 1024
WKV_CHUNK = 64
WKV_PAIRS = 4
FFN_CHUNK = 256
VMEM_LIMIT_BYTES = 56 * 1024 * 1024


def _params(*sem):
    return pltpu.CompilerParams(dimension_semantics=sem, vmem_limit_bytes=VMEM_LIMIT_BYTES)


def _dot(a, b):
    return jnp.dot(a, b, preferred_element_type=F32)


def _dot_nt(a, b):
    return lax.dot_general(a, b, (((1,), (1,)), ((), ())), preferred_element_type=F32)


def _split(x, n):
    parts = []
    for i in range(n):
        p = x.astype(BF16)
        parts.append(p)
        if i + 1 < n:
            x = x - p.astype(F32)
    return parts


def _mm(a, b, passes, nt=False):
    dot = _dot_nt if nt else _dot
    if passes == 1:
        return dot(a.astype(BF16), b.astype(BF16))
    a_hi, a_lo = _split(a, 2)
    b_hi, b_lo = _split(b, 2)
    return dot(a_hi, b_hi) + (dot(a_hi, b_lo) + dot(a_lo, b_hi))


def _mm_exact_rhs(a, b_bf16, n):
    out = None
    for p in _split(a, n):
        t = _dot(p, b_bf16)
        out = t if out is None else out + t
    return out


def _mm_exact_lhs(a_bf16, b, n):
    out = None
    for p in _split(b, n):
        t = _dot(a_bf16, p)
        out = t if out is None else out + t
    return out


def _rms(x, gain):
    ms = jnp.mean(x * x, axis=-1, keepdims=True)
    return x * lax.rsqrt(ms + NORM_EPS) * gain


def _group_ones(n, group):
    r = lax.broadcasted_iota(jnp.int32, (n, n), 0) // group
    c = lax.broadcasted_iota(jnp.int32, (n, n), 1) // group
    return jnp.where(r == c, 1.0, 0.0).astype(BF16)


def _mem_kv_kernel(x_ref, g_ref, w_ref, k_ref, v_ref):
    d = x_ref.shape[-1]
    h = _rms(x_ref[...], g_ref[0]).astype(BF16)
    kv = _dot(h, w_ref[0])
    k_ref[0] = kv[:, :d]
    v_ref[0] = kv[:, d:]


def _mem_kv(mem, gains, w_kv):
    depth = w_kv.shape[0]
    rows, d = mem.shape
    tm = min(ROW_TILE, rows)
    out = jax.ShapeDtypeStruct((depth, rows, d), F32)
    return pl.pallas_call(
        _mem_kv_kernel,
        grid=(depth, rows // tm),
        in_specs=[pl.BlockSpec((tm, d), lambda l, i: (i, 0)),
                  pl.BlockSpec((1, 1, d), lambda l, i: (l, 0, 0)),
                  pl.BlockSpec((1, d, 2 * d), lambda l, i: (l, 0, 0))],
        out_specs=[pl.BlockSpec((1, tm, d), lambda l, i: (l, i, 0)),
                   pl.BlockSpec((1, tm, d), lambda l, i: (l, i, 0))],
        out_shape=[out, out],
        compiler_params=_params("arbitrary", "arbitrary"),
        name="mem_kv",
    )(mem, gains.reshape(depth, 1, d), w_kv)


def _sb_qkv_kernel(x_ref, g_ref, w_ref, k_ref, v_ref, qkv_ref, *, q_scale):
    d = x_ref.shape[-1]
    h = _rms(x_ref[...], g_ref[...]).astype(BF16)
    qkv = _dot(h, w_ref[...])
    k_ref[...] = qkv[:, d:2 * d]
    v_ref[...] = qkv[:, 2 * d:]
    qkv_ref[:, :d] = (qkv[:, :d] * q_scale).astype(BF16)
    qkv_ref[:, d:] = qkv[:, d:].astype(BF16)


def _sb_qkv(x, gain, w):
    rows, d = x.shape
    tm = min(ROW_TILE, rows)
    q_scale = float((d // SB_HEADS) ** -0.5)
    return pl.pallas_call(
        functools.partial(_sb_qkv_kernel, q_scale=q_scale),
        grid=(rows // tm,),
        in_specs=[pl.BlockSpec((tm, d), lambda i: (i, 0)),
                  pl.BlockSpec((1, d), lambda i: (0, 0)),
                  pl.BlockSpec((d, 3 * d), lambda i: (0, 0))],
        out_specs=[pl.BlockSpec((tm, d), lambda i: (i, 0)),
                   pl.BlockSpec((tm, d), lambda i: (i, 0)),
                   pl.BlockSpec((tm, 3 * d), lambda i: (i, 0))],
        out_shape=[jax.ShapeDtypeStruct((rows, d), F32),
                   jax.ShapeDtypeStruct((rows, d), F32),
                   jax.ShapeDtypeStruct((rows, 3 * d), BF16)],
        compiler_params=_params("arbitrary"),
        name="sb_qkv",
    )(x, gain.reshape(1, d), w)


def _sb_chain(q2, kb, vb, carry, acc, neg_upper, mask):
    z = _dot_nt(q2, kb)
    sp = jnp.maximum(z, 0.0) + jnp.log(1.0 + jnp.exp(-jnp.abs(z)))
    if mask is not None:
        sp = jnp.where(mask, sp, 0.0)
    hi, lo = _split(sp, 2)
    after = _dot(hi, neg_upper) + _dot(lo, neg_upper)
    later = jnp.concatenate([carry] * (kb.shape[0] // LANES), axis=1)
    w = jnp.exp((z - sp) + after + later)
    if mask is not None:
        w = jnp.where(mask, w, 0.0)
    total = jnp.broadcast_to(after[:, 0:1] - sp[:, 0:1], carry.shape)
    return carry + total, acc + _dot(w.astype(BF16), vb)


def _sb_consts(bk):
    r = lax.broadcasted_iota(jnp.int32, (bk, bk), 0)
    c = lax.broadcasted_iota(jnp.int32, (bk, bk), 1)
    return jnp.where(r > c, -1.0, 0.0).astype(BF16)


def _sb_stack_heads(q_ref, q2_ref):
    tq = q_ref.shape[0]
    lane = lax.broadcasted_iota(jnp.int32, (tq, LANES), 1)
    for g in range(q2_ref.shape[0]):
        q = q_ref[:, g * LANES:(g + 1) * LANES]
        zero = jnp.zeros_like(q)
        q2_ref[g] = jnp.concatenate([jnp.where(lane < LANES // 2, q, zero),
                                     jnp.where(lane >= LANES // 2, q, zero)], axis=0)


def _sb_causal_mask(tq, bk):
    t = lax.broadcasted_iota(jnp.int32, (2 * tq, bk), 0)
    s = lax.broadcasted_iota(jnp.int32, (2 * tq, bk), 1)
    return s < jnp.where(t >= tq, t - tq, t)


def _sb_write(o_ref, acc_ref):
    tq = o_ref.shape[0]
    lane = lax.broadcasted_iota(jnp.int32, (tq, LANES), 1)
    for g in range(acc_ref.shape[0]):
        acc = acc_ref[g]
        o_ref[:, g * LANES:(g + 1) * LANES] = jnp.where(lane < LANES // 2, acc[:tq], acc[tq:]).astype(o_ref.dtype)


def _sb_prompt_kernel(q_ref, k_ref, v_ref, o_ref, q2_ref, carry_ref, acc_ref):
    blk = q_ref.shape[0]
    groups = q2_ref.shape[0]
    i = pl.program_id(2)
    neg_upper = _sb_consts(blk)
    _sb_stack_heads(q_ref, q2_ref)
    zero = jnp.zeros((2 * blk, LANES), F32)
    mask = _sb_causal_mask(blk, blk)
    d0 = pl.multiple_of(i * blk, blk)
    for g in range(groups):
        sl = slice(g * LANES, (g + 1) * LANES)
        carry_ref[g], acc_ref[g] = _sb_chain(q2_ref[g], k_ref[pl.ds(d0, blk), sl], v_ref[pl.ds(d0, blk), sl],
                                             zero, zero, neg_upper, mask)

    def body(it, c):
        j0 = pl.multiple_of((i - 1 - it) * blk, blk)
        for g in range(groups):
            sl = slice(g * LANES, (g + 1) * LANES)
            carry_ref[g], acc_ref[g] = _sb_chain(q2_ref[g], k_ref[pl.ds(j0, blk), sl], v_ref[pl.ds(j0, blk), sl],
                                                 carry_ref[g], acc_ref[g], neg_upper, None)
        return c

    lax.fori_loop(0, i, body, 0)
    _sb_write(o_ref, acc_ref)


def _sb_attention_prompt(qkv, batch):
    rows, d3 = qkv.shape
    d = d3 // 3
    seq = rows // batch
    blk = min(SB_BLOCK, seq)
    nq = seq // blk
    width = SB_PAIRS * LANES
    ng = d // width
    return pl.pallas_call(
        _sb_prompt_kernel,
        grid=(batch, ng, nq),
        in_specs=[pl.BlockSpec((blk, width), lambda b, p, i: (b * nq + i, p)),
                  pl.BlockSpec((seq, width), lambda b, p, i: (b, ng + p)),
                  pl.BlockSpec((seq, width), lambda b, p, i: (b, 2 * ng + p))],
        out_specs=pl.BlockSpec((blk, width), lambda b, p, i: (b * nq + i, p)),
        out_shape=jax.ShapeDtypeStruct((rows, d), BF16),
        scratch_shapes=[pltpu.VMEM((SB_PAIRS, 2 * blk, LANES), BF16),
                        pltpu.VMEM((SB_PAIRS, 2 * blk, LANES), F32),
                        pltpu.VMEM((SB_PAIRS, 2 * blk, LANES), F32)],
        compiler_params=_params("arbitrary", "arbitrary", "arbitrary"),
        name="sb_attention_prompt",
    )(qkv, qkv, qkv)


def _sb_sample_kernel(q_ref, kn_ref, vn_ref, kp_ref, vp_ref, o_ref, q2_ref, carry_ref, acc_ref, *, blk):
    tq = q_ref.shape[0]
    groups = q2_ref.shape[0]
    c = pl.program_id(1)
    neg_upper = _sb_consts(blk)

    @pl.when(c == 0)
    def _():
        _sb_stack_heads(q_ref, q2_ref)
        zero = jnp.zeros((2 * tq, LANES), F32)
        mask = _sb_causal_mask(tq, blk)
        for g in range(groups):
            sl = slice(g * LANES, (g + 1) * LANES)
            carry_ref[g], acc_ref[g] = _sb_chain(q2_ref[g], kn_ref[0, :, sl], vn_ref[0, :, sl],
                                                 zero, zero, neg_upper, mask)

    for sb in reversed(range(kp_ref.shape[2] // blk)):
        rows = slice(sb * blk, (sb + 1) * blk)
        for g in range(groups):
            sl = slice(g * LANES, (g + 1) * LANES)
            carry_ref[g], acc_ref[g] = _sb_chain(q2_ref[g], kp_ref[0, 0, rows, sl].astype(BF16),
                                                 vp_ref[0, 0, rows, sl].astype(BF16),
                                                 carry_ref[g], acc_ref[g], neg_upper, None)

    @pl.when(c == pl.num_programs(1) - 1)
    def _():
        _sb_write(o_ref, acc_ref)


def _sb_attention_sample(qkv, batch, past_k, past_v, layer):
    rows, d3 = qkv.shape
    d = d3 // 3
    seq = rows // batch
    blk = SB_BLOCK
    pairs = d // LANES
    past_len = past_k.shape[2]
    span = min(SB_PAST_SPAN, past_len)
    steps = past_len // span
    assert seq <= blk and past_len % span == 0 and span % blk == 0
    new = qkv[:, d:].reshape(batch, seq, 2 * d)
    new = jnp.pad(new, ((0, 0), (0, blk - seq), (0, 0)))
    return pl.pallas_call(
        functools.partial(_sb_sample_kernel, blk=blk),
        grid=(batch, steps),
        in_specs=[pl.BlockSpec((seq, d), lambda b, c: (b, 0)),
                  pl.BlockSpec((1, blk, d), lambda b, c: (b, 0, 0)),
                  pl.BlockSpec((1, blk, d), lambda b, c: (b, 0, 1)),
                  pl.BlockSpec((1, 1, span, d), lambda b, c: (layer, b, steps - 1 - c, 0)),
                  pl.BlockSpec((1, 1, span, d), lambda b, c: (layer, b, steps - 1 - c, 0))],
        out_specs=pl.BlockSpec((seq, d), lambda b, c: (b, 0)),
        out_shape=jax.ShapeDtypeStruct((rows, d), BF16),
        scratch_shapes=[pltpu.VMEM((pairs, 2 * seq, LANES), BF16),
                        pltpu.VMEM((pairs, 2 * seq, LANES), F32),
                        pltpu.VMEM((pairs, 2 * seq, LANES), F32)],
        compiler_params=_params("arbitrary", "arbitrary"),
        name="sb_attention_sample",
    )(qkv, new, new, past_k, past_v)


def _linear_residual_kernel(a_ref, w_ref, x_ref, o_ref):
    o_ref[...] = x_ref[...] + _dot(a_ref[...], w_ref[...])


def _linear_residual(a, w, x):
    rows, d = x.shape
    tm = min(ROW_TILE, rows)
    return pl.pallas_call(
        _linear_residual_kernel,
        grid=(rows // tm,),
        in_specs=[pl.BlockSpec((tm, a.shape[1]), lambda i: (i, 0)),
                  pl.BlockSpec(w.shape, lambda i: (0, 0)),
                  pl.BlockSpec((tm, d), lambda i: (i, 0))],
        out_specs=pl.BlockSpec((tm, d), lambda i: (i, 0)),
        out_shape=jax.ShapeDtypeStruct((rows, d), F32),
        input_output_aliases={2: 0},
        compiler_params=_params("arbitrary"),
        name="linear_residual",
    )(a, w, x)


def _xattn_kernel(x_ref, g_ref, wq_ref, wo_ref, mk_ref, mv_ref, o_ref, *, heads):
    x = x_ref[...]
    d = x.shape[-1]
    dh = d // heads
    h = _rms(x, g_ref[...]).astype(BF16)
    q = _dot(h, wq_ref[...]).astype(BF16)
    outs = []
    for hd in range(heads):
        sl = slice(hd * dh, (hd + 1) * dh)
        s = _dot_nt(q[:, sl], mk_ref[0, :, sl]) * (dh ** -0.5)
        e = jnp.exp(s - jnp.max(s, axis=-1, keepdims=True))
        p = e / jnp.sum(e, axis=-1, keepdims=True)
        outs.append(_dot(p.astype(BF16), mv_ref[0, :, sl]).astype(BF16))
    o = jnp.concatenate(outs, axis=-1)
    o_ref[...] = x + _dot(o, wo_ref[...])


def _xattn(x, gain, wq, wo, mem_k, mem_v):
    rows, d = x.shape
    batch, n_mem, _ = mem_k.shape
    seq = rows // batch
    tm = min(ROW_TILE, seq)
    nt = seq // tm
    return pl.pallas_call(
        functools.partial(_xattn_kernel, heads=XA_HEADS),
        grid=(batch, nt),
        in_specs=[pl.BlockSpec((tm, d), lambda b, i: (b * nt + i, 0)),
                  pl.BlockSpec((1, d), lambda b, i: (0, 0)),
                  pl.BlockSpec((d, d), lambda b, i: (0, 0)),
                  pl.BlockSpec((d, d), lambda b, i: (0, 0)),
                  pl.BlockSpec((1, n_mem, d), lambda b, i: (b, 0, 0)),
                  pl.BlockSpec((1, n_mem, d), lambda b, i: (b, 0, 0))],
        out_specs=pl.BlockSpec((tm, d), lambda b, i: (b * nt + i, 0)),
        out_shape=jax.ShapeDtypeStruct((rows, d), F32),
        input_output_aliases={0: 0},
        compiler_params=_params("arbitrary", "arbitrary"),
        name="xattn",
    )(x, gain.reshape(1, d), wq, wo, mem_k, mem_v)


def _ffn_kernel(x_ref, g_ref, wgu_ref, wd_ref, o_ref, *, final_gain):
    x = x_ref[...]
    hidden = wd_ref.shape[0]
    h = _rms(x, g_ref[0:1, :]).astype(BF16)
    acc = x
    for c0 in range(0, hidden, FFN_CHUNK):
        c1 = min(c0 + FFN_CHUNK, hidden)
        gate = _dot(h, wgu_ref[:, c0:c1])
        up = _dot(h, wgu_ref[:, hidden + c0:hidden + c1])
        act = (gate * jax.nn.sigmoid(gate) * up).astype(BF16)
        acc = acc + _dot(act, wd_ref[c0:c1, :])
    if final_gain:
        acc = _rms(acc, g_ref[1:2, :])
    o_ref[...] = acc


def _ffn(x, gain, w_gate_up, w_down, final_gain=None):
    rows, d = x.shape
    tm = min(ROW_TILE, rows)
    gains = jnp.stack([gain, gain if final_gain is None else final_gain])
    return pl.pallas_call(
        functools.partial(_ffn_kernel, final_gain=final_gain is not None),
        grid=(rows // tm,),
        in_specs=[pl.BlockSpec((tm, d), lambda i: (i, 0)),
                  pl.BlockSpec((2, d), lambda i: (0, 0)),
                  pl.BlockSpec(w_gate_up.shape, lambda i: (0, 0)),
                  pl.BlockSpec(w_down.shape, lambda i: (0, 0))],
        out_specs=pl.BlockSpec((tm, d), lambda i: (i, 0)),
        out_shape=jax.ShapeDtypeStruct((rows, d), F32),
        input_output_aliases={0: 0},
        compiler_params=_params("arbitrary"),
        name="ffn",
    )(x, gains, w_gate_up, w_down)


def _rw_proj_kernel(*refs, has_v_res):
    if has_v_res:
        (x_ref, gain_ref, shift0_ref, mu_ref, wr_ref, wk_ref, wv_ref, vec_ref,
         w1_ref, w2_ref, a1_ref, a2_ref, g1_ref, g2_ref, v1_ref, v2_ref, vfirst_ref,
         r_ref, lw_ref, k_ref, v_ref, kk_ref, a_ref, g_ref, shift_ref, prev_ref) = refs
    else:
        (x_ref, gain_ref, shift0_ref, mu_ref, wr_ref, wk_ref, wv_ref, vec_ref,
         w1_ref, w2_ref, a1_ref, a2_ref, g1_ref, g2_ref,
         r_ref, lw_ref, k_ref, v_ref, kk_ref, a_ref, g_ref, shift_ref, prev_ref) = refs
    i = pl.program_id(1)
    h = _rms(x_ref[...], gain_ref[...])
    tm = h.shape[0]

    @pl.when(i == 0)
    def _():
        prev_ref[...] = shift0_ref[0]

    row = lax.broadcasted_iota(jnp.int32, h.shape, 0)
    h_prev = jnp.where(row == 0, prev_ref[...], pltpu.roll(h, 1, 0))
    last = h[tm - 1:tm, :]
    prev_ref[...] = last
    shift_ref[0] = last

    xx = h_prev - h
    x_r, x_w, x_k, x_v, x_a, x_g = ((h + xx * mu_ref[j:j + 1, :]).astype(BF16) for j in range(6))
    w0, a0, k_k, k_a, v0 = (vec_ref[j:j + 1, :] for j in range(5))

    r = _dot(x_r, wr_ref[...])
    k = _dot(x_k, wk_ref[...])
    v = _dot(x_v, wv_ref[...])
    dec = w0 + _dot(jnp.tanh(_dot(x_w, w1_ref[...])).astype(BF16), w2_ref[...])
    w_log = -jax.nn.softplus(-dec) - 0.5
    if has_v_res:
        mix = jax.nn.sigmoid(v0 + _dot(_dot(x_v, v1_ref[...]).astype(BF16), v2_ref[...]))
        v = v + (vfirst_ref[...] - v) * mix
    a = jax.nn.sigmoid(a0 + _dot(_dot(x_a, a1_ref[...]).astype(BF16), a2_ref[...]))
    g = _dot(jax.nn.sigmoid(_dot(x_g, g1_ref[...])).astype(BF16), g2_ref[...])

    r_ref[...] = r
    lw_ref[...] = -jnp.exp(w_log)
    kk_ref[...] = k * k_k
    k_ref[...] = k * (1.0 + (a - 1.0) * k_a)
    v_ref[...] = v
    a_ref[...] = a
    g_ref[...] = g


def _pad_cols(w, mult=LANES):
    pad = -w.shape[1] % mult
    return jnp.pad(w, ((0, 0), (0, pad))) if pad else w


def _pad_rows(w, mult=LANES):
    pad = -w.shape[0] % mult
    return jnp.pad(w, ((0, pad), (0, 0))) if pad else w


def _rw_proj(x, batch, gain, shift0, p, v_first):
    rows, d = x.shape
    seq = rows // batch
    tm = min(ROW_TILE, seq)
    nt = seq // tm
    has_v_res = v_first is not None
    zeros = jnp.zeros((d,), F32)
    vec = jnp.stack([p["w0"], p["a0"], p["k_k"], p["k_a"], p["v0"] if has_v_res else zeros,
                     zeros, zeros, zeros])
    row_spec = pl.BlockSpec((tm, d), lambda b, i: (b * nt + i, 0))

    def full(arr):
        return pl.BlockSpec(arr.shape, lambda b, i: (0,) * arr.ndim)

    ins = [x, gain.reshape(1, d), shift0.reshape(batch, 1, d), p["mu"], p["w_r"], p["w_k"], p["w_v"], vec,
           p["w1"], p["w2"], p["a1"], p["a2"], p["g1"], p["g2"]]
    in_specs = [row_spec, full(ins[1]), pl.BlockSpec((1, 1, d), lambda b, i: (b, 0, 0))]
    in_specs += [full(a) for a in ins[3:]]
    if has_v_res:
        ins += [p["v1"], p["v2"], v_first]
        in_specs += [full(p["v1"]), full(p["v2"]), row_spec]
    tok = jax.ShapeDtypeStruct((rows, d), F32)
    outs = pl.pallas_call(
        functools.partial(_rw_proj_kernel, has_v_res=has_v_res),
        grid=(batch, nt),
        in_specs=in_specs,
        out_specs=[row_spec] * 7 + [pl.BlockSpec((1, 1, d), lambda b, i: (b, 0, 0))],
        out_shape=[tok] * 7 + [jax.ShapeDtypeStruct((batch, 1, d), F32)],
        scratch_shapes=[pltpu.VMEM((1, d), F32)],
        compiler_params=_params("arbitrary", "arbitrary"),
        name="rw_proj",
    )(*ins)
    return outs[:7], outs[7].reshape(batch, d)


WKV_SUM_PASSES = 3
WKV_STATE_PASSES = 3
WKV_SOLVE_PASSES = 1


def _wkv_block(pairs, c_):
    n = pairs[0][0].shape[0]
    inv_hd = 1.0 / RW_HEAD_DIM
    sum_l = lambda x: _mm_exact_lhs(c_["tri_incl"], x, WKV_SUM_PASSES)
    sum_g = lambda x: _mm_exact_rhs(x, c_["gsum"], WKV_SUM_PASSES)
    solve_mm = lambda x, y, nt=False: _mm(x, y, WKV_SOLVE_PASSES, nt=nt)
    np_ = len(pairs)

    cs = [sum_l(p[1]) for p in pairs]
    kk_sq = [sum_g(p[4] * p[4]) for p in pairs]
    pre = []
    for (r, lw, k, v, kk, a, g, vec, state), cs_p, sq in zip(pairs, cs, kk_sq):
        cs_last = cs_p[n - 1:n, :]
        d_inv = jnp.exp(-cs_p)
        d_rest = jnp.exp(cs_last - cs_p)
        kkn = kk * lax.rsqrt(jnp.maximum(sq, 1e-24))
        b = kkn * a
        pre.append(dict(a_t=-kkn * jnp.exp(cs_p - lw), r_t=r * jnp.exp(cs_p), b_t=b * d_inv, k_t=k * d_inv,
                        bk_rest=jnp.concatenate([b * d_rest, k * d_rest], axis=0), decay=jnp.exp(cs_last)))

    from_state = [_mm(jnp.concatenate([q["a_t"], q["r_t"]], axis=0), p[8], WKV_STATE_PASSES, nt=True)
                  for p, q in zip(pairs, pre)]
    chains = [(i, m) for i in range(np_) for m in c_["head_masks"]]
    grams = []
    for i, m in chains:
        q = pre[i]
        zero = jnp.zeros_like(q["a_t"])
        lhs = jnp.concatenate([jnp.where(m, q["a_t"], zero), jnp.where(m, q["r_t"], zero)], axis=0)
        grams.append(solve_mm(lhs, jnp.concatenate([q["b_t"], q["k_t"]], axis=0), nt=True))
    lows = [jnp.where(c_["strict"], gm[:n, :n], 0.0) for gm in grams]
    m_aks = [jnp.where(c_["strict_k"], gm[:n], 0.0) for gm in grams]
    m_rs = [jnp.where(c_["incl2"], gm[n:], 0.0) for gm in grams]
    rhs = [from_state[i][:n] + solve_mm(mk, jnp.concatenate([jnp.zeros_like(pairs[i][3]), pairs[i][3]], axis=0))
           for (i, _), mk in zip(chains, m_aks)]
    sig = [x + solve_mm(lo, x) for lo, x in zip(lows, rhs)]
    pw = lows
    step = 1
    while 2 * step < n:
        pw = [solve_mm(x, x) for x in pw]
        sig = [x + solve_mm(p2, x) for p2, x in zip(pw, sig)]
        step *= 2
    ys = [from_state[i][n:] + solve_mm(mr, jnp.concatenate([s, pairs[i][3]], axis=0))
          for (i, _), mr, s in zip(chains, m_rs, sig)]

    first = c_["head_masks"][0]
    sig_p = [jnp.where(first, sig[2 * i], sig[2 * i + 1]) for i in range(np_)]
    y_p = [jnp.where(first, ys[2 * i], ys[2 * i + 1]) for i in range(np_)]
    upd = [_mm(jnp.concatenate([s, p[3]], axis=0).T, q["bk_rest"], WKV_STATE_PASSES)
           for s, p, q in zip(sig_p, pairs, pre)]
    new_states = [p[8] * q["decay"] + jnp.where(c_["same_head"], u, 0.0) for p, q, u in zip(pairs, pre, upd)]

    mean = [sum_g(y) * inv_hd for y in y_p]
    dev = [y - m for y, m in zip(y_p, mean)]
    var = [sum_g(dv * dv) * inv_hd for dv in dev]
    bonus = [sum_g(p[0] * p[2] * p[7][0:1, :]) * p[3] for p in pairs]
    outs = []
    for p, dv, vr, bo in zip(pairs, dev, var, bonus):
        vec = p[7]
        o = dv * lax.rsqrt(vr + RW_GN_EPS) * vec[1:2, :] + vec[2:3, :]
        outs.append((o + bo) * p[6])
    return list(zip(outs, new_states))


def _wkv_kernel(*refs, has_state0):
    if has_state0:
        (r_ref, lw_ref, k_ref, v_ref, kk_ref, a_ref, g_ref, vec_ref, s0_ref,
         o_ref, sout_ref, state_ref) = refs
    else:
        (r_ref, lw_ref, k_ref, v_ref, kk_ref, a_ref, g_ref, vec_ref,
         o_ref, sout_ref, state_ref) = refs
    c = pl.program_id(2)
    n = r_ref.shape[0]
    hd = RW_HEAD_DIM

    @pl.when(c == 0)
    def _():
        if has_state0:
            state_ref[...] = s0_ref[0]
        else:
            state_ref[...] = jnp.zeros_like(state_ref)

    tr = lax.broadcasted_iota(jnp.int32, (n, n), 0)
    tc = lax.broadcasted_iota(jnp.int32, (n, n), 1)
    tr2 = lax.broadcasted_iota(jnp.int32, (n, 2 * n), 0)
    tc2 = lax.broadcasted_iota(jnp.int32, (n, 2 * n), 1)
    lane = lax.broadcasted_iota(jnp.int32, (n, LANES), 1)
    sr = lax.broadcasted_iota(jnp.int32, (LANES, LANES), 0) // hd
    sc = lax.broadcasted_iota(jnp.int32, (LANES, LANES), 1) // hd
    consts = dict(
        gsum=_group_ones(LANES, hd),
        tri_incl=jnp.where(tc <= tr, 1.0, 0.0).astype(BF16),
        strict=tc < tr,
        strict_k=(tc2 >= n) & (tc2 - n < tr2),
        incl2=jnp.where(tc2 >= n, tc2 - n, tc2) <= tr2,
        eye=jnp.where(tc == tr, 1.0, 0.0).astype(F32),
        head_masks=[lane < hd, lane >= hd],
        same_head=sr == sc,
    )
    pairs = []
    for p in range(state_ref.shape[0]):
        sl = slice(p * LANES, (p + 1) * LANES)
        pairs.append((r_ref[:, sl], lw_ref[:, sl], k_ref[:, sl], v_ref[:, sl], kk_ref[:, sl],
                      a_ref[:, sl], g_ref[:, sl], vec_ref[:, sl], state_ref[p]))
    results = _wkv_block(pairs, consts)
    o_ref[...] = jnp.concatenate([o for o, _ in results], axis=1).astype(o_ref.dtype)
    new_states = jnp.stack([s for _, s in results])
    state_ref[...] = new_states
    sout_ref[0] = new_states


def _wkv(tok, batch, vec, state0):
    rows, d = tok[0].shape
    seq = rows // batch
    n = min(WKV_CHUNK, seq)
    nc = seq // n
    pairs = d // LANES
    gp = WKV_PAIRS
    width = gp * LANES
    has_state0 = state0 is not None
    tok_spec = pl.BlockSpec((n, width), lambda b, p, c: (b * nc + c, p))
    state_spec = pl.BlockSpec((1, gp, LANES, LANES), lambda b, p, c: (b, p, 0, 0))
    ins = list(tok) + [vec]
    in_specs = [tok_spec] * 7 + [pl.BlockSpec((8, width), lambda b, p, c: (0, p))]
    if has_state0:
        ins.append(state0)
        in_specs.append(state_spec)
    return pl.pallas_call(
        functools.partial(_wkv_kernel, has_state0=has_state0),
        grid=(batch, pairs // gp, nc),
        in_specs=in_specs,
        out_specs=[tok_spec, state_spec],
        out_shape=[jax.ShapeDtypeStruct((rows, d), BF16),
                   jax.ShapeDtypeStruct((batch, pairs, LANES, LANES), F32)],
        scratch_shapes=[pltpu.VMEM((gp, LANES, LANES), F32)],
        compiler_params=_params("arbitrary", "arbitrary", "arbitrary"),
        name="wkv",
    )(*ins)


def _state_to_blockdiag(state):
    b, h, n, _ = state.shape
    s = state.reshape(b, h // 2, 2, n, n)
    z = jnp.zeros_like(s[:, :, 0])
    top = jnp.concatenate([s[:, :, 0], z], axis=-1)
    bot = jnp.concatenate([z, s[:, :, 1]], axis=-1)
    return jnp.concatenate([top, bot], axis=-2)


def _state_from_blockdiag(bd):
    b, p, n2, _ = bd.shape
    n = n2 // 2
    return jnp.stack([bd[:, :, :n, :n], bd[:, :, n:, n:]], axis=2).reshape(b, 2 * p, n, n)


def _trunk(x, batch, mem_k, mem_v, past_k, past_v, rw_state0, rw_shift0, w):
    depth = len(w["ffn_gu"])
    d = x.shape[-1]
    sb_k, sb_v, rw_states, rw_shifts = [], [], [], []
    v_first = None
    for layer in range(depth):
        j = layer // 2
        if layer % 2 == 0:
            k, v, qkv = _sb_qkv(x, w["norm_mix"][layer], w["sb_qkv"][j])
            sb_k.append(k)
            sb_v.append(v)
            if past_k is None:
                o = _sb_attention_prompt(qkv, batch)
            else:
                o = _sb_attention_sample(qkv, batch, past_k, past_v, j)
            x = _linear_residual(o, w["sb_o"][j], x)
        else:
            if rw_state0 is None:
                state0 = None
                shift0 = jnp.zeros((batch, d), F32)
            else:
                state0 = _state_to_blockdiag(rw_state0[j])
                shift0 = rw_shift0[j]
            tok, shift = _rw_proj(x, batch, w["norm_mix"][layer], shift0, w["rw"][j], v_first)
            if v_first is None:
                v_first = tok[3]
            o, state = _wkv(tok, batch, w["rw"][j]["out_vec"], state0)
            rw_states.append(_state_from_blockdiag(state))
            rw_shifts.append(shift)
            x = _linear_residual(o, w["rw"][j]["w_o"], x)
        x = _xattn(x, w["norm_xattn"][layer], w["xa_q"][layer], w["xa_o"][layer], mem_k[layer], mem_v[layer])
        final = w["norm_final"] if layer == depth - 1 else None
        x = _ffn(x, w["norm_ffn"][layer], w["ffn_gu"][layer], w["ffn_down"][layer], final)
    return x, jnp.stack(sb_k), jnp.stack(sb_v), jnp.stack(rw_states), jnp.stack(rw_shifts)


def kernel(x_prompt, x_sample, mem_prompt, cache_sb_k, cache_sb_v, state_rwkv_wkv, state_rwkv_shift,
           cache_mem_k, cache_mem_v, norm_mix, norm_xattn, norm_ffn, norm_mem, norm_final,
           sb_w_qkv, sb_w_o, rw_mu, rw_w_r, rw_w_k, rw_w_v, rw_w_o, rw_w0, rw_w1, rw_w2,
           rw_a0, rw_a1, rw_a2, rw_v0, rw_v1, rw_v2, rw_g1, rw_g2, rw_k_k, rw_k_a, rw_r_k,
           rw_lnx_w, rw_lnx_b, xa_w_q, xa_w_kv, xa_w_o, ffn_w_gate_up, ffn_w_down):
    batch, seq, d = x_prompt.shape
    dec_batch, dec_seq, _ = x_sample.shape
    depth = norm_mix.shape[0]
    n_rw = rw_w_r.shape[0]
    n_sb = sb_w_qkv.shape[0]
    n_mem = mem_prompt.shape[1]
    bf = lambda t: t.astype(BF16)

    rw = []
    for j in range(n_rw):
        pj = dict(mu=rw_mu[j], w_r=bf(rw_w_r[j]), w_k=bf(rw_w_k[j]), w_v=bf(rw_w_v[j]), w_o=bf(rw_w_o[j]),
                  w0=rw_w0[j], a0=rw_a0[j], k_k=rw_k_k[j], k_a=rw_k_a[j],
                  w1=bf(_pad_cols(rw_w1[j])), w2=bf(_pad_rows(rw_w2[j])),
                  a1=bf(_pad_cols(rw_a1[j])), a2=bf(_pad_rows(rw_a2[j])),
                  g1=bf(_pad_cols(rw_g1[j])), g2=bf(_pad_rows(rw_g2[j])))
        if j > 0:
            pj.update(v0=rw_v0[j - 1], v1=bf(_pad_cols(rw_v1[j - 1])), v2=bf(_pad_rows(rw_v2[j - 1])))
        zeros = jnp.zeros((d,), F32)
        pj["out_vec"] = jnp.stack([rw_r_k[j].reshape(d), rw_lnx_w[j], rw_lnx_b[j]] + [zeros] * 5)
        rw.append(pj)
    w = dict(norm_mix=norm_mix, norm_xattn=norm_xattn, norm_ffn=norm_ffn, norm_final=norm_final,
             sb_qkv=bf(sb_w_qkv), sb_o=bf(sb_w_o), rw=rw, xa_q=bf(xa_w_q), xa_o=bf(xa_w_o),
             ffn_gu=bf(ffn_w_gate_up), ffn_down=bf(ffn_w_down))

    mem_k, mem_v = _mem_kv(mem_prompt.reshape(batch * n_mem, d), norm_mem, bf(xa_w_kv))
    mem_shape = (depth, batch, n_mem, XA_HEADS, d // XA_HEADS)
    y_p, p_sb_k, p_sb_v, p_wkv, p_shift = _trunk(
        x_prompt.reshape(batch * seq, d), batch,
        bf(mem_k).reshape(depth, batch, n_mem, d), bf(mem_v).reshape(depth, batch, n_mem, d),
        None, None, None, None, w)

    past_len = cache_sb_k.shape[2]
    y_s, s_sb_k, s_sb_v, s_wkv, s_shift = _trunk(
        x_sample.reshape(dec_batch * dec_seq, d), dec_batch,
        bf(cache_mem_k).reshape(depth, dec_batch, n_mem, d), bf(cache_mem_v).reshape(depth, dec_batch, n_mem, d),
        cache_sb_k.reshape(n_sb, dec_batch, past_len, d), cache_sb_v.reshape(n_sb, dec_batch, past_len, d),
        state_rwkv_wkv, state_rwkv_shift, w)

    hd = d // SB_HEADS
    return (y_p.reshape(batch, seq, d), y_s.reshape(dec_batch, dec_seq, d),
            p_sb_k.reshape(n_sb, batch, seq, SB_HEADS, hd), p_sb_v.reshape(n_sb, batch, seq, SB_HEADS, hd),
            p_wkv, p_shift,
            mem_k.reshape(mem_shape), mem_v.reshape(mem_shape),
            s_sb_k.reshape(n_sb, dec_batch, dec_seq, SB_HEADS, hd),
            s_sb_v.reshape(n_sb, dec_batch, dec_seq, SB_HEADS, hd),
            s_wkv, s_shift)
```

```python
import functools

import jax
import jax.numpy as jnp
from jax import lax
from jax.experimental import pallas as pl
from jax.experimental.pallas import tpu as pltpu

F32 = jnp.float32
BF16 = jnp.bfloat16

NORM_EPS = 1e-6
RW_GN_EPS = 64e-5
SB_HEADS = 16
XA_HEADS = 4
RW_HEAD_DIM = 64
LANES = 128
ROW_TILE = 256
SB_BLOCK = 256
SB_PAIRS = 4
SB_PAST_SPAN = 1024
WKV_CHUNK = 64
WKV_PAIRS = 8
FFN_CHUNK = 256
VMEM_LIMIT_BYTES = 56 * 1024 * 1024


def _params(*sem):
    return pltpu.CompilerParams(dimension_semantics=sem, vmem_limit_bytes=VMEM_LIMIT_BYTES)


def _dot(a, b):
    return jnp.dot(a, b, preferred_element_type=F32)


def _dot_nt(a, b):
    return lax.dot_general(a, b, (((1,), (1,)), ((), ())), preferred_element_type=F32)


def _split(x, n):
    parts = []
    for i in range(n):
        p = x.astype(BF16)
        parts.append(p)
        if i + 1 < n:
            x = x - p.astype(F32)
    return parts


def _mm(a, b, passes, nt=False):
    dot = _dot_nt if nt else _dot
    if passes == 1:
        return dot(a.astype(BF16), b.astype(BF16))
    a_hi, a_lo = _split(a, 2)
    b_hi, b_lo = _split(b, 2)
    return dot(a_hi, b_hi) + (dot(a_hi, b_lo) + dot(a_lo, b_hi))


def _mm_exact_rhs(a, b_bf16, n):
    out = None
    for p in _split(a, n):
        t = _dot(p, b_bf16)
        out = t if out is None else out + t
    return out


def _mm_exact_lhs(a_bf16, b, n):
    out = None
    for p in _split(b, n):
        t = _dot(a_bf16, p)
        out = t if out is None else out + t
    return out


def _rms(x, gain):
    ms = jnp.mean(x * x, axis=-1, keepdims=True)
    return x * lax.rsqrt(ms + NORM_EPS) * gain


def _group_ones(n, group):
    r = lax.broadcasted_iota(jnp.int32, (n, n), 0) // group
    c = lax.broadcasted_iota(jnp.int32, (n, n), 1) // group
    return jnp.where(r == c, 1.0, 0.0).astype(BF16)


def _mem_kv_kernel(x_ref, g_ref, w_ref, k_ref, v_ref):
    d = x_ref.shape[-1]
    h = _rms(x_ref[...], g_ref[0]).astype(BF16)
    kv = _dot(h, w_ref[0])
    k_ref[0] = kv[:, :d]
    v_ref[0] = kv[:, d:]


def _mem_kv(mem, gains, w_kv):
    depth = w_kv.shape[0]
    rows, d = mem.shape
    tm = min(ROW_TILE, rows)
    out = jax.ShapeDtypeStruct((depth, rows, d), F32)
    return pl.pallas_call(
        _mem_kv_kernel,
        grid=(depth, rows // tm),
        in_specs=[pl.BlockSpec((tm, d), lambda l, i: (i, 0)),
                  pl.BlockSpec((1, 1, d), lambda l, i: (l, 0, 0)),
                  pl.BlockSpec((1, d, 2 * d), lambda l, i: (l, 0, 0))],
        out_specs=[pl.BlockSpec((1, tm, d), lambda l, i: (l, i, 0)),
                   pl.BlockSpec((1, tm, d), lambda l, i: (l, i, 0))],
        out_shape=[out, out],
        compiler_params=_params("arbitrary", "arbitrary"),
        name="mem_kv",
    )(mem, gains.reshape(depth, 1, d), w_kv)


def _sb_qkv_kernel(x_ref, g_ref, w_ref, *rest, q_scale, slot):
    k_ref, v_ref, qkv_ref = rest[-3:]
    d = x_ref.shape[-1]
    h = _rms(x_ref[...], g_ref[...]).astype(BF16)
    qkv = _dot(h, w_ref[...])
    own = slot if k_ref.shape[0] > 1 else 0
    for s in range(k_ref.shape[0]):
        if s == own:
            k_ref[s] = qkv[:, d:2 * d]
            v_ref[s] = qkv[:, 2 * d:]
        else:
            k_ref[s] = jnp.zeros(k_ref.shape[1:], k_ref.dtype)
            v_ref[s] = jnp.zeros(v_ref.shape[1:], v_ref.dtype)
    qkv_ref[:, :d] = (qkv[:, :d] * q_scale).astype(BF16)
    qkv_ref[:, d:] = qkv[:, d:].astype(BF16)


def _sb_qkv(x, gain, w, slot, n_slots, k_all=None, v_all=None):
    rows, d = x.shape
    tm = min(ROW_TILE, rows)
    q_scale = float((d // SB_HEADS) ** -0.5)
    ins = [x, gain.reshape(1, d), w]
    in_specs = [pl.BlockSpec((tm, d), lambda i: (i, 0)),
                pl.BlockSpec((1, d), lambda i: (0, 0)),
                pl.BlockSpec((d, 3 * d), lambda i: (0, 0))]
    if k_all is None:
        aliases = {}
        slot_spec = pl.BlockSpec((n_slots, tm, d), lambda i: (0, i, 0))
    else:
        ins += [k_all, v_all]
        in_specs += [pl.BlockSpec(memory_space=pl.ANY)] * 2
        aliases = {3: 0, 4: 1}
        slot_spec = pl.BlockSpec((1, tm, d), lambda i: (slot, i, 0))
    return pl.pallas_call(
        functools.partial(_sb_qkv_kernel, q_scale=q_scale, slot=slot),
        grid=(rows // tm,),
        in_specs=in_specs,
        out_specs=[slot_spec, slot_spec, pl.BlockSpec((tm, 3 * d), lambda i: (i, 0))],
        out_shape=[jax.ShapeDtypeStruct((n_slots, rows, d), F32),
                   jax.ShapeDtypeStruct((n_slots, rows, d), F32),
                   jax.ShapeDtypeStruct((rows, 3 * d), BF16)],
        input_output_aliases=aliases,
        compiler_params=_params("arbitrary"),
        name="sb_qkv",
    )(*ins)


def _sb_step(q2s, kbs, vbs, carry, acc, neg_upper, mask):
    m = q2s[0].shape[0]
    z = jnp.concatenate([_dot_nt(q2, kb) for q2, kb in zip(q2s, kbs)], axis=0)
    sp = jnp.maximum(z, 0.0) + jnp.log(1.0 + jnp.exp(-jnp.abs(z)))
    if mask is not None:
        sp = jnp.where(mask, sp, 0.0)
    hi, lo = _split(sp, 2)
    after = _dot(hi, neg_upper) + _dot(lo, neg_upper)
    later = jnp.concatenate([carry] * (z.shape[1] // LANES), axis=1)
    w = jnp.exp((z - sp) + after + later)
    if mask is not None:
        w = jnp.where(mask, w, 0.0)
    w = w.astype(BF16)
    total = jnp.broadcast_to(after[:, 0:1] - sp[:, 0:1], carry.shape)
    pv = jnp.concatenate([_dot(w[g * m:(g + 1) * m], vb) for g, vb in enumerate(vbs)], axis=0)
    return carry + total, acc + pv


def _sb_consts(bk):
    r = lax.broadcasted_iota(jnp.int32, (bk, bk), 0)
    c = lax.broadcasted_iota(jnp.int32, (bk, bk), 1)
    return jnp.where(r > c, -1.0, 0.0).astype(BF16)


def _sb_stack_heads(q_ref, q2_ref):
    tq = q_ref.shape[0]
    lane = lax.broadcasted_iota(jnp.int32, (tq, LANES), 1)
    for g in range(q2_ref.shape[0]):
        q = q_ref[:, g * LANES:(g + 1) * LANES]
        zero = jnp.zeros_like(q)
        q2_ref[g] = jnp.concatenate([jnp.where(lane < LANES // 2, q, zero),
                                     jnp.where(lane >= LANES // 2, q, zero)], axis=0)


def _sb_causal_mask(groups, tq, bk):
    t = lax.broadcasted_iota(jnp.int32, (2 * groups, tq, bk), 1).reshape(2 * groups * tq, bk)
    s = lax.broadcasted_iota(jnp.int32, (2 * groups * tq, bk), 1)
    return s < t


def _sb_write(o_ref, acc_ref):
    tq = o_ref.shape[0]
    lane = lax.broadcasted_iota(jnp.int32, (tq, LANES), 1)
    for g in range(acc_ref.shape[0] // (2 * tq)):
        o_ref[:, g * LANES:(g + 1) * LANES] = jnp.where(
            lane < LANES // 2, acc_ref[2 * g * tq:(2 * g + 1) * tq, :],
            acc_ref[(2 * g + 1) * tq:(2 * g + 2) * tq, :]).astype(o_ref.dtype)


def _sb_prompt_kernel(q_ref, k_ref, v_ref, o_ref, q2_ref, carry_ref, acc_ref):
    blk = q_ref.shape[0]
    groups = q2_ref.shape[0]
    i = pl.program_id(2)
    neg_upper = _sb_consts(blk)
    _sb_stack_heads(q_ref, q2_ref)
    lanes = [slice(g * LANES, (g + 1) * LANES) for g in range(groups)]

    def step(row0, carry, acc, mask):
        return _sb_step([q2_ref[g] for g in range(groups)],
                        [k_ref[pl.ds(row0, blk), sl] for sl in lanes],
                        [v_ref[pl.ds(row0, blk), sl] for sl in lanes], carry, acc, neg_upper, mask)

    zero = jnp.zeros(carry_ref.shape, F32)
    carry_ref[...], acc_ref[...] = step(pl.multiple_of(i * blk, blk), zero, zero,
                                        _sb_causal_mask(groups, blk, blk))

    def body(it, c):
        carry_ref[...], acc_ref[...] = step(pl.multiple_of((i - 1 - it) * blk, blk),
                                            carry_ref[...], acc_ref[...], None)
        return c

    lax.fori_loop(0, i, body, 0)
    _sb_write(o_ref, acc_ref)


def _sb_attention_prompt(qkv, batch):
    rows, d3 = qkv.shape
    d = d3 // 3
    seq = rows // batch
    blk = min(SB_BLOCK, seq)
    nq = seq // blk
    width = SB_PAIRS * LANES
    ng = d // width
    return pl.pallas_call(
        _sb_prompt_kernel,
        grid=(batch, ng, nq),
        in_specs=[pl.BlockSpec((blk, width), lambda b, p, i: (b * nq + i, p)),
                  pl.BlockSpec((seq, width), lambda b, p, i: (b, ng + p)),
                  pl.BlockSpec((seq, width), lambda b, p, i: (b, 2 * ng + p))],
        out_specs=pl.BlockSpec((blk, width), lambda b, p, i: (b * nq + i, p)),
        out_shape=jax.ShapeDtypeStruct((rows, d), BF16),
        scratch_shapes=[pltpu.VMEM((SB_PAIRS, 2 * blk, LANES), BF16),
                        pltpu.VMEM((SB_PAIRS * 2 * blk, LANES), F32),
                        pltpu.VMEM((SB_PAIRS * 2 * blk, LANES), F32)],
        compiler_params=_params("arbitrary", "arbitrary", "arbitrary"),
        name="sb_attention_prompt",
    )(qkv, qkv, qkv)


def _sb_sample_kernel(q_ref, kn_ref, vn_ref, kp_ref, vp_ref, o_ref, q2_ref, carry_ref, acc_ref, *, blk):
    tq = q_ref.shape[0]
    groups = q2_ref.shape[0]
    c = pl.program_id(1)
    neg_upper = _sb_consts(blk)

    lanes = [slice(g * LANES, (g + 1) * LANES) for g in range(groups)]

    @pl.when(c == 0)
    def _():
        _sb_stack_heads(q_ref, q2_ref)
        zero = jnp.zeros(carry_ref.shape, F32)
        carry_ref[...], acc_ref[...] = _sb_step(
            [q2_ref[g] for g in range(groups)], [kn_ref[0, :, sl] for sl in lanes],
            [vn_ref[0, :, sl] for sl in lanes], zero, zero, neg_upper, _sb_causal_mask(groups, tq, blk))

    carry, acc = carry_ref[...], acc_ref[...]
    for sb in reversed(range(kp_ref.shape[2] // blk)):
        rows = slice(sb * blk, (sb + 1) * blk)
        carry, acc = _sb_step([q2_ref[g] for g in range(groups)],
                              [kp_ref[0, 0, rows, sl].astype(BF16) for sl in lanes],
                              [vp_ref[0, 0, rows, sl].astype(BF16) for sl in lanes],
                              carry, acc, neg_upper, None)
    carry_ref[...], acc_ref[...] = carry, acc

    @pl.when(c == pl.num_programs(1) - 1)
    def _():
        _sb_write(o_ref, acc_ref)


def _sb_attention_sample(qkv, batch, past_k, past_v, layer):
    rows, d3 = qkv.shape
    d = d3 // 3
    seq = rows // batch
    blk = SB_BLOCK
    pairs = d // LANES
    past_len = past_k.shape[2]
    span = min(SB_PAST_SPAN, past_len)
    steps = past_len // span
    assert seq <= blk and past_len % span == 0 and span % blk == 0
    new = qkv[:, d:].reshape(batch, seq, 2 * d)
    new = jnp.pad(new, ((0, 0), (0, blk - seq), (0, 0)))
    return pl.pallas_call(
        functools.partial(_sb_sample_kernel, blk=blk),
        grid=(batch, steps),
        in_specs=[pl.BlockSpec((seq, d), lambda b, c: (b, 0)),
                  pl.BlockSpec((1, blk, d), lambda b, c: (b, 0, 0)),
                  pl.BlockSpec((1, blk, d), lambda b, c: (b, 0, 1)),
                  pl.BlockSpec((1, 1, span, d), lambda b, c: (layer, b, steps - 1 - c, 0)),
                  pl.BlockSpec((1, 1, span, d), lambda b, c: (layer, b, steps - 1 - c, 0))],
        out_specs=pl.BlockSpec((seq, d), lambda b, c: (b, 0)),
        out_shape=jax.ShapeDtypeStruct((rows, d), BF16),
        scratch_shapes=[pltpu.VMEM((pairs, 2 * seq, LANES), BF16),
                        pltpu.VMEM((pairs * 2 * seq, LANES), F32),
                        pltpu.VMEM((pairs * 2 * seq, LANES), F32)],
        compiler_params=_params("arbitrary", "arbitrary"),
        name="sb_attention_sample",
    )(qkv, new, new, past_k, past_v)


def _linear_residual_kernel(a_ref, w_ref, x_ref, o_ref):
    o_ref[...] = x_ref[...] + _dot(a_ref[...], w_ref[...])


def _linear_residual(a, w, x):
    rows, d = x.shape
    tm = min(ROW_TILE, rows)
    return pl.pallas_call(
        _linear_residual_kernel,
        grid=(rows // tm,),
        in_specs=[pl.BlockSpec((tm, a.shape[1]), lambda i: (i, 0)),
                  pl.BlockSpec(w.shape, lambda i: (0, 0)),
                  pl.BlockSpec((tm, d), lambda i: (i, 0))],
        out_specs=pl.BlockSpec((tm, d), lambda i: (i, 0)),
        out_shape=jax.ShapeDtypeStruct((rows, d), F32),
        input_output_aliases={2: 0},
        compiler_params=_params("arbitrary"),
        name="linear_residual",
    )(a, w, x)


def _xattn_kernel(x_ref, g_ref, wq_ref, wo_ref, mk_ref, mv_ref, o_ref, *, heads):
    x = x_ref[...]
    d = x.shape[-1]
    dh = d // heads
    h = _rms(x, g_ref[...]).astype(BF16)
    q = _dot(h, wq_ref[...]).astype(BF16)
    outs = []
    for hd in range(heads):
        sl = slice(hd * dh, (hd + 1) * dh)
        s = _dot_nt(q[:, sl], mk_ref[0, :, sl]) * (dh ** -0.5)
        e = jnp.exp(s - jnp.max(s, axis=-1, keepdims=True))
        p = e / jnp.sum(e, axis=-1, keepdims=True)
        outs.append(_dot(p.astype(BF16), mv_ref[0, :, sl]).astype(BF16))
    o = jnp.concatenate(outs, axis=-1)
    o_ref[...] = x + _dot(o, wo_ref[...])


def _xattn(x, gain, wq, wo, mem_k, mem_v):
    rows, d = x.shape
    batch, n_mem, _ = mem_k.shape
    seq = rows // batch
    tm = min(ROW_TILE, seq)
    nt = seq // tm
    return pl.pallas_call(
        functools.partial(_xattn_kernel, heads=XA_HEADS),
        grid=(batch, nt),
        in_specs=[pl.BlockSpec((tm, d), lambda b, i: (b * nt + i, 0)),
                  pl.BlockSpec((1, d), lambda b, i: (0, 0)),
                  pl.BlockSpec((d, d), lambda b, i: (0, 0)),
                  pl.BlockSpec((d, d), lambda b, i: (0, 0)),
                  pl.BlockSpec((1, n_mem, d), lambda b, i: (b, 0, 0)),
                  pl.BlockSpec((1, n_mem, d), lambda b, i: (b, 0, 0))],
        out_specs=pl.BlockSpec((tm, d), lambda b, i: (b * nt + i, 0)),
        out_shape=jax.ShapeDtypeStruct((rows, d), F32),
        input_output_aliases={0: 0},
        compiler_params=_params("arbitrary", "arbitrary"),
        name="xattn",
    )(x, gain.reshape(1, d), wq, wo, mem_k, mem_v)


def _ffn_kernel(x_ref, g_ref, wgu_ref, wd_ref, o_ref, *, final_gain):
    x = x_ref[...]
    hidden = wd_ref.shape[0]
    h = _rms(x, g_ref[0:1, :]).astype(BF16)
    acc = x
    for c0 in range(0, hidden, FFN_CHUNK):
        c1 = min(c0 + FFN_CHUNK, hidden)
        gate = _dot(h, wgu_ref[:, c0:c1])
        up = _dot(h, wgu_ref[:, hidden + c0:hidden + c1])
        act = (gate * jax.nn.sigmoid(gate) * up).astype(BF16)
        acc = acc + _dot(act, wd_ref[c0:c1, :])
    if final_gain:
        acc = _rms(acc, g_ref[1:2, :])
    o_ref[...] = acc


def _ffn(x, gain, w_gate_up, w_down, final_gain=None):
    rows, d = x.shape
    tm = min(ROW_TILE, rows)
    gains = jnp.stack([gain, gain if final_gain is None else final_gain])
    return pl.pallas_call(
        functools.partial(_ffn_kernel, final_gain=final_gain is not None),
        grid=(rows // tm,),
        in_specs=[pl.BlockSpec((tm, d), lambda i: (i, 0)),
                  pl.BlockSpec((2, d), lambda i: (0, 0)),
                  pl.BlockSpec(w_gate_up.shape, lambda i: (0, 0)),
                  pl.BlockSpec(w_down.shape, lambda i: (0, 0))],
        out_specs=pl.BlockSpec((tm, d), lambda i: (i, 0)),
        out_shape=jax.ShapeDtypeStruct((rows, d), F32),
        input_output_aliases={0: 0},
        compiler_params=_params("arbitrary"),
        name="ffn",
    )(x, gains, w_gate_up, w_down)


def _rw_proj_kernel(*refs, has_v_res):
    if has_v_res:
        (x_ref, gain_ref, shift0_ref, mu_ref, wr_ref, wk_ref, wv_ref, vec_ref,
         w1_ref, w2_ref, a1_ref, a2_ref, g1_ref, g2_ref, v1_ref, v2_ref, vfirst_ref,
         r_ref, lw_ref, k_ref, v_ref, kk_ref, a_ref, g_ref, shift_ref, prev_ref) = refs
    else:
        (x_ref, gain_ref, shift0_ref, mu_ref, wr_ref, wk_ref, wv_ref, vec_ref,
         w1_ref, w2_ref, a1_ref, a2_ref, g1_ref, g2_ref,
         r_ref, lw_ref, k_ref, v_ref, kk_ref, a_ref, g_ref, shift_ref, prev_ref) = refs
    i = pl.program_id(1)
    h = _rms(x_ref[...], gain_ref[...])
    tm = h.shape[0]

    @pl.when(i == 0)
    def _():
        prev_ref[...] = shift0_ref[0]

    row = lax.broadcasted_iota(jnp.int32, h.shape, 0)
    h_prev = jnp.where(row == 0, prev_ref[...], pltpu.roll(h, 1, 0))
    last = h[tm - 1:tm, :]
    prev_ref[...] = last
    shift_ref[0] = last

    xx = h_prev - h
    x_r, x_w, x_k, x_v, x_a, x_g = ((h + xx * mu_ref[j:j + 1, :]).astype(BF16) for j in range(6))
    w0, a0, k_k, k_a, v0 = (vec_ref[j:j + 1, :] for j in range(5))

    r = _dot(x_r, wr_ref[...])
    k = _dot(x_k, wk_ref[...])
    v = _dot(x_v, wv_ref[...])
    dec = w0 + _dot(jnp.tanh(_dot(x_w, w1_ref[...])).astype(BF16), w2_ref[...])
    w_log = -jax.nn.softplus(-dec) - 0.5
    if has_v_res:
        mix = jax.nn.sigmoid(v0 + _dot(_dot(x_v, v1_ref[...]).astype(BF16), v2_ref[...]))
        v = v + (vfirst_ref[...] - v) * mix
    a = jax.nn.sigmoid(a0 + _dot(_dot(x_a, a1_ref[...]).astype(BF16), a2_ref[...]))
    g = _dot(jax.nn.sigmoid(_dot(x_g, g1_ref[...])).astype(BF16), g2_ref[...])

    r_ref[...] = r
    lw_ref[...] = -jnp.exp(w_log)
    kk_ref[...] = k * k_k
    k_ref[...] = k * (1.0 + (a - 1.0) * k_a)
    v_ref[...] = v
    a_ref[...] = a
    g_ref[...] = g


def _pad_cols(w, mult=LANES):
    pad = -w.shape[1] % mult
    return jnp.pad(w, ((0, 0), (0, pad))) if pad else w


def _pad_rows(w, mult=LANES):
    pad = -w.shape[0] % mult
    return jnp.pad(w, ((0, pad), (0, 0))) if pad else w


def _rw_proj(x, batch, gain, shift0, p, v_first):
    rows, d = x.shape
    seq = rows // batch
    tm = min(ROW_TILE, seq)
    nt = seq // tm
    has_v_res = v_first is not None
    zeros = jnp.zeros((d,), F32)
    vec = jnp.stack([p["w0"], p["a0"], p["k_k"], p["k_a"], p["v0"] if has_v_res else zeros,
                     zeros, zeros, zeros])
    row_spec = pl.BlockSpec((tm, d), lambda b, i: (b * nt + i, 0))

    def full(arr):
        return pl.BlockSpec(arr.shape, lambda b, i: (0,) * arr.ndim)

    ins = [x, gain.reshape(1, d), shift0.reshape(batch, 1, d), p["mu"], p["w_r"], p["w_k"], p["w_v"], vec,
           p["w1"], p["w2"], p["a1"], p["a2"], p["g1"], p["g2"]]
    in_specs = [row_spec, full(ins[1]), pl.BlockSpec((1, 1, d), lambda b, i: (b, 0, 0))]
    in_specs += [full(a) for a in ins[3:]]
    if has_v_res:
        ins += [p["v1"], p["v2"], v_first]
        in_specs += [full(p["v1"]), full(p["v2"]), row_spec]
    tok = jax.ShapeDtypeStruct((rows, d), F32)
    outs = pl.pallas_call(
        functools.partial(_rw_proj_kernel, has_v_res=has_v_res),
        grid=(batch, nt),
        in_specs=in_specs,
        out_specs=[row_spec] * 7 + [pl.BlockSpec((1, 1, d), lambda b, i: (b, 0, 0))],
        out_shape=[tok] * 7 + [jax.ShapeDtypeStruct((batch, 1, d), F32)],
        scratch_shapes=[pltpu.VMEM((1, d), F32)],
        compiler_params=_params("arbitrary", "arbitrary"),
        name="rw_proj",
    )(*ins)
    return outs[:7], outs[7].reshape(batch, d)


WKV_SUM_PASSES = 3
WKV_STATE_PASSES = 3
WKV_SOLVE_PASSES = 1


def _wkv_block(pairs, c_):
    n = pairs[0][0].shape[0]
    inv_hd = 1.0 / RW_HEAD_DIM
    sum_l = lambda x: _mm_exact_lhs(c_["tri_incl"], x, WKV_SUM_PASSES)
    sum_g = lambda x: _mm_exact_rhs(x, c_["gsum"], WKV_SUM_PASSES)
    solve_mm = lambda x, y, nt=False: _mm(x, y, WKV_SOLVE_PASSES, nt=nt)
    np_ = len(pairs)

    cs = [sum_l(p[1]) for p in pairs]
    kk_sq = [sum_g(p[4] * p[4]) for p in pairs]
    pre = []
    for (r, lw, k, v, kk, a, g, vec, state), cs_p, sq in zip(pairs, cs, kk_sq):
        cs_last = cs_p[n - 1:n, :]
        d_inv = jnp.exp(-cs_p)
        d_rest = jnp.exp(cs_last - cs_p)
        kkn = kk * lax.rsqrt(jnp.maximum(sq, 1e-24))
        b = kkn * a
        pre.append(dict(a_t=-kkn * jnp.exp(cs_p - lw), r_t=r * jnp.exp(cs_p), b_t=b * d_inv, k_t=k * d_inv,
                        bk_rest=jnp.concatenate([b * d_rest, k * d_rest], axis=0), decay=jnp.exp(cs_last)))

    from_state = [_mm(jnp.concatenate([q["a_t"], q["r_t"]], axis=0), p[8], WKV_STATE_PASSES, nt=True)
                  for p, q in zip(pairs, pre)]
    chains = [(i, m) for i in range(np_) for m in c_["head_masks"]]
    grams = []
    for i, m in chains:
        q = pre[i]
        zero = jnp.zeros_like(q["a_t"])
        lhs = jnp.concatenate([jnp.where(m, q["a_t"], zero), jnp.where(m, q["r_t"], zero)], axis=0)
        grams.append(solve_mm(lhs, jnp.concatenate([q["b_t"], q["k_t"]], axis=0), nt=True))
    lows = [jnp.where(c_["strict"], gm[:n, :n], 0.0) for gm in grams]
    m_aks = [jnp.where(c_["strict_k"], gm[:n], 0.0) for gm in grams]
    m_rs = [jnp.where(c_["incl2"], gm[n:], 0.0) for gm in grams]
    rhs = [from_state[i][:n] + solve_mm(mk, jnp.concatenate([jnp.zeros_like(pairs[i][3]), pairs[i][3]], axis=0))
           for (i, _), mk in zip(chains, m_aks)]
    sig = [x + solve_mm(lo, x) for lo, x in zip(lows, rhs)]
    pw = lows
    step = 1
    while 2 * step < n:
        pw = [solve_mm(x, x) for x in pw]
        sig = [x + solve_mm(p2, x) for p2, x in zip(pw, sig)]
        step *= 2
    ys = [from_state[i][n:] + solve_mm(mr, jnp.concatenate([s, pairs[i][3]], axis=0))
          for (i, _), mr, s in zip(chains, m_rs, sig)]

    first = c_["head_masks"][0]
    sig_p = [jnp.where(first, sig[2 * i], sig[2 * i + 1]) for i in range(np_)]
    y_p = [jnp.where(first, ys[2 * i], ys[2 * i + 1]) for i in range(np_)]
    upd = [_mm(jnp.concatenate([s, p[3]], axis=0).T, q["bk_rest"], WKV_STATE_PASSES)
           for s, p, q in zip(sig_p, pairs, pre)]
    new_states = [p[8] * q["decay"] + jnp.where(c_["same_head"], u, 0.0) for p, q, u in zip(pairs, pre, upd)]

    mean = [sum_g(y) * inv_hd for y in y_p]
    dev = [y - m for y, m in zip(y_p, mean)]
    var = [sum_g(dv * dv) * inv_hd for dv in dev]
    bonus = [sum_g(p[0] * p[2] * p[7][0:1, :]) * p[3] for p in pairs]
    outs = []
    for p, dv, vr, bo in zip(pairs, dev, var, bonus):
        vec = p[7]
        o = dv * lax.rsqrt(vr + RW_GN_EPS) * vec[1:2, :] + vec[2:3, :]
        outs.append((o + bo) * p[6])
    return list(zip(outs, new_states))


def _wkv_kernel(*refs, has_state0):
    if has_state0:
        (r_ref, lw_ref, k_ref, v_ref, kk_ref, a_ref, g_ref, vec_ref, s0_ref,
         o_ref, sout_ref, state_ref) = refs
    else:
        (r_ref, lw_ref, k_ref, v_ref, kk_ref, a_ref, g_ref, vec_ref,
         o_ref, sout_ref, state_ref) = refs
    c = pl.program_id(2)
    n = r_ref.shape[0]
    hd = RW_HEAD_DIM

    @pl.when(c == 0)
    def _():
        if has_state0:
            state_ref[...] = s0_ref[0]
        else:
            state_ref[...] = jnp.zeros_like(state_ref)

    tr = lax.broadcasted_iota(jnp.int32, (n, n), 0)
    tc = lax.broadcasted_iota(jnp.int32, (n, n), 1)
    tr2 = lax.broadcasted_iota(jnp.int32, (n, 2 * n), 0)
    tc2 = lax.broadcasted_iota(jnp.int32, (n, 2 * n), 1)
    lane = lax.broadcasted_iota(jnp.int32, (n, LANES), 1)
    sr = lax.broadcasted_iota(jnp.int32, (LANES, LANES), 0) // hd
    sc = lax.broadcasted_iota(jnp.int32, (LANES, LANES), 1) // hd
    consts = dict(
        gsum=_group_ones(LANES, hd),
        tri_incl=jnp.where(tc <= tr, 1.0, 0.0).astype(BF16),
        strict=tc < tr,
        strict_k=(tc2 >= n) & (tc2 - n < tr2),
        incl2=jnp.where(tc2 >= n, tc2 - n, tc2) <= tr2,
        eye=jnp.where(tc == tr, 1.0, 0.0).astype(F32),
        head_masks=[lane < hd, lane >= hd],
        same_head=sr == sc,
    )
    pairs = []
    for p in range(state_ref.shape[0]):
        sl = slice(p * LANES, (p + 1) * LANES)
        pairs.append((r_ref[:, sl], lw_ref[:, sl], k_ref[:, sl], v_ref[:, sl], kk_ref[:, sl],
                      a_ref[:, sl], g_ref[:, sl], vec_ref[:, sl], state_ref[p]))
    results = _wkv_block(pairs, consts)
    o_ref[...] = jnp.concatenate([o for o, _ in results], axis=1).astype(o_ref.dtype)
    new_states = jnp.stack([s for _, s in results])
    state_ref[...] = new_states
    sout_ref[0] = new_states


def _wkv(tok, batch, vec, state0):
    rows, d = tok[0].shape
    seq = rows // batch
    n = min(WKV_CHUNK, seq)
    nc = seq // n
    pairs = d // LANES
    gp = WKV_PAIRS
    width = gp * LANES
    has_state0 = state0 is not None
    tok_spec = pl.BlockSpec((n, width), lambda b, p, c: (b * nc + c, p))
    state_spec = pl.BlockSpec((1, gp, LANES, LANES), lambda b, p, c: (b, p, 0, 0))
    ins = list(tok) + [vec]
    in_specs = [tok_spec] * 7 + [pl.BlockSpec((8, width), lambda b, p, c: (0, p))]
    if has_state0:
        ins.append(state0)
        in_specs.append(state_spec)
    return pl.pallas_call(
        functools.partial(_wkv_kernel, has_state0=has_state0),
        grid=(batch, pairs // gp, nc),
        in_specs=in_specs,
        out_specs=[tok_spec, state_spec],
        out_shape=[jax.ShapeDtypeStruct((rows, d), BF16),
                   jax.ShapeDtypeStruct((batch, pairs, LANES, LANES), F32)],
        scratch_shapes=[pltpu.VMEM((gp, LANES, LANES), F32)],
        compiler_params=_params("arbitrary", "arbitrary", "arbitrary"),
        name="wkv",
    )(*ins)


def _state_to_blockdiag(state):
    b, h, n, _ = state.shape
    s = state.reshape(b, h // 2, 2, n, n)
    z = jnp.zeros_like(s[:, :, 0])
    top = jnp.concatenate([s[:, :, 0], z], axis=-1)
    bot = jnp.concatenate([z, s[:, :, 1]], axis=-1)
    return jnp.concatenate([top, bot], axis=-2)


def _state_from_blockdiag(bd):
    b, p, n2, _ = bd.shape
    n = n2 // 2
    return jnp.stack([bd[:, :, :n, :n], bd[:, :, n:, n:]], axis=2).reshape(b, 2 * p, n, n)


def _trunk(x, batch, mem_k, mem_v, past_k, past_v, rw_state0, rw_shift0, w):
    depth = len(w["ffn_gu"])
    d = x.shape[-1]
    sb_k, sb_v, rw_states, rw_shifts = None, None, [], []
    n_sb = (depth + 1) // 2
    v_first = None
    for layer in range(depth):
        j = layer // 2
        if layer % 2 == 0:
            sb_k, sb_v, qkv = _sb_qkv(x, w["norm_mix"][layer], w["sb_qkv"][j], j, n_sb, sb_k, sb_v)
            if past_k is None:
                o = _sb_attention_prompt(qkv, batch)
            else:
                o = _sb_attention_sample(qkv, batch, past_k, past_v, j)
            x = _linear_residual(o, w["sb_o"][j], x)
        else:
            if rw_state0 is None:
                state0 = None
                shift0 = jnp.zeros((batch, d), F32)
            else:
                state0 = _state_to_blockdiag(rw_state0[j])
                shift0 = rw_shift0[j]
            tok, shift = _rw_proj(x, batch, w["norm_mix"][layer], shift0, w["rw"][j], v_first)
            if v_first is None:
                v_first = tok[3]
            o, state = _wkv(tok, batch, w["rw"][j]["out_vec"], state0)
            rw_states.append(_state_from_blockdiag(state))
            rw_shifts.append(shift)
            x = _linear_residual(o, w["rw"][j]["w_o"], x)
        x = _xattn(x, w["norm_xattn"][layer], w["xa_q"][layer], w["xa_o"][layer], mem_k[layer], mem_v[layer])
        final = w["norm_final"] if layer == depth - 1 else None
        x = _ffn(x, w["norm_ffn"][layer], w["ffn_gu"][layer], w["ffn_down"][layer], final)
    return x, sb_k, sb_v, jnp.stack(rw_states), jnp.stack(rw_shifts)


def kernel(x_prompt, x_sample, mem_prompt, cache_sb_k, cache_sb_v, state_rwkv_wkv, state_rwkv_shift,
           cache_mem_k, cache_mem_v, norm_mix, norm_xattn, norm_ffn, norm_mem, norm_final,
           sb_w_qkv, sb_w_o, rw_mu, rw_w_r, rw_w_k, rw_w_v, rw_w_o, rw_w0, rw_w1, rw_w2,
           rw_a0, rw_a1, rw_a2, rw_v0, rw_v1, rw_v2, rw_g1, rw_g2, rw_k_k, rw_k_a, rw_r_k,
           rw_lnx_w, rw_lnx_b, xa_w_q, xa_w_kv, xa_w_o, ffn_w_gate_up, ffn_w_down):
    batch, seq, d = x_prompt.shape
    dec_batch, dec_seq, _ = x_sample.shape
    depth = norm_mix.shape[0]
    n_rw = rw_w_r.shape[0]
    n_sb = sb_w_qkv.shape[0]
    n_mem = mem_prompt.shape[1]
    bf = lambda t: t.astype(BF16)

    rw = []
    for j in range(n_rw):
        pj = dict(mu=rw_mu[j], w_r=bf(rw_w_r[j]), w_k=bf(rw_w_k[j]), w_v=bf(rw_w_v[j]), w_o=bf(rw_w_o[j]),
                  w0=rw_w0[j], a0=rw_a0[j], k_k=rw_k_k[j], k_a=rw_k_a[j],
                  w1=bf(_pad_cols(rw_w1[j])), w2=bf(_pad_rows(rw_w2[j])),
                  a1=bf(_pad_cols(rw_a1[j])), a2=bf(_pad_rows(rw_a2[j])),
                  g1=bf(_pad_cols(rw_g1[j])), g2=bf(_pad_rows(rw_g2[j])))
        if j > 0:
            pj.update(v0=rw_v0[j - 1], v1=bf(_pad_cols(rw_v1[j - 1])), v2=bf(_pad_rows(rw_v2[j - 1])))
        zeros = jnp.zeros((d,), F32)
        pj["out_vec"] = jnp.stack([rw_r_k[j].reshape(d), rw_lnx_w[j], rw_lnx_b[j]] + [zeros] * 5)
        rw.append(pj)
    w = dict(norm_mix=norm_mix, norm_xattn=norm_xattn, norm_ffn=norm_ffn, norm_final=norm_final,
             sb_qkv=bf(sb_w_qkv), sb_o=bf(sb_w_o), rw=rw, xa_q=bf(xa_w_q), xa_o=bf(xa_w_o),
             ffn_gu=bf(ffn_w_gate_up), ffn_down=bf(ffn_w_down))

    mem_k, mem_v = _mem_kv(mem_prompt.reshape(batch * n_mem, d), norm_mem, bf(xa_w_kv))
    mem_shape = (depth, batch, n_mem, XA_HEADS, d // XA_HEADS)
    y_p, p_sb_k, p_sb_v, p_wkv, p_shift = _trunk(
        x_prompt.reshape(batch * seq, d), batch,
        bf(mem_k).reshape(depth, batch, n_mem, d), bf(mem_v).reshape(depth, batch, n_mem, d),
        None, None, None, None, w)

    past_len = cache_sb_k.shape[2]
    y_s, s_sb_k, s_sb_v, s_wkv, s_shift = _trunk(
        x_sample.reshape(dec_batch * dec_seq, d), dec_batch,
        bf(cache_mem_k).reshape(depth, dec_batch, n_mem, d), bf(cache_mem_v).reshape(depth, dec_batch, n_mem, d),
        cache_sb_k.reshape(n_sb, dec_batch, past_len, d), cache_sb_v.reshape(n_sb, dec_batch, past_len, d),
        state_rwkv_wkv, state_rwkv_shift, w)

    hd = d // SB_HEADS
    return (y_p.reshape(batch, seq, d), y_s.reshape(dec_batch, dec_seq, d),
            p_sb_k.reshape(n_sb, batch, seq, SB_HEADS, hd), p_sb_v.reshape(n_sb, batch, seq, SB_HEADS, hd),
            p_wkv, p_shift,
            mem_k.reshape(mem_shape), mem_v.reshape(mem_shape),
            s_sb_k.reshape(n_sb, dec_batch, dec_seq, SB_HEADS, hd),
            s_sb_v.reshape(n_sb, dec_batch, dec_seq, SB_HEADS, hd),
            s_wkv, s_shift)
```

```python
import functools

import jax
import jax.numpy as jnp
from jax import lax
from jax.experimental import pallas as pl
from jax.experimental.pallas import tpu as pltpu

F32 = jnp.float32
BF16 = jnp.bfloat16

NORM_EPS = 1e-6
RW_GN_EPS = 64e-5
SB_HEADS = 16
XA_HEADS = 4
RW_HEAD_DIM = 64
LANES = 128
ROW_TILE = 256
FFN_ROW_TILE = 512
SB_BLOCK = 256
SB_PAIRS = 4
SB_WAVES = 2
SB_PAST_SPAN = 1024
WKV_CHUNK = 64
WKV_PAIRS = 8
FFN_CHUNK = 256
VMEM_LIMIT_BYTES = 56 * 1024 * 1024


def _params(*sem):
    return pltpu.CompilerParams(dimension_semantics=sem, vmem_limit_bytes=VMEM_LIMIT_BYTES)


def _dot(a, b):
    return jnp.dot(a, b, preferred_element_type=F32)


def _dot_nt(a, b):
    return lax.dot_general(a, b, (((1,), (1,)), ((), ())), preferred_element_type=F32)


def _split(x, n):
    parts = []
    for i in range(n):
        p = x.astype(BF16)
        parts.append(p)
        if i + 1 < n:
            x = x - p.astype(F32)
    return parts


def _mm(a, b, passes, nt=False):
    dot = _dot_nt if nt else _dot
    if passes == 1:
        return dot(a.astype(BF16), b.astype(BF16))
    a_hi, a_lo = _split(a, 2)
    b_hi, b_lo = _split(b, 2)
    return dot(a_hi, b_hi) + (dot(a_hi, b_lo) + dot(a_lo, b_hi))


def _mm_exact_rhs(a, b_bf16, n):
    out = None
    for p in _split(a, n):
        t = _dot(p, b_bf16)
        out = t if out is None else out + t
    return out


def _mm_exact_lhs(a_bf16, b, n):
    out = None
    for p in _split(b, n):
        t = _dot(a_bf16, p)
        out = t if out is None else out + t
    return out


def _rms(x, gain):
    ms = jnp.mean(x * x, axis=-1, keepdims=True)
    return x * lax.rsqrt(ms + NORM_EPS) * gain


def _group_ones(n, group):
    r = lax.broadcasted_iota(jnp.int32, (n, n), 0) // group
    c = lax.broadcasted_iota(jnp.int32, (n, n), 1) // group
    return jnp.where(r == c, 1.0, 0.0).astype(BF16)


def _mem_kv_kernel(x_ref, g_ref, w_ref, k_ref, v_ref):
    d = x_ref.shape[-1]
    h = _rms(x_ref[...], g_ref[0]).astype(BF16)
    kv = _dot(h, w_ref[0])
    k_ref[0] = kv[:, :d]
    v_ref[0] = kv[:, d:]


def _mem_kv(mem, gains, w_kv):
    depth = w_kv.shape[0]
    rows, d = mem.shape
    tm = min(ROW_TILE, rows)
    out = jax.ShapeDtypeStruct((depth, rows, d), F32)
    return pl.pallas_call(
        _mem_kv_kernel,
        grid=(depth, rows // tm),
        in_specs=[pl.BlockSpec((tm, d), lambda l, i: (i, 0)),
                  pl.BlockSpec((1, 1, d), lambda l, i: (l, 0, 0)),
                  pl.BlockSpec((1, d, 2 * d), lambda l, i: (l, 0, 0))],
        out_specs=[pl.BlockSpec((1, tm, d), lambda l, i: (l, i, 0)),
                   pl.BlockSpec((1, tm, d), lambda l, i: (l, i, 0))],
        out_shape=[out, out],
        compiler_params=_params("arbitrary", "arbitrary"),
        name="mem_kv",
    )(mem, gains.reshape(depth, 1, d), w_kv)


def _split_heads(x, hd):
    tm, d = x.shape
    blocks = []
    for p in range(d // LANES):
        slab = x[:, p * LANES:(p + 1) * LANES]
        blocks += [slab, pltpu.roll(slab, LANES - hd, 1)]
    return jnp.concatenate(blocks, axis=1).reshape(tm, d // hd, LANES)[:, :, :hd]


def _sb_qkv_kernel(x_ref, g_ref, w_ref, *rest, q_scale, slot):
    k_ref, v_ref, qkv_ref = rest[-3:]
    d = x_ref.shape[-1]
    hd = k_ref.shape[-1]
    h = _rms(x_ref[...], g_ref[...]).astype(BF16)
    qkv = _dot(h, w_ref[...])
    qkv_ref[:, :d] = (qkv[:, :d] * q_scale).astype(BF16)
    qkv_ref[:, d:] = qkv[:, d:].astype(BF16)
    k3 = _split_heads(qkv[:, d:2 * d], hd)
    v3 = _split_heads(qkv[:, 2 * d:], hd)
    own = slot if k_ref.shape[0] > 1 else 0
    for s in range(k_ref.shape[0]):
        if s == own:
            k_ref[s, 0] = k3
            v_ref[s, 0] = v3
        else:
            k_ref[s, 0] = jnp.zeros(k3.shape, k_ref.dtype)
            v_ref[s, 0] = jnp.zeros(v3.shape, v_ref.dtype)


def _sb_qkv(x, batch, gain, w, slot, n_slots, k_all=None, v_all=None):
    rows, d = x.shape
    seq = rows // batch
    tm = min(ROW_TILE, seq)
    nt = seq // tm
    hd = d // SB_HEADS
    assert 2 * hd == LANES
    q_scale = float(hd ** -0.5)
    ins = [x, gain.reshape(1, d), w]
    in_specs = [pl.BlockSpec((tm, d), lambda i: (i, 0)),
                pl.BlockSpec((1, d), lambda i: (0, 0)),
                pl.BlockSpec(w.shape, lambda i: (0, 0))]
    if k_all is None:
        aliases = {}
        slot_spec = pl.BlockSpec((n_slots, 1, tm, SB_HEADS, hd), lambda i: (0, i // nt, i % nt, 0, 0))
    else:
        ins += [k_all, v_all]
        in_specs += [pl.BlockSpec(memory_space=pl.ANY)] * 2
        aliases = {3: 0, 4: 1}
        slot_spec = pl.BlockSpec((1, 1, tm, SB_HEADS, hd), lambda i: (slot, i // nt, i % nt, 0, 0))
    kv_shape = jax.ShapeDtypeStruct((n_slots, batch, seq, SB_HEADS, hd), F32)
    return pl.pallas_call(
        functools.partial(_sb_qkv_kernel, q_scale=q_scale, slot=slot),
        grid=(rows // tm,),
        in_specs=in_specs,
        out_specs=[slot_spec, slot_spec, pl.BlockSpec((tm, 3 * d), lambda i: (i, 0))],
        out_shape=[kv_shape, kv_shape, jax.ShapeDtypeStruct((rows, 3 * d), BF16)],
        input_output_aliases=aliases,
        compiler_params=_params("arbitrary"),
        name="sb_qkv",
    )(*ins)


def _sb_step(q2s, kbs, vbs, carry, acc, neg_lower2, mask):
    m = q2s[0].shape[0]
    groups = len(q2s)
    per = max(groups // SB_WAVES, 1)
    waves = [range(c, min(c + per, groups)) for c in range(0, groups, per)]
    rows = [slice(w_[0] * m, (w_[-1] + 1) * m) for w_ in waves]

    def scores(c):
        return jnp.concatenate([_dot_nt(q2s[g], kbs[g]) for g in waves[c]], axis=0)

    def softplus_pieces(c, z):
        sp = jnp.maximum(z, 0.0) + jnp.log(1.0 + jnp.exp(-jnp.abs(z)))
        if mask is not None:
            sp = jnp.where(mask[rows[c]], sp, 0.0)
        return jnp.concatenate(_split(sp, 2), axis=1)

    def weights(c, z, from_s):
        later = jnp.concatenate([carry[rows[c]]] * (z.shape[1] // LANES), axis=1)
        w = jnp.exp(z + from_s + later)
        if mask is not None:
            w = jnp.where(mask[rows[c]], w, 0.0)
        return w.astype(BF16)

    def values(c, w):
        return jnp.concatenate([_dot(w[i * m:(i + 1) * m], vbs[g]) for i, g in enumerate(waves[c])], axis=0)

    n = len(waves)
    z, pieces, from_s, w, pv = ([None] * n for _ in range(5))
    for tick in range(n + 4):
        for c in range(n):
            stage = tick - c
            if stage == 0:
                z[c] = scores(c)
            elif stage == 1:
                pieces[c] = softplus_pieces(c, z[c])
            elif stage == 2:
                from_s[c] = _dot(pieces[c], neg_lower2)
            elif stage == 3:
                w[c] = weights(c, z[c], from_s[c])
            elif stage == 4:
                pv[c] = values(c, w[c])
    total = jnp.concatenate([jnp.broadcast_to(f[:, 0:1], (f.shape[0], LANES)) for f in from_s], axis=0)
    return carry + total, acc + jnp.concatenate(pv, axis=0)


def _sb_consts(bk):
    r = lax.broadcasted_iota(jnp.int32, (2, bk, bk), 1).reshape(2 * bk, bk)
    c = lax.broadcasted_iota(jnp.int32, (2 * bk, bk), 1)
    return jnp.where(r >= c, -1.0, 0.0).astype(BF16)


def _sb_stack_heads(q_ref, q2_ref):
    tq = q_ref.shape[0]
    lane = lax.broadcasted_iota(jnp.int32, (tq, LANES), 1)
    for g in range(q2_ref.shape[0]):
        q = q_ref[:, g * LANES:(g + 1) * LANES]
        zero = jnp.zeros_like(q)
        q2_ref[g] = jnp.concatenate([jnp.where(lane < LANES // 2, q, zero),
                                     jnp.where(lane >= LANES // 2, q, zero)], axis=0)


def _sb_causal_mask(groups, tq, bk):
    t = lax.broadcasted_iota(jnp.int32, (2 * groups, tq, bk), 1).reshape(2 * groups * tq, bk)
    s = lax.broadcasted_iota(jnp.int32, (2 * groups * tq, bk), 1)
    return s < t


def _sb_write(o_ref, acc_ref):
    tq = o_ref.shape[0]
    lane = lax.broadcasted_iota(jnp.int32, (tq, LANES), 1)
    for g in range(acc_ref.shape[0] // (2 * tq)):
        o_ref[:, g * LANES:(g + 1) * LANES] = jnp.where(
            lane < LANES // 2, acc_ref[2 * g * tq:(2 * g + 1) * tq, :],
            acc_ref[(2 * g + 1) * tq:(2 * g + 2) * tq, :]).astype(o_ref.dtype)


def _sb_prompt_kernel(q_ref, k_ref, v_ref, o_ref, q2_ref, carry_ref, acc_ref):
    blk = q_ref.shape[0]
    groups = q2_ref.shape[0]
    i = pl.program_id(2)
    neg_lower2 = _sb_consts(blk)
    _sb_stack_heads(q_ref, q2_ref)
    lanes = [slice(g * LANES, (g + 1) * LANES) for g in range(groups)]

    def step(row0, carry, acc, mask):
        return _sb_step([q2_ref[g] for g in range(groups)],
                        [k_ref[pl.ds(row0, blk), sl] for sl in lanes],
                        [v_ref[pl.ds(row0, blk), sl] for sl in lanes], carry, acc, neg_lower2, mask)

    zero = jnp.zeros(carry_ref.shape, F32)
    carry_ref[...], acc_ref[...] = step(pl.multiple_of(i * blk, blk), zero, zero,
                                        _sb_causal_mask(groups, blk, blk))

    def body(it, c):
        carry_ref[...], acc_ref[...] = step(pl.multiple_of((i - 1 - it) * blk, blk),
                                            carry_ref[...], acc_ref[...], None)
        return c

    lax.fori_loop(0, i, body, 0)
    _sb_write(o_ref, acc_ref)


def _sb_attention_prompt(qkv, batch):
    rows, d3 = qkv.shape
    d = d3 // 3
    seq = rows // batch
    blk = min(SB_BLOCK, seq)
    nq = seq // blk
    width = SB_PAIRS * LANES
    ng = d // width
    return pl.pallas_call(
        _sb_prompt_kernel,
        grid=(batch, ng, nq),
        in_specs=[pl.BlockSpec((blk, width), lambda b, p, i: (b * nq + i, p)),
                  pl.BlockSpec((seq, width), lambda b, p, i: (b, ng + p)),
                  pl.BlockSpec((seq, width), lambda b, p, i: (b, 2 * ng + p))],
        out_specs=pl.BlockSpec((blk, width), lambda b, p, i: (b * nq + i, p)),
        out_shape=jax.ShapeDtypeStruct((rows, d), BF16),
        scratch_shapes=[pltpu.VMEM((SB_PAIRS, 2 * blk, LANES), BF16),
                        pltpu.VMEM((SB_PAIRS * 2 * blk, LANES), F32),
                        pltpu.VMEM((SB_PAIRS * 2 * blk, LANES), F32)],
        compiler_params=_params("arbitrary", "arbitrary", "arbitrary"),
        name="sb_attention_prompt",
    )(qkv, qkv, qkv)


def _sb_sample_kernel(q_ref, kn_ref, vn_ref, kp_ref, vp_ref, o_ref, q2_ref, carry_ref, acc_ref, *, blk):
    tq = q_ref.shape[0]
    groups = q2_ref.shape[0]
    c = pl.program_id(1)
    neg_lower2 = _sb_consts(blk)

    lanes = [slice(g * LANES, (g + 1) * LANES) for g in range(groups)]

    @pl.when(c == 0)
    def _():
        _sb_stack_heads(q_ref, q2_ref)
        zero = jnp.zeros(carry_ref.shape, F32)
        carry_ref[...], acc_ref[...] = _sb_step(
            [q2_ref[g] for g in range(groups)], [kn_ref[0, :, sl] for sl in lanes],
            [vn_ref[0, :, sl] for sl in lanes], zero, zero, neg_lower2, _sb_causal_mask(groups, tq, blk))

    carry, acc = carry_ref[...], acc_ref[...]
    for sb in reversed(range(kp_ref.shape[2] // blk)):
        rows = slice(sb * blk, (sb + 1) * blk)
        carry, acc = _sb_step([q2_ref[g] for g in range(groups)],
                              [kp_ref[0, 0, rows, sl].astype(BF16) for sl in lanes],
                              [vp_ref[0, 0, rows, sl].astype(BF16) for sl in lanes],
                              carry, acc, neg_lower2, None)
    carry_ref[...], acc_ref[...] = carry, acc

    @pl.when(c == pl.num_programs(1) - 1)
    def _():
        _sb_write(o_ref, acc_ref)


def _sb_attention_sample(qkv, batch, past_k, past_v, layer):
    rows, d3 = qkv.shape
    d = d3 // 3
    seq = rows // batch
    blk = SB_BLOCK
    pairs = d // LANES
    past_len = past_k.shape[2]
    span = min(SB_PAST_SPAN, past_len)
    steps = past_len // span
    assert seq <= blk and past_len % span == 0 and span % blk == 0
    new = qkv[:, d:].reshape(batch, seq, 2 * d)
    new = jnp.pad(new, ((0, 0), (0, blk - seq), (0, 0)))
    return pl.pallas_call(
        functools.partial(_sb_sample_kernel, blk=blk),
        grid=(batch, steps),
        in_specs=[pl.BlockSpec((seq, d), lambda b, c: (b, 0)),
                  pl.BlockSpec((1, blk, d), lambda b, c: (b, 0, 0)),
                  pl.BlockSpec((1, blk, d), lambda b, c: (b, 0, 1)),
                  pl.BlockSpec((1, 1, span, d), lambda b, c: (layer, b, steps - 1 - c, 0)),
                  pl.BlockSpec((1, 1, span, d), lambda b, c: (layer, b, steps - 1 - c, 0))],
        out_specs=pl.BlockSpec((seq, d), lambda b, c: (b, 0)),
        out_shape=jax.ShapeDtypeStruct((rows, d), BF16),
        scratch_shapes=[pltpu.VMEM((pairs, 2 * seq, LANES), BF16),
                        pltpu.VMEM((pairs * 2 * seq, LANES), F32),
                        pltpu.VMEM((pairs * 2 * seq, LANES), F32)],
        compiler_params=_params("arbitrary", "arbitrary"),
        name="sb_attention_sample",
    )(qkv, new, new, past_k, past_v)


def _linear_residual_kernel(a_ref, w_ref, x_ref, o_ref):
    o_ref[...] = x_ref[...] + _dot(a_ref[...], w_ref[...])


def _linear_residual(a, w, x):
    rows, d = x.shape
    tm = min(ROW_TILE, rows)
    return pl.pallas_call(
        _linear_residual_kernel,
        grid=(rows // tm,),
        in_specs=[pl.BlockSpec((tm, a.shape[1]), lambda i: (i, 0)),
                  pl.BlockSpec(w.shape, lambda i: (0, 0)),
                  pl.BlockSpec((tm, d), lambda i: (i, 0))],
        out_specs=pl.BlockSpec((tm, d), lambda i: (i, 0)),
        out_shape=jax.ShapeDtypeStruct((rows, d), F32),
        input_output_aliases={2: 0},
        compiler_params=_params("arbitrary"),
        name="linear_residual",
    )(a, w, x)


def _xattn_kernel(x_ref, g_ref, wq_ref, wo_ref, mk_ref, mv_ref, o_ref, *, heads):
    x = x_ref[...]
    d = x.shape[-1]
    dh = d // heads
    h = _rms(x, g_ref[...]).astype(BF16)
    q = _dot(h, wq_ref[...]).astype(BF16)
    outs = []
    for hd in range(heads):
        sl = slice(hd * dh, (hd + 1) * dh)
        s = _dot_nt(q[:, sl], mk_ref[0, :, sl]) * (dh ** -0.5)
        e = jnp.exp(s - jnp.max(s, axis=-1, keepdims=True))
        p = e / jnp.sum(e, axis=-1, keepdims=True)
        outs.append(_dot(p.astype(BF16), mv_ref[0, :, sl]).astype(BF16))
    o = jnp.concatenate(outs, axis=-1)
    o_ref[...] = x + _dot(o, wo_ref[...])


def _xattn(x, gain, wq, wo, mem_k, mem_v):
    rows, d = x.shape
    batch, n_mem, _ = mem_k.shape
    seq = rows // batch
    tm = min(ROW_TILE, seq)
    nt = seq // tm
    return pl.pallas_call(
        functools.partial(_xattn_kernel, heads=XA_HEADS),
        grid=(batch, nt),
        in_specs=[pl.BlockSpec((tm, d), lambda b, i: (b * nt + i, 0)),
                  pl.BlockSpec((1, d), lambda b, i: (0, 0)),
                  pl.BlockSpec((d, d), lambda b, i: (0, 0)),
                  pl.BlockSpec((d, d), lambda b, i: (0, 0)),
                  pl.BlockSpec((1, n_mem, d), lambda b, i: (b, 0, 0)),
                  pl.BlockSpec((1, n_mem, d), lambda b, i: (b, 0, 0))],
        out_specs=pl.BlockSpec((tm, d), lambda b, i: (b * nt + i, 0)),
        out_shape=jax.ShapeDtypeStruct((rows, d), F32),
        input_output_aliases={0: 0},
        compiler_params=_params("arbitrary", "arbitrary"),
        name="xattn",
    )(x, gain.reshape(1, d), wq, wo, mem_k, mem_v)


def _ffn_kernel(x_ref, g_ref, wgu_ref, wd_ref, o_ref, *, final_gain):
    x = x_ref[...]
    hidden = wd_ref.shape[0]
    h = _rms(x, g_ref[0:1, :]).astype(BF16)
    acc = x
    for c0 in range(0, hidden, FFN_CHUNK):
        c1 = min(c0 + FFN_CHUNK, hidden)
        gate = _dot(h, wgu_ref[:, c0:c1])
        up = _dot(h, wgu_ref[:, hidden + c0:hidden + c1])
        act = (gate * jax.nn.sigmoid(gate) * up).astype(BF16)
        acc = acc + _dot(act, wd_ref[c0:c1, :])
    if final_gain:
        acc = _rms(acc, g_ref[1:2, :])
    o_ref[...] = acc


def _ffn(x, gain, w_gate_up, w_down, final_gain=None):
    rows, d = x.shape
    tm = min(FFN_ROW_TILE, rows)
    gains = jnp.stack([gain, gain if final_gain is None else final_gain])
    return pl.pallas_call(
        functools.partial(_ffn_kernel, final_gain=final_gain is not None),
        grid=(rows // tm,),
        in_specs=[pl.BlockSpec((tm, d), lambda i: (i, 0)),
                  pl.BlockSpec((2, d), lambda i: (0, 0)),
                  pl.BlockSpec(w_gate_up.shape, lambda i: (0, 0)),
                  pl.BlockSpec(w_down.shape, lambda i: (0, 0))],
        out_specs=pl.BlockSpec((tm, d), lambda i: (i, 0)),
        out_shape=jax.ShapeDtypeStruct((rows, d), F32),
        input_output_aliases={0: 0},
        compiler_params=_params("arbitrary"),
        name="ffn",
    )(x, gains, w_gate_up, w_down)


def _rw_proj_kernel(*refs, has_v_res):
    if has_v_res:
        (x_ref, gain_ref, shift0_ref, mu_ref, wr_ref, wk_ref, wv_ref, vec_ref,
         w1_ref, w2_ref, a1_ref, a2_ref, g1_ref, g2_ref, v1_ref, v2_ref, vfirst_ref,
         r_ref, lw_ref, k_ref, v_ref, kk_ref, a_ref, g_ref, shift_ref, prev_ref) = refs
    else:
        (x_ref, gain_ref, shift0_ref, mu_ref, wr_ref, wk_ref, wv_ref, vec_ref,
         w1_ref, w2_ref, a1_ref, a2_ref, g1_ref, g2_ref,
         r_ref, lw_ref, k_ref, v_ref, kk_ref, a_ref, g_ref, shift_ref, prev_ref) = refs
    i = pl.program_id(1)
    h = _rms(x_ref[...], gain_ref[...])
    tm = h.shape[0]

    @pl.when(i == 0)
    def _():
        prev_ref[...] = shift0_ref[0]

    row = lax.broadcasted_iota(jnp.int32, h.shape, 0)
    h_prev = jnp.where(row == 0, prev_ref[...], pltpu.roll(h, 1, 0))
    last = h[tm - 1:tm, :]
    prev_ref[...] = last
    shift_ref[0] = last

    xx = h_prev - h
    x_r, x_w, x_k, x_v, x_a, x_g = ((h + xx * mu_ref[j:j + 1, :]).astype(BF16) for j in range(6))
    w0, a0, k_k, k_a, v0 = (vec_ref[j:j + 1, :] for j in range(5))

    r = _dot(x_r, wr_ref[...])
    k = _dot(x_k, wk_ref[...])
    v = _dot(x_v, wv_ref[...])
    dec = w0 + _dot(jnp.tanh(_dot(x_w, w1_ref[...])).astype(BF16), w2_ref[...])
    w_log = -jax.nn.softplus(-dec) - 0.5
    if has_v_res:
        mix = jax.nn.sigmoid(v0 + _dot(_dot(x_v, v1_ref[...]).astype(BF16), v2_ref[...]))
        v = v + (vfirst_ref[...] - v) * mix
    a = jax.nn.sigmoid(a0 + _dot(_dot(x_a, a1_ref[...]).astype(BF16), a2_ref[...]))
    g = _dot(jax.nn.sigmoid(_dot(x_g, g1_ref[...])).astype(BF16), g2_ref[...])

    r_ref[...] = r
    lw_ref[...] = -jnp.exp(w_log)
    kk_ref[...] = k * k_k
    k_ref[...] = k * (1.0 + (a - 1.0) * k_a)
    v_ref[...] = v
    a_ref[...] = a
    g_ref[...] = g


def _pad_cols(w, mult=LANES):
    pad = -w.shape[1] % mult
    return jnp.pad(w, ((0, 0), (0, pad))) if pad else w


def _pad_rows(w, mult=LANES):
    pad = -w.shape[0] % mult
    return jnp.pad(w, ((0, pad), (0, 0))) if pad else w


def _rw_proj(x, batch, gain, shift0, p, v_first):
    rows, d = x.shape
    seq = rows // batch
    tm = min(ROW_TILE, seq)
    nt = seq // tm
    has_v_res = v_first is not None
    zeros = jnp.zeros((d,), F32)
    vec = jnp.stack([p["w0"], p["a0"], p["k_k"], p["k_a"], p["v0"] if has_v_res else zeros,
                     zeros, zeros, zeros])
    row_spec = pl.BlockSpec((tm, d), lambda b, i: (b * nt + i, 0))

    def full(arr):
        return pl.BlockSpec(arr.shape, lambda b, i: (0,) * arr.ndim)

    ins = [x, gain.reshape(1, d), shift0.reshape(batch, 1, d), p["mu"], p["w_r"], p["w_k"], p["w_v"], vec,
           p["w1"], p["w2"], p["a1"], p["a2"], p["g1"], p["g2"]]
    in_specs = [row_spec, full(ins[1]), pl.BlockSpec((1, 1, d), lambda b, i: (b, 0, 0))]
    in_specs += [full(a) for a in ins[3:]]
    if has_v_res:
        ins += [p["v1"], p["v2"], v_first]
        in_specs += [full(p["v1"]), full(p["v2"]), row_spec]
    tok = jax.ShapeDtypeStruct((rows, d), F32)
    outs = pl.pallas_call(
        functools.partial(_rw_proj_kernel, has_v_res=has_v_res),
        grid=(batch, nt),
        in_specs=in_specs,
        out_specs=[row_spec] * 7 + [pl.BlockSpec((1, 1, d), lambda b, i: (b, 0, 0))],
        out_shape=[tok] * 7 + [jax.ShapeDtypeStruct((batch, 1, d), F32)],
        scratch_shapes=[pltpu.VMEM((1, d), F32)],
        compiler_params=_params("arbitrary", "arbitrary"),
        name="rw_proj",
    )(*ins)
    return outs[:7], outs[7].reshape(batch, d)


WKV_SUM_PASSES = 3
WKV_STATE_PASSES = 3
WKV_SOLVE_PASSES = 1


def _wkv_block(pairs, c_):
    n = pairs[0][0].shape[0]
    inv_hd = 1.0 / RW_HEAD_DIM
    sum_l = lambda x: _mm_exact_lhs(c_["tri_incl"], x, WKV_SUM_PASSES)
    sum_g = lambda x: _mm_exact_rhs(x, c_["gsum"], WKV_SUM_PASSES)
    solve_mm = lambda x, y, nt=False: _mm(x, y, WKV_SOLVE_PASSES, nt=nt)
    np_ = len(pairs)

    cs = [sum_l(p[1]) for p in pairs]
    kk_sq = [sum_g(p[4] * p[4]) for p in pairs]
    pre = []
    for (r, lw, k, v, kk, a, g, vec, state), cs_p, sq in zip(pairs, cs, kk_sq):
        cs_last = cs_p[n - 1:n, :]
        d_inv = jnp.exp(-cs_p)
        d_rest = jnp.exp(cs_last - cs_p)
        kkn = kk * lax.rsqrt(jnp.maximum(sq, 1e-24))
        b = kkn * a
        pre.append(dict(a_t=-kkn * jnp.exp(cs_p - lw), r_t=r * jnp.exp(cs_p), b_t=b * d_inv, k_t=k * d_inv,
                        bk_rest=jnp.concatenate([b * d_rest, k * d_rest], axis=0), decay=jnp.exp(cs_last)))

    from_state = [_mm(jnp.concatenate([q["a_t"], q["r_t"]], axis=0), p[8], WKV_STATE_PASSES, nt=True)
                  for p, q in zip(pairs, pre)]
    chains = [(i, m) for i in range(np_) for m in c_["head_masks"]]
    grams = []
    for i, m in chains:
        q = pre[i]
        zero = jnp.zeros_like(q["a_t"])
        lhs = jnp.concatenate([jnp.where(m, q["a_t"], zero), jnp.where(m, q["r_t"], zero)], axis=0)
        grams.append(solve_mm(lhs, jnp.concatenate([q["b_t"], q["k_t"]], axis=0), nt=True))
    lows = [jnp.where(c_["strict"], gm[:n, :n], 0.0) for gm in grams]
    m_aks = [jnp.where(c_["strict_k"], gm[:n], 0.0) for gm in grams]
    m_rs = [jnp.where(c_["incl2"], gm[n:], 0.0) for gm in grams]
    rhs = [from_state[i][:n] + solve_mm(mk, jnp.concatenate([jnp.zeros_like(pairs[i][3]), pairs[i][3]], axis=0))
           for (i, _), mk in zip(chains, m_aks)]
    sig = [x + solve_mm(lo, x) for lo, x in zip(lows, rhs)]
    pw = lows
    step = 1
    while 2 * step < n:
        pw = [solve_mm(x, x) for x in pw]
        sig = [x + solve_mm(p2, x) for p2, x in zip(pw, sig)]
        step *= 2
    ys = [from_state[i][n:] + solve_mm(mr, jnp.concatenate([s, pairs[i][3]], axis=0))
          for (i, _), mr, s in zip(chains, m_rs, sig)]

    first = c_["head_masks"][0]
    sig_p = [jnp.where(first, sig[2 * i], sig[2 * i + 1]) for i in range(np_)]
    y_p = [jnp.where(first, ys[2 * i], ys[2 * i + 1]) for i in range(np_)]
    upd = [_mm(jnp.concatenate([s, p[3]], axis=0).T, q["bk_rest"], WKV_STATE_PASSES)
           for s, p, q in zip(sig_p, pairs, pre)]
    new_states = [p[8] * q["decay"] + jnp.where(c_["same_head"], u, 0.0) for p, q, u in zip(pairs, pre, upd)]

    mean = [sum_g(y) * inv_hd for y in y_p]
    dev = [y - m for y, m in zip(y_p, mean)]
    var = [sum_g(dv * dv) * inv_hd for dv in dev]
    bonus = [sum_g(p[0] * p[2] * p[7][0:1, :]) * p[3] for p in pairs]
    outs = []
    for p, dv, vr, bo in zip(pairs, dev, var, bonus):
        vec = p[7]
        o = dv * lax.rsqrt(vr + RW_GN_EPS) * vec[1:2, :] + vec[2:3, :]
        outs.append((o + bo) * p[6])
    return list(zip(outs, new_states))


def _wkv_kernel(*refs, has_state0):
    if has_state0:
        (r_ref, lw_ref, k_ref, v_ref, kk_ref, a_ref, g_ref, vec_ref, s0_ref,
         o_ref, sout_ref, state_ref) = refs
    else:
        (r_ref, lw_ref, k_ref, v_ref, kk_ref, a_ref, g_ref, vec_ref,
         o_ref, sout_ref, state_ref) = refs
    c = pl.program_id(2)
    n = r_ref.shape[0]
    hd = RW_HEAD_DIM

    @pl.when(c == 0)
    def _():
        if has_state0:
            state_ref[...] = s0_ref[0]
        else:
            state_ref[...] = jnp.zeros_like(state_ref)

    tr = lax.broadcasted_iota(jnp.int32, (n, n), 0)
    tc = lax.broadcasted_iota(jnp.int32, (n, n), 1)
    tr2 = lax.broadcasted_iota(jnp.int32, (n, 2 * n), 0)
    tc2 = lax.broadcasted_iota(jnp.int32, (n, 2 * n), 1)
    lane = lax.broadcasted_iota(jnp.int32, (n, LANES), 1)
    sr = lax.broadcasted_iota(jnp.int32, (LANES, LANES), 0) // hd
    sc = lax.broadcasted_iota(jnp.int32, (LANES, LANES), 1) // hd
    consts = dict(
        gsum=_group_ones(LANES, hd),
        tri_incl=jnp.where(tc <= tr, 1.0, 0.0).astype(BF16),
        strict=tc < tr,
        strict_k=(tc2 >= n) & (tc2 - n < tr2),
        incl2=jnp.where(tc2 >= n, tc2 - n, tc2) <= tr2,
        eye=jnp.where(tc == tr, 1.0, 0.0).astype(F32),
        head_masks=[lane < hd, lane >= hd],
        same_head=sr == sc,
    )
    pairs = []
    for p in range(state_ref.shape[0]):
        sl = slice(p * LANES, (p + 1) * LANES)
        pairs.append((r_ref[:, sl], lw_ref[:, sl], k_ref[:, sl], v_ref[:, sl], kk_ref[:, sl],
                      a_ref[:, sl], g_ref[:, sl], vec_ref[:, sl], state_ref[p]))
    results = _wkv_block(pairs, consts)
    o_ref[...] = jnp.concatenate([o for o, _ in results], axis=1).astype(o_ref.dtype)
    new_states = jnp.stack([s for _, s in results])
    state_ref[...] = new_states
    sout_ref[0] = new_states


def _wkv(tok, batch, vec, state0):
    rows, d = tok[0].shape
    seq = rows // batch
    n = min(WKV_CHUNK, seq)
    nc = seq // n
    pairs = d // LANES
    gp = WKV_PAIRS
    width = gp * LANES
    has_state0 = state0 is not None
    tok_spec = pl.BlockSpec((n, width), lambda b, p, c: (b * nc + c, p))
    state_spec = pl.BlockSpec((1, gp, LANES, LANES), lambda b, p, c: (b, p, 0, 0))
    ins = list(tok) + [vec]
    in_specs = [tok_spec] * 7 + [pl.BlockSpec((8, width), lambda b, p, c: (0, p))]
    if has_state0:
        ins.append(state0)
        in_specs.append(state_spec)
    return pl.pallas_call(
        functools.partial(_wkv_kernel, has_state0=has_state0),
        grid=(batch, pairs // gp, nc),
        in_specs=in_specs,
        out_specs=[tok_spec, state_spec],
        out_shape=[jax.ShapeDtypeStruct((rows, d), BF16),
                   jax.ShapeDtypeStruct((batch, pairs, LANES, LANES), F32)],
        scratch_shapes=[pltpu.VMEM((gp, LANES, LANES), F32)],
        compiler_params=_params("arbitrary", "arbitrary", "arbitrary"),
        name="wkv",
    )(*ins)


def _state_to_blockdiag(state):
    b, h, n, _ = state.shape
    s = state.reshape(b, h // 2, 2, n, n)
    z = jnp.zeros_like(s[:, :, 0])
    top = jnp.concatenate([s[:, :, 0], z], axis=-1)
    bot = jnp.concatenate([z, s[:, :, 1]], axis=-1)
    return jnp.concatenate([top, bot], axis=-2)


def _state_from_blockdiag(bd):
    b, p, n2, _ = bd.shape
    n = n2 // 2
    return jnp.stack([bd[:, :, :n, :n], bd[:, :, n:, n:]], axis=2).reshape(b, 2 * p, n, n)


def _trunk(x, batch, mem_k, mem_v, past_k, past_v, rw_state0, rw_shift0, w):
    depth = len(w["ffn_gu"])
    d = x.shape[-1]
    sb_k, sb_v, rw_states, rw_shifts = None, None, [], []
    n_sb = (depth + 1) // 2
    v_first = None
    for layer in range(depth):
        j = layer // 2
        if layer % 2 == 0:
            sb_k, sb_v, qkv = _sb_qkv(x, batch, w["norm_mix"][layer], w["sb_qkv"][j], j, n_sb, sb_k, sb_v)
            if past_k is None:
                o = _sb_attention_prompt(qkv, batch)
            else:
                o = _sb_attention_sample(qkv, batch, past_k, past_v, j)
            x = _linear_residual(o, w["sb_o"][j], x)
        else:
            if rw_state0 is None:
                state0 = None
                shift0 = jnp.zeros((batch, d), F32)
            else:
                state0 = _state_to_blockdiag(rw_state0[j])
                shift0 = rw_shift0[j]
            tok, shift = _rw_proj(x, batch, w["norm_mix"][layer], shift0, w["rw"][j], v_first)
            if v_first is None:
                v_first = tok[3]
            o, state = _wkv(tok, batch, w["rw"][j]["out_vec"], state0)
            rw_states.append(_state_from_blockdiag(state))
            rw_shifts.append(shift)
            x = _linear_residual(o, w["rw"][j]["w_o"], x)
        x = _xattn(x, w["norm_xattn"][layer], w["xa_q"][layer], w["xa_o"][layer], mem_k[layer], mem_v[layer])
        final = w["norm_final"] if layer == depth - 1 else None
        x = _ffn(x, w["norm_ffn"][layer], w["ffn_gu"][layer], w["ffn_down"][layer], final)
    return x, sb_k, sb_v, jnp.stack(rw_states), jnp.stack(rw_shifts)


def kernel(x_prompt, x_sample, mem_prompt, cache_sb_k, cache_sb_v, state_rwkv_wkv, state_rwkv_shift,
           cache_mem_k, cache_mem_v, norm_mix, norm_xattn, norm_ffn, norm_mem, norm_final,
           sb_w_qkv, sb_w_o, rw_mu, rw_w_r, rw_w_k, rw_w_v, rw_w_o, rw_w0, rw_w1, rw_w2,
           rw_a0, rw_a1, rw_a2, rw_v0, rw_v1, rw_v2, rw_g1, rw_g2, rw_k_k, rw_k_a, rw_r_k,
           rw_lnx_w, rw_lnx_b, xa_w_q, xa_w_kv, xa_w_o, ffn_w_gate_up, ffn_w_down):
    batch, seq, d = x_prompt.shape
    dec_batch, dec_seq, _ = x_sample.shape
    depth = norm_mix.shape[0]
    n_rw = rw_w_r.shape[0]
    n_sb = sb_w_qkv.shape[0]
    n_mem = mem_prompt.shape[1]
    bf = lambda t: t.astype(BF16)

    rw = []
    for j in range(n_rw):
        pj = dict(mu=rw_mu[j], w_r=bf(rw_w_r[j]), w_k=bf(rw_w_k[j]), w_v=bf(rw_w_v[j]), w_o=bf(rw_w_o[j]),
                  w0=rw_w0[j], a0=rw_a0[j], k_k=rw_k_k[j], k_a=rw_k_a[j],
                  w1=bf(_pad_cols(rw_w1[j])), w2=bf(_pad_rows(rw_w2[j])),
                  a1=bf(_pad_cols(rw_a1[j])), a2=bf(_pad_rows(rw_a2[j])),
                  g1=bf(_pad_cols(rw_g1[j])), g2=bf(_pad_rows(rw_g2[j])))
        if j > 0:
            pj.update(v0=rw_v0[j - 1], v1=bf(_pad_cols(rw_v1[j - 1])), v2=bf(_pad_rows(rw_v2[j - 1])))
        zeros = jnp.zeros((d,), F32)
        pj["out_vec"] = jnp.stack([rw_r_k[j].reshape(d), rw_lnx_w[j], rw_lnx_b[j]] + [zeros] * 5)
        rw.append(pj)
    w = dict(norm_mix=norm_mix, norm_xattn=norm_xattn, norm_ffn=norm_ffn, norm_final=norm_final,
             sb_qkv=bf(sb_w_qkv), sb_o=bf(sb_w_o), rw=rw, xa_q=bf(xa_w_q), xa_o=bf(xa_w_o),
             ffn_gu=bf(ffn_w_gate_up), ffn_down=bf(ffn_w_down))

    mem_k, mem_v = _mem_kv(mem_prompt.reshape(batch * n_mem, d), norm_mem, bf(xa_w_kv))
    mem_shape = (depth, batch, n_mem, XA_HEADS, d // XA_HEADS)
    y_p, p_sb_k, p_sb_v, p_wkv, p_shift = _trunk(
        x_prompt.reshape(batch * seq, d), batch,
        bf(mem_k).reshape(depth, batch, n_mem, d), bf(mem_v).reshape(depth, batch, n_mem, d),
        None, None, None, None, w)

    past_len = cache_sb_k.shape[2]
    y_s, s_sb_k, s_sb_v, s_wkv, s_shift = _trunk(
        x_sample.reshape(dec_batch * dec_seq, d), dec_batch,
        bf(cache_mem_k).reshape(depth, dec_batch, n_mem, d), bf(cache_mem_v).reshape(depth, dec_batch, n_mem, d),
        cache_sb_k.reshape(n_sb, dec_batch, past_len, d), cache_sb_v.reshape(n_sb, dec_batch, past_len, d),
        state_rwkv_wkv, state_rwkv_shift, w)

    hd = d // SB_HEADS
    return (y_p.reshape(batch, seq, d), y_s.reshape(dec_batch, dec_seq, d),
            p_sb_k.reshape(n_sb, batch, seq, SB_HEADS, hd), p_sb_v.reshape(n_sb, batch, seq, SB_HEADS, hd),
            p_wkv, p_shift,
            mem_k.reshape(mem_shape), mem_v.reshape(mem_shape),
            s_sb_k.reshape(n_sb, dec_batch, dec_seq, SB_HEADS, hd),
            s_sb_v.reshape(n_sb, dec_batch, dec_seq, SB_HEADS, hd),
            s_wkv, s_shift)
```

```python
import functools

import jax
import jax.numpy as jnp
from jax import lax
from jax.experimental import pallas as pl
from jax.experimental.pallas import tpu as pltpu

F32 = jnp.float32
BF16 = jnp.bfloat16

NORM_EPS = 1e-6
RW_GN_EPS = 64e-5
SB_HEADS = 16
XA_HEADS = 4
RW_HEAD_DIM = 64
LANES = 128
ROW_TILE = 256
WIDE_ROW_TILE = 512
SB_BLOCK = 256
SB_PAIRS = 4
SB_WAVES = 2
SB_UNROLL = 2
SB_PAST_SPAN = 1024
WKV_CHUNK = 64
WKV_PAIRS = 8
FFN_CHUNK = 256
VMEM_LIMIT_BYTES = 56 * 1024 * 1024


def _params(*sem):
    return pltpu.CompilerParams(dimension_semantics=sem, vmem_limit_bytes=VMEM_LIMIT_BYTES)


def _dot(a, b):
    return jnp.dot(a, b, preferred_element_type=F32)


def _dot_nt(a, b):
    return lax.dot_general(a, b, (((1,), (1,)), ((), ())), preferred_element_type=F32)


def _split(x, n):
    parts = []
    for i in range(n):
        p = x.astype(BF16)
        parts.append(p)
        if i + 1 < n:
            x = x - p.astype(F32)
    return parts


def _mm(a, b, passes, nt=False):
    dot = _dot_nt if nt else _dot
    if passes == 1:
        return dot(a.astype(BF16), b.astype(BF16))
    a_hi, a_lo = _split(a, 2)
    b_hi, b_lo = _split(b, 2)
    return dot(a_hi, b_hi) + (dot(a_hi, b_lo) + dot(a_lo, b_hi))


def _mm_exact_rhs(a, b_bf16, n):
    out = None
    for p in _split(a, n):
        t = _dot(p, b_bf16)
        out = t if out is None else out + t
    return out


def _mm_exact_lhs(a_bf16, b, n):
    out = None
    for p in _split(b, n):
        t = _dot(a_bf16, p)
        out = t if out is None else out + t
    return out


def _rms(x, gain):
    ms = jnp.mean(x * x, axis=-1, keepdims=True)
    return x * lax.rsqrt(ms + NORM_EPS) * gain


def _group_ones(n, group):
    r = lax.broadcasted_iota(jnp.int32, (n, n), 0) // group
    c = lax.broadcasted_iota(jnp.int32, (n, n), 1) // group
    return jnp.where(r == c, 1.0, 0.0).astype(BF16)


def _mem_kv_kernel(x_ref, g_ref, w_ref, k_ref, v_ref):
    d = x_ref.shape[-1]
    h = _rms(x_ref[...], g_ref[0]).astype(BF16)
    kv = _dot(h, w_ref[0])
    k_ref[0] = kv[:, :d]
    v_ref[0] = kv[:, d:]


def _mem_kv(mem, gains, w_kv):
    depth = w_kv.shape[0]
    rows, d = mem.shape
    tm = min(ROW_TILE, rows)
    out = jax.ShapeDtypeStruct((depth, rows, d), F32)
    return pl.pallas_call(
        _mem_kv_kernel,
        grid=(depth, rows // tm),
        in_specs=[pl.BlockSpec((tm, d), lambda l, i: (i, 0)),
                  pl.BlockSpec((1, 1, d), lambda l, i: (l, 0, 0)),
                  pl.BlockSpec((1, d, 2 * d), lambda l, i: (l, 0, 0))],
        out_specs=[pl.BlockSpec((1, tm, d), lambda l, i: (l, i, 0)),
                   pl.BlockSpec((1, tm, d), lambda l, i: (l, i, 0))],
        out_shape=[out, out],
        compiler_params=_params("arbitrary", "arbitrary"),
        name="mem_kv",
    )(mem, gains.reshape(depth, 1, d), w_kv)


def _split_heads(x, hd):
    tm, d = x.shape
    blocks = []
    for p in range(d // LANES):
        slab = x[:, p * LANES:(p + 1) * LANES]
        blocks += [slab, pltpu.roll(slab, LANES - hd, 1)]
    return jnp.concatenate(blocks, axis=1).reshape(tm, d // hd, LANES)[:, :, :hd]


def _sb_qkv_kernel(x_ref, g_ref, w_ref, *rest, q_scale, slot):
    k_ref, v_ref, qkv_ref = rest[-3:]
    d = x_ref.shape[-1]
    hd = k_ref.shape[-1]
    h = _rms(x_ref[...], g_ref[...]).astype(BF16)
    qkv = _dot(h, w_ref[...])
    qkv_ref[:, :d] = (qkv[:, :d] * q_scale).astype(BF16)
    qkv_ref[:, d:] = qkv[:, d:].astype(BF16)
    k3 = _split_heads(qkv[:, d:2 * d], hd)
    v3 = _split_heads(qkv[:, 2 * d:], hd)
    own = slot if k_ref.shape[0] > 1 else 0
    for s in range(k_ref.shape[0]):
        if s == own:
            k_ref[s, 0] = k3
            v_ref[s, 0] = v3
        else:
            k_ref[s, 0] = jnp.zeros(k3.shape, k_ref.dtype)
            v_ref[s, 0] = jnp.zeros(v3.shape, v_ref.dtype)


def _sb_qkv(x, batch, gain, w, slot, n_slots, k_all=None, v_all=None):
    rows, d = x.shape
    seq = rows // batch
    tm = min(ROW_TILE, seq)
    nt = seq // tm
    hd = d // SB_HEADS
    assert 2 * hd == LANES
    q_scale = float(hd ** -0.5)
    ins = [x, gain.reshape(1, d), w]
    in_specs = [pl.BlockSpec((tm, d), lambda i: (i, 0)),
                pl.BlockSpec((1, d), lambda i: (0, 0)),
                pl.BlockSpec(w.shape, lambda i: (0, 0))]
    if k_all is None:
        aliases = {}
        slot_spec = pl.BlockSpec((n_slots, 1, tm, SB_HEADS, hd), lambda i: (0, i // nt, i % nt, 0, 0))
    else:
        ins += [k_all, v_all]
        in_specs += [pl.BlockSpec(memory_space=pl.ANY)] * 2
        aliases = {3: 0, 4: 1}
        slot_spec = pl.BlockSpec((1, 1, tm, SB_HEADS, hd), lambda i: (slot, i // nt, i % nt, 0, 0))
    kv_shape = jax.ShapeDtypeStruct((n_slots, batch, seq, SB_HEADS, hd), F32)
    return pl.pallas_call(
        functools.partial(_sb_qkv_kernel, q_scale=q_scale, slot=slot),
        grid=(rows // tm,),
        in_specs=in_specs,
        out_specs=[slot_spec, slot_spec, pl.BlockSpec((tm, 3 * d), lambda i: (i, 0))],
        out_shape=[kv_shape, kv_shape, jax.ShapeDtypeStruct((rows, 3 * d), BF16)],
        input_output_aliases=aliases,
        compiler_params=_params("arbitrary"),
        name="sb_qkv",
    )(*ins)


def _sb_step(q2s, kv_blocks, carry, acc, neg_lower2, mask):
    m = q2s[0].shape[0]
    groups = len(q2s)
    per = max(groups // SB_WAVES, 1)
    waves = [range(c, min(c + per, groups)) for c in range(0, groups, per)]
    rows = [slice(w_[0] * m, (w_[-1] + 1) * m) for w_ in waves]
    units = [(b, c) for b in range(len(kv_blocks)) for c in range(len(waves))]

    def scores(b, c):
        return jnp.concatenate([_dot_nt(q2s[g], kv_blocks[b][0][g]) for g in waves[c]], axis=0)

    def softplus_pieces(c, z):
        sp = jnp.maximum(z, 0.0) + jnp.log(1.0 + jnp.exp(-jnp.abs(z)))
        if mask is not None:
            sp = jnp.where(mask[rows[c]], sp, 0.0)
        return jnp.concatenate(_split(sp, 2), axis=1)

    def weights(c, z, from_s, later):
        w = jnp.exp(z + from_s + jnp.concatenate([later] * (z.shape[1] // LANES), axis=1))
        if mask is not None:
            w = jnp.where(mask[rows[c]], w, 0.0)
        return w.astype(BF16)

    def values(b, c, w):
        return jnp.concatenate([_dot(w[i * m:(i + 1) * m], kv_blocks[b][1][g])
                                for i, g in enumerate(waves[c])], axis=0)

    n = len(units)
    z, pieces, from_s, w = ([None] * n for _ in range(4))
    later = [carry[r] for r in rows]
    pv = [None] * len(waves)
    for tick in range(n + 4):
        for u, (b, c) in enumerate(units):
            stage = tick - u
            if stage == 0:
                z[u] = scores(b, c)
            elif stage == 1:
                pieces[u] = softplus_pieces(c, z[u])
            elif stage == 2:
                from_s[u] = _dot(pieces[u], neg_lower2)
            elif stage == 3:
                w[u] = weights(c, z[u], from_s[u], later[c])
                later[c] = later[c] + jnp.broadcast_to(from_s[u][:, 0:1], later[c].shape)
            elif stage == 4:
                out = values(b, c, w[u])
                pv[c] = out if pv[c] is None else pv[c] + out
    return jnp.concatenate(later, axis=0), acc + jnp.concatenate(pv, axis=0)


def _sb_consts(bk):
    r = lax.broadcasted_iota(jnp.int32, (2, bk, bk), 1).reshape(2 * bk, bk)
    c = lax.broadcasted_iota(jnp.int32, (2 * bk, bk), 1)
    return jnp.where(r >= c, -1.0, 0.0).astype(BF16)


def _sb_stack_heads(q_ref, q2_ref):
    tq = q_ref.shape[0]
    lane = lax.broadcasted_iota(jnp.int32, (tq, LANES), 1)
    for g in range(q2_ref.shape[0]):
        q = q_ref[:, g * LANES:(g + 1) * LANES]
        zero = jnp.zeros_like(q)
        q2_ref[g] = jnp.concatenate([jnp.where(lane < LANES // 2, q, zero),
                                     jnp.where(lane >= LANES // 2, q, zero)], axis=0)


def _sb_causal_mask(groups, tq, bk):
    t = lax.broadcasted_iota(jnp.int32, (2 * groups, tq, bk), 1).reshape(2 * groups * tq, bk)
    s = lax.broadcasted_iota(jnp.int32, (2 * groups * tq, bk), 1)
    return s < t


def _sb_write(o_ref, acc_ref):
    tq = o_ref.shape[0]
    lane = lax.broadcasted_iota(jnp.int32, (tq, LANES), 1)
    for g in range(acc_ref.shape[0] // (2 * tq)):
        o_ref[:, g * LANES:(g + 1) * LANES] = jnp.where(
            lane < LANES // 2, acc_ref[2 * g * tq:(2 * g + 1) * tq, :],
            acc_ref[(2 * g + 1) * tq:(2 * g + 2) * tq, :]).astype(o_ref.dtype)


def _sb_prompt_kernel(q_ref, k_ref, v_ref, o_ref, q2_ref, carry_ref, acc_ref):
    blk = q_ref.shape[0]
    groups = q2_ref.shape[0]
    i = pl.program_id(2)
    neg_lower2 = _sb_consts(blk)
    _sb_stack_heads(q_ref, q2_ref)
    lanes = [slice(g * LANES, (g + 1) * LANES) for g in range(groups)]

    def block(j):
        row0 = pl.multiple_of(j * blk, blk)
        return ([k_ref[pl.ds(row0, blk), sl] for sl in lanes], [v_ref[pl.ds(row0, blk), sl] for sl in lanes])

    def step(blocks, carry, acc, mask):
        return _sb_step([q2_ref[g] for g in range(groups)], blocks, carry, acc, neg_lower2, mask)

    zero = jnp.zeros(carry_ref.shape, F32)
    carry_ref[...], acc_ref[...] = step([block(i)], zero, zero, _sb_causal_mask(groups, blk, blk))

    odd = i % SB_UNROLL

    def single(it, c):
        carry_ref[...], acc_ref[...] = step([block(i - 1 - it)], carry_ref[...], acc_ref[...], None)
        return c

    def multi(it, c):
        top = i - 1 - odd - it * SB_UNROLL
        carry_ref[...], acc_ref[...] = step([block(top - u) for u in range(SB_UNROLL)],
                                            carry_ref[...], acc_ref[...], None)
        return c

    lax.fori_loop(0, odd, single, 0)
    lax.fori_loop(0, i // SB_UNROLL, multi, 0)
    _sb_write(o_ref, acc_ref)


def _sb_attention_prompt(qkv, batch):
    rows, d3 = qkv.shape
    d = d3 // 3
    seq = rows // batch
    blk = min(SB_BLOCK, seq)
    nq = seq // blk
    width = SB_PAIRS * LANES
    ng = d // width
    return pl.pallas_call(
        _sb_prompt_kernel,
        grid=(batch, ng, nq),
        in_specs=[pl.BlockSpec((blk, width), lambda b, p, i: (b * nq + i, p)),
                  pl.BlockSpec((seq, width), lambda b, p, i: (b, ng + p)),
                  pl.BlockSpec((seq, width), lambda b, p, i: (b, 2 * ng + p))],
        out_specs=pl.BlockSpec((blk, width), lambda b, p, i: (b * nq + i, p)),
        out_shape=jax.ShapeDtypeStruct((rows, d), BF16),
        scratch_shapes=[pltpu.VMEM((SB_PAIRS, 2 * blk, LANES), BF16),
                        pltpu.VMEM((SB_PAIRS * 2 * blk, LANES), F32),
                        pltpu.VMEM((SB_PAIRS * 2 * blk, LANES), F32)],
        compiler_params=_params("arbitrary", "arbitrary", "arbitrary"),
        name="sb_attention_prompt",
    )(qkv, qkv, qkv)


def _sb_sample_kernel(q_ref, kn_ref, vn_ref, kp_ref, vp_ref, o_ref, q2_ref, carry_ref, acc_ref, *, blk):
    tq = q_ref.shape[0]
    groups = q2_ref.shape[0]
    c = pl.program_id(1)
    neg_lower2 = _sb_consts(blk)

    lanes = [slice(g * LANES, (g + 1) * LANES) for g in range(groups)]

    @pl.when(c == 0)
    def _():
        _sb_stack_heads(q_ref, q2_ref)
        zero = jnp.zeros(carry_ref.shape, F32)
        carry_ref[...], acc_ref[...] = _sb_step(
            [q2_ref[g] for g in range(groups)],
            [([kn_ref[0, :, sl] for sl in lanes], [vn_ref[0, :, sl] for sl in lanes])],
            zero, zero, neg_lower2, _sb_causal_mask(groups, tq, blk))

    blocks = []
    for sb in reversed(range(kp_ref.shape[2] // blk)):
        rows = slice(sb * blk, (sb + 1) * blk)
        blocks.append(([kp_ref[0, 0, rows, sl].astype(BF16) for sl in lanes],
                       [vp_ref[0, 0, rows, sl].astype(BF16) for sl in lanes]))
    carry_ref[...], acc_ref[...] = _sb_step([q2_ref[g] for g in range(groups)], blocks,
                                            carry_ref[...], acc_ref[...], neg_lower2, None)

    @pl.when(c == pl.num_programs(1) - 1)
    def _():
        _sb_write(o_ref, acc_ref)


def _sb_attention_sample(qkv, batch, past_k, past_v, layer):
    rows, d3 = qkv.shape
    d = d3 // 3
    seq = rows // batch
    blk = SB_BLOCK
    pairs = d // LANES
    past_len = past_k.shape[2]
    span = min(SB_PAST_SPAN, past_len)
    steps = past_len // span
    assert seq <= blk and past_len % span == 0 and span % blk == 0
    new = qkv[:, d:].reshape(batch, seq, 2 * d)
    new = jnp.pad(new, ((0, 0), (0, blk - seq), (0, 0)))
    return pl.pallas_call(
        functools.partial(_sb_sample_kernel, blk=blk),
        grid=(batch, steps),
        in_specs=[pl.BlockSpec((seq, d), lambda b, c: (b, 0)),
                  pl.BlockSpec((1, blk, d), lambda b, c: (b, 0, 0)),
                  pl.BlockSpec((1, blk, d), lambda b, c: (b, 0, 1)),
                  pl.BlockSpec((1, 1, span, d), lambda b, c: (layer, b, steps - 1 - c, 0)),
                  pl.BlockSpec((1, 1, span, d), lambda b, c: (layer, b, steps - 1 - c, 0))],
        out_specs=pl.BlockSpec((seq, d), lambda b, c: (b, 0)),
        out_shape=jax.ShapeDtypeStruct((rows, d), BF16),
        scratch_shapes=[pltpu.VMEM((pairs, 2 * seq, LANES), BF16),
                        pltpu.VMEM((pairs * 2 * seq, LANES), F32),
                        pltpu.VMEM((pairs * 2 * seq, LANES), F32)],
        compiler_params=_params("arbitrary", "arbitrary"),
        name="sb_attention_sample",
    )(qkv, new, new, past_k, past_v)


def _linear_residual_kernel(a_ref, w_ref, x_ref, o_ref):
    o_ref[...] = x_ref[...] + _dot(a_ref[...], w_ref[...])


def _linear_residual(a, w, x):
    rows, d = x.shape
    tm = min(WIDE_ROW_TILE, rows)
    return pl.pallas_call(
        _linear_residual_kernel,
        grid=(rows // tm,),
        in_specs=[pl.BlockSpec((tm, a.shape[1]), lambda i: (i, 0)),
                  pl.BlockSpec(w.shape, lambda i: (0, 0)),
                  pl.BlockSpec((tm, d), lambda i: (i, 0))],
        out_specs=pl.BlockSpec((tm, d), lambda i: (i, 0)),
        out_shape=jax.ShapeDtypeStruct((rows, d), F32),
        input_output_aliases={2: 0},
        compiler_params=_params("arbitrary"),
        name="linear_residual",
    )(a, w, x)


def _xattn_kernel(x_ref, g_ref, wq_ref, wo_ref, mk_ref, mv_ref, o_ref, *, heads):
    x = x_ref[...]
    d = x.shape[-1]
    dh = d // heads
    h = _rms(x, g_ref[...]).astype(BF16)
    q = _dot(h, wq_ref[...]).astype(BF16)
    outs = []
    for hd in range(heads):
        sl = slice(hd * dh, (hd + 1) * dh)
        s = _dot_nt(q[:, sl], mk_ref[0, :, sl]) * (dh ** -0.5)
        e = jnp.exp(s - jnp.max(s, axis=-1, keepdims=True))
        p = e / jnp.sum(e, axis=-1, keepdims=True)
        outs.append(_dot(p.astype(BF16), mv_ref[0, :, sl]).astype(BF16))
    o = jnp.concatenate(outs, axis=-1)
    o_ref[...] = x + _dot(o, wo_ref[...])


def _xattn(x, gain, wq, wo, mem_k, mem_v):
    rows, d = x.shape
    batch, n_mem, _ = mem_k.shape
    seq = rows // batch
    tm = min(WIDE_ROW_TILE, seq)
    nt = seq // tm
    return pl.pallas_call(
        functools.partial(_xattn_kernel, heads=XA_HEADS),
        grid=(batch, nt),
        in_specs=[pl.BlockSpec((tm, d), lambda b, i: (b * nt + i, 0)),
                  pl.BlockSpec((1, d), lambda b, i: (0, 0)),
                  pl.BlockSpec((d, d), lambda b, i: (0, 0)),
                  pl.BlockSpec((d, d), lambda b, i: (0, 0)),
                  pl.BlockSpec((1, n_mem, d), lambda b, i: (b, 0, 0)),
                  pl.BlockSpec((1, n_mem, d), lambda b, i: (b, 0, 0))],
        out_specs=pl.BlockSpec((tm, d), lambda b, i: (b * nt + i, 0)),
        out_shape=jax.ShapeDtypeStruct((rows, d), F32),
        input_output_aliases={0: 0},
        compiler_params=_params("arbitrary", "arbitrary"),
        name="xattn",
    )(x, gain.reshape(1, d), wq, wo, mem_k, mem_v)


def _ffn_kernel(x_ref, g_ref, wgu_ref, wd_ref, o_ref, *, final_gain):
    x = x_ref[...]
    hidden = wd_ref.shape[0]
    h = _rms(x, g_ref[0:1, :]).astype(BF16)
    acc = x
    for c0 in range(0, hidden, FFN_CHUNK):
        c1 = min(c0 + FFN_CHUNK, hidden)
        gate = _dot(h, wgu_ref[:, c0:c1])
        up = _dot(h, wgu_ref[:, hidden + c0:hidden + c1])
        act = (gate * jax.nn.sigmoid(gate) * up).astype(BF16)
        acc = acc + _dot(act, wd_ref[c0:c1, :])
    if final_gain:
        acc = _rms(acc, g_ref[1:2, :])
    o_ref[...] = acc


def _ffn(x, gain, w_gate_up, w_down, final_gain=None):
    rows, d = x.shape
    tm = min(WIDE_ROW_TILE, rows)
    gains = jnp.stack([gain, gain if final_gain is None else final_gain])
    return pl.pallas_call(
        functools.partial(_ffn_kernel, final_gain=final_gain is not None),
        grid=(rows // tm,),
        in_specs=[pl.BlockSpec((tm, d), lambda i: (i, 0)),
                  pl.BlockSpec((2, d), lambda i: (0, 0)),
                  pl.BlockSpec(w_gate_up.shape, lambda i: (0, 0)),
                  pl.BlockSpec(w_down.shape, lambda i: (0, 0))],
        out_specs=pl.BlockSpec((tm, d), lambda i: (i, 0)),
        out_shape=jax.ShapeDtypeStruct((rows, d), F32),
        input_output_aliases={0: 0},
        compiler_params=_params("arbitrary"),
        name="ffn",
    )(x, gains, w_gate_up, w_down)


def _rw_proj_kernel(*refs, has_v_res):
    if has_v_res:
        (x_ref, gain_ref, shift0_ref, mu_ref, wr_ref, wk_ref, wv_ref, vec_ref,
         w1_ref, w2_ref, a1_ref, a2_ref, g1_ref, g2_ref, v1_ref, v2_ref, vfirst_ref,
         r_ref, lw_ref, k_ref, v_ref, kk_ref, a_ref, g_ref, shift_ref, prev_ref) = refs
    else:
        (x_ref, gain_ref, shift0_ref, mu_ref, wr_ref, wk_ref, wv_ref, vec_ref,
         w1_ref, w2_ref, a1_ref, a2_ref, g1_ref, g2_ref,
         r_ref, lw_ref, k_ref, v_ref, kk_ref, a_ref, g_ref, shift_ref, prev_ref) = refs
    i = pl.program_id(1)
    h = _rms(x_ref[...], gain_ref[...])
    tm = h.shape[0]

    @pl.when(i == 0)
    def _():
        prev_ref[...] = shift0_ref[0]

    row = lax.broadcasted_iota(jnp.int32, h.shape, 0)
    h_prev = jnp.where(row == 0, prev_ref[...], pltpu.roll(h, 1, 0))
    last = h[tm - 1:tm, :]
    prev_ref[...] = last
    shift_ref[0] = last

    xx = h_prev - h
    x_r, x_w, x_k, x_v, x_a, x_g = ((h + xx * mu_ref[j:j + 1, :]).astype(BF16) for j in range(6))
    w0, a0, k_k, k_a, v0 = (vec_ref[j:j + 1, :] for j in range(5))

    r = _dot(x_r, wr_ref[...])
    k = _dot(x_k, wk_ref[...])
    v = _dot(x_v, wv_ref[...])
    dec = w0 + _dot(jnp.tanh(_dot(x_w, w1_ref[...])).astype(BF16), w2_ref[...])
    w_log = -jax.nn.softplus(-dec) - 0.5
    if has_v_res:
        mix = jax.nn.sigmoid(v0 + _dot(_dot(x_v, v1_ref[...]).astype(BF16), v2_ref[...]))
        v = v + (vfirst_ref[...] - v) * mix
    a = jax.nn.sigmoid(a0 + _dot(_dot(x_a, a1_ref[...]).astype(BF16), a2_ref[...]))
    g = _dot(jax.nn.sigmoid(_dot(x_g, g1_ref[...])).astype(BF16), g2_ref[...])

    r_ref[...] = r
    lw_ref[...] = -jnp.exp(w_log)
    kk_ref[...] = k * k_k
    k_ref[...] = k * (1.0 + (a - 1.0) * k_a)
    v_ref[...] = v
    a_ref[...] = a
    g_ref[...] = g


def _pad_cols(w, mult=LANES):
    pad = -w.shape[1] % mult
    return jnp.pad(w, ((0, 0), (0, pad))) if pad else w


def _pad_rows(w, mult=LANES):
    pad = -w.shape[0] % mult
    return jnp.pad(w, ((0, pad), (0, 0))) if pad else w


def _rw_proj(x, batch, gain, shift0, p, v_first):
    rows, d = x.shape
    seq = rows // batch
    tm = min(ROW_TILE, seq)
    nt = seq // tm
    has_v_res = v_first is not None
    zeros = jnp.zeros((d,), F32)
    vec = jnp.stack([p["w0"], p["a0"], p["k_k"], p["k_a"], p["v0"] if has_v_res else zeros,
                     zeros, zeros, zeros])
    row_spec = pl.BlockSpec((tm, d), lambda b, i: (b * nt + i, 0))

    def full(arr):
        return pl.BlockSpec(arr.shape, lambda b, i: (0,) * arr.ndim)

    ins = [x, gain.reshape(1, d), shift0.reshape(batch, 1, d), p["mu"], p["w_r"], p["w_k"], p["w_v"], vec,
           p["w1"], p["w2"], p["a1"], p["a2"], p["g1"], p["g2"]]
    in_specs = [row_spec, full(ins[1]), pl.BlockSpec((1, 1, d), lambda b, i: (b, 0, 0))]
    in_specs += [full(a) for a in ins[3:]]
    if has_v_res:
        ins += [p["v1"], p["v2"], v_first]
        in_specs += [full(p["v1"]), full(p["v2"]), row_spec]
    tok = jax.ShapeDtypeStruct((rows, d), F32)
    outs = pl.pallas_call(
        functools.partial(_rw_proj_kernel, has_v_res=has_v_res),
        grid=(batch, nt),
        in_specs=in_specs,
        out_specs=[row_spec] * 7 + [pl.BlockSpec((1, 1, d), lambda b, i: (b, 0, 0))],
        out_shape=[tok] * 7 + [jax.ShapeDtypeStruct((batch, 1, d), F32)],
        scratch_shapes=[pltpu.VMEM((1, d), F32)],
        compiler_params=_params("arbitrary", "arbitrary"),
        name="rw_proj",
    )(*ins)
    return outs[:7], outs[7].reshape(batch, d)


WKV_DECAY_PIECES = 3
WKV_GROUP_PIECES = 2
WKV_STATE_PASSES = 1
WKV_SOLVE_PASSES = 1


def _wkv_block(pairs, c_):
    n = pairs[0][0].shape[0]
    inv_hd = 1.0 / RW_HEAD_DIM
    sum_l = lambda x: _mm_exact_lhs(c_["tri_incl"], x, WKV_DECAY_PIECES)
    sum_g = lambda x: _mm_exact_rhs(x, c_["gsum"], WKV_GROUP_PIECES)
    solve_mm = lambda x, y, nt=False: _mm(x, y, WKV_SOLVE_PASSES, nt=nt)
    np_ = len(pairs)

    cs = [sum_l(p[1]) for p in pairs]
    head_sums = [sum_g(jnp.concatenate([p[4] * p[4], p[0] * p[2] * p[7][0:1, :]], axis=0)) for p in pairs]
    kk_sq = [hs[:n] for hs in head_sums]
    pre = []
    for (r, lw, k, v, kk, a, g, vec, state), cs_p, sq in zip(pairs, cs, kk_sq):
        cs_last = cs_p[n - 1:n, :]
        d_inv = jnp.exp(-cs_p)
        d_rest = jnp.exp(cs_last - cs_p)
        kkn = kk * lax.rsqrt(jnp.maximum(sq, 1e-24))
        b = kkn * a
        pre.append(dict(a_t=-kkn * jnp.exp(cs_p - lw), r_t=r * jnp.exp(cs_p), b_t=b * d_inv, k_t=k * d_inv,
                        bk_rest=jnp.concatenate([b * d_rest, k * d_rest], axis=0), decay=jnp.exp(cs_last)))

    from_state = [_mm(jnp.concatenate([q["a_t"], q["r_t"]], axis=0), p[8], WKV_STATE_PASSES, nt=True)
                  for p, q in zip(pairs, pre)]
    chains = [(i, m) for i in range(np_) for m in c_["head_masks"]]
    grams = []
    for i, m in chains:
        q = pre[i]
        zero = jnp.zeros_like(q["a_t"])
        lhs = jnp.concatenate([jnp.where(m, q["a_t"], zero), jnp.where(m, q["r_t"], zero)], axis=0)
        grams.append(solve_mm(lhs, jnp.concatenate([q["b_t"], q["k_t"]], axis=0), nt=True))
    lows = [jnp.where(c_["strict"], gm[:n, :n], 0.0) for gm in grams]
    m_aks = [jnp.where(c_["strict_k"], gm[:n], 0.0) for gm in grams]
    m_rs = [jnp.where(c_["incl2"], gm[n:], 0.0) for gm in grams]
    rhs = [from_state[i][:n] + solve_mm(mk, jnp.concatenate([jnp.zeros_like(pairs[i][3]), pairs[i][3]], axis=0))
           for (i, _), mk in zip(chains, m_aks)]
    sig = [x + solve_mm(lo, x) for lo, x in zip(lows, rhs)]
    pw = lows
    step = 1
    while 2 * step < n:
        pw = [solve_mm(x, x) for x in pw]
        sig = [x + solve_mm(p2, x) for p2, x in zip(pw, sig)]
        step *= 2
    ys = [from_state[i][n:] + solve_mm(mr, jnp.concatenate([s, pairs[i][3]], axis=0))
          for (i, _), mr, s in zip(chains, m_rs, sig)]

    first = c_["head_masks"][0]
    sig_p = [jnp.where(first, sig[2 * i], sig[2 * i + 1]) for i in range(np_)]
    y_p = [jnp.where(first, ys[2 * i], ys[2 * i + 1]) for i in range(np_)]
    upd = [_mm(jnp.concatenate([s, p[3]], axis=0).T, q["bk_rest"], WKV_STATE_PASSES)
           for s, p, q in zip(sig_p, pairs, pre)]
    new_states = [p[8] * q["decay"] + jnp.where(c_["same_head"], u, 0.0) for p, q, u in zip(pairs, pre, upd)]

    mean = [sum_g(y) * inv_hd for y in y_p]
    dev = [y - m for y, m in zip(y_p, mean)]
    var = [sum_g(dv * dv) * inv_hd for dv in dev]
    bonus = [hs[n:] * p[3] for hs, p in zip(head_sums, pairs)]
    outs = []
    for p, dv, vr, bo in zip(pairs, dev, var, bonus):
        vec = p[7]
        o = dv * lax.rsqrt(vr + RW_GN_EPS) * vec[1:2, :] + vec[2:3, :]
        outs.append((o + bo) * p[6])
    return list(zip(outs, new_states))


def _wkv_kernel(*refs, has_state0):
    if has_state0:
        (r_ref, lw_ref, k_ref, v_ref, kk_ref, a_ref, g_ref, vec_ref, s0_ref,
         o_ref, sout_ref, state_ref) = refs
    else:
        (r_ref, lw_ref, k_ref, v_ref, kk_ref, a_ref, g_ref, vec_ref,
         o_ref, sout_ref, state_ref) = refs
    c = pl.program_id(2)
    n = r_ref.shape[0]
    hd = RW_HEAD_DIM

    @pl.when(c == 0)
    def _():
        if has_state0:
            state_ref[...] = s0_ref[0]
        else:
            state_ref[...] = jnp.zeros_like(state_ref)

    tr = lax.broadcasted_iota(jnp.int32, (n, n), 0)
    tc = lax.broadcasted_iota(jnp.int32, (n, n), 1)
    tr2 = lax.broadcasted_iota(jnp.int32, (n, 2 * n), 0)
    tc2 = lax.broadcasted_iota(jnp.int32, (n, 2 * n), 1)
    lane = lax.broadcasted_iota(jnp.int32, (n, LANES), 1)
    sr = lax.broadcasted_iota(jnp.int32, (LANES, LANES), 0) // hd
    sc = lax.broadcasted_iota(jnp.int32, (LANES, LANES), 1) // hd
    consts = dict(
        gsum=_group_ones(LANES, hd),
        tri_incl=jnp.where(tc <= tr, 1.0, 0.0).astype(BF16),
        strict=tc < tr,
        strict_k=(tc2 >= n) & (tc2 - n < tr2),
        incl2=jnp.where(tc2 >= n, tc2 - n, tc2) <= tr2,
        eye=jnp.where(tc == tr, 1.0, 0.0).astype(F32),
        head_masks=[lane < hd, lane >= hd],
        same_head=sr == sc,
    )
    pairs = []
    for p in range(state_ref.shape[0]):
        sl = slice(p * LANES, (p + 1) * LANES)
        pairs.append((r_ref[:, sl], lw_ref[:, sl], k_ref[:, sl], v_ref[:, sl], kk_ref[:, sl],
                      a_ref[:, sl], g_ref[:, sl], vec_ref[:, sl], state_ref[p]))
    results = _wkv_block(pairs, consts)
    o_ref[...] = jnp.concatenate([o for o, _ in results], axis=1).astype(o_ref.dtype)
    new_states = jnp.stack([s for _, s in results])
    state_ref[...] = new_states
    sout_ref[0] = new_states


def _wkv(tok, batch, vec, state0):
    rows, d = tok[0].shape
    seq = rows // batch
    n = min(WKV_CHUNK, seq)
    nc = seq // n
    pairs = d // LANES
    gp = WKV_PAIRS
    width = gp * LANES
    has_state0 = state0 is not None
    tok_spec = pl.BlockSpec((n, width), lambda b, p, c: (b * nc + c, p))
    state_spec = pl.BlockSpec((1, gp, LANES, LANES), lambda b, p, c: (b, p, 0, 0))
    ins = list(tok) + [vec]
    in_specs = [tok_spec] * 7 + [pl.BlockSpec((8, width), lambda b, p, c: (0, p))]
    if has_state0:
        ins.append(state0)
        in_specs.append(state_spec)
    return pl.pallas_call(
        functools.partial(_wkv_kernel, has_state0=has_state0),
        grid=(batch, pairs // gp, nc),
        in_specs=in_specs,
        out_specs=[tok_spec, state_spec],
        out_shape=[jax.ShapeDtypeStruct((rows, d), BF16),
                   jax.ShapeDtypeStruct((batch, pairs, LANES, LANES), F32)],
        scratch_shapes=[pltpu.VMEM((gp, LANES, LANES), F32)],
        compiler_params=_params("arbitrary", "arbitrary", "arbitrary"),
        name="wkv",
    )(*ins)


def _state_to_blockdiag(state):
    b, h, n, _ = state.shape
    s = state.reshape(b, h // 2, 2, n, n)
    z = jnp.zeros_like(s[:, :, 0])
    top = jnp.concatenate([s[:, :, 0], z], axis=-1)
    bot = jnp.concatenate([z, s[:, :, 1]], axis=-1)
    return jnp.concatenate([top, bot], axis=-2)


def _state_from_blockdiag(bd):
    b, p, n2, _ = bd.shape
    n = n2 // 2
    return jnp.stack([bd[:, :, :n, :n], bd[:, :, n:, n:]], axis=2).reshape(b, 2 * p, n, n)


def _trunk(x, batch, mem_k, mem_v, past_k, past_v, rw_state0, rw_shift0, w):
    depth = len(w["ffn_gu"])
    d = x.shape[-1]
    sb_k, sb_v, rw_states, rw_shifts = None, None, [], []
    n_sb = (depth + 1) // 2
    v_first = None
    for layer in range(depth):
        j = layer // 2
        if layer % 2 == 0:
            sb_k, sb_v, qkv = _sb_qkv(x, batch, w["norm_mix"][layer], w["sb_qkv"][j], j, n_sb, sb_k, sb_v)
            if past_k is None:
                o = _sb_attention_prompt(qkv, batch)
            else:
                o = _sb_attention_sample(qkv, batch, past_k, past_v, j)
            x = _linear_residual(o, w["sb_o"][j], x)
        else:
            if rw_state0 is None:
                state0 = None
                shift0 = jnp.zeros((batch, d), F32)
            else:
                state0 = _state_to_blockdiag(rw_state0[j])
                shift0 = rw_shift0[j]
            tok, shift = _rw_proj(x, batch, w["norm_mix"][layer], shift0, w["rw"][j], v_first)
            if v_first is None:
                v_first = tok[3]
            o, state = _wkv(tok, batch, w["rw"][j]["out_vec"], state0)
            rw_states.append(_state_from_blockdiag(state))
            rw_shifts.append(shift)
            x = _linear_residual(o, w["rw"][j]["w_o"], x)
        x = _xattn(x, w["norm_xattn"][layer], w["xa_q"][layer], w["xa_o"][layer], mem_k[layer], mem_v[layer])
        final = w["norm_final"] if layer == depth - 1 else None
        x = _ffn(x, w["norm_ffn"][layer], w["ffn_gu"][layer], w["ffn_down"][layer], final)
    return x, sb_k, sb_v, jnp.stack(rw_states), jnp.stack(rw_shifts)


def kernel(x_prompt, x_sample, mem_prompt, cache_sb_k, cache_sb_v, state_rwkv_wkv, state_rwkv_shift,
           cache_mem_k, cache_mem_v, norm_mix, norm_xattn, norm_ffn, norm_mem, norm_final,
           sb_w_qkv, sb_w_o, rw_mu, rw_w_r, rw_w_k, rw_w_v, rw_w_o, rw_w0, rw_w1, rw_w2,
           rw_a0, rw_a1, rw_a2, rw_v0, rw_v1, rw_v2, rw_g1, rw_g2, rw_k_k, rw_k_a, rw_r_k,
           rw_lnx_w, rw_lnx_b, xa_w_q, xa_w_kv, xa_w_o, ffn_w_gate_up, ffn_w_down):
    batch, seq, d = x_prompt.shape
    dec_batch, dec_seq, _ = x_sample.shape
    depth = norm_mix.shape[0]
    n_rw = rw_w_r.shape[0]
    n_sb = sb_w_qkv.shape[0]
    n_mem = mem_prompt.shape[1]
    bf = lambda t: t.astype(BF16)

    rw = []
    for j in range(n_rw):
        pj = dict(mu=rw_mu[j], w_r=bf(rw_w_r[j]), w_k=bf(rw_w_k[j]), w_v=bf(rw_w_v[j]), w_o=bf(rw_w_o[j]),
                  w0=rw_w0[j], a0=rw_a0[j], k_k=rw_k_k[j], k_a=rw_k_a[j],
                  w1=bf(_pad_cols(rw_w1[j])), w2=bf(_pad_rows(rw_w2[j])),
                  a1=bf(_pad_cols(rw_a1[j])), a2=bf(_pad_rows(rw_a2[j])),
                  g1=bf(_pad_cols(rw_g1[j])), g2=bf(_pad_rows(rw_g2[j])))
        if j > 0:
            pj.update(v0=rw_v0[j - 1], v1=bf(_pad_cols(rw_v1[j - 1])), v2=bf(_pad_rows(rw_v2[j - 1])))
        zeros = jnp.zeros((d,), F32)
        pj["out_vec"] = jnp.stack([rw_r_k[j].reshape(d), rw_lnx_w[j], rw_lnx_b[j]] + [zeros] * 5)
        rw.append(pj)
    w = dict(norm_mix=norm_mix, norm_xattn=norm_xattn, norm_ffn=norm_ffn, norm_final=norm_final,
             sb_qkv=bf(sb_w_qkv), sb_o=bf(sb_w_o), rw=rw, xa_q=bf(xa_w_q), xa_o=bf(xa_w_o),
             ffn_gu=bf(ffn_w_gate_up), ffn_down=bf(ffn_w_down))

    mem_k, mem_v = _mem_kv(mem_prompt.reshape(batch * n_mem, d), norm_mem, bf(xa_w_kv))
    mem_shape = (depth, batch, n_mem, XA_HEADS, d // XA_HEADS)
    y_p, p_sb_k, p_sb_v, p_wkv, p_shift = _trunk(
        x_prompt.reshape(batch * seq, d), batch,
        bf(mem_k).reshape(depth, batch, n_mem, d), bf(mem_v).reshape(depth, batch, n_mem, d),
        None, None, None, None, w)

    past_len = cache_sb_k.shape[2]
    y_s, s_sb_k, s_sb_v, s_wkv, s_shift = _trunk(
        x_sample.reshape(dec_batch * dec_seq, d), dec_batch,
        bf(cache_mem_k).reshape(depth, dec_batch, n_mem, d), bf(cache_mem_v).reshape(depth, dec_batch, n_mem, d),
        cache_sb_k.reshape(n_sb, dec_batch, past_len, d), cache_sb_v.reshape(n_sb, dec_batch, past_len, d),
        state_rwkv_wkv, state_rwkv_shift, w)

    hd = d // SB_HEADS
    return (y_p.reshape(batch, seq, d), y_s.reshape(dec_batch, dec_seq, d),
            p_sb_k.reshape(n_sb, batch, seq, SB_HEADS, hd), p_sb_v.reshape(n_sb, batch, seq, SB_HEADS, hd),
            p_wkv, p_shift,
            mem_k.reshape(mem_shape), mem_v.reshape(mem_shape),
            s_sb_k.reshape(n_sb, dec_batch, dec_seq, SB_HEADS, hd),
            s_sb_v.reshape(n_sb, dec_batch, dec_seq, SB_HEADS, hd),
            s_wkv, s_shift)
```

```python
import functools

import jax
import jax.numpy as jnp
from jax import lax
from jax.experimental import pallas as pl
from jax.experimental.pallas import tpu as pltpu

F32 = jnp.float32
BF16 = jnp.bfloat16

NORM_EPS = 1e-6
RW_GN_EPS = 64e-5
SB_HEADS = 16
XA_HEADS = 4
RW_HEAD_DIM = 64
LANES = 128
ROW_TILE = 256
WIDE_ROW_TILE = 512
SB_BLOCK = 256
SB_PAIRS = 4
SB_WAVES = 2
SB_UNROLL = 2
SB_PAST_SPAN = 1024
WKV_CHUNK = 64
WKV_PAIRS = 8
FFN_CHUNK = 256
VMEM_LIMIT_BYTES = 56 * 1024 * 1024


def _params(*sem):
    return pltpu.CompilerParams(dimension_semantics=sem, vmem_limit_bytes=VMEM_LIMIT_BYTES)


def _dot(a, b):
    return jnp.dot(a, b, preferred_element_type=F32)


def _dot_nt(a, b):
    return lax.dot_general(a, b, (((1,), (1,)), ((), ())), preferred_element_type=F32)


def _split(x, n):
    parts = []
    for i in range(n):
        p = x.astype(BF16)
        parts.append(p)
        if i + 1 < n:
            x = x - p.astype(F32)
    return parts


def _mm(a, b, passes, nt=False):
    dot = _dot_nt if nt else _dot
    if passes == 1:
        return dot(a.astype(BF16), b.astype(BF16))
    a_hi, a_lo = _split(a, 2)
    b_hi, b_lo = _split(b, 2)
    return dot(a_hi, b_hi) + (dot(a_hi, b_lo) + dot(a_lo, b_hi))


def _mm_exact_rhs(a, b_bf16, n):
    out = None
    for p in _split(a, n):
        t = _dot(p, b_bf16)
        out = t if out is None else out + t
    return out


def _mm_exact_lhs(a_bf16, b, n):
    out = None
    for p in _split(b, n):
        t = _dot(a_bf16, p)
        out = t if out is None else out + t
    return out


def _rms(x, gain):
    ms = jnp.mean(x * x, axis=-1, keepdims=True)
    return x * lax.rsqrt(ms + NORM_EPS) * gain


def _group_ones(n, group):
    r = lax.broadcasted_iota(jnp.int32, (n, n), 0) // group
    c = lax.broadcasted_iota(jnp.int32, (n, n), 1) // group
    return jnp.where(r == c, 1.0, 0.0).astype(BF16)


def _mem_kv_kernel(x_ref, g_ref, w_ref, k_ref, v_ref, kb_ref, vb_ref):
    tm, d = x_ref.shape
    heads, dh = k_ref.shape[-2:]
    h = _rms(x_ref[...], g_ref[0]).astype(BF16)
    kv = _dot(h, w_ref[0])
    k, v = kv[:, :d], kv[:, d:]
    k_ref[0, 0] = k.reshape(tm, heads, dh)
    v_ref[0, 0] = v.reshape(tm, heads, dh)
    kb_ref[0, 0] = k.astype(BF16)
    vb_ref[0, 0] = v.astype(BF16)


def _mem_kv(mem, gains, w_kv):
    depth = w_kv.shape[0]
    batch, n_mem, d = mem.shape
    tm = min(ROW_TILE, n_mem)
    nt = n_mem // tm
    dh = d // XA_HEADS
    assert n_mem % tm == 0 and dh % LANES == 0
    out = jax.ShapeDtypeStruct((depth, batch, n_mem, XA_HEADS, dh), F32)
    out_b = jax.ShapeDtypeStruct((depth, batch, n_mem, d), BF16)
    spec = pl.BlockSpec((1, 1, tm, XA_HEADS, dh), lambda l, i: (l, i // nt, i % nt, 0, 0))
    spec_b = pl.BlockSpec((1, 1, tm, d), lambda l, i: (l, i // nt, i % nt, 0))
    return pl.pallas_call(
        _mem_kv_kernel,
        grid=(depth, batch * nt),
        in_specs=[pl.BlockSpec((tm, d), lambda l, i: (i, 0)),
                  pl.BlockSpec((1, 1, d), lambda l, i: (l, 0, 0)),
                  pl.BlockSpec((1, d, 2 * d), lambda l, i: (l, 0, 0))],
        out_specs=[spec, spec, spec_b, spec_b],
        out_shape=[out, out, out_b, out_b],
        compiler_params=_params("arbitrary", "arbitrary"),
        name="mem_kv",
    )(mem.reshape(batch * n_mem, d), gains.reshape(depth, 1, d), w_kv)


def _split_heads(x, hd):
    tm, d = x.shape
    blocks = []
    for p in range(d // LANES):
        slab = x[:, p * LANES:(p + 1) * LANES]
        blocks += [slab, pltpu.roll(slab, LANES - hd, 1)]
    return jnp.concatenate(blocks, axis=1).reshape(tm, d // hd, LANES)[:, :, :hd]


def _sb_qkv_kernel(x_ref, g_ref, w_ref, *rest, q_scale, slot):
    k_ref, v_ref, qkv_ref = rest[-3:]
    d = x_ref.shape[-1]
    hd = k_ref.shape[-1]
    h = _rms(x_ref[...], g_ref[...]).astype(BF16)
    qkv = _dot(h, w_ref[...])
    qkv_ref[:, :d] = (qkv[:, :d] * q_scale).astype(BF16)
    qkv_ref[:, d:] = qkv[:, d:].astype(BF16)
    k3 = _split_heads(qkv[:, d:2 * d], hd)
    v3 = _split_heads(qkv[:, 2 * d:], hd)
    own = slot if k_ref.shape[0] > 1 else 0
    for s in range(k_ref.shape[0]):
        if s == own:
            k_ref[s, 0] = k3
            v_ref[s, 0] = v3
        else:
            k_ref[s, 0] = jnp.zeros(k3.shape, k_ref.dtype)
            v_ref[s, 0] = jnp.zeros(v3.shape, v_ref.dtype)


def _sb_qkv(x, batch, gain, w, slot, n_slots, k_all=None, v_all=None):
    rows, d = x.shape
    seq = rows // batch
    tm = min(ROW_TILE, seq)
    nt = seq // tm
    hd = d // SB_HEADS
    assert 2 * hd == LANES
    q_scale = float(hd ** -0.5)
    ins = [x, gain.reshape(1, d), w]
    in_specs = [pl.BlockSpec((tm, d), lambda i: (i, 0)),
                pl.BlockSpec((1, d), lambda i: (0, 0)),
                pl.BlockSpec(w.shape, lambda i: (0, 0))]
    if k_all is None:
        aliases = {}
        slot_spec = pl.BlockSpec((n_slots, 1, tm, SB_HEADS, hd), lambda i: (0, i // nt, i % nt, 0, 0))
    else:
        ins += [k_all, v_all]
        in_specs += [pl.BlockSpec(memory_space=pl.ANY)] * 2
        aliases = {3: 0, 4: 1}
        slot_spec = pl.BlockSpec((1, 1, tm, SB_HEADS, hd), lambda i: (slot, i // nt, i % nt, 0, 0))
    kv_shape = jax.ShapeDtypeStruct((n_slots, batch, seq, SB_HEADS, hd), F32)
    return pl.pallas_call(
        functools.partial(_sb_qkv_kernel, q_scale=q_scale, slot=slot),
        grid=(rows // tm,),
        in_specs=in_specs,
        out_specs=[slot_spec, slot_spec, pl.BlockSpec((tm, 3 * d), lambda i: (i, 0))],
        out_shape=[kv_shape, kv_shape, jax.ShapeDtypeStruct((rows, 3 * d), BF16)],
        input_output_aliases=aliases,
        compiler_params=_params("arbitrary"),
        name="sb_qkv",
    )(*ins)


def _sb_step(q2s, kv_blocks, carry, acc, neg_lower2, mask):
    m = q2s[0].shape[0]
    groups = len(q2s)
    per = max(groups // SB_WAVES, 1)
    waves = [range(c, min(c + per, groups)) for c in range(0, groups, per)]
    rows = [slice(w_[0] * m, (w_[-1] + 1) * m) for w_ in waves]
    units = [(b, c) for b in range(len(kv_blocks)) for c in range(len(waves))]

    def scores(b, c):
        return jnp.concatenate([_dot_nt(q2s[g], kv_blocks[b][0][g]) for g in waves[c]], axis=0)

    def softplus_pieces(c, z):
        sp = jnp.maximum(z, 0.0) + jnp.log(1.0 + jnp.exp(-jnp.abs(z)))
        if mask is not None:
            sp = jnp.where(mask[rows[c]], sp, 0.0)
        return jnp.concatenate(_split(sp, 2), axis=1)

    def weights(c, z, from_s, later):
        w = jnp.exp(z + from_s + jnp.concatenate([later] * (z.shape[1] // LANES), axis=1))
        if mask is not None:
            w = jnp.where(mask[rows[c]], w, 0.0)
        return w.astype(BF16)

    def values(b, c, w):
        return jnp.concatenate([_dot(w[i * m:(i + 1) * m], kv_blocks[b][1][g])
                                for i, g in enumerate(waves[c])], axis=0)

    n = len(units)
    z, pieces, from_s, w = ([None] * n for _ in range(4))
    later = [carry[r] for r in rows]
    pv = [None] * len(waves)
    for tick in range(n + 4):
        for u, (b, c) in enumerate(units):
            stage = tick - u
            if stage == 0:
                z[u] = scores(b, c)
            elif stage == 1:
                pieces[u] = softplus_pieces(c, z[u])
            elif stage == 2:
                from_s[u] = _dot(pieces[u], neg_lower2)
            elif stage == 3:
                w[u] = weights(c, z[u], from_s[u], later[c])
                later[c] = later[c] + jnp.broadcast_to(from_s[u][:, 0:1], later[c].shape)
            elif stage == 4:
                out = values(b, c, w[u])
                pv[c] = out if pv[c] is None else pv[c] + out
    return jnp.concatenate(later, axis=0), acc + jnp.concatenate(pv, axis=0)


def _sb_consts(bk):
    r = lax.broadcasted_iota(jnp.int32, (2, bk, bk), 1).reshape(2 * bk, bk)
    c = lax.broadcasted_iota(jnp.int32, (2 * bk, bk), 1)
    return jnp.where(r >= c, -1.0, 0.0).astype(BF16)


def _sb_stack_heads(q_ref, q2_ref):
    tq = q_ref.shape[0]
    lane = lax.broadcasted_iota(jnp.int32, (tq, LANES), 1)
    for g in range(q2_ref.shape[0]):
        q = q_ref[:, g * LANES:(g + 1) * LANES]
        zero = jnp.zeros_like(q)
        q2_ref[g] = jnp.concatenate([jnp.where(lane < LANES // 2, q, zero),
                                     jnp.where(lane >= LANES // 2, q, zero)], axis=0)


def _sb_causal_mask(groups, tq, bk):
    t = lax.broadcasted_iota(jnp.int32, (2 * groups, tq, bk), 1).reshape(2 * groups * tq, bk)
    s = lax.broadcasted_iota(jnp.int32, (2 * groups * tq, bk), 1)
    return s < t


def _sb_write(o_ref, acc_ref):
    tq = o_ref.shape[0]
    lane = lax.broadcasted_iota(jnp.int32, (tq, LANES), 1)
    for g in range(acc_ref.shape[0] // (2 * tq)):
        o_ref[:, g * LANES:(g + 1) * LANES] = jnp.where(
            lane < LANES // 2, acc_ref[2 * g * tq:(2 * g + 1) * tq, :],
            acc_ref[(2 * g + 1) * tq:(2 * g + 2) * tq, :]).astype(o_ref.dtype)


def _sb_prompt_kernel(q_ref, k_ref, v_ref, o_ref, q2_ref, carry_ref, acc_ref):
    blk = q_ref.shape[0]
    groups = q2_ref.shape[0]
    i = pl.program_id(2)
    neg_lower2 = _sb_consts(blk)
    _sb_stack_heads(q_ref, q2_ref)
    lanes = [slice(g * LANES, (g + 1) * LANES) for g in range(groups)]

    def block(j):
        row0 = pl.multiple_of(j * blk, blk)
        return ([k_ref[pl.ds(row0, blk), sl] for sl in lanes], [v_ref[pl.ds(row0, blk), sl] for sl in lanes])

    def step(blocks, carry, acc, mask):
        return _sb_step([q2_ref[g] for g in range(groups)], blocks, carry, acc, neg_lower2, mask)

    zero = jnp.zeros(carry_ref.shape, F32)
    carry_ref[...], acc_ref[...] = step([block(i)], zero, zero, _sb_causal_mask(groups, blk, blk))

    odd = i % SB_UNROLL

    def single(it, c):
        carry_ref[...], acc_ref[...] = step([block(i - 1 - it)], carry_ref[...], acc_ref[...], None)
        return c

    def multi(it, c):
        top = i - 1 - odd - it * SB_UNROLL
        carry_ref[...], acc_ref[...] = step([block(top - u) for u in range(SB_UNROLL)],
                                            carry_ref[...], acc_ref[...], None)
        return c

    lax.fori_loop(0, odd, single, 0)
    lax.fori_loop(0, i // SB_UNROLL, multi, 0)
    _sb_write(o_ref, acc_ref)


def _sb_attention_prompt(qkv, batch):
    rows, d3 = qkv.shape
    d = d3 // 3
    seq = rows // batch
    blk = min(SB_BLOCK, seq)
    nq = seq // blk
    width = SB_PAIRS * LANES
    ng = d // width
    return pl.pallas_call(
        _sb_prompt_kernel,
        grid=(batch, ng, nq),
        in_specs=[pl.BlockSpec((blk, width), lambda b, p, i: (b * nq + i, p)),
                  pl.BlockSpec((seq, width), lambda b, p, i: (b, ng + p)),
                  pl.BlockSpec((seq, width), lambda b, p, i: (b, 2 * ng + p))],
        out_specs=pl.BlockSpec((blk, width), lambda b, p, i: (b * nq + i, p)),
        out_shape=jax.ShapeDtypeStruct((rows, d), BF16),
        scratch_shapes=[pltpu.VMEM((SB_PAIRS, 2 * blk, LANES), BF16),
                        pltpu.VMEM((SB_PAIRS * 2 * blk, LANES), F32),
                        pltpu.VMEM((SB_PAIRS * 2 * blk, LANES), F32)],
        compiler_params=_params("arbitrary", "arbitrary", "arbitrary"),
        name="sb_attention_prompt",
    )(qkv, qkv, qkv)


def _sb_sample_kernel(q_ref, kn_ref, vn_ref, kp_ref, vp_ref, o_ref, q2_ref, carry_ref, acc_ref, *, blk):
    tq = q_ref.shape[0]
    groups = q2_ref.shape[0]
    c = pl.program_id(1)
    neg_lower2 = _sb_consts(blk)

    lanes = [slice(g * LANES, (g + 1) * LANES) for g in range(groups)]

    @pl.when(c == 0)
    def _():
        _sb_stack_heads(q_ref, q2_ref)
        zero = jnp.zeros(carry_ref.shape, F32)
        carry_ref[...], acc_ref[...] = _sb_step(
            [q2_ref[g] for g in range(groups)],
            [([kn_ref[0, :, sl] for sl in lanes], [vn_ref[0, :, sl] for sl in lanes])],
            zero, zero, neg_lower2, _sb_causal_mask(groups, tq, blk))

    blocks = []
    for sb in reversed(range(kp_ref.shape[2] // blk)):
        rows = slice(sb * blk, (sb + 1) * blk)
        blocks.append(([kp_ref[0, 0, rows, sl] for sl in lanes], [vp_ref[0, 0, rows, sl] for sl in lanes]))
    carry_ref[...], acc_ref[...] = _sb_step([q2_ref[g] for g in range(groups)], blocks,
                                            carry_ref[...], acc_ref[...], neg_lower2, None)

    @pl.when(c == pl.num_programs(1) - 1)
    def _():
        _sb_write(o_ref, acc_ref)


def _sb_attention_sample(qkv, batch, past_k, past_v, layer):
    rows, d3 = qkv.shape
    d = d3 // 3
    seq = rows // batch
    blk = SB_BLOCK
    pairs = d // LANES
    past_len = past_k.shape[2]
    span = min(SB_PAST_SPAN, past_len)
    steps = past_len // span
    assert seq <= blk and past_len % span == 0 and span % blk == 0
    new = qkv[:, d:].reshape(batch, seq, 2 * d)
    new = jnp.pad(new, ((0, 0), (0, blk - seq), (0, 0)))
    return pl.pallas_call(
        functools.partial(_sb_sample_kernel, blk=blk),
        grid=(batch, steps),
        in_specs=[pl.BlockSpec((seq, d), lambda b, c: (b, 0)),
                  pl.BlockSpec((1, blk, d), lambda b, c: (b, 0, 0)),
                  pl.BlockSpec((1, blk, d), lambda b, c: (b, 0, 1)),
                  pl.BlockSpec((1, 1, span, d), lambda b, c: (layer, b, steps - 1 - c, 0)),
                  pl.BlockSpec((1, 1, span, d), lambda b, c: (layer, b, steps - 1 - c, 0))],
        out_specs=pl.BlockSpec((seq, d), lambda b, c: (b, 0)),
        out_shape=jax.ShapeDtypeStruct((rows, d), BF16),
        scratch_shapes=[pltpu.VMEM((pairs, 2 * seq, LANES), BF16),
                        pltpu.VMEM((pairs * 2 * seq, LANES), F32),
                        pltpu.VMEM((pairs * 2 * seq, LANES), F32)],
        compiler_params=_params("arbitrary", "arbitrary"),
        name="sb_attention_sample",
    )(qkv, new, new, past_k, past_v)


def _xattn_kernel(a_ref, wa_ref, x_ref, g_ref, wq_ref, wo_ref, mk_ref, mv_ref, o_ref, *, heads):
    x = x_ref[...] + _dot(a_ref[...], wa_ref[...])
    d = x.shape[-1]
    dh = d // heads
    h = _rms(x, g_ref[...]).astype(BF16)
    q = _dot(h, wq_ref[...]).astype(BF16)
    outs = []
    for hd in range(heads):
        sl = slice(hd * dh, (hd + 1) * dh)
        s = _dot_nt(q[:, sl], mk_ref[0, :, sl]) * (dh ** -0.5)
        e = jnp.exp(s - jnp.max(s, axis=-1, keepdims=True))
        p = e / jnp.sum(e, axis=-1, keepdims=True)
        outs.append(_dot(p.astype(BF16), mv_ref[0, :, sl]).astype(BF16))
    o = jnp.concatenate(outs, axis=-1)
    o_ref[...] = x + _dot(o, wo_ref[...])


def _xattn(a, wa, x, gain, wq, wo, mem_k, mem_v):
    rows, d = x.shape
    batch, n_mem, _ = mem_k.shape
    seq = rows // batch
    tm = min(WIDE_ROW_TILE, seq)
    nt = seq // tm
    row_spec = pl.BlockSpec((tm, d), lambda b, i: (b * nt + i, 0))
    weight_spec = pl.BlockSpec((d, d), lambda b, i: (0, 0))
    mem_spec = pl.BlockSpec((1, n_mem, d), lambda b, i: (b, 0, 0))
    return pl.pallas_call(
        functools.partial(_xattn_kernel, heads=XA_HEADS),
        grid=(batch, nt),
        in_specs=[row_spec, weight_spec, row_spec, pl.BlockSpec((1, d), lambda b, i: (0, 0)),
                  weight_spec, weight_spec, mem_spec, mem_spec],
        out_specs=row_spec,
        out_shape=jax.ShapeDtypeStruct((rows, d), F32),
        input_output_aliases={2: 0},
        compiler_params=_params("arbitrary", "arbitrary"),
        name="xattn",
    )(a, wa, x, gain.reshape(1, d), wq, wo, mem_k, mem_v)


def _ffn_kernel(x_ref, g_ref, wgu_ref, wd_ref, o_ref, *, final_gain):
    x = x_ref[...]
    hidden = wd_ref.shape[0]
    h = _rms(x, g_ref[0:1, :]).astype(BF16)
    acc = x
    for c0 in range(0, hidden, FFN_CHUNK):
        c1 = min(c0 + FFN_CHUNK, hidden)
        gate = _dot(h, wgu_ref[:, c0:c1])
        up = _dot(h, wgu_ref[:, hidden + c0:hidden + c1])
        act = (gate * jax.nn.sigmoid(gate) * up).astype(BF16)
        acc = acc + _dot(act, wd_ref[c0:c1, :])
    if final_gain:
        acc = _rms(acc, g_ref[1:2, :])
    o_ref[...] = acc


def _ffn(x, gain, w_gate_up, w_down, final_gain=None):
    rows, d = x.shape
    tm = min(WIDE_ROW_TILE, rows)
    gains = jnp.stack([gain, gain if final_gain is None else final_gain])
    return pl.pallas_call(
        functools.partial(_ffn_kernel, final_gain=final_gain is not None),
        grid=(rows // tm,),
        in_specs=[pl.BlockSpec((tm, d), lambda i: (i, 0)),
                  pl.BlockSpec((2, d), lambda i: (0, 0)),
                  pl.BlockSpec(w_gate_up.shape, lambda i: (0, 0)),
                  pl.BlockSpec(w_down.shape, lambda i: (0, 0))],
        out_specs=pl.BlockSpec((tm, d), lambda i: (i, 0)),
        out_shape=jax.ShapeDtypeStruct((rows, d), F32),
        input_output_aliases={0: 0},
        compiler_params=_params("arbitrary"),
        name="ffn",
    )(x, gains, w_gate_up, w_down)


def _rw_proj_kernel(*refs, has_v_res):
    if has_v_res:
        (x_ref, gain_ref, shift0_ref, mu_ref, wr_ref, wk_ref, wv_ref, vec_ref,
         w1_ref, w2_ref, a1_ref, a2_ref, g1_ref, g2_ref, v1_ref, v2_ref, vfirst_ref,
         r_ref, lw_ref, k_ref, v_ref, kk_ref, a_ref, g_ref, shift_ref, prev_ref) = refs
    else:
        (x_ref, gain_ref, shift0_ref, mu_ref, wr_ref, wk_ref, wv_ref, vec_ref,
         w1_ref, w2_ref, a1_ref, a2_ref, g1_ref, g2_ref,
         r_ref, lw_ref, k_ref, v_ref, kk_ref, a_ref, g_ref, shift_ref, prev_ref) = refs
    i = pl.program_id(1)
    h = _rms(x_ref[...], gain_ref[...])
    tm = h.shape[0]

    @pl.when(i == 0)
    def _():
        prev_ref[...] = shift0_ref[0]

    row = lax.broadcasted_iota(jnp.int32, h.shape, 0)
    h_prev = jnp.where(row == 0, prev_ref[...], pltpu.roll(h, 1, 0))
    last = h[tm - 1:tm, :]
    prev_ref[...] = last
    shift_ref[0] = last

    xx = h_prev - h
    x_r, x_w, x_k, x_v, x_a, x_g = ((h + xx * mu_ref[j:j + 1, :]).astype(BF16) for j in range(6))
    w0, a0, k_k, k_a, v0 = (vec_ref[j:j + 1, :] for j in range(5))

    r = _dot(x_r, wr_ref[...])
    k = _dot(x_k, wk_ref[...])
    v = _dot(x_v, wv_ref[...])
    dec = w0 + _dot(jnp.tanh(_dot(x_w, w1_ref[...])).astype(BF16), w2_ref[...])
    w_log = -jax.nn.softplus(-dec) - 0.5
    if has_v_res:
        mix = jax.nn.sigmoid(v0 + _dot(_dot(x_v, v1_ref[...]).astype(BF16), v2_ref[...]))
        v = v + (vfirst_ref[...] - v) * mix
    a = jax.nn.sigmoid(a0 + _dot(_dot(x_a, a1_ref[...]).astype(BF16), a2_ref[...]))
    g = _dot(jax.nn.sigmoid(_dot(x_g, g1_ref[...])).astype(BF16), g2_ref[...])

    r_ref[...] = r
    lw_ref[...] = -jnp.exp(w_log)
    kk_ref[...] = k * k_k
    k_ref[...] = k * (1.0 + (a - 1.0) * k_a)
    v_ref[...] = v
    a_ref[...] = a
    g_ref[...] = g


def _pad_cols(w, mult=LANES):
    pad = -w.shape[1] % mult
    return jnp.pad(w, ((0, 0), (0, pad))) if pad else w


def _pad_rows(w, mult=LANES):
    pad = -w.shape[0] % mult
    return jnp.pad(w, ((0, pad), (0, 0))) if pad else w


def _rw_proj(x, batch, gain, shift0, p, v_first):
    rows, d = x.shape
    seq = rows // batch
    tm = min(WIDE_ROW_TILE, seq)
    nt = seq // tm
    has_v_res = v_first is not None
    zeros = jnp.zeros((d,), F32)
    vec = jnp.stack([p["w0"], p["a0"], p["k_k"], p["k_a"], p["v0"] if has_v_res else zeros,
                     zeros, zeros, zeros])
    row_spec = pl.BlockSpec((tm, d), lambda b, i: (b * nt + i, 0))

    def full(arr):
        return pl.BlockSpec(arr.shape, lambda b, i: (0,) * arr.ndim, pipeline_mode=pl.Buffered(1))

    ins = [x, gain.reshape(1, d), shift0.reshape(batch, 1, d), p["mu"], p["w_r"], p["w_k"], p["w_v"], vec,
           p["w1"], p["w2"], p["a1"], p["a2"], p["g1"], p["g2"]]
    in_specs = [row_spec, full(ins[1]), pl.BlockSpec((1, 1, d), lambda b, i: (b, 0, 0))]
    in_specs += [full(a) for a in ins[3:]]
    if has_v_res:
        ins += [p["v1"], p["v2"], v_first]
        in_specs += [full(p["v1"]), full(p["v2"]), row_spec]
    tok = jax.ShapeDtypeStruct((rows, d), F32)
    outs = pl.pallas_call(
        functools.partial(_rw_proj_kernel, has_v_res=has_v_res),
        grid=(batch, nt),
        in_specs=in_specs,
        out_specs=[row_spec] * 7 + [pl.BlockSpec((1, 1, d), lambda b, i: (b, 0, 0))],
        out_shape=[tok] * 7 + [jax.ShapeDtypeStruct((batch, 1, d), F32)],
        scratch_shapes=[pltpu.VMEM((1, d), F32)],
        compiler_params=_params("arbitrary", "arbitrary"),
        name="rw_proj",
    )(*ins)
    return outs[:7], outs[7].reshape(batch, d)


WKV_DECAY_PIECES = 3
WKV_GROUP_PIECES = 2
WKV_STATE_PASSES = 1
WKV_SOLVE_PASSES = 1


def _wkv_block(pairs, c_):
    n = pairs[0][0].shape[0]
    inv_hd = 1.0 / RW_HEAD_DIM
    sum_l = lambda x: _mm_exact_lhs(c_["tri_incl"], x, WKV_DECAY_PIECES)
    sum_g = lambda x: _mm_exact_rhs(x, c_["gsum"], WKV_GROUP_PIECES)
    solve_mm = lambda x, y, nt=False: _mm(x, y, WKV_SOLVE_PASSES, nt=nt)
    np_ = len(pairs)

    cs = [sum_l(p[1]) for p in pairs]
    head_sums = [sum_g(jnp.concatenate([p[4] * p[4], p[0] * p[2] * p[7][0:1, :]], axis=0)) for p in pairs]
    kk_sq = [hs[:n] for hs in head_sums]
    pre = []
    for (r, lw, k, v, kk, a, g, vec, state), cs_p, sq in zip(pairs, cs, kk_sq):
        cs_last = cs_p[n - 1:n, :]
        d_inv = jnp.exp(-cs_p)
        d_rest = jnp.exp(cs_last - cs_p)
        kkn = kk * lax.rsqrt(jnp.maximum(sq, 1e-24))
        b = kkn * a
        pre.append(dict(a_t=-kkn * jnp.exp(cs_p - lw), r_t=r * jnp.exp(cs_p), b_t=b * d_inv, k_t=k * d_inv,
                        bk_rest=jnp.concatenate([b * d_rest, k * d_rest], axis=0), decay=jnp.exp(cs_last)))

    from_state = [_mm(jnp.concatenate([q["a_t"], q["r_t"]], axis=0), p[8], WKV_STATE_PASSES, nt=True)
                  for p, q in zip(pairs, pre)]
    chains = [(i, m) for i in range(np_) for m in c_["head_masks"]]
    grams = []
    for i, m in chains:
        q = pre[i]
        zero = jnp.zeros_like(q["a_t"])
        lhs = jnp.concatenate([jnp.where(m, q["a_t"], zero), jnp.where(m, q["r_t"], zero)], axis=0)
        grams.append(solve_mm(lhs, jnp.concatenate([q["b_t"], q["k_t"]], axis=0), nt=True))
    lows = [jnp.where(c_["strict"], gm[:n, :n], 0.0) for gm in grams]
    m_aks = [jnp.where(c_["strict_k"], gm[:n], 0.0) for gm in grams]
    m_rs = [jnp.where(c_["incl2"], gm[n:], 0.0) for gm in grams]
    rhs = [from_state[i][:n] + solve_mm(mk, jnp.concatenate([jnp.zeros_like(pairs[i][3]), pairs[i][3]], axis=0))
           for (i, _), mk in zip(chains, m_aks)]
    sig = [x + solve_mm(lo, x) for lo, x in zip(lows, rhs)]
    pw = lows
    step = 1
    while 2 * step < n:
        pw = [solve_mm(x, x) for x in pw]
        sig = [x + solve_mm(p2, x) for p2, x in zip(pw, sig)]
        step *= 2
    ys = [from_state[i][n:] + solve_mm(mr, jnp.concatenate([s, pairs[i][3]], axis=0))
          for (i, _), mr, s in zip(chains, m_rs, sig)]

    first = c_["head_masks"][0]
    sig_p = [jnp.where(first, sig[2 * i], sig[2 * i + 1]) for i in range(np_)]
    y_p = [jnp.where(first, ys[2 * i], ys[2 * i + 1]) for i in range(np_)]
    upd = [_mm(jnp.concatenate([s, p[3]], axis=0).T, q["bk_rest"], WKV_STATE_PASSES)
           for s, p, q in zip(sig_p, pairs, pre)]
    new_states = [p[8] * q["decay"] + jnp.where(c_["same_head"], u, 0.0) for p, q, u in zip(pairs, pre, upd)]

    mean = [sum_g(y) * inv_hd for y in y_p]
    dev = [y - m for y, m in zip(y_p, mean)]
    var = [sum_g(dv * dv) * inv_hd for dv in dev]
    bonus = [hs[n:] * p[3] for hs, p in zip(head_sums, pairs)]
    outs = []
    for p, dv, vr, bo in zip(pairs, dev, var, bonus):
        vec = p[7]
        o = dv * lax.rsqrt(vr + RW_GN_EPS) * vec[1:2, :] + vec[2:3, :]
        outs.append((o + bo) * p[6])
    return list(zip(outs, new_states))


def _wkv_kernel(*refs, has_state0):
    if has_state0:
        (r_ref, lw_ref, k_ref, v_ref, kk_ref, a_ref, g_ref, vec_ref, s0_ref,
         o_ref, sout_ref, state_ref) = refs
    else:
        (r_ref, lw_ref, k_ref, v_ref, kk_ref, a_ref, g_ref, vec_ref,
         o_ref, sout_ref, state_ref) = refs
    c = pl.program_id(2)
    n = r_ref.shape[0]
    hd = RW_HEAD_DIM

    @pl.when(c == 0)
    def _():
        if has_state0:
            state_ref[...] = s0_ref[0]
        else:
            state_ref[...] = jnp.zeros_like(state_ref)

    tr = lax.broadcasted_iota(jnp.int32, (n, n), 0)
    tc = lax.broadcasted_iota(jnp.int32, (n, n), 1)
    tr2 = lax.broadcasted_iota(jnp.int32, (n, 2 * n), 0)
    tc2 = lax.broadcasted_iota(jnp.int32, (n, 2 * n), 1)
    lane = lax.broadcasted_iota(jnp.int32, (n, LANES), 1)
    sr = lax.broadcasted_iota(jnp.int32, (LANES, LANES), 0) // hd
    sc = lax.broadcasted_iota(jnp.int32, (LANES, LANES), 1) // hd
    consts = dict(
        gsum=_group_ones(LANES, hd),
        tri_incl=jnp.where(tc <= tr, 1.0, 0.0).astype(BF16),
        strict=tc < tr,
        strict_k=(tc2 >= n) & (tc2 - n < tr2),
        incl2=jnp.where(tc2 >= n, tc2 - n, tc2) <= tr2,
        eye=jnp.where(tc == tr, 1.0, 0.0).astype(F32),
        head_masks=[lane < hd, lane >= hd],
        same_head=sr == sc,
    )
    pairs = []
    for p in range(state_ref.shape[0]):
        sl = slice(p * LANES, (p + 1) * LANES)
        pairs.append((r_ref[:, sl], lw_ref[:, sl], k_ref[:, sl], v_ref[:, sl], kk_ref[:, sl],
                      a_ref[:, sl], g_ref[:, sl], vec_ref[:, sl], state_ref[p]))
    results = _wkv_block(pairs, consts)
    o_ref[...] = jnp.concatenate([o for o, _ in results], axis=1).astype(o_ref.dtype)
    new_states = jnp.stack([s for _, s in results])
    state_ref[...] = new_states
    sout_ref[0] = new_states


def _wkv(tok, batch, vec, state0):
    rows, d = tok[0].shape
    seq = rows // batch
    n = min(WKV_CHUNK, seq)
    nc = seq // n
    pairs = d // LANES
    gp = WKV_PAIRS
    width = gp * LANES
    has_state0 = state0 is not None
    tok_spec = pl.BlockSpec((n, width), lambda b, p, c: (b * nc + c, p))
    state_spec = pl.BlockSpec((1, gp, LANES, LANES), lambda b, p, c: (b, p, 0, 0))
    ins = list(tok) + [vec]
    in_specs = [tok_spec] * 7 + [pl.BlockSpec((8, width), lambda b, p, c: (0, p))]
    if has_state0:
        ins.append(state0)
        in_specs.append(state_spec)
    return pl.pallas_call(
        functools.partial(_wkv_kernel, has_state0=has_state0),
        grid=(batch, pairs // gp, nc),
        in_specs=in_specs,
        out_specs=[tok_spec, state_spec],
        out_shape=[jax.ShapeDtypeStruct((rows, d), BF16),
                   jax.ShapeDtypeStruct((batch, pairs, LANES, LANES), F32)],
        scratch_shapes=[pltpu.VMEM((gp, LANES, LANES), F32)],
        compiler_params=_params("arbitrary", "arbitrary", "arbitrary"),
        name="wkv",
    )(*ins)


def _state_to_blockdiag(state):
    b, h, n, _ = state.shape
    s = state.reshape(b, h // 2, 2, n, n)
    z = jnp.zeros_like(s[:, :, 0])
    top = jnp.concatenate([s[:, :, 0], z], axis=-1)
    bot = jnp.concatenate([z, s[:, :, 1]], axis=-1)
    return jnp.concatenate([top, bot], axis=-2)


def _state_from_blockdiag(bd):
    b, p, n2, _ = bd.shape
    n = n2 // 2
    return jnp.stack([bd[:, :, :n, :n], bd[:, :, n:, n:]], axis=2).reshape(b, 2 * p, n, n)


def _trunk(x, batch, mem_k, mem_v, past_k, past_v, rw_state0, rw_shift0, w):
    depth = len(w["ffn_gu"])
    d = x.shape[-1]
    sb_k, sb_v, rw_states, rw_shifts = None, None, [], []
    n_sb = (depth + 1) // 2
    v_first = None
    for layer in range(depth):
        j = layer // 2
        if layer % 2 == 0:
            sb_k, sb_v, qkv = _sb_qkv(x, batch, w["norm_mix"][layer], w["sb_qkv"][j], j, n_sb, sb_k, sb_v)
            if past_k is None:
                o = _sb_attention_prompt(qkv, batch)
            else:
                o = _sb_attention_sample(qkv, batch, past_k, past_v, j)
            w_o = w["sb_o"][j]
        else:
            if rw_state0 is None:
                state0 = None
                shift0 = jnp.zeros((batch, d), F32)
            else:
                state0 = _state_to_blockdiag(rw_state0[j])
                shift0 = rw_shift0[j]
            tok, shift = _rw_proj(x, batch, w["norm_mix"][layer], shift0, w["rw"][j], v_first)
            if v_first is None:
                v_first = tok[3]
            o, state = _wkv(tok, batch, w["rw"][j]["out_vec"], state0)
            rw_states.append(_state_from_blockdiag(state))
            rw_shifts.append(shift)
            w_o = w["rw"][j]["w_o"]
        x = _xattn(o, w_o, x, w["norm_xattn"][layer], w["xa_q"][layer], w["xa_o"][layer],
                   mem_k[layer], mem_v[layer])
        final = w["norm_final"] if layer == depth - 1 else None
        x = _ffn(x, w["norm_ffn"][layer], w["ffn_gu"][layer], w["ffn_down"][layer], final)
    return x, sb_k, sb_v, jnp.stack(rw_states), jnp.stack(rw_shifts)


def kernel(x_prompt, x_sample, mem_prompt, cache_sb_k, cache_sb_v, state_rwkv_wkv, state_rwkv_shift,
           cache_mem_k, cache_mem_v, norm_mix, norm_xattn, norm_ffn, norm_mem, norm_final,
           sb_w_qkv, sb_w_o, rw_mu, rw_w_r, rw_w_k, rw_w_v, rw_w_o, rw_w0, rw_w1, rw_w2,
           rw_a0, rw_a1, rw_a2, rw_v0, rw_v1, rw_v2, rw_g1, rw_g2, rw_k_k, rw_k_a, rw_r_k,
           rw_lnx_w, rw_lnx_b, xa_w_q, xa_w_kv, xa_w_o, ffn_w_gate_up, ffn_w_down):
    batch, seq, d = x_prompt.shape
    dec_batch, dec_seq, _ = x_sample.shape
    depth = norm_mix.shape[0]
    n_rw = rw_w_r.shape[0]
    n_sb = sb_w_qkv.shape[0]
    n_mem = mem_prompt.shape[1]
    bf = lambda t: t.astype(BF16)

    rw = []
    for j in range(n_rw):
        pj = dict(mu=rw_mu[j], w_r=bf(rw_w_r[j]), w_k=bf(rw_w_k[j]), w_v=bf(rw_w_v[j]), w_o=bf(rw_w_o[j]),
                  w0=rw_w0[j], a0=rw_a0[j], k_k=rw_k_k[j], k_a=rw_k_a[j],
                  w1=bf(_pad_cols(rw_w1[j])), w2=bf(_pad_rows(rw_w2[j])),
                  a1=bf(_pad_cols(rw_a1[j])), a2=bf(_pad_rows(rw_a2[j])),
                  g1=bf(_pad_cols(rw_g1[j])), g2=bf(_pad_rows(rw_g2[j])))
        if j > 0:
            pj.update(v0=rw_v0[j - 1], v1=bf(_pad_cols(rw_v1[j - 1])), v2=bf(_pad_rows(rw_v2[j - 1])))
        zeros = jnp.zeros((d,), F32)
        pj["out_vec"] = jnp.stack([rw_r_k[j].reshape(d), rw_lnx_w[j], rw_lnx_b[j]] + [zeros] * 5)
        rw.append(pj)
    w = dict(norm_mix=norm_mix, norm_xattn=norm_xattn, norm_ffn=norm_ffn, norm_final=norm_final,
             sb_qkv=bf(sb_w_qkv), sb_o=bf(sb_w_o), rw=rw, xa_q=bf(xa_w_q), xa_o=bf(xa_w_o),
             ffn_gu=bf(ffn_w_gate_up), ffn_down=bf(ffn_w_down))

    mem_k, mem_v, mem_k_bf, mem_v_bf = _mem_kv(mem_prompt, norm_mem, bf(xa_w_kv))
    y_p, p_sb_k, p_sb_v, p_wkv, p_shift = _trunk(
        x_prompt.reshape(batch * seq, d), batch, mem_k_bf, mem_v_bf, None, None, None, None, w)

    y_s, s_sb_k, s_sb_v, s_wkv, s_shift = _trunk(
        x_sample.reshape(dec_batch * dec_seq, d), dec_batch,
        bf(cache_mem_k).reshape(depth, dec_batch, n_mem, d), bf(cache_mem_v).reshape(depth, dec_batch, n_mem, d),
        bf(cache_sb_k).reshape(n_sb, dec_batch, -1, d), bf(cache_sb_v).reshape(n_sb, dec_batch, -1, d),
        state_rwkv_wkv, state_rwkv_shift, w)

    hd = d // SB_HEADS
    return (y_p.reshape(batch, seq, d), y_s.reshape(dec_batch, dec_seq, d),
            p_sb_k.reshape(n_sb, batch, seq, SB_HEADS, hd), p_sb_v.reshape(n_sb, batch, seq, SB_HEADS, hd),
            p_wkv, p_shift,
            mem_k, mem_v,
            s_sb_k.reshape(n_sb, dec_batch, dec_seq, SB_HEADS, hd),
            s_sb_v.reshape(n_sb, dec_batch, dec_seq, SB_HEADS, hd),
            s_wkv, s_shift)
```

```python
import functools

import jax
import jax.numpy as jnp
from jax import lax
from jax.experimental import pallas as pl
from jax.experimental.pallas import tpu as pltpu

F32 = jnp.float32
BF16 = jnp.bfloat16

NORM_EPS = 1e-6
NEG_LOG2_E = -1.4426950408889634
RW_GN_EPS = 64e-5
SB_HEADS = 16
XA_HEADS = 4
RW_HEAD_DIM = 64
LANES = 128
ROW_TILE = 256
WIDE_ROW_TILE = 512
SB_BLOCK = 256
SB_PAIRS = 4
SB_WAVES = 2
SB_UNROLL = 2
SB_PAST_SPAN = 1024
WKV_CHUNK = 64
WKV_CHUNKS_PER_STEP = 4
WKV_PAIRS = 8
FFN_CHUNK = 256
VMEM_LIMIT_BYTES = 56 * 1024 * 1024


def _params(*sem):
    return pltpu.CompilerParams(dimension_semantics=sem, vmem_limit_bytes=VMEM_LIMIT_BYTES)


def _dot(a, b):
    return jnp.dot(a, b, preferred_element_type=F32)


def _dot_nt(a, b):
    return lax.dot_general(a, b, (((1,), (1,)), ((), ())), preferred_element_type=F32)


def _split(x, n):
    parts = []
    for i in range(n):
        p = x.astype(BF16)
        parts.append(p)
        if i + 1 < n:
            x = x - p.astype(F32)
    return parts


def _mm(a, b, passes, nt=False):
    dot = _dot_nt if nt else _dot
    if passes == 1:
        return dot(a.astype(BF16), b.astype(BF16))
    a_hi, a_lo = _split(a, 2)
    b_hi, b_lo = _split(b, 2)
    return dot(a_hi, b_hi) + (dot(a_hi, b_lo) + dot(a_lo, b_hi))


def _mm_exact_rhs(a, b_bf16, n):
    out = None
    for p in _split(a, n):
        t = _dot(p, b_bf16)
        out = t if out is None else out + t
    return out


def _mm_exact_lhs(a_bf16, b, n):
    out = None
    for p in _split(b, n):
        t = _dot(a_bf16, p)
        out = t if out is None else out + t
    return out


def _rms(x, gain):
    ms = jnp.mean(x * x, axis=-1, keepdims=True)
    return x * lax.rsqrt(ms + NORM_EPS) * gain


def _group_ones(n, group):
    r = lax.broadcasted_iota(jnp.int32, (n, n), 0) // group
    c = lax.broadcasted_iota(jnp.int32, (n, n), 1) // group
    return jnp.where(r == c, 1.0, 0.0).astype(BF16)


def _mem_kv_kernel(x_ref, g_ref, w_ref, k_ref, v_ref, kb_ref, vb_ref):
    tm, d = x_ref.shape
    heads, dh = k_ref.shape[-2:]
    h = _rms(x_ref[...], g_ref[0]).astype(BF16)
    kv = _dot(h, w_ref[0])
    k, v = kv[:, :d], kv[:, d:]
    k_ref[0, 0] = k.reshape(tm, heads, dh)
    v_ref[0, 0] = v.reshape(tm, heads, dh)
    kb_ref[0, 0] = k.astype(BF16)
    vb_ref[0, 0] = v.astype(BF16)


def _mem_kv(mem, gains, w_kv):
    depth = w_kv.shape[0]
    batch, n_mem, d = mem.shape
    tm = min(ROW_TILE, n_mem)
    nt = n_mem // tm
    dh = d // XA_HEADS
    assert n_mem % tm == 0 and dh % LANES == 0
    out = jax.ShapeDtypeStruct((depth, batch, n_mem, XA_HEADS, dh), F32)
    out_b = jax.ShapeDtypeStruct((depth, batch, n_mem, d), BF16)
    spec = pl.BlockSpec((1, 1, tm, XA_HEADS, dh), lambda l, i: (l, i // nt, i % nt, 0, 0))
    spec_b = pl.BlockSpec((1, 1, tm, d), lambda l, i: (l, i // nt, i % nt, 0))
    return pl.pallas_call(
        _mem_kv_kernel,
        grid=(depth, batch * nt),
        in_specs=[pl.BlockSpec((tm, d), lambda l, i: (i, 0)),
                  pl.BlockSpec((1, 1, d), lambda l, i: (l, 0, 0)),
                  pl.BlockSpec((1, d, 2 * d), lambda l, i: (l, 0, 0))],
        out_specs=[spec, spec, spec_b, spec_b],
        out_shape=[out, out, out_b, out_b],
        compiler_params=_params("arbitrary", "arbitrary"),
        name="mem_kv",
    )(mem.reshape(batch * n_mem, d), gains.reshape(depth, 1, d), w_kv)


def _split_heads(x, hd):
    tm, d = x.shape
    blocks = []
    for p in range(d // LANES):
        slab = x[:, p * LANES:(p + 1) * LANES]
        blocks += [slab, pltpu.roll(slab, LANES - hd, 1)]
    return jnp.concatenate(blocks, axis=1).reshape(tm, d // hd, LANES)[:, :, :hd]


def _sb_qkv_kernel(x_ref, g_ref, w_ref, *rest, q_scale, slot):
    k_ref, v_ref, qkv_ref = rest[-3:]
    d = x_ref.shape[-1]
    hd = k_ref.shape[-1]
    h = _rms(x_ref[...], g_ref[...]).astype(BF16)
    qkv = _dot(h, w_ref[...])
    qkv_ref[:, :d] = (qkv[:, :d] * q_scale).astype(BF16)
    qkv_ref[:, d:] = qkv[:, d:].astype(BF16)
    k3 = _split_heads(qkv[:, d:2 * d], hd)
    v3 = _split_heads(qkv[:, 2 * d:], hd)
    own = slot if k_ref.shape[0] > 1 else 0
    for s in range(k_ref.shape[0]):
        if s == own:
            k_ref[s, 0] = k3
            v_ref[s, 0] = v3
        else:
            k_ref[s, 0] = jnp.zeros(k3.shape, k_ref.dtype)
            v_ref[s, 0] = jnp.zeros(v3.shape, v_ref.dtype)


def _sb_qkv(x, batch, gain, w, slot, n_slots, k_all=None, v_all=None):
    rows, d = x.shape
    seq = rows // batch
    tm = min(ROW_TILE, seq)
    nt = seq // tm
    hd = d // SB_HEADS
    assert 2 * hd == LANES
    q_scale = float(hd ** -0.5)
    ins = [x, gain.reshape(1, d), w]
    in_specs = [pl.BlockSpec((tm, d), lambda i: (i, 0)),
                pl.BlockSpec((1, d), lambda i: (0, 0)),
                pl.BlockSpec(w.shape, lambda i: (0, 0))]
    if k_all is None:
        aliases = {}
        slot_spec = pl.BlockSpec((n_slots, 1, tm, SB_HEADS, hd), lambda i: (0, i // nt, i % nt, 0, 0))
    else:
        ins += [k_all, v_all]
        in_specs += [pl.BlockSpec(memory_space=pl.ANY)] * 2
        aliases = {3: 0, 4: 1}
        slot_spec = pl.BlockSpec((1, 1, tm, SB_HEADS, hd), lambda i: (slot, i // nt, i % nt, 0, 0))
    kv_shape = jax.ShapeDtypeStruct((n_slots, batch, seq, SB_HEADS, hd), F32)
    return pl.pallas_call(
        functools.partial(_sb_qkv_kernel, q_scale=q_scale, slot=slot),
        grid=(rows // tm,),
        in_specs=in_specs,
        out_specs=[slot_spec, slot_spec, pl.BlockSpec((tm, 3 * d), lambda i: (i, 0))],
        out_shape=[kv_shape, kv_shape, jax.ShapeDtypeStruct((rows, 3 * d), BF16)],
        input_output_aliases=aliases,
        compiler_params=_params("arbitrary"),
        name="sb_qkv",
    )(*ins)


def _sb_step(q2s, kv_blocks, carry, acc, neg_lower2, mask):
    m = q2s[0].shape[0]
    groups = len(q2s)
    per = max(groups // SB_WAVES, 1)
    waves = [range(c, min(c + per, groups)) for c in range(0, groups, per)]
    rows = [slice(w_[0] * m, (w_[-1] + 1) * m) for w_ in waves]
    units = [(b, c) for b in range(len(kv_blocks)) for c in range(len(waves))]

    def scores(b, c):
        return jnp.concatenate([_dot_nt(q2s[g], kv_blocks[b][0][g]) for g in waves[c]], axis=0)

    def softplus_pieces(c, z):
        sp = jnp.maximum(z, 0.0) + jnp.log(1.0 + jnp.exp2(jnp.abs(z) * NEG_LOG2_E))
        if mask is not None:
            sp = jnp.where(mask[rows[c]], sp, 0.0)
        return jnp.concatenate(_split(sp, 2), axis=1)

    def weights(c, z, from_s, later):
        w = jnp.exp(z + from_s + jnp.concatenate([later] * (z.shape[1] // LANES), axis=1))
        if mask is not None:
            w = jnp.where(mask[rows[c]], w, 0.0)
        return w.astype(BF16)

    def values(b, c, w):
        return jnp.concatenate([_dot(w[i * m:(i + 1) * m], kv_blocks[b][1][g])
                                for i, g in enumerate(waves[c])], axis=0)

    n = len(units)
    z, pieces, from_s, w = ([None] * n for _ in range(4))
    later = [carry[r] for r in rows]
    pv = [None] * len(waves)
    for tick in range(n + 4):
        for u, (b, c) in enumerate(units):
            stage = tick - u
            if stage == 0:
                z[u] = scores(b, c)
            elif stage == 1:
                pieces[u] = softplus_pieces(c, z[u])
            elif stage == 2:
                from_s[u] = _dot(pieces[u], neg_lower2)
            elif stage == 3:
                w[u] = weights(c, z[u], from_s[u], later[c])
                later[c] = later[c] + jnp.broadcast_to(from_s[u][:, 0:1], later[c].shape)
            elif stage == 4:
                out = values(b, c, w[u])
                pv[c] = out if pv[c] is None else pv[c] + out
    return jnp.concatenate(later, axis=0), acc + jnp.concatenate(pv, axis=0)


def _sb_consts(bk):
    r = lax.broadcasted_iota(jnp.int32, (2, bk, bk), 1).reshape(2 * bk, bk)
    c = lax.broadcasted_iota(jnp.int32, (2 * bk, bk), 1)
    return jnp.where(r >= c, -1.0, 0.0).astype(BF16)


def _sb_stack_heads(q_ref, q2_ref):
    tq = q_ref.shape[0]
    lane = lax.broadcasted_iota(jnp.int32, (tq, LANES), 1)
    for g in range(q2_ref.shape[0]):
        q = q_ref[:, g * LANES:(g + 1) * LANES]
        zero = jnp.zeros_like(q)
        q2_ref[g] = jnp.concatenate([jnp.where(lane < LANES // 2, q, zero),
                                     jnp.where(lane >= LANES // 2, q, zero)], axis=0)


def _sb_causal_mask(groups, tq, bk):
    t = lax.broadcasted_iota(jnp.int32, (2 * groups, tq, bk), 1).reshape(2 * groups * tq, bk)
    s = lax.broadcasted_iota(jnp.int32, (2 * groups * tq, bk), 1)
    return s < t


def _sb_write(o_ref, acc_ref):
    tq = o_ref.shape[0]
    lane = lax.broadcasted_iota(jnp.int32, (tq, LANES), 1)
    for g in range(acc_ref.shape[0] // (2 * tq)):
        o_ref[:, g * LANES:(g + 1) * LANES] = jnp.where(
            lane < LANES // 2, acc_ref[2 * g * tq:(2 * g + 1) * tq, :],
            acc_ref[(2 * g + 1) * tq:(2 * g + 2) * tq, :]).astype(o_ref.dtype)


def _sb_prompt_kernel(q_ref, k_ref, v_ref, o_ref, q2_ref, carry_ref, acc_ref):
    blk = q_ref.shape[0]
    groups = q2_ref.shape[0]
    i = pl.program_id(2)
    neg_lower2 = _sb_consts(blk)
    _sb_stack_heads(q_ref, q2_ref)
    lanes = [slice(g * LANES, (g + 1) * LANES) for g in range(groups)]

    def block(j):
        row0 = pl.multiple_of(j * blk, blk)
        return ([k_ref[pl.ds(row0, blk), sl] for sl in lanes], [v_ref[pl.ds(row0, blk), sl] for sl in lanes])

    def step(blocks, carry, acc, mask):
        return _sb_step([q2_ref[g] for g in range(groups)], blocks, carry, acc, neg_lower2, mask)

    zero = jnp.zeros(carry_ref.shape, F32)
    carry_ref[...], acc_ref[...] = step([block(i)], zero, zero, _sb_causal_mask(groups, blk, blk))

    odd = i % SB_UNROLL

    def single(it, c):
        carry_ref[...], acc_ref[...] = step([block(i - 1 - it)], carry_ref[...], acc_ref[...], None)
        return c

    def multi(it, c):
        top = i - 1 - odd - it * SB_UNROLL
        carry_ref[...], acc_ref[...] = step([block(top - u) for u in range(SB_UNROLL)],
                                            carry_ref[...], acc_ref[...], None)
        return c

    lax.fori_loop(0, odd, single, 0)
    lax.fori_loop(0, i // SB_UNROLL, multi, 0)
    _sb_write(o_ref, acc_ref)


def _sb_attention_prompt(qkv, batch):
    rows, d3 = qkv.shape
    d = d3 // 3
    seq = rows // batch
    blk = min(SB_BLOCK, seq)
    nq = seq // blk
    width = SB_PAIRS * LANES
    ng = d // width
    return pl.pallas_call(
        _sb_prompt_kernel,
        grid=(batch, ng, nq),
        in_specs=[pl.BlockSpec((blk, width), lambda b, p, i: (b * nq + i, p)),
                  pl.BlockSpec((seq, width), lambda b, p, i: (b, ng + p)),
                  pl.BlockSpec((seq, width), lambda b, p, i: (b, 2 * ng + p))],
        out_specs=pl.BlockSpec((blk, width), lambda b, p, i: (b * nq + i, p)),
        out_shape=jax.ShapeDtypeStruct((rows, d), BF16),
        scratch_shapes=[pltpu.VMEM((SB_PAIRS, 2 * blk, LANES), BF16),
                        pltpu.VMEM((SB_PAIRS * 2 * blk, LANES), F32),
                        pltpu.VMEM((SB_PAIRS * 2 * blk, LANES), F32)],
        compiler_params=_params("arbitrary", "arbitrary", "arbitrary"),
        name="sb_attention_prompt",
    )(qkv, qkv, qkv)


def _sb_sample_kernel(q_ref, kn_ref, vn_ref, kp_ref, vp_ref, o_ref, q2_ref, carry_ref, acc_ref, *, blk):
    tq = q_ref.shape[0]
    groups = q2_ref.shape[0]
    c = pl.program_id(1)
    neg_lower2 = _sb_consts(blk)

    lanes = [slice(g * LANES, (g + 1) * LANES) for g in range(groups)]

    @pl.when(c == 0)
    def _():
        _sb_stack_heads(q_ref, q2_ref)
        zero = jnp.zeros(carry_ref.shape, F32)
        carry_ref[...], acc_ref[...] = _sb_step(
            [q2_ref[g] for g in range(groups)],
            [([kn_ref[0, :, sl] for sl in lanes], [vn_ref[0, :, sl] for sl in lanes])],
            zero, zero, neg_lower2, _sb_causal_mask(groups, tq, blk))

    blocks = []
    for sb in reversed(range(kp_ref.shape[2] // blk)):
        rows = slice(sb * blk, (sb + 1) * blk)
        blocks.append(([kp_ref[0, 0, rows, sl].astype(BF16) for sl in lanes],
                       [vp_ref[0, 0, rows, sl].astype(BF16) for sl in lanes]))
    carry_ref[...], acc_ref[...] = _sb_step([q2_ref[g] for g in range(groups)], blocks,
                                            carry_ref[...], acc_ref[...], neg_lower2, None)

    @pl.when(c == pl.num_programs(1) - 1)
    def _():
        _sb_write(o_ref, acc_ref)


def _sb_attention_sample(qkv, batch, past_k, past_v, layer):
    rows, d3 = qkv.shape
    d = d3 // 3
    seq = rows // batch
    blk = SB_BLOCK
    pairs = d // LANES
    past_len = past_k.shape[2]
    span = min(SB_PAST_SPAN, past_len)
    steps = past_len // span
    assert seq <= blk and past_len % span == 0 and span % blk == 0
    new = qkv[:, d:].reshape(batch, seq, 2 * d)
    new = jnp.pad(new, ((0, 0), (0, blk - seq), (0, 0)))
    return pl.pallas_call(
        functools.partial(_sb_sample_kernel, blk=blk),
        grid=(batch, steps),
        in_specs=[pl.BlockSpec((seq, d), lambda b, c: (b, 0)),
                  pl.BlockSpec((1, blk, d), lambda b, c: (b, 0, 0)),
                  pl.BlockSpec((1, blk, d), lambda b, c: (b, 0, 1)),
                  pl.BlockSpec((1, 1, span, d), lambda b, c: (layer, b, steps - 1 - c, 0)),
                  pl.BlockSpec((1, 1, span, d), lambda b, c: (layer, b, steps - 1 - c, 0))],
        out_specs=pl.BlockSpec((seq, d), lambda b, c: (b, 0)),
        out_shape=jax.ShapeDtypeStruct((rows, d), BF16),
        scratch_shapes=[pltpu.VMEM((pairs, 2 * seq, LANES), BF16),
                        pltpu.VMEM((pairs * 2 * seq, LANES), F32),
                        pltpu.VMEM((pairs * 2 * seq, LANES), F32)],
        compiler_params=_params("arbitrary", "arbitrary"),
        name="sb_attention_sample",
    )(qkv, new, new, past_k, past_v)


def _xattn_kernel(a_ref, wa_ref, x_ref, g_ref, wq_ref, wo_ref, mk_ref, mv_ref, o_ref, *, heads):
    x = x_ref[...] + _dot(a_ref[...], wa_ref[...])
    d = x.shape[-1]
    dh = d // heads
    h = _rms(x, g_ref[...]).astype(BF16)
    q = _dot(h, wq_ref[...]).astype(BF16)
    outs = []
    for hd in range(heads):
        sl = slice(hd * dh, (hd + 1) * dh)
        s = _dot_nt(q[:, sl], mk_ref[0, :, sl]) * (dh ** -0.5)
        e = jnp.exp(s - jnp.max(s, axis=-1, keepdims=True))
        p = e / jnp.sum(e, axis=-1, keepdims=True)
        outs.append(_dot(p.astype(BF16), mv_ref[0, :, sl]).astype(BF16))
    o = jnp.concatenate(outs, axis=-1)
    o_ref[...] = x + _dot(o, wo_ref[...])


def _xattn(a, wa, x, gain, wq, wo, mem_k, mem_v):
    rows, d = x.shape
    batch, n_mem, _ = mem_k.shape
    seq = rows // batch
    tm = min(WIDE_ROW_TILE, seq)
    nt = seq // tm
    row_spec = pl.BlockSpec((tm, d), lambda b, i: (b * nt + i, 0))
    weight_spec = pl.BlockSpec((d, d), lambda b, i: (0, 0))
    mem_spec = pl.BlockSpec((1, n_mem, d), lambda b, i: (b, 0, 0))
    return pl.pallas_call(
        functools.partial(_xattn_kernel, heads=XA_HEADS),
        grid=(batch, nt),
        in_specs=[row_spec, weight_spec, row_spec, pl.BlockSpec((1, d), lambda b, i: (0, 0)),
                  weight_spec, weight_spec, mem_spec, mem_spec],
        out_specs=row_spec,
        out_shape=jax.ShapeDtypeStruct((rows, d), F32),
        input_output_aliases={2: 0},
        compiler_params=_params("arbitrary", "arbitrary"),
        name="xattn",
    )(a, wa, x, gain.reshape(1, d), wq, wo, mem_k, mem_v)


def _ffn_kernel(x_ref, g_ref, wgu_ref, wd_ref, o_ref, *, final_gain):
    x = x_ref[...]
    hidden = wd_ref.shape[0]
    h = _rms(x, g_ref[0:1, :]).astype(BF16)
    acc = x
    for c0 in range(0, hidden, FFN_CHUNK):
        c1 = min(c0 + FFN_CHUNK, hidden)
        gate = _dot(h, wgu_ref[:, c0:c1])
        up = _dot(h, wgu_ref[:, hidden + c0:hidden + c1])
        act = (gate * jax.nn.sigmoid(gate) * up).astype(BF16)
        acc = acc + _dot(act, wd_ref[c0:c1, :])
    if final_gain:
        acc = _rms(acc, g_ref[1:2, :])
    o_ref[...] = acc


def _ffn(x, gain, w_gate_up, w_down, final_gain=None):
    rows, d = x.shape
    tm = min(WIDE_ROW_TILE, rows)
    gains = jnp.stack([gain, gain if final_gain is None else final_gain])
    return pl.pallas_call(
        functools.partial(_ffn_kernel, final_gain=final_gain is not None),
        grid=(rows // tm,),
        in_specs=[pl.BlockSpec((tm, d), lambda i: (i, 0)),
                  pl.BlockSpec((2, d), lambda i: (0, 0)),
                  pl.BlockSpec(w_gate_up.shape, lambda i: (0, 0)),
                  pl.BlockSpec(w_down.shape, lambda i: (0, 0))],
        out_specs=pl.BlockSpec((tm, d), lambda i: (i, 0)),
        out_shape=jax.ShapeDtypeStruct((rows, d), F32),
        input_output_aliases={0: 0},
        compiler_params=_params("arbitrary"),
        name="ffn",
    )(x, gains, w_gate_up, w_down)


def _rw_proj_kernel(*refs, has_v_res):
    if has_v_res:
        (x_ref, gain_ref, shift0_ref, mu_ref, wr_ref, wk_ref, wv_ref, vec_ref,
         w1_ref, w2_ref, a1_ref, a2_ref, g1_ref, g2_ref, v1_ref, v2_ref, vfirst_ref,
         r_ref, lw_ref, k_ref, v_ref, kk_ref, a_ref, g_ref, shift_ref, prev_ref) = refs
    else:
        (x_ref, gain_ref, shift0_ref, mu_ref, wr_ref, wk_ref, wv_ref, vec_ref,
         w1_ref, w2_ref, a1_ref, a2_ref, g1_ref, g2_ref,
         r_ref, lw_ref, k_ref, v_ref, kk_ref, a_ref, g_ref, shift_ref, prev_ref) = refs
    i = pl.program_id(1)
    h = _rms(x_ref[...], gain_ref[...])
    tm = h.shape[0]

    @pl.when(i == 0)
    def _():
        prev_ref[...] = shift0_ref[0]

    row = lax.broadcasted_iota(jnp.int32, h.shape, 0)
    h_prev = jnp.where(row == 0, prev_ref[...], pltpu.roll(h, 1, 0))
    last = h[tm - 1:tm, :]
    prev_ref[...] = last
    shift_ref[0] = last

    xx = h_prev - h
    x_r, x_w, x_k, x_v, x_a, x_g = ((h + xx * mu_ref[j:j + 1, :]).astype(BF16) for j in range(6))
    w0, a0, k_k, k_a, v0 = (vec_ref[j:j + 1, :] for j in range(5))

    r = _dot(x_r, wr_ref[...])
    k = _dot(x_k, wk_ref[...])
    v = _dot(x_v, wv_ref[...])
    dec = w0 + _dot(jnp.tanh(_dot(x_w, w1_ref[...])).astype(BF16), w2_ref[...])
    w_log = -jax.nn.softplus(-dec) - 0.5
    if has_v_res:
        mix = jax.nn.sigmoid(v0 + _dot(_dot(x_v, v1_ref[...]).astype(BF16), v2_ref[...]))
        v = v + (vfirst_ref[...] - v) * mix
    a = jax.nn.sigmoid(a0 + _dot(_dot(x_a, a1_ref[...]).astype(BF16), a2_ref[...]))
    g = _dot(jax.nn.sigmoid(_dot(x_g, g1_ref[...])).astype(BF16), g2_ref[...])

    r_ref[...] = r.astype(r_ref.dtype)
    lw_ref[...] = -jnp.exp(w_log)
    kk_ref[...] = (k * k_k).astype(kk_ref.dtype)
    k_ref[...] = (k * (1.0 + (a - 1.0) * k_a)).astype(k_ref.dtype)
    v_ref[...] = v
    a_ref[...] = a.astype(a_ref.dtype)
    g_ref[...] = g.astype(g_ref.dtype)


def _pad_cols(w, mult=LANES):
    pad = -w.shape[1] % mult
    return jnp.pad(w, ((0, 0), (0, pad))) if pad else w


def _pad_rows(w, mult=LANES):
    pad = -w.shape[0] % mult
    return jnp.pad(w, ((0, pad), (0, 0))) if pad else w


def _rw_proj(x, batch, gain, shift0, p, v_first):
    rows, d = x.shape
    seq = rows // batch
    tm = min(WIDE_ROW_TILE, seq)
    nt = seq // tm
    has_v_res = v_first is not None
    zeros = jnp.zeros((d,), F32)
    vec = jnp.stack([p["w0"], p["a0"], p["k_k"], p["k_a"], p["v0"] if has_v_res else zeros,
                     zeros, zeros, zeros])
    row_spec = pl.BlockSpec((tm, d), lambda b, i: (b * nt + i, 0))

    def full(arr):
        return pl.BlockSpec(arr.shape, lambda b, i: (0,) * arr.ndim, pipeline_mode=pl.Buffered(1))

    ins = [x, gain.reshape(1, d), shift0.reshape(batch, 1, d), p["mu"], p["w_r"], p["w_k"], p["w_v"], vec,
           p["w1"], p["w2"], p["a1"], p["a2"], p["g1"], p["g2"]]
    in_specs = [row_spec, full(ins[1]), pl.BlockSpec((1, 1, d), lambda b, i: (b, 0, 0))]
    in_specs += [full(a) for a in ins[3:]]
    if has_v_res:
        ins += [p["v1"], p["v2"], v_first]
        in_specs += [full(p["v1"]), full(p["v2"]), row_spec]
    tok = [jax.ShapeDtypeStruct((rows, d), dt) for dt in (BF16, F32, BF16, F32, BF16, BF16, BF16)]
    outs = pl.pallas_call(
        functools.partial(_rw_proj_kernel, has_v_res=has_v_res),
        grid=(batch, nt),
        in_specs=in_specs,
        out_specs=[row_spec] * 7 + [pl.BlockSpec((1, 1, d), lambda b, i: (b, 0, 0))],
        out_shape=tok + [jax.ShapeDtypeStruct((batch, 1, d), F32)],
        scratch_shapes=[pltpu.VMEM((1, d), F32)],
        compiler_params=_params("arbitrary", "arbitrary"),
        name="rw_proj",
    )(*ins)
    return outs[:7], outs[7].reshape(batch, d)


WKV_DECAY_PIECES = 3
WKV_GROUP_PIECES = 2
WKV_STATE_PASSES = 1
WKV_SOLVE_PASSES = 1


def _wkv_prepare(units, c_):
    n = units[0][0].shape[0]
    sum_l = lambda x: _mm_exact_lhs(c_["tri_incl"], x, WKV_DECAY_PIECES)
    sum_g = lambda x: _mm_exact_rhs(x, c_["gsum"], WKV_GROUP_PIECES)
    solve_mm = lambda x, y, nt=False: _mm(x, y, WKV_SOLVE_PASSES, nt=nt)

    cs = [sum_l(u[1]) for u in units]
    head_sums = [sum_g(jnp.concatenate([u[4] * u[4], u[0] * u[2] * u[7][0:1, :]], axis=0)) for u in units]
    pre = []
    for (r, lw, k, v, kk, a, g, vec), cs_u, hs in zip(units, cs, head_sums):
        cs_last = cs_u[n - 1:n, :]
        d_inv = jnp.exp(-cs_u)
        d_rest = jnp.exp(cs_last - cs_u)
        kkn = kk * lax.rsqrt(jnp.maximum(hs[:n], 1e-24))
        b = kkn * a
        pre.append(dict(a_t=-kkn * jnp.exp(cs_u - lw), r_t=r * jnp.exp(cs_u), b_t=b * d_inv, k_t=k * d_inv,
                        bk_rest=jnp.concatenate([b * d_rest, k * d_rest], axis=0), decay=jnp.exp(cs_last),
                        v=v, g=g, vec=vec, bonus=hs[n:] * v))

    chains = [(i, m) for i in range(len(units)) for m in c_["head_masks"]]
    grams = []
    for i, m in chains:
        q = pre[i]
        zero = jnp.zeros_like(q["a_t"])
        lhs = jnp.concatenate([jnp.where(m, q["a_t"], zero), jnp.where(m, q["r_t"], zero)], axis=0)
        grams.append(solve_mm(lhs, jnp.concatenate([q["b_t"], q["k_t"]], axis=0), nt=True))
    lows = [jnp.where(c_["strict"], gm[:n, :n], 0.0) for gm in grams]
    m_aks = [jnp.where(c_["strict_k"], gm[:n], 0.0) for gm in grams]
    m_rs = [jnp.where(c_["incl2"], gm[n:], 0.0) for gm in grams]
    from_v = [solve_mm(mk, jnp.concatenate([jnp.zeros_like(pre[i]["v"]), pre[i]["v"]], axis=0))
              for (i, _), mk in zip(chains, m_aks)]
    inv = [c_["eye"] + lo for lo in lows]
    pw = lows
    step = 1
    while 2 * step < n:
        pw = [solve_mm(x, x) for x in pw]
        inv = [t + solve_mm(t, p2) for t, p2 in zip(inv, pw)]
        step *= 2
    for i, q in enumerate(pre):
        q["heads"] = [(inv[2 * i + h], from_v[2 * i + h], m_rs[2 * i + h]) for h in range(2)]
    return pre


def _wkv_advance(pre, states, c_):
    n = pre[0]["v"].shape[0]
    inv_hd = 1.0 / RW_HEAD_DIM
    sum_g = lambda x: _mm_exact_rhs(x, c_["gsum"], WKV_GROUP_PIECES)
    solve_mm = lambda x, y: _mm(x, y, WKV_SOLVE_PASSES)
    first = c_["head_masks"][0]

    from_state = [_mm(jnp.concatenate([q["a_t"], q["r_t"]], axis=0), s, WKV_STATE_PASSES, nt=True)
                  for q, s in zip(pre, states)]
    sig = [[solve_mm(inv, fs[:n] + fv) for inv, fv, _ in q["heads"]] for q, fs in zip(pre, from_state)]
    ys = [[fs[n:] + solve_mm(m_r, jnp.concatenate([sg, q["v"]], axis=0))
           for (_, _, m_r), sg in zip(q["heads"], sgs)] for q, fs, sgs in zip(pre, from_state, sig)]
    sig_p = [jnp.where(first, s0, s1) for s0, s1 in sig]
    y_p = [jnp.where(first, y0, y1) for y0, y1 in ys]
    upd = [_mm(jnp.concatenate([s, q["v"]], axis=0).T, q["bk_rest"], WKV_STATE_PASSES)
           for s, q in zip(sig_p, pre)]
    new_states = [s * q["decay"] + jnp.where(c_["same_head"], u, 0.0) for s, q, u in zip(states, pre, upd)]

    mean = [sum_g(y) * inv_hd for y in y_p]
    dev = [y - m for y, m in zip(y_p, mean)]
    var = [sum_g(dv * dv) * inv_hd for dv in dev]
    outs = []
    for q, dv, vr in zip(pre, dev, var):
        o = dv * lax.rsqrt(vr + RW_GN_EPS) * q["vec"][1:2, :] + q["vec"][2:3, :]
        outs.append((o + q["bonus"]) * q["g"])
    return outs, new_states


def _wkv_kernel(*refs, has_state0, n):
    if has_state0:
        (r_ref, lw_ref, k_ref, v_ref, kk_ref, a_ref, g_ref, vec_ref, s0_ref,
         o_ref, sout_ref, state_ref) = refs
    else:
        (r_ref, lw_ref, k_ref, v_ref, kk_ref, a_ref, g_ref, vec_ref,
         o_ref, sout_ref, state_ref) = refs
    c = pl.program_id(2)
    hd = RW_HEAD_DIM

    @pl.when(c == 0)
    def _():
        if has_state0:
            state_ref[...] = s0_ref[0]
        else:
            state_ref[...] = jnp.zeros_like(state_ref)

    tr = lax.broadcasted_iota(jnp.int32, (n, n), 0)
    tc = lax.broadcasted_iota(jnp.int32, (n, n), 1)
    tr2 = lax.broadcasted_iota(jnp.int32, (n, 2 * n), 0)
    tc2 = lax.broadcasted_iota(jnp.int32, (n, 2 * n), 1)
    lane = lax.broadcasted_iota(jnp.int32, (n, LANES), 1)
    sr = lax.broadcasted_iota(jnp.int32, (LANES, LANES), 0) // hd
    sc = lax.broadcasted_iota(jnp.int32, (LANES, LANES), 1) // hd
    consts = dict(
        gsum=_group_ones(LANES, hd),
        tri_incl=jnp.where(tc <= tr, 1.0, 0.0).astype(BF16),
        strict=tc < tr,
        strict_k=(tc2 >= n) & (tc2 - n < tr2),
        incl2=jnp.where(tc2 >= n, tc2 - n, tc2) <= tr2,
        eye=jnp.where(tc == tr, 1.0, 0.0).astype(F32),
        head_masks=[lane < hd, lane >= hd],
        same_head=sr == sc,
    )
    n_pairs = state_ref.shape[0]
    n_chunks = r_ref.shape[0] // n
    units = []
    for ci in range(n_chunks):
        rows = slice(ci * n, (ci + 1) * n)
        for p in range(n_pairs):
            sl = slice(p * LANES, (p + 1) * LANES)
            units.append(tuple(ref[rows, sl].astype(F32)
                               for ref in (r_ref, lw_ref, k_ref, v_ref, kk_ref, a_ref, g_ref)) + (vec_ref[:, sl],))
    pre = _wkv_prepare(units, consts)
    states = [state_ref[p] for p in range(n_pairs)]
    outs = []
    for ci in range(n_chunks):
        out, states = _wkv_advance(pre[ci * n_pairs:(ci + 1) * n_pairs], states, consts)
        outs.append(jnp.concatenate(out, axis=1))
    o_ref[...] = jnp.concatenate(outs, axis=0).astype(o_ref.dtype)
    new_states = jnp.stack(states)
    state_ref[...] = new_states
    sout_ref[0] = new_states


def _wkv(tok, batch, vec, state0):
    rows, d = tok[0].shape
    seq = rows // batch
    n = min(WKV_CHUNK, seq)
    span = n * min(WKV_CHUNKS_PER_STEP, seq // n)
    nc = seq // span
    pairs = d // LANES
    gp = WKV_PAIRS
    width = gp * LANES
    has_state0 = state0 is not None
    tok_spec = pl.BlockSpec((span, width), lambda b, p, c: (b * nc + c, p))
    state_spec = pl.BlockSpec((1, gp, LANES, LANES), lambda b, p, c: (b, p, 0, 0))
    ins = list(tok) + [vec]
    in_specs = [tok_spec] * 7 + [pl.BlockSpec((8, width), lambda b, p, c: (0, p))]
    if has_state0:
        ins.append(state0)
        in_specs.append(state_spec)
    return pl.pallas_call(
        functools.partial(_wkv_kernel, has_state0=has_state0, n=n),
        grid=(batch, pairs // gp, nc),
        in_specs=in_specs,
        out_specs=[tok_spec, state_spec],
        out_shape=[jax.ShapeDtypeStruct((rows, d), BF16),
                   jax.ShapeDtypeStruct((batch, pairs, LANES, LANES), F32)],
        scratch_shapes=[pltpu.VMEM((gp, LANES, LANES), F32)],
        compiler_params=_params("arbitrary", "arbitrary", "arbitrary"),
        name="wkv",
    )(*ins)


def _state_to_blockdiag(state):
    b, h, n, _ = state.shape
    s = state.reshape(b, h // 2, 2, n, n)
    z = jnp.zeros_like(s[:, :, 0])
    top = jnp.concatenate([s[:, :, 0], z], axis=-1)
    bot = jnp.concatenate([z, s[:, :, 1]], axis=-1)
    return jnp.concatenate([top, bot], axis=-2)


def _state_from_blockdiag(bd):
    b, p, n2, _ = bd.shape
    n = n2 // 2
    return jnp.stack([bd[:, :, :n, :n], bd[:, :, n:, n:]], axis=2).reshape(b, 2 * p, n, n)


def _trunk(x, batch, mem_k, mem_v, past_k, past_v, rw_state0, rw_shift0, w):
    depth = len(w["ffn_gu"])
    d = x.shape[-1]
    sb_k, sb_v, rw_states, rw_shifts = None, None, [], []
    n_sb = (depth + 1) // 2
    v_first = None
    for layer in range(depth):
        j = layer // 2
        if layer % 2 == 0:
            sb_k, sb_v, qkv = _sb_qkv(x, batch, w["norm_mix"][layer], w["sb_qkv"][j], j, n_sb, sb_k, sb_v)
            if past_k is None:
                o = _sb_attention_prompt(qkv, batch)
            else:
                o = _sb_attention_sample(qkv, batch, past_k, past_v, j)
            w_o = w["sb_o"][j]
        else:
            if rw_state0 is None:
                state0 = None
                shift0 = jnp.zeros((batch, d), F32)
            else:
                state0 = _state_to_blockdiag(rw_state0[j])
                shift0 = rw_shift0[j]
            tok, shift = _rw_proj(x, batch, w["norm_mix"][layer], shift0, w["rw"][j], v_first)
            if v_first is None:
                v_first = tok[3]
            o, state = _wkv(tok, batch, w["rw"][j]["out_vec"], state0)
            rw_states.append(_state_from_blockdiag(state))
            rw_shifts.append(shift)
            w_o = w["rw"][j]["w_o"]
        x = _xattn(o, w_o, x, w["norm_xattn"][layer], w["xa_q"][layer], w["xa_o"][layer],
                   mem_k[layer], mem_v[layer])
        final = w["norm_final"] if layer == depth - 1 else None
        x = _ffn(x, w["norm_ffn"][layer], w["ffn_gu"][layer], w["ffn_down"][layer], final)
    return x, sb_k, sb_v, jnp.stack(rw_states), jnp.stack(rw_shifts)


def kernel(x_prompt, x_sample, mem_prompt, cache_sb_k, cache_sb_v, state_rwkv_wkv, state_rwkv_shift,
           cache_mem_k, cache_mem_v, norm_mix, norm_xattn, norm_ffn, norm_mem, norm_final,
           sb_w_qkv, sb_w_o, rw_mu, rw_w_r, rw_w_k, rw_w_v, rw_w_o, rw_w0, rw_w1, rw_w2,
           rw_a0, rw_a1, rw_a2, rw_v0, rw_v1, rw_v2, rw_g1, rw_g2, rw_k_k, rw_k_a, rw_r_k,
           rw_lnx_w, rw_lnx_b, xa_w_q, xa_w_kv, xa_w_o, ffn_w_gate_up, ffn_w_down):
    batch, seq, d = x_prompt.shape
    dec_batch, dec_seq, _ = x_sample.shape
    depth = norm_mix.shape[0]
    n_rw = rw_w_r.shape[0]
    n_sb = sb_w_qkv.shape[0]
    n_mem = mem_prompt.shape[1]
    bf = lambda t: t.astype(BF16)

    rw = []
    for j in range(n_rw):
        pj = dict(mu=rw_mu[j], w_r=bf(rw_w_r[j]), w_k=bf(rw_w_k[j]), w_v=bf(rw_w_v[j]), w_o=bf(rw_w_o[j]),
                  w0=rw_w0[j], a0=rw_a0[j], k_k=rw_k_k[j], k_a=rw_k_a[j],
                  w1=bf(_pad_cols(rw_w1[j])), w2=bf(_pad_rows(rw_w2[j])),
                  a1=bf(_pad_cols(rw_a1[j])), a2=bf(_pad_rows(rw_a2[j])),
                  g1=bf(_pad_cols(rw_g1[j])), g2=bf(_pad_rows(rw_g2[j])))
        if j > 0:
            pj.update(v0=rw_v0[j - 1], v1=bf(_pad_cols(rw_v1[j - 1])), v2=bf(_pad_rows(rw_v2[j - 1])))
        zeros = jnp.zeros((d,), F32)
        pj["out_vec"] = jnp.stack([rw_r_k[j].reshape(d), rw_lnx_w[j], rw_lnx_b[j]] + [zeros] * 5)
        rw.append(pj)
    w = dict(norm_mix=norm_mix, norm_xattn=norm_xattn, norm_ffn=norm_ffn, norm_final=norm_final,
             sb_qkv=bf(sb_w_qkv), sb_o=bf(sb_w_o), rw=rw, xa_q=bf(xa_w_q), xa_o=bf(xa_w_o),
             ffn_gu=bf(ffn_w_gate_up), ffn_down=bf(ffn_w_down))

    mem_k, mem_v, mem_k_bf, mem_v_bf = _mem_kv(mem_prompt, norm_mem, bf(xa_w_kv))
    y_p, p_sb_k, p_sb_v, p_wkv, p_shift = _trunk(
        x_prompt.reshape(batch * seq, d), batch, mem_k_bf, mem_v_bf, None, None, None, None, w)

    y_s, s_sb_k, s_sb_v, s_wkv, s_shift = _trunk(
        x_sample.reshape(dec_batch * dec_seq, d), dec_batch,
        bf(cache_mem_k).reshape(depth, dec_batch, n_mem, d), bf(cache_mem_v).reshape(depth, dec_batch, n_mem, d),
        cache_sb_k.reshape(n_sb, dec_batch, -1, d), cache_sb_v.reshape(n_sb, dec_batch, -1, d),
        state_rwkv_wkv, state_rwkv_shift, w)

    hd = d // SB_HEADS
    return (y_p.reshape(batch, seq, d), y_s.reshape(dec_batch, dec_seq, d),
            p_sb_k.reshape(n_sb, batch, seq, SB_HEADS, hd), p_sb_v.reshape(n_sb, batch, seq, SB_HEADS, hd),
            p_wkv, p_shift,
            mem_k, mem_v,
            s_sb_k.reshape(n_sb, dec_batch, dec_seq, SB_HEADS, hd),
            s_sb_v.reshape(n_sb, dec_batch, dec_seq, SB_HEADS, hd),
            s_wkv, s_shift)
```

```python
import functools

import jax
import jax.numpy as jnp
from jax import lax
from jax.experimental import pallas as pl
from jax.experimental.pallas import tpu as pltpu

F32 = jnp.float32
BF16 = jnp.bfloat16

NORM_EPS = 1e-6
NEG_LOG2_E = -1.4426950408889634
RW_GN_EPS = 64e-5
SB_HEADS = 16
XA_HEADS = 4
RW_HEAD_DIM = 64
LANES = 128
ROW_TILE = 256
WIDE_ROW_TILE = 512
SB_BLOCK = 256
SB_PAIRS = 4
SB_WAVES = 2
SB_UNROLL = 2
SB_PAST_SPAN = 2048
WKV_CHUNK = 64
WKV_CHUNKS_PER_STEP = 4
WKV_PAIRS = 8
FFN_CHUNK = 256
VMEM_LIMIT_BYTES = 56 * 1024 * 1024


def _params(*sem):
    return pltpu.CompilerParams(dimension_semantics=sem, vmem_limit_bytes=VMEM_LIMIT_BYTES)


def _dot(a, b):
    return jnp.dot(a, b, preferred_element_type=F32)


def _dot_nt(a, b):
    return lax.dot_general(a, b, (((1,), (1,)), ((), ())), preferred_element_type=F32)


def _split(x, n):
    parts = []
    for i in range(n):
        p = x.astype(BF16)
        parts.append(p)
        if i + 1 < n:
            x = x - p.astype(F32)
    return parts


def _mm(a, b, passes, nt=False):
    dot = _dot_nt if nt else _dot
    if passes == 1:
        return dot(a.astype(BF16), b.astype(BF16))
    a_hi, a_lo = _split(a, 2)
    b_hi, b_lo = _split(b, 2)
    return dot(a_hi, b_hi) + (dot(a_hi, b_lo) + dot(a_lo, b_hi))


def _mm_exact_rhs(a, b_bf16, n):
    out = None
    for p in _split(a, n):
        t = _dot(p, b_bf16)
        out = t if out is None else out + t
    return out


def _mm_exact_lhs(a_bf16, b, n):
    out = None
    for p in _split(b, n):
        t = _dot(a_bf16, p)
        out = t if out is None else out + t
    return out


def _rms(x, gain):
    ms = jnp.mean(x * x, axis=-1, keepdims=True)
    return x * lax.rsqrt(ms + NORM_EPS) * gain


def _group_ones(n, group):
    r = lax.broadcasted_iota(jnp.int32, (n, n), 0) // group
    c = lax.broadcasted_iota(jnp.int32, (n, n), 1) // group
    return jnp.where(r == c, 1.0, 0.0).astype(BF16)


def _mem_kv_kernel(x_ref, g_ref, w_ref, k_ref, v_ref, kb_ref, vb_ref):
    tm, d = x_ref.shape
    heads, dh = k_ref.shape[-2:]
    h = _rms(x_ref[...], g_ref[0]).astype(BF16)
    kv = _dot(h, w_ref[0])
    k, v = kv[:, :d], kv[:, d:]
    k_ref[0, 0] = k.reshape(tm, heads, dh)
    v_ref[0, 0] = v.reshape(tm, heads, dh)
    kb_ref[0, 0] = k.astype(BF16)
    vb_ref[0, 0] = v.astype(BF16)


def _mem_kv(mem, gains, w_kv):
    depth = w_kv.shape[0]
    batch, n_mem, d = mem.shape
    tm = min(ROW_TILE, n_mem)
    nt = n_mem // tm
    dh = d // XA_HEADS
    assert n_mem % tm == 0 and dh % LANES == 0
    out = jax.ShapeDtypeStruct((depth, batch, n_mem, XA_HEADS, dh), F32)
    out_b = jax.ShapeDtypeStruct((depth, batch, n_mem, d), BF16)
    spec = pl.BlockSpec((1, 1, tm, XA_HEADS, dh), lambda l, i: (l, i // nt, i % nt, 0, 0))
    spec_b = pl.BlockSpec((1, 1, tm, d), lambda l, i: (l, i // nt, i % nt, 0))
    return pl.pallas_call(
        _mem_kv_kernel,
        grid=(depth, batch * nt),
        in_specs=[pl.BlockSpec((tm, d), lambda l, i: (i, 0)),
                  pl.BlockSpec((1, 1, d), lambda l, i: (l, 0, 0)),
                  pl.BlockSpec((1, d, 2 * d), lambda l, i: (l, 0, 0))],
        out_specs=[spec, spec, spec_b, spec_b],
        out_shape=[out, out, out_b, out_b],
        compiler_params=_params("arbitrary", "arbitrary"),
        name="mem_kv",
    )(mem.reshape(batch * n_mem, d), gains.reshape(depth, 1, d), w_kv)


def _split_heads(x, hd):
    tm, d = x.shape
    blocks = []
    for p in range(d // LANES):
        slab = x[:, p * LANES:(p + 1) * LANES]
        blocks += [slab, pltpu.roll(slab, LANES - hd, 1)]
    return jnp.concatenate(blocks, axis=1).reshape(tm, d // hd, LANES)[:, :, :hd]


def _sb_qkv_kernel(x_ref, g_ref, w_ref, *rest, q_scale, slot):
    k_ref, v_ref, qkv_ref = rest[-3:]
    d = x_ref.shape[-1]
    hd = k_ref.shape[-1]
    h = _rms(x_ref[...], g_ref[...]).astype(BF16)
    qkv = _dot(h, w_ref[...])
    qkv_ref[:, :d] = (qkv[:, :d] * q_scale).astype(BF16)
    qkv_ref[:, d:] = qkv[:, d:].astype(BF16)
    k3 = _split_heads(qkv[:, d:2 * d], hd)
    v3 = _split_heads(qkv[:, 2 * d:], hd)
    own = slot if k_ref.shape[0] > 1 else 0
    for s in range(k_ref.shape[0]):
        if s == own:
            k_ref[s, 0] = k3
            v_ref[s, 0] = v3
        else:
            k_ref[s, 0] = jnp.zeros(k3.shape, k_ref.dtype)
            v_ref[s, 0] = jnp.zeros(v3.shape, v_ref.dtype)


def _sb_qkv(x, batch, gain, w, slot, n_slots, k_all=None, v_all=None):
    rows, d = x.shape
    seq = rows // batch
    tm = min(ROW_TILE, seq)
    nt = seq // tm
    hd = d // SB_HEADS
    assert 2 * hd == LANES
    q_scale = float(hd ** -0.5)
    ins = [x, gain.reshape(1, d), w]
    in_specs = [pl.BlockSpec((tm, d), lambda i: (i, 0)),
                pl.BlockSpec((1, d), lambda i: (0, 0)),
                pl.BlockSpec(w.shape, lambda i: (0, 0))]
    if k_all is None:
        aliases = {}
        slot_spec = pl.BlockSpec((n_slots, 1, tm, SB_HEADS, hd), lambda i: (0, i // nt, i % nt, 0, 0))
    else:
        ins += [k_all, v_all]
        in_specs += [pl.BlockSpec(memory_space=pl.ANY)] * 2
        aliases = {3: 0, 4: 1}
        slot_spec = pl.BlockSpec((1, 1, tm, SB_HEADS, hd), lambda i: (slot, i // nt, i % nt, 0, 0))
    kv_shape = jax.ShapeDtypeStruct((n_slots, batch, seq, SB_HEADS, hd), F32)
    return pl.pallas_call(
        functools.partial(_sb_qkv_kernel, q_scale=q_scale, slot=slot),
        grid=(rows // tm,),
        in_specs=in_specs,
        out_specs=[slot_spec, slot_spec, pl.BlockSpec((tm, 3 * d), lambda i: (i, 0))],
        out_shape=[kv_shape, kv_shape, jax.ShapeDtypeStruct((rows, 3 * d), BF16)],
        input_output_aliases=aliases,
        compiler_params=_params("arbitrary"),
        name="sb_qkv",
    )(*ins)


def _sb_step(q2s, kv_blocks, carry, acc, neg_lower2, mask):
    m = q2s[0].shape[0]
    groups = len(q2s)
    per = max(groups // SB_WAVES, 1)
    waves = [range(c, min(c + per, groups)) for c in range(0, groups, per)]
    rows = [slice(w_[0] * m, (w_[-1] + 1) * m) for w_ in waves]
    units = [(b, c) for b in range(len(kv_blocks)) for c in range(len(waves))]

    def scores(b, c):
        return jnp.concatenate([_dot_nt(q2s[g], kv_blocks[b][0][g]) for g in waves[c]], axis=0)

    def softplus_pieces(c, z):
        sp = jnp.maximum(z, 0.0) + jnp.log(1.0 + jnp.exp2(jnp.abs(z) * NEG_LOG2_E))
        if mask is not None:
            sp = jnp.where(mask[rows[c]], sp, 0.0)
        return jnp.concatenate(_split(sp, 2), axis=1)

    def weights(c, z, from_s, later):
        w = jnp.exp(z + from_s + jnp.concatenate([later] * (z.shape[1] // LANES), axis=1))
        if mask is not None:
            w = jnp.where(mask[rows[c]], w, 0.0)
        return w.astype(BF16)

    def values(b, c, w):
        return jnp.concatenate([_dot(w[i * m:(i + 1) * m], kv_blocks[b][1][g])
                                for i, g in enumerate(waves[c])], axis=0)

    n = len(units)
    z, pieces, from_s, w = ([None] * n for _ in range(4))
    later = [carry[r] for r in rows]
    pv = [None] * len(waves)
    for tick in range(n + 4):
        for u, (b, c) in enumerate(units):
            stage = tick - u
            if stage == 0:
                z[u] = scores(b, c)
            elif stage == 1:
                pieces[u] = softplus_pieces(c, z[u])
            elif stage == 2:
                from_s[u] = _dot(pieces[u], neg_lower2)
            elif stage == 3:
                w[u] = weights(c, z[u], from_s[u], later[c])
                later[c] = later[c] + jnp.broadcast_to(from_s[u][:, 0:1], later[c].shape)
            elif stage == 4:
                out = values(b, c, w[u])
                pv[c] = out if pv[c] is None else pv[c] + out
    return jnp.concatenate(later, axis=0), acc + jnp.concatenate(pv, axis=0)


def _sb_consts(bk):
    r = lax.broadcasted_iota(jnp.int32, (2, bk, bk), 1).reshape(2 * bk, bk)
    c = lax.broadcasted_iota(jnp.int32, (2 * bk, bk), 1)
    return jnp.where(r >= c, -1.0, 0.0).astype(BF16)


def _sb_stack_heads(q_ref, q2_ref):
    tq = q_ref.shape[0]
    lane = lax.broadcasted_iota(jnp.int32, (tq, LANES), 1)
    for g in range(q2_ref.shape[0]):
        q = q_ref[:, g * LANES:(g + 1) * LANES]
        zero = jnp.zeros_like(q)
        q2_ref[g] = jnp.concatenate([jnp.where(lane < LANES // 2, q, zero),
                                     jnp.where(lane >= LANES // 2, q, zero)], axis=0)


def _sb_causal_mask(groups, tq, bk):
    t = lax.broadcasted_iota(jnp.int32, (2 * groups, tq, bk), 1).reshape(2 * groups * tq, bk)
    s = lax.broadcasted_iota(jnp.int32, (2 * groups * tq, bk), 1)
    return s < t


def _sb_write(o_ref, acc_ref):
    tq = o_ref.shape[0]
    lane = lax.broadcasted_iota(jnp.int32, (tq, LANES), 1)
    for g in range(acc_ref.shape[0] // (2 * tq)):
        o_ref[:, g * LANES:(g + 1) * LANES] = jnp.where(
            lane < LANES // 2, acc_ref[2 * g * tq:(2 * g + 1) * tq, :],
            acc_ref[(2 * g + 1) * tq:(2 * g + 2) * tq, :]).astype(o_ref.dtype)


def _sb_prompt_kernel(q_ref, k_ref, v_ref, o_ref, q2_ref, carry_ref, acc_ref):
    blk = q_ref.shape[0]
    groups = q2_ref.shape[0]
    i = pl.program_id(2)
    neg_lower2 = _sb_consts(blk)
    _sb_stack_heads(q_ref, q2_ref)
    lanes = [slice(g * LANES, (g + 1) * LANES) for g in range(groups)]

    def block(j):
        row0 = pl.multiple_of(j * blk, blk)
        return ([k_ref[pl.ds(row0, blk), sl] for sl in lanes], [v_ref[pl.ds(row0, blk), sl] for sl in lanes])

    def step(blocks, carry, acc, mask):
        return _sb_step([q2_ref[g] for g in range(groups)], blocks, carry, acc, neg_lower2, mask)

    zero = jnp.zeros(carry_ref.shape, F32)
    carry_ref[...], acc_ref[...] = step([block(i)], zero, zero, _sb_causal_mask(groups, blk, blk))

    odd = i % SB_UNROLL

    def single(it, c):
        carry_ref[...], acc_ref[...] = step([block(i - 1 - it)], carry_ref[...], acc_ref[...], None)
        return c

    def multi(it, c):
        top = i - 1 - odd - it * SB_UNROLL
        carry_ref[...], acc_ref[...] = step([block(top - u) for u in range(SB_UNROLL)],
                                            carry_ref[...], acc_ref[...], None)
        return c

    lax.fori_loop(0, odd, single, 0)
    lax.fori_loop(0, i // SB_UNROLL, multi, 0)
    _sb_write(o_ref, acc_ref)


def _sb_attention_prompt(qkv, batch):
    rows, d3 = qkv.shape
    d = d3 // 3
    seq = rows // batch
    blk = min(SB_BLOCK, seq)
    nq = seq // blk
    width = SB_PAIRS * LANES
    ng = d // width
    return pl.pallas_call(
        _sb_prompt_kernel,
        grid=(batch, ng, nq),
        in_specs=[pl.BlockSpec((blk, width), lambda b, p, i: (b * nq + i, p)),
                  pl.BlockSpec((seq, width), lambda b, p, i: (b, ng + p)),
                  pl.BlockSpec((seq, width), lambda b, p, i: (b, 2 * ng + p))],
        out_specs=pl.BlockSpec((blk, width), lambda b, p, i: (b * nq + i, p)),
        out_shape=jax.ShapeDtypeStruct((rows, d), BF16),
        scratch_shapes=[pltpu.VMEM((SB_PAIRS, 2 * blk, LANES), BF16),
                        pltpu.VMEM((SB_PAIRS * 2 * blk, LANES), F32),
                        pltpu.VMEM((SB_PAIRS * 2 * blk, LANES), F32)],
        compiler_params=_params("arbitrary", "arbitrary", "arbitrary"),
        name="sb_attention_prompt",
    )(qkv, qkv, qkv)


def _sb_sample_kernel(q_ref, kn_ref, vn_ref, kp_ref, vp_ref, o_ref, q2_ref, carry_ref, acc_ref, *, blk):
    tq = q_ref.shape[0]
    groups = q2_ref.shape[0]
    c = pl.program_id(1)
    neg_lower2 = _sb_consts(blk)

    lanes = [slice(g * LANES, (g + 1) * LANES) for g in range(groups)]

    @pl.when(c == 0)
    def _():
        _sb_stack_heads(q_ref, q2_ref)
        zero = jnp.zeros(carry_ref.shape, F32)
        carry_ref[...], acc_ref[...] = _sb_step(
            [q2_ref[g] for g in range(groups)],
            [([kn_ref[0, :, sl] for sl in lanes], [vn_ref[0, :, sl] for sl in lanes])],
            zero, zero, neg_lower2, _sb_causal_mask(groups, tq, blk))

    blocks = []
    for sb in reversed(range(kp_ref.shape[2] // blk)):
        rows = slice(sb * blk, (sb + 1) * blk)
        blocks.append(([kp_ref[0, 0, rows, sl].astype(BF16) for sl in lanes],
                       [vp_ref[0, 0, rows, sl].astype(BF16) for sl in lanes]))
    carry_ref[...], acc_ref[...] = _sb_step([q2_ref[g] for g in range(groups)], blocks,
                                            carry_ref[...], acc_ref[...], neg_lower2, None)

    @pl.when(c == pl.num_programs(1) - 1)
    def _():
        _sb_write(o_ref, acc_ref)


def _sb_attention_sample(qkv, batch, past_k, past_v, layer):
    rows, d3 = qkv.shape
    d = d3 // 3
    seq = rows // batch
    blk = SB_BLOCK
    pairs = d // LANES
    past_len = past_k.shape[2]
    span = min(SB_PAST_SPAN, past_len)
    steps = past_len // span
    assert seq <= blk and past_len % span == 0 and span % blk == 0
    new = qkv[:, d:].reshape(batch, seq, 2 * d)
    new = jnp.pad(new, ((0, 0), (0, blk - seq), (0, 0)))
    return pl.pallas_call(
        functools.partial(_sb_sample_kernel, blk=blk),
        grid=(batch, steps),
        in_specs=[pl.BlockSpec((seq, d), lambda b, c: (b, 0)),
                  pl.BlockSpec((1, blk, d), lambda b, c: (b, 0, 0)),
                  pl.BlockSpec((1, blk, d), lambda b, c: (b, 0, 1)),
                  pl.BlockSpec((1, 1, span, d), lambda b, c: (layer, b, steps - 1 - c, 0)),
                  pl.BlockSpec((1, 1, span, d), lambda b, c: (layer, b, steps - 1 - c, 0))],
        out_specs=pl.BlockSpec((seq, d), lambda b, c: (b, 0)),
        out_shape=jax.ShapeDtypeStruct((rows, d), BF16),
        scratch_shapes=[pltpu.VMEM((pairs, 2 * seq, LANES), BF16),
                        pltpu.VMEM((pairs * 2 * seq, LANES), F32),
                        pltpu.VMEM((pairs * 2 * seq, LANES), F32)],
        compiler_params=_params("arbitrary", "arbitrary"),
        name="sb_attention_sample",
    )(qkv, new, new, past_k, past_v)


def _xattn_kernel(a_ref, wa_ref, x_ref, g_ref, wq_ref, wo_ref, mk_ref, mv_ref, o_ref, *, heads):
    x = x_ref[...] + _dot(a_ref[...], wa_ref[...])
    d = x.shape[-1]
    dh = d // heads
    h = _rms(x, g_ref[...]).astype(BF16)
    q = _dot(h, wq_ref[...]).astype(BF16)
    nb = mk_ref.shape[0]
    seq = x.shape[0] // nb
    per_seq = []
    for b in range(nb):
        rows = slice(b * seq, (b + 1) * seq)
        outs = []
        for hd in range(heads):
            sl = slice(hd * dh, (hd + 1) * dh)
            s = _dot_nt(q[rows, sl], mk_ref[b, :, sl]) * (dh ** -0.5)
            e = jnp.exp(s - jnp.max(s, axis=-1, keepdims=True))
            p = e / jnp.sum(e, axis=-1, keepdims=True)
            outs.append(_dot(p.astype(BF16), mv_ref[b, :, sl]).astype(BF16))
        per_seq.append(jnp.concatenate(outs, axis=-1))
    o = per_seq[0] if nb == 1 else jnp.concatenate(per_seq, axis=0)
    o_ref[...] = x + _dot(o, wo_ref[...])


def _xattn(a, wa, x, gain, wq, wo, mem_k, mem_v):
    rows, d = x.shape
    batch, n_mem, _ = mem_k.shape
    seq = rows // batch
    nb = min(batch, max(1, WIDE_ROW_TILE // seq))
    assert batch % nb == 0
    tm = nb * seq if nb > 1 else min(WIDE_ROW_TILE, seq)
    nt = 1 if nb > 1 else seq // tm
    row_spec = pl.BlockSpec((tm, d), lambda b, i: (b * nt + i, 0))
    weight_spec = pl.BlockSpec((d, d), lambda b, i: (0, 0))
    mem_spec = pl.BlockSpec((nb, n_mem, d), lambda b, i: (b, 0, 0))
    return pl.pallas_call(
        functools.partial(_xattn_kernel, heads=XA_HEADS),
        grid=(batch // nb, nt),
        in_specs=[row_spec, weight_spec, row_spec, pl.BlockSpec((1, d), lambda b, i: (0, 0)),
                  weight_spec, weight_spec, mem_spec, mem_spec],
        out_specs=row_spec,
        out_shape=jax.ShapeDtypeStruct((rows, d), F32),
        input_output_aliases={2: 0},
        compiler_params=_params("arbitrary", "arbitrary"),
        name="xattn",
    )(a, wa, x, gain.reshape(1, d), wq, wo, mem_k, mem_v)


def _ffn_kernel(x_ref, g_ref, wgu_ref, wd_ref, o_ref, *, final_gain):
    x = x_ref[...]
    hidden = wd_ref.shape[0]
    h = _rms(x, g_ref[0:1, :]).astype(BF16)
    acc = x
    for c0 in range(0, hidden, FFN_CHUNK):
        c1 = min(c0 + FFN_CHUNK, hidden)
        gate = _dot(h, wgu_ref[:, c0:c1])
        up = _dot(h, wgu_ref[:, hidden + c0:hidden + c1])
        act = (gate * jax.nn.sigmoid(gate) * up).astype(BF16)
        acc = acc + _dot(act, wd_ref[c0:c1, :])
    if final_gain:
        acc = _rms(acc, g_ref[1:2, :])
    o_ref[...] = acc


def _ffn(x, gain, w_gate_up, w_down, final_gain=None):
    rows, d = x.shape
    tm = min(WIDE_ROW_TILE, rows)
    gains = jnp.stack([gain, gain if final_gain is None else final_gain])
    return pl.pallas_call(
        functools.partial(_ffn_kernel, final_gain=final_gain is not None),
        grid=(rows // tm,),
        in_specs=[pl.BlockSpec((tm, d), lambda i: (i, 0)),
                  pl.BlockSpec((2, d), lambda i: (0, 0)),
                  pl.BlockSpec(w_gate_up.shape, lambda i: (0, 0)),
                  pl.BlockSpec(w_down.shape, lambda i: (0, 0))],
        out_specs=pl.BlockSpec((tm, d), lambda i: (i, 0)),
        out_shape=jax.ShapeDtypeStruct((rows, d), F32),
        input_output_aliases={0: 0},
        compiler_params=_params("arbitrary"),
        name="ffn",
    )(x, gains, w_gate_up, w_down)


def _rw_proj_kernel(*refs, has_v_res):
    if has_v_res:
        (x_ref, gain_ref, shift0_ref, mu_ref, wr_ref, wk_ref, wv_ref, vec_ref,
         w1_ref, w2_ref, a1_ref, a2_ref, g1_ref, g2_ref, v1_ref, v2_ref, vfirst_ref,
         r_ref, lw_ref, k_ref, v_ref, kk_ref, a_ref, g_ref, shift_ref, prev_ref) = refs
    else:
        (x_ref, gain_ref, shift0_ref, mu_ref, wr_ref, wk_ref, wv_ref, vec_ref,
         w1_ref, w2_ref, a1_ref, a2_ref, g1_ref, g2_ref,
         r_ref, lw_ref, k_ref, v_ref, kk_ref, a_ref, g_ref, shift_ref, prev_ref) = refs
    i = pl.program_id(1)
    h = _rms(x_ref[...], gain_ref[...])
    tm = h.shape[0]

    @pl.when(i == 0)
    def _():
        prev_ref[...] = shift0_ref[0]

    row = lax.broadcasted_iota(jnp.int32, h.shape, 0)
    h_prev = jnp.where(row == 0, prev_ref[...], pltpu.roll(h, 1, 0))
    last = h[tm - 1:tm, :]
    prev_ref[...] = last
    shift_ref[0] = last

    xx = h_prev - h
    mixed = lambda j: (h + xx * mu_ref[j:j + 1, :]).astype(BF16)
    w0, a0, k_k, k_a, v0 = (vec_ref[j:j + 1, :] for j in range(5))

    r_ref[...] = _dot(mixed(0), wr_ref[...]).astype(r_ref.dtype)
    dec_in = _dot(mixed(1), w1_ref[...])
    k = _dot(mixed(2), wk_ref[...])
    dec = w0 + _dot(jnp.tanh(dec_in).astype(BF16), w2_ref[...])
    x_v = mixed(3)
    v = _dot(x_v, wv_ref[...])
    lw_ref[...] = -jnp.exp(-jax.nn.softplus(-dec) - 0.5)
    if has_v_res:
        mix = jax.nn.sigmoid(v0 + _dot(_dot(x_v, v1_ref[...]).astype(BF16), v2_ref[...]))
        v = v + (vfirst_ref[...] - v) * mix
    v_ref[...] = v
    a_in = _dot(mixed(4), a1_ref[...])
    g_in = _dot(mixed(5), g1_ref[...])
    a = jax.nn.sigmoid(a0 + _dot(a_in.astype(BF16), a2_ref[...]))
    kk_ref[...] = (k * k_k).astype(kk_ref.dtype)
    k_ref[...] = (k * (1.0 + (a - 1.0) * k_a)).astype(k_ref.dtype)
    a_ref[...] = a.astype(a_ref.dtype)
    g_ref[...] = _dot(jax.nn.sigmoid(g_in).astype(BF16), g2_ref[...]).astype(g_ref.dtype)


def _pad_cols(w, mult=LANES):
    pad = -w.shape[1] % mult
    return jnp.pad(w, ((0, 0), (0, pad))) if pad else w


def _pad_rows(w, mult=LANES):
    pad = -w.shape[0] % mult
    return jnp.pad(w, ((0, pad), (0, 0))) if pad else w


def _rw_proj(x, batch, gain, shift0, p, v_first):
    rows, d = x.shape
    seq = rows // batch
    tm = min(WIDE_ROW_TILE, seq)
    nt = seq // tm
    has_v_res = v_first is not None
    zeros = jnp.zeros((d,), F32)
    vec = jnp.stack([p["w0"], p["a0"], p["k_k"], p["k_a"], p["v0"] if has_v_res else zeros,
                     zeros, zeros, zeros])
    row_spec = pl.BlockSpec((tm, d), lambda b, i: (b * nt + i, 0))

    def full(arr):
        return pl.BlockSpec(arr.shape, lambda b, i: (0,) * arr.ndim, pipeline_mode=pl.Buffered(1))

    ins = [x, gain.reshape(1, d), shift0.reshape(batch, 1, d), p["mu"], p["w_r"], p["w_k"], p["w_v"], vec,
           p["w1"], p["w2"], p["a1"], p["a2"], p["g1"], p["g2"]]
    in_specs = [row_spec, full(ins[1]), pl.BlockSpec((1, 1, d), lambda b, i: (b, 0, 0))]
    in_specs += [full(a) for a in ins[3:]]
    if has_v_res:
        ins += [p["v1"], p["v2"], v_first]
        in_specs += [full(p["v1"]), full(p["v2"]), row_spec]
    tok = [jax.ShapeDtypeStruct((rows, d), dt) for dt in (BF16, F32, BF16, F32, BF16, BF16, BF16)]
    outs = pl.pallas_call(
        functools.partial(_rw_proj_kernel, has_v_res=has_v_res),
        grid=(batch, nt),
        in_specs=in_specs,
        out_specs=[row_spec] * 7 + [pl.BlockSpec((1, 1, d), lambda b, i: (b, 0, 0))],
        out_shape=tok + [jax.ShapeDtypeStruct((batch, 1, d), F32)],
        scratch_shapes=[pltpu.VMEM((1, d), F32)],
        compiler_params=_params("arbitrary", "arbitrary"),
        name="rw_proj",
    )(*ins)
    return outs[:7], outs[7].reshape(batch, d)


WKV_DECAY_PIECES = 3
WKV_GROUP_PIECES = 2
WKV_STATE_PASSES = 1
WKV_SOLVE_PASSES = 1


def _wkv_prepare(units, c_):
    n = units[0][0].shape[0]
    sum_l = lambda x: _mm_exact_lhs(c_["tri_incl"], x, WKV_DECAY_PIECES)
    sum_g = lambda x: _mm_exact_rhs(x, c_["gsum"], WKV_GROUP_PIECES)
    solve_mm = lambda x, y, nt=False: _mm(x, y, WKV_SOLVE_PASSES, nt=nt)

    cs = [sum_l(u[1]) for u in units]
    head_sums = [sum_g(jnp.concatenate([u[4] * u[4], u[0] * u[2] * u[7][0:1, :]], axis=0)) for u in units]
    pre = []
    for (r, lw, k, v, kk, a, g, vec), cs_u, hs in zip(units, cs, head_sums):
        cs_last = cs_u[n - 1:n, :]
        d_inv = jnp.exp(-cs_u)
        d_rest = jnp.exp(cs_last - cs_u)
        kkn = kk * lax.rsqrt(jnp.maximum(hs[:n], 1e-24))
        b = kkn * a
        pre.append(dict(a_t=-kkn * jnp.exp(cs_u - lw), r_t=r * jnp.exp(cs_u), b_t=b * d_inv, k_t=k * d_inv,
                        bk_rest=jnp.concatenate([b * d_rest, k * d_rest], axis=0), decay=jnp.exp(cs_last),
                        v=v, g=g, vec=vec, bonus=hs[n:] * v))

    by_head = lambda x: jnp.concatenate([jnp.where(m, x, 0.0) for m in c_["head_masks"]], axis=0)
    block_diag = lambda w: jnp.where(c_["same_block"], jnp.concatenate([w, w], axis=0), 0.0)
    grams = [solve_mm(jnp.concatenate([q["a_t"], q["r_t"]], axis=0),
                      jnp.concatenate([by_head(q["b_t"]), by_head(q["k_t"])], axis=0), nt=True)
             for q in pre]
    lows = [jnp.where(c_["strict_w"], gm[:n, :2 * n], 0.0) for gm in grams]
    m_aks = [jnp.where(c_["strict_w"], gm[:n, 2 * n:], 0.0) for gm in grams]
    for q, gm in zip(pre, grams):
        q["m_r"] = jnp.where(c_["incl_w2"], gm[n:], 0.0)
        q["v_heads"] = by_head(q["v"])
    from_v = [solve_mm(mk, q["v_heads"]) for mk, q in zip(m_aks, pre)]
    inv = [c_["eye_w"] + lo for lo in lows]
    if n > 2:
        pw = [solve_mm(lo, block_diag(lo)) for lo in lows]
        step = 2
        while 2 * step < n:
            both = [solve_mm(jnp.concatenate([t, p2], axis=0), block_diag(p2)) for t, p2 in zip(inv, pw)]
            inv = [t + bo[:n] for t, bo in zip(inv, both)]
            pw = [bo[n:] for bo in both]
            step *= 2
        inv = [t + solve_mm(t, block_diag(p2)) for t, p2 in zip(inv, pw)]
    for q, t, fv in zip(pre, inv, from_v):
        q["inv"], q["from_v"] = t, fv
    return pre


def _wkv_advance(pre, states, c_):
    n = pre[0]["v"].shape[0]
    inv_hd = 1.0 / RW_HEAD_DIM
    sum_g = lambda x: _mm_exact_rhs(x, c_["gsum"], WKV_GROUP_PIECES)
    solve_mm = lambda x, y: _mm(x, y, WKV_SOLVE_PASSES)
    by_head = lambda x: jnp.concatenate([jnp.where(m, x, 0.0) for m in c_["head_masks"]], axis=0)

    from_state = [_mm(jnp.concatenate([q["a_t"], q["r_t"]], axis=0), s, WKV_STATE_PASSES, nt=True)
                  for q, s in zip(pre, states)]
    sig_p = [solve_mm(q["inv"], by_head(fs[:n] + q["from_v"])) for q, fs in zip(pre, from_state)]
    y_p = [fs[n:] + solve_mm(q["m_r"], jnp.concatenate([by_head(sg), q["v_heads"]], axis=0))
           for q, fs, sg in zip(pre, from_state, sig_p)]
    upd = [_mm(jnp.concatenate([s, q["v"]], axis=0).T, q["bk_rest"], WKV_STATE_PASSES)
           for s, q in zip(sig_p, pre)]
    new_states = [s * q["decay"] + jnp.where(c_["same_head"], u, 0.0) for s, q, u in zip(states, pre, upd)]

    mean = [sum_g(y) * inv_hd for y in y_p]
    dev = [y - m for y, m in zip(y_p, mean)]
    var = [sum_g(dv * dv) * inv_hd for dv in dev]
    outs = []
    for q, dv, vr in zip(pre, dev, var):
        o = dv * lax.rsqrt(vr + RW_GN_EPS) * q["vec"][1:2, :] + q["vec"][2:3, :]
        outs.append((o + q["bonus"]) * q["g"])
    return outs, new_states


def _wkv_kernel(*refs, has_state0, n):
    if has_state0:
        (r_ref, lw_ref, k_ref, v_ref, kk_ref, a_ref, g_ref, vec_ref, s0_ref,
         o_ref, sout_ref, state_ref) = refs
    else:
        (r_ref, lw_ref, k_ref, v_ref, kk_ref, a_ref, g_ref, vec_ref,
         o_ref, sout_ref, state_ref) = refs
    c = pl.program_id(2)
    hd = RW_HEAD_DIM

    @pl.when(c == 0)
    def _():
        if has_state0:
            state_ref[...] = s0_ref[0]
        else:
            state_ref[...] = jnp.zeros_like(state_ref)

    tr = lax.broadcasted_iota(jnp.int32, (n, n), 0)
    tc = lax.broadcasted_iota(jnp.int32, (n, n), 1)
    tr2 = lax.broadcasted_iota(jnp.int32, (n, 2 * n), 0)
    tc2 = lax.rem(lax.broadcasted_iota(jnp.int32, (n, 2 * n), 1), n)
    tr4 = lax.broadcasted_iota(jnp.int32, (n, 4 * n), 0)
    tc4 = lax.rem(lax.broadcasted_iota(jnp.int32, (n, 4 * n), 1), n)
    br = lax.broadcasted_iota(jnp.int32, (2 * n, 2 * n), 0) // n
    bc = lax.broadcasted_iota(jnp.int32, (2 * n, 2 * n), 1) // n
    lane = lax.broadcasted_iota(jnp.int32, (n, LANES), 1)
    sr = lax.broadcasted_iota(jnp.int32, (LANES, LANES), 0) // hd
    sc = lax.broadcasted_iota(jnp.int32, (LANES, LANES), 1) // hd
    consts = dict(
        gsum=_group_ones(LANES, hd),
        tri_incl=jnp.where(tc <= tr, 1.0, 0.0).astype(BF16),
        strict_w=tc2 < tr2,
        incl_w2=tc4 <= tr4,
        eye_w=jnp.where(tc2 == tr2, 1.0, 0.0).astype(F32),
        same_block=br == bc,
        head_masks=[lane < hd, lane >= hd],
        same_head=sr == sc,
    )
    n_pairs = state_ref.shape[0]
    n_chunks = r_ref.shape[0] // n
    units = []
    for ci in range(n_chunks):
        rows = slice(ci * n, (ci + 1) * n)
        for p in range(n_pairs):
            sl = slice(p * LANES, (p + 1) * LANES)
            units.append(tuple(ref[rows, sl].astype(F32)
                               for ref in (r_ref, lw_ref, k_ref, v_ref, kk_ref, a_ref, g_ref)) + (vec_ref[:, sl],))
    pre = _wkv_prepare(units, consts)
    states = [state_ref[p] for p in range(n_pairs)]
    outs = []
    for ci in range(n_chunks):
        out, states = _wkv_advance(pre[ci * n_pairs:(ci + 1) * n_pairs], states, consts)
        outs.append(jnp.concatenate(out, axis=1))
    o_ref[...] = jnp.concatenate(outs, axis=0).astype(o_ref.dtype)
    new_states = jnp.stack(states)
    state_ref[...] = new_states
    sout_ref[0] = new_states


def _wkv(tok, batch, vec, state0):
    rows, d = tok[0].shape
    seq = rows // batch
    n = min(WKV_CHUNK, seq)
    span = n * min(WKV_CHUNKS_PER_STEP, seq // n)
    nc = seq // span
    pairs = d // LANES
    gp = WKV_PAIRS
    width = gp * LANES
    has_state0 = state0 is not None
    tok_spec = pl.BlockSpec((span, width), lambda b, p, c: (b * nc + c, p))
    state_spec = pl.BlockSpec((1, gp, LANES, LANES), lambda b, p, c: (b, p, 0, 0))
    ins = list(tok) + [vec]
    in_specs = [tok_spec] * 7 + [pl.BlockSpec((8, width), lambda b, p, c: (0, p))]
    if has_state0:
        ins.append(state0)
        in_specs.append(state_spec)
    return pl.pallas_call(
        functools.partial(_wkv_kernel, has_state0=has_state0, n=n),
        grid=(batch, pairs // gp, nc),
        in_specs=in_specs,
        out_specs=[tok_spec, state_spec],
        out_shape=[jax.ShapeDtypeStruct((rows, d), BF16),
                   jax.ShapeDtypeStruct((batch, pairs, LANES, LANES), F32)],
        scratch_shapes=[pltpu.VMEM((gp, LANES, LANES), F32)],
        compiler_params=_params("arbitrary", "arbitrary", "arbitrary"),
        name="wkv",
    )(*ins)


def _state_to_blockdiag(state):
    b, h, n, _ = state.shape
    s = state.reshape(b, h // 2, 2, n, n)
    z = jnp.zeros_like(s[:, :, 0])
    top = jnp.concatenate([s[:, :, 0], z], axis=-1)
    bot = jnp.concatenate([z, s[:, :, 1]], axis=-1)
    return jnp.concatenate([top, bot], axis=-2)


def _state_from_blockdiag(bd):
    b, p, n2, _ = bd.shape
    n = n2 // 2
    return jnp.stack([bd[:, :, :n, :n], bd[:, :, n:, n:]], axis=2).reshape(b, 2 * p, n, n)


def _trunk(x, batch, mem_k, mem_v, past_k, past_v, rw_state0, rw_shift0, w):
    depth = len(w["ffn_gu"])
    d = x.shape[-1]
    sb_k, sb_v, rw_states, rw_shifts = None, None, [], []
    n_sb = (depth + 1) // 2
    v_first = None
    for layer in range(depth):
        j = layer // 2
        if layer % 2 == 0:
            sb_k, sb_v, qkv = _sb_qkv(x, batch, w["norm_mix"][layer], w["sb_qkv"][j], j, n_sb, sb_k, sb_v)
            if past_k is None:
                o = _sb_attention_prompt(qkv, batch)
            else:
                o = _sb_attention_sample(qkv, batch, past_k, past_v, j)
            w_o = w["sb_o"][j]
        else:
            if rw_state0 is None:
                state0 = None
                shift0 = jnp.zeros((batch, d), F32)
            else:
                state0 = _state_to_blockdiag(rw_state0[j])
                shift0 = rw_shift0[j]
            tok, shift = _rw_proj(x, batch, w["norm_mix"][layer], shift0, w["rw"][j], v_first)
            if v_first is None:
                v_first = tok[3]
            o, state = _wkv(tok, batch, w["rw"][j]["out_vec"], state0)
            rw_states.append(_state_from_blockdiag(state))
            rw_shifts.append(shift)
            w_o = w["rw"][j]["w_o"]
        x = _xattn(o, w_o, x, w["norm_xattn"][layer], w["xa_q"][layer], w["xa_o"][layer],
                   mem_k[layer], mem_v[layer])
        final = w["norm_final"] if layer == depth - 1 else None
        x = _ffn(x, w["norm_ffn"][layer], w["ffn_gu"][layer], w["ffn_down"][layer], final)
    return x, sb_k, sb_v, jnp.stack(rw_states), jnp.stack(rw_shifts)


def kernel(x_prompt, x_sample, mem_prompt, cache_sb_k, cache_sb_v, state_rwkv_wkv, state_rwkv_shift,
           cache_mem_k, cache_mem_v, norm_mix, norm_xattn, norm_ffn, norm_mem, norm_final,
           sb_w_qkv, sb_w_o, rw_mu, rw_w_r, rw_w_k, rw_w_v, rw_w_o, rw_w0, rw_w1, rw_w2,
           rw_a0, rw_a1, rw_a2, rw_v0, rw_v1, rw_v2, rw_g1, rw_g2, rw_k_k, rw_k_a, rw_r_k,
           rw_lnx_w, rw_lnx_b, xa_w_q, xa_w_kv, xa_w_o, ffn_w_gate_up, ffn_w_down):
    batch, seq, d = x_prompt.shape
    dec_batch, dec_seq, _ = x_sample.shape
    depth = norm_mix.shape[0]
    n_rw = rw_w_r.shape[0]
    n_sb = sb_w_qkv.shape[0]
    n_mem = mem_prompt.shape[1]
    bf = lambda t: t.astype(BF16)

    rw = []
    for j in range(n_rw):
        pj = dict(mu=rw_mu[j], w_r=bf(rw_w_r[j]), w_k=bf(rw_w_k[j]), w_v=bf(rw_w_v[j]), w_o=bf(rw_w_o[j]),
                  w0=rw_w0[j], a0=rw_a0[j], k_k=rw_k_k[j], k_a=rw_k_a[j],
                  w1=bf(_pad_cols(rw_w1[j])), w2=bf(_pad_rows(rw_w2[j])),
                  a1=bf(_pad_cols(rw_a1[j])), a2=bf(_pad_rows(rw_a2[j])),
                  g1=bf(_pad_cols(rw_g1[j])), g2=bf(_pad_rows(rw_g2[j])))
        if j > 0:
            pj.update(v0=rw_v0[j - 1], v1=bf(_pad_cols(rw_v1[j - 1])), v2=bf(_pad_rows(rw_v2[j - 1])))
        zeros = jnp.zeros((d,), F32)
        pj["out_vec"] = jnp.stack([rw_r_k[j].reshape(d), rw_lnx_w[j], rw_lnx_b[j]] + [zeros] * 5)
        rw.append(pj)
    w = dict(norm_mix=norm_mix, norm_xattn=norm_xattn, norm_ffn=norm_ffn, norm_final=norm_final,
             sb_qkv=bf(sb_w_qkv), sb_o=bf(sb_w_o), rw=rw, xa_q=bf(xa_w_q), xa_o=bf(xa_w_o),
             ffn_gu=bf(ffn_w_gate_up), ffn_down=bf(ffn_w_down))

    mem_k, mem_v, mem_k_bf, mem_v_bf = _mem_kv(mem_prompt, norm_mem, bf(xa_w_kv))
    y_p, p_sb_k, p_sb_v, p_wkv, p_shift = _trunk(
        x_prompt.reshape(batch * seq, d), batch, mem_k_bf, mem_v_bf, None, None, None, None, w)

    y_s, s_sb_k, s_sb_v, s_wkv, s_shift = _trunk(
        x_sample.reshape(dec_batch * dec_seq, d), dec_batch,
        bf(cache_mem_k).reshape(depth, dec_batch, n_mem, d), bf(cache_mem_v).reshape(depth, dec_batch, n_mem, d),
        cache_sb_k.reshape(n_sb, dec_batch, -1, d), cache_sb_v.reshape(n_sb, dec_batch, -1, d),
        state_rwkv_wkv, state_rwkv_shift, w)

    hd = d // SB_HEADS
    return (y_p.reshape(batch, seq, d), y_s.reshape(dec_batch, dec_seq, d),
            p_sb_k.reshape(n_sb, batch, seq, SB_HEADS, hd), p_sb_v.reshape(n_sb, batch, seq, SB_HEADS, hd),
            p_wkv, p_shift,
            mem_k, mem_v,
            s_sb_k.reshape(n_sb, dec_batch, dec_seq, SB_HEADS, hd),
            s_sb_v.reshape(n_sb, dec_batch, dec_seq, SB_HEADS, hd),
            s_wkv, s_shift)
```

```python
import functools

import jax
import jax.numpy as jnp
from jax import lax
from jax.experimental import pallas as pl
from jax.experimental.pallas import tpu as pltpu

F32 = jnp.float32
BF16 = jnp.bfloat16

NORM_EPS = 1e-6
NEG_LOG2_E = -1.4426950408889634
RW_GN_EPS = 64e-5
SB_HEADS = 16
XA_HEADS = 4
RW_HEAD_DIM = 64
LANES = 128
ROW_TILE = 256
WIDE_ROW_TILE = 512
SB_BLOCK = 256
SB_PAIRS = 8
SB_WAVES = 4
SB_DECODE_WAVES = 2
SB_UNROLL = 2
SB_PAST_SPAN = 2048
WKV_CHUNK = 64
WKV_CHUNKS_PER_STEP = 4
WKV_PAIRS = 8
FFN_CHUNK = 256
VMEM_LIMIT_BYTES = 56 * 1024 * 1024


def _params(*sem):
    return pltpu.CompilerParams(dimension_semantics=sem, vmem_limit_bytes=VMEM_LIMIT_BYTES)


def _dot(a, b):
    return jnp.dot(a, b, preferred_element_type=F32)


def _dot_nt(a, b):
    return lax.dot_general(a, b, (((1,), (1,)), ((), ())), preferred_element_type=F32)


def _split(x, n):
    parts = []
    for i in range(n):
        p = x.astype(BF16)
        parts.append(p)
        if i + 1 < n:
            x = x - p.astype(F32)
    return parts


def _mm(a, b, passes, nt=False):
    dot = _dot_nt if nt else _dot
    if passes == 1:
        return dot(a.astype(BF16), b.astype(BF16))
    a_hi, a_lo = _split(a, 2)
    b_hi, b_lo = _split(b, 2)
    return dot(a_hi, b_hi) + (dot(a_hi, b_lo) + dot(a_lo, b_hi))


def _mm_exact_rhs(a, b_bf16, n):
    out = None
    for p in _split(a, n):
        t = _dot(p, b_bf16)
        out = t if out is None else out + t
    return out


def _mm_exact_lhs(a_bf16, b, n):
    out = None
    for p in _split(b, n):
        t = _dot(a_bf16, p)
        out = t if out is None else out + t
    return out


def _rms(x, gain):
    ms = jnp.mean(x * x, axis=-1, keepdims=True)
    return x * lax.rsqrt(ms + NORM_EPS) * gain


def _group_ones(n, group):
    r = lax.broadcasted_iota(jnp.int32, (n, n), 0) // group
    c = lax.broadcasted_iota(jnp.int32, (n, n), 1) // group
    return jnp.where(r == c, 1.0, 0.0).astype(BF16)


def _mem_kv_kernel(x_ref, g_ref, w_ref, k_ref, v_ref, kb_ref, vb_ref):
    tm, d = x_ref.shape
    heads, dh = k_ref.shape[-2:]
    h = _rms(x_ref[...], g_ref[0]).astype(BF16)
    kv = _dot(h, w_ref[0])
    k, v = kv[:, :d], kv[:, d:]
    k_ref[0, 0] = k.reshape(tm, heads, dh)
    v_ref[0, 0] = v.reshape(tm, heads, dh)
    kb_ref[0, 0] = k.astype(BF16)
    vb_ref[0, 0] = v.astype(BF16)


def _mem_kv(mem, gains, w_kv):
    depth = w_kv.shape[0]
    batch, n_mem, d = mem.shape
    tm = min(ROW_TILE, n_mem)
    nt = n_mem // tm
    dh = d // XA_HEADS
    assert n_mem % tm == 0 and dh % LANES == 0
    out = jax.ShapeDtypeStruct((depth, batch, n_mem, XA_HEADS, dh), F32)
    out_b = jax.ShapeDtypeStruct((depth, batch, n_mem, d), BF16)
    spec = pl.BlockSpec((1, 1, tm, XA_HEADS, dh), lambda l, i: (l, i // nt, i % nt, 0, 0))
    spec_b = pl.BlockSpec((1, 1, tm, d), lambda l, i: (l, i // nt, i % nt, 0))
    return pl.pallas_call(
        _mem_kv_kernel,
        grid=(depth, batch * nt),
        in_specs=[pl.BlockSpec((tm, d), lambda l, i: (i, 0)),
                  pl.BlockSpec((1, 1, d), lambda l, i: (l, 0, 0)),
                  pl.BlockSpec((1, d, 2 * d), lambda l, i: (l, 0, 0))],
        out_specs=[spec, spec, spec_b, spec_b],
        out_shape=[out, out, out_b, out_b],
        compiler_params=_params("arbitrary", "arbitrary"),
        name="mem_kv",
    )(mem.reshape(batch * n_mem, d), gains.reshape(depth, 1, d), w_kv)


def _split_heads(x, hd):
    tm, d = x.shape
    blocks = []
    for p in range(d // LANES):
        slab = x[:, p * LANES:(p + 1) * LANES]
        blocks += [slab, pltpu.roll(slab, LANES - hd, 1)]
    return jnp.concatenate(blocks, axis=1).reshape(tm, d // hd, LANES)[:, :, :hd]


def _sb_qkv_kernel(x_ref, g_ref, w_ref, *rest, q_scale, slot):
    k_ref, v_ref, qkv_ref = rest[-3:]
    d = x_ref.shape[-1]
    hd = k_ref.shape[-1]
    h = _rms(x_ref[...], g_ref[...]).astype(BF16)
    qkv = _dot(h, w_ref[...])
    qkv_ref[:, :d] = (qkv[:, :d] * q_scale).astype(BF16)
    qkv_ref[:, d:] = qkv[:, d:].astype(BF16)
    k3 = _split_heads(qkv[:, d:2 * d], hd)
    v3 = _split_heads(qkv[:, 2 * d:], hd)
    own = slot if k_ref.shape[0] > 1 else 0
    for s in range(k_ref.shape[0]):
        if s == own:
            k_ref[s, 0] = k3
            v_ref[s, 0] = v3
        else:
            k_ref[s, 0] = jnp.zeros(k3.shape, k_ref.dtype)
            v_ref[s, 0] = jnp.zeros(v3.shape, v_ref.dtype)


def _sb_qkv(x, batch, gain, w, slot, n_slots, k_all=None, v_all=None):
    rows, d = x.shape
    seq = rows // batch
    tm = min(ROW_TILE, seq)
    nt = seq // tm
    hd = d // SB_HEADS
    assert 2 * hd == LANES
    q_scale = float(hd ** -0.5)
    ins = [x, gain.reshape(1, d), w]
    in_specs = [pl.BlockSpec((tm, d), lambda i: (i, 0)),
                pl.BlockSpec((1, d), lambda i: (0, 0)),
                pl.BlockSpec(w.shape, lambda i: (0, 0))]
    if k_all is None:
        aliases = {}
        slot_spec = pl.BlockSpec((n_slots, 1, tm, SB_HEADS, hd), lambda i: (0, i // nt, i % nt, 0, 0))
    else:
        ins += [k_all, v_all]
        in_specs += [pl.BlockSpec(memory_space=pl.ANY)] * 2
        aliases = {3: 0, 4: 1}
        slot_spec = pl.BlockSpec((1, 1, tm, SB_HEADS, hd), lambda i: (slot, i // nt, i % nt, 0, 0))
    kv_shape = jax.ShapeDtypeStruct((n_slots, batch, seq, SB_HEADS, hd), F32)
    return pl.pallas_call(
        functools.partial(_sb_qkv_kernel, q_scale=q_scale, slot=slot),
        grid=(rows // tm,),
        in_specs=in_specs,
        out_specs=[slot_spec, slot_spec, pl.BlockSpec((tm, 3 * d), lambda i: (i, 0))],
        out_shape=[kv_shape, kv_shape, jax.ShapeDtypeStruct((rows, 3 * d), BF16)],
        input_output_aliases=aliases,
        compiler_params=_params("arbitrary"),
        name="sb_qkv",
    )(*ins)


def _sb_step(q2s, kv_blocks, carry, acc, neg_lower2, mask, n_waves):
    m = q2s[0].shape[0]
    groups = len(q2s)
    per = max(groups // n_waves, 1)
    waves = [range(c, min(c + per, groups)) for c in range(0, groups, per)]
    rows = [slice(w_[0] * m, (w_[-1] + 1) * m) for w_ in waves]
    units = [(b, c) for b in range(len(kv_blocks)) for c in range(len(waves))]

    def scores(b, c):
        return jnp.concatenate([_dot_nt(q2s[g], kv_blocks[b][0][g]) for g in waves[c]], axis=0)

    def softplus_pieces(c, z):
        sp = jnp.maximum(z, 0.0) + jnp.log(1.0 + jnp.exp2(jnp.abs(z) * NEG_LOG2_E))
        if mask is not None:
            sp = jnp.where(mask[rows[c]], sp, 0.0)
        return jnp.concatenate(_split(sp, 2), axis=1)

    def weights(c, z, from_s, later):
        w = jnp.exp(z + from_s + jnp.concatenate([later] * (z.shape[1] // LANES), axis=1))
        if mask is not None:
            w = jnp.where(mask[rows[c]], w, 0.0)
        return w.astype(BF16)

    def values(b, c, w):
        return jnp.concatenate([_dot(w[i * m:(i + 1) * m], kv_blocks[b][1][g])
                                for i, g in enumerate(waves[c])], axis=0)

    n = len(units)
    z, pieces, from_s, w = ([None] * n for _ in range(4))
    later = [carry[r] for r in rows]
    pv = [None] * len(waves)
    for tick in range(n + 4):
        for u, (b, c) in enumerate(units):
            stage = tick - u
            if stage == 0:
                z[u] = scores(b, c)
            elif stage == 1:
                pieces[u] = softplus_pieces(c, z[u])
            elif stage == 2:
                from_s[u] = _dot(pieces[u], neg_lower2)
            elif stage == 3:
                w[u] = weights(c, z[u], from_s[u], later[c])
                later[c] = later[c] + jnp.broadcast_to(from_s[u][:, 0:1], later[c].shape)
            elif stage == 4:
                out = values(b, c, w[u])
                pv[c] = out if pv[c] is None else pv[c] + out
    return jnp.concatenate(later, axis=0), acc + jnp.concatenate(pv, axis=0)


def _sb_consts(bk):
    r = lax.broadcasted_iota(jnp.int32, (2, bk, bk), 1).reshape(2 * bk, bk)
    c = lax.broadcasted_iota(jnp.int32, (2 * bk, bk), 1)
    return jnp.where(r >= c, -1.0, 0.0).astype(BF16)


def _sb_stack_heads(q_ref, q2_ref):
    tq = q_ref.shape[0]
    lane = lax.broadcasted_iota(jnp.int32, (tq, LANES), 1)
    for g in range(q2_ref.shape[0]):
        q = q_ref[:, g * LANES:(g + 1) * LANES]
        zero = jnp.zeros_like(q)
        q2_ref[g] = jnp.concatenate([jnp.where(lane < LANES // 2, q, zero),
                                     jnp.where(lane >= LANES // 2, q, zero)], axis=0)


def _sb_causal_mask(groups, tq, bk, t0=0):
    t = lax.broadcasted_iota(jnp.int32, (2 * groups, tq, bk), 1).reshape(2 * groups * tq, bk)
    s = lax.broadcasted_iota(jnp.int32, (2 * groups * tq, bk), 1)
    return s < t + t0


def _sb_write(o_ref, acc_ref):
    tq = o_ref.shape[0]
    lane = lax.broadcasted_iota(jnp.int32, (tq, LANES), 1)
    for g in range(acc_ref.shape[0] // (2 * tq)):
        o_ref[:, g * LANES:(g + 1) * LANES] = jnp.where(
            lane < LANES // 2, acc_ref[2 * g * tq:(2 * g + 1) * tq, :],
            acc_ref[(2 * g + 1) * tq:(2 * g + 2) * tq, :]).astype(o_ref.dtype)


def _sb_prompt_kernel(q_ref, k_ref, v_ref, o_ref, q2_ref, carry_ref, acc_ref):
    blk = q_ref.shape[0]
    groups = q2_ref.shape[0]
    i = pl.program_id(2)
    neg_lower2 = _sb_consts(blk)
    _sb_stack_heads(q_ref, q2_ref)
    lanes = [slice(g * LANES, (g + 1) * LANES) for g in range(groups)]

    def block(j):
        row0 = pl.multiple_of(j * blk, blk)
        return ([k_ref[pl.ds(row0, blk), sl] for sl in lanes], [v_ref[pl.ds(row0, blk), sl] for sl in lanes])

    def step(blocks, carry, acc, mask):
        return _sb_step([q2_ref[g] for g in range(groups)], blocks, carry, acc, neg_lower2, mask, SB_WAVES)

    half = blk // 2
    d0 = pl.multiple_of(i * blk, blk)
    zero = jnp.zeros((groups * blk, LANES), F32)
    halves = []
    for t0, keys, consts in ((half, blk, neg_lower2), (0, half, _sb_consts(half))):
        q2_half = [jnp.concatenate([q2_ref[g, t0:t0 + half, :], q2_ref[g, blk + t0:blk + t0 + half, :]], axis=0)
                   for g in range(groups)]
        own = ([k_ref[pl.ds(d0, keys), sl] for sl in lanes], [v_ref[pl.ds(d0, keys), sl] for sl in lanes])
        halves.append(_sb_step(q2_half, [own], zero, zero, consts, _sb_causal_mask(groups, half, keys, t0),
                               SB_WAVES))
    (carry_late, acc_late), (carry_early, acc_early) = halves
    for g in range(groups):
        for h in range(2):
            src = slice((2 * g + h) * half, (2 * g + h + 1) * half)
            dst = (2 * g + h) * blk
            for ref, early, late in ((carry_ref, carry_early, carry_late), (acc_ref, acc_early, acc_late)):
                ref[dst:dst + half, :] = early[src]
                ref[dst + half:dst + blk, :] = late[src]

    odd = i % SB_UNROLL

    def single(it, c):
        carry_ref[...], acc_ref[...] = step([block(i - 1 - it)], carry_ref[...], acc_ref[...], None)
        return c

    def multi(it, c):
        top = i - 1 - odd - it * SB_UNROLL
        carry_ref[...], acc_ref[...] = step([block(top - u) for u in range(SB_UNROLL)],
                                            carry_ref[...], acc_ref[...], None)
        return c

    lax.fori_loop(0, odd, single, 0)
    lax.fori_loop(0, i // SB_UNROLL, multi, 0)
    _sb_write(o_ref, acc_ref)


def _sb_attention_prompt(qkv, batch):
    rows, d3 = qkv.shape
    d = d3 // 3
    seq = rows // batch
    blk = min(SB_BLOCK, seq)
    nq = seq // blk
    width = SB_PAIRS * LANES
    ng = d // width
    return pl.pallas_call(
        _sb_prompt_kernel,
        grid=(batch, ng, nq),
        in_specs=[pl.BlockSpec((blk, width), lambda b, p, i: (b * nq + i, p)),
                  pl.BlockSpec((seq, width), lambda b, p, i: (b, ng + p)),
                  pl.BlockSpec((seq, width), lambda b, p, i: (b, 2 * ng + p))],
        out_specs=pl.BlockSpec((blk, width), lambda b, p, i: (b * nq + i, p)),
        out_shape=jax.ShapeDtypeStruct((rows, d), BF16),
        scratch_shapes=[pltpu.VMEM((SB_PAIRS, 2 * blk, LANES), BF16),
                        pltpu.VMEM((SB_PAIRS * 2 * blk, LANES), F32),
                        pltpu.VMEM((SB_PAIRS * 2 * blk, LANES), F32)],
        compiler_params=_params("arbitrary", "arbitrary", "arbitrary"),
        name="sb_attention_prompt",
    )(qkv, qkv, qkv)


def _sb_sample_kernel(q_ref, kn_ref, vn_ref, kp_ref, vp_ref, o_ref, q2_ref, carry_ref, acc_ref, *, blk):
    tq = q_ref.shape[0]
    groups = q2_ref.shape[0]
    c = pl.program_id(1)
    neg_lower2 = _sb_consts(blk)

    lanes = [slice(g * LANES, (g + 1) * LANES) for g in range(groups)]

    @pl.when(c == 0)
    def _():
        _sb_stack_heads(q_ref, q2_ref)
        zero = jnp.zeros(carry_ref.shape, F32)
        carry_ref[...], acc_ref[...] = _sb_step(
            [q2_ref[g] for g in range(groups)],
            [([kn_ref[0, :, sl] for sl in lanes], [vn_ref[0, :, sl] for sl in lanes])],
            zero, zero, neg_lower2, _sb_causal_mask(groups, tq, blk), SB_DECODE_WAVES)

    blocks = []
    for sb in reversed(range(kp_ref.shape[2] // blk)):
        rows = slice(sb * blk, (sb + 1) * blk)
        blocks.append(([kp_ref[0, 0, rows, sl].astype(BF16) for sl in lanes],
                       [vp_ref[0, 0, rows, sl].astype(BF16) for sl in lanes]))
    carry_ref[...], acc_ref[...] = _sb_step([q2_ref[g] for g in range(groups)], blocks,
                                            carry_ref[...], acc_ref[...], neg_lower2, None, SB_DECODE_WAVES)

    @pl.when(c == pl.num_programs(1) - 1)
    def _():
        _sb_write(o_ref, acc_ref)


def _sb_attention_sample(qkv, batch, past_k, past_v, layer):
    rows, d3 = qkv.shape
    d = d3 // 3
    seq = rows // batch
    blk = SB_BLOCK
    pairs = d // LANES
    past_len = past_k.shape[2]
    span = min(SB_PAST_SPAN, past_len)
    steps = past_len // span
    assert seq <= blk and past_len % span == 0 and span % blk == 0
    new = qkv[:, d:].reshape(batch, seq, 2 * d)
    new = jnp.pad(new, ((0, 0), (0, blk - seq), (0, 0)))
    return pl.pallas_call(
        functools.partial(_sb_sample_kernel, blk=blk),
        grid=(batch, steps),
        in_specs=[pl.BlockSpec((seq, d), lambda b, c: (b, 0)),
                  pl.BlockSpec((1, blk, d), lambda b, c: (b, 0, 0)),
                  pl.BlockSpec((1, blk, d), lambda b, c: (b, 0, 1)),
                  pl.BlockSpec((1, 1, span, d), lambda b, c: (layer, b, steps - 1 - c, 0)),
                  pl.BlockSpec((1, 1, span, d), lambda b, c: (layer, b, steps - 1 - c, 0))],
        out_specs=pl.BlockSpec((seq, d), lambda b, c: (b, 0)),
        out_shape=jax.ShapeDtypeStruct((rows, d), BF16),
        scratch_shapes=[pltpu.VMEM((pairs, 2 * seq, LANES), BF16),
                        pltpu.VMEM((pairs * 2 * seq, LANES), F32),
                        pltpu.VMEM((pairs * 2 * seq, LANES), F32)],
        compiler_params=_params("arbitrary", "arbitrary"),
        name="sb_attention_sample",
    )(qkv, new, new, past_k, past_v)


def _xattn_kernel(a_ref, wa_ref, x_ref, g_ref, wq_ref, wo_ref, mk_ref, mv_ref, o_ref, *, heads):
    x = x_ref[...] + _dot(a_ref[...], wa_ref[...])
    d = x.shape[-1]
    dh = d // heads
    h = _rms(x, g_ref[...]).astype(BF16)
    q = _dot(h, wq_ref[...]).astype(BF16)
    nb = mk_ref.shape[0]
    seq = x.shape[0] // nb
    per_seq = []
    for b in range(nb):
        rows = slice(b * seq, (b + 1) * seq)
        outs = []
        for hd in range(heads):
            sl = slice(hd * dh, (hd + 1) * dh)
            s = _dot_nt(q[rows, sl], mk_ref[b, :, sl]) * (dh ** -0.5)
            e = jnp.exp(s - jnp.max(s, axis=-1, keepdims=True))
            p = e / jnp.sum(e, axis=-1, keepdims=True)
            outs.append(_dot(p.astype(BF16), mv_ref[b, :, sl]).astype(BF16))
        per_seq.append(jnp.concatenate(outs, axis=-1))
    o = per_seq[0] if nb == 1 else jnp.concatenate(per_seq, axis=0)
    o_ref[...] = x + _dot(o, wo_ref[...])


def _xattn(a, wa, x, gain, wq, wo, mem_k, mem_v):
    rows, d = x.shape
    batch, n_mem, _ = mem_k.shape
    seq = rows // batch
    nb = min(batch, max(1, WIDE_ROW_TILE // seq))
    assert batch % nb == 0
    tm = nb * seq if nb > 1 else min(WIDE_ROW_TILE, seq)
    nt = 1 if nb > 1 else seq // tm
    row_spec = pl.BlockSpec((tm, d), lambda b, i: (b * nt + i, 0))
    weight_spec = pl.BlockSpec((d, d), lambda b, i: (0, 0))
    mem_spec = pl.BlockSpec((nb, n_mem, d), lambda b, i: (b, 0, 0))
    return pl.pallas_call(
        functools.partial(_xattn_kernel, heads=XA_HEADS),
        grid=(batch // nb, nt),
        in_specs=[row_spec, weight_spec, row_spec, pl.BlockSpec((1, d), lambda b, i: (0, 0)),
                  weight_spec, weight_spec, mem_spec, mem_spec],
        out_specs=row_spec,
        out_shape=jax.ShapeDtypeStruct((rows, d), F32),
        input_output_aliases={2: 0},
        compiler_params=_params("arbitrary", "arbitrary"),
        name="xattn",
    )(a, wa, x, gain.reshape(1, d), wq, wo, mem_k, mem_v)


def _ffn_kernel(x_ref, g_ref, wgu_ref, wd_ref, o_ref, *, final_gain):
    x = x_ref[...]
    hidden = wd_ref.shape[0]
    h = _rms(x, g_ref[0:1, :]).astype(BF16)
    acc = x
    for c0 in range(0, hidden, FFN_CHUNK):
        c1 = min(c0 + FFN_CHUNK, hidden)
        gate = _dot(h, wgu_ref[:, c0:c1])
        up = _dot(h, wgu_ref[:, hidden + c0:hidden + c1])
        act = (gate * jax.nn.sigmoid(gate) * up).astype(BF16)
        acc = acc + _dot(act, wd_ref[c0:c1, :])
    if final_gain:
        acc = _rms(acc, g_ref[1:2, :])
    o_ref[...] = acc


def _ffn(x, gain, w_gate_up, w_down, final_gain=None):
    rows, d = x.shape
    tm = min(WIDE_ROW_TILE, rows)
    gains = jnp.stack([gain, gain if final_gain is None else final_gain])
    return pl.pallas_call(
        functools.partial(_ffn_kernel, final_gain=final_gain is not None),
        grid=(rows // tm,),
        in_specs=[pl.BlockSpec((tm, d), lambda i: (i, 0)),
                  pl.BlockSpec((2, d), lambda i: (0, 0)),
                  pl.BlockSpec(w_gate_up.shape, lambda i: (0, 0)),
                  pl.BlockSpec(w_down.shape, lambda i: (0, 0))],
        out_specs=pl.BlockSpec((tm, d), lambda i: (i, 0)),
        out_shape=jax.ShapeDtypeStruct((rows, d), F32),
        input_output_aliases={0: 0},
        compiler_params=_params("arbitrary"),
        name="ffn",
    )(x, gains, w_gate_up, w_down)


def _rw_proj_kernel(*refs, has_v_res):
    if has_v_res:
        (x_ref, gain_ref, shift0_ref, mu_ref, wr_ref, wk_ref, wv_ref, vec_ref,
         w1_ref, w2_ref, a1_ref, a2_ref, g1_ref, g2_ref, v1_ref, v2_ref, vfirst_ref,
         r_ref, lw_ref, k_ref, v_ref, kk_ref, a_ref, g_ref, shift_ref, prev_ref) = refs
    else:
        (x_ref, gain_ref, shift0_ref, mu_ref, wr_ref, wk_ref, wv_ref, vec_ref,
         w1_ref, w2_ref, a1_ref, a2_ref, g1_ref, g2_ref,
         r_ref, lw_ref, k_ref, v_ref, kk_ref, a_ref, g_ref, shift_ref, prev_ref) = refs
    i = pl.program_id(1)
    h = _rms(x_ref[...], gain_ref[...])
    tm = h.shape[0]

    @pl.when(i == 0)
    def _():
        prev_ref[...] = shift0_ref[0]

    row = lax.broadcasted_iota(jnp.int32, h.shape, 0)
    h_prev = jnp.where(row == 0, prev_ref[...], pltpu.roll(h, 1, 0))
    last = h[tm - 1:tm, :]
    prev_ref[...] = last
    shift_ref[0] = last

    xx = h_prev - h
    mixed = lambda j: (h + xx * mu_ref[j:j + 1, :]).astype(BF16)
    w0, a0, k_k, k_a, v0 = (vec_ref[j:j + 1, :] for j in range(5))

    r_ref[...] = _dot(mixed(0), wr_ref[...]).astype(r_ref.dtype)
    dec_in = _dot(mixed(1), w1_ref[...])
    k = _dot(mixed(2), wk_ref[...])
    dec = w0 + _dot(jnp.tanh(dec_in).astype(BF16), w2_ref[...])
    x_v = mixed(3)
    v = _dot(x_v, wv_ref[...])
    lw_ref[...] = -jnp.exp(-jax.nn.softplus(-dec) - 0.5)
    if has_v_res:
        mix = jax.nn.sigmoid(v0 + _dot(_dot(x_v, v1_ref[...]).astype(BF16), v2_ref[...]))
        v = v + (vfirst_ref[...] - v) * mix
    v_ref[...] = v
    a_in = _dot(mixed(4), a1_ref[...])
    g_in = _dot(mixed(5), g1_ref[...])
    a = jax.nn.sigmoid(a0 + _dot(a_in.astype(BF16), a2_ref[...]))
    kk_ref[...] = (k * k_k).astype(kk_ref.dtype)
    k_ref[...] = (k * (1.0 + (a - 1.0) * k_a)).astype(k_ref.dtype)
    a_ref[...] = a.astype(a_ref.dtype)
    g_ref[...] = _dot(jax.nn.sigmoid(g_in).astype(BF16), g2_ref[...]).astype(g_ref.dtype)


def _pad_cols(w, mult=LANES):
    pad = -w.shape[1] % mult
    return jnp.pad(w, ((0, 0), (0, pad))) if pad else w


def _pad_rows(w, mult=LANES):
    pad = -w.shape[0] % mult
    return jnp.pad(w, ((0, pad), (0, 0))) if pad else w


def _rw_proj(x, batch, gain, shift0, p, v_first):
    rows, d = x.shape
    seq = rows // batch
    tm = min(WIDE_ROW_TILE, seq)
    nt = seq // tm
    has_v_res = v_first is not None
    zeros = jnp.zeros((d,), F32)
    vec = jnp.stack([p["w0"], p["a0"], p["k_k"], p["k_a"], p["v0"] if has_v_res else zeros,
                     zeros, zeros, zeros])
    row_spec = pl.BlockSpec((tm, d), lambda b, i: (b * nt + i, 0))

    def full(arr):
        return pl.BlockSpec(arr.shape, lambda b, i: (0,) * arr.ndim, pipeline_mode=pl.Buffered(1))

    ins = [x, gain.reshape(1, d), shift0.reshape(batch, 1, d), p["mu"], p["w_r"], p["w_k"], p["w_v"], vec,
           p["w1"], p["w2"], p["a1"], p["a2"], p["g1"], p["g2"]]
    in_specs = [row_spec, full(ins[1]), pl.BlockSpec((1, 1, d), lambda b, i: (b, 0, 0))]
    in_specs += [full(a) for a in ins[3:]]
    if has_v_res:
        ins += [p["v1"], p["v2"], v_first]
        in_specs += [full(p["v1"]), full(p["v2"]), row_spec]
    tok = [jax.ShapeDtypeStruct((rows, d), dt) for dt in (BF16, F32, BF16, F32, BF16, BF16, BF16)]
    outs = pl.pallas_call(
        functools.partial(_rw_proj_kernel, has_v_res=has_v_res),
        grid=(batch, nt),
        in_specs=in_specs,
        out_specs=[row_spec] * 7 + [pl.BlockSpec((1, 1, d), lambda b, i: (b, 0, 0))],
        out_shape=tok + [jax.ShapeDtypeStruct((batch, 1, d), F32)],
        scratch_shapes=[pltpu.VMEM((1, d), F32)],
        compiler_params=_params("arbitrary", "arbitrary"),
        name="rw_proj",
    )(*ins)
    return outs[:7], outs[7].reshape(batch, d)


WKV_DECAY_PIECES = 3
WKV_GROUP_PIECES = 2
WKV_STATE_PASSES = 1
WKV_SOLVE_PASSES = 1


def _wkv_prepare(units, c_):
    n = units[0][0].shape[0]
    sum_l = lambda x: _mm_exact_lhs(c_["tri_incl"], x, WKV_DECAY_PIECES)
    sum_g = lambda x: _mm_exact_rhs(x, c_["gsum"], WKV_GROUP_PIECES)
    solve_mm = lambda x, y, nt=False: _mm(x, y, WKV_SOLVE_PASSES, nt=nt)

    cs = [sum_l(u[1]) for u in units]
    head_sums = [sum_g(jnp.concatenate([u[4] * u[4], u[0] * u[2] * u[7][0:1, :]], axis=0)) for u in units]
    pre = []
    for (r, lw, k, v, kk, a, g, vec), cs_u, hs in zip(units, cs, head_sums):
        cs_last = cs_u[n - 1:n, :]
        d_inv = jnp.exp(-cs_u)
        d_rest = jnp.exp(cs_last - cs_u)
        kkn = kk * lax.rsqrt(jnp.maximum(hs[:n], 1e-24))
        b = kkn * a
        pre.append(dict(a_t=-kkn * jnp.exp(cs_u - lw), r_t=r * jnp.exp(cs_u), b_t=b * d_inv, k_t=k * d_inv,
                        bk_rest=jnp.concatenate([b * d_rest, k * d_rest], axis=0), decay=jnp.exp(cs_last),
                        v=v, g=g, vec=vec, bonus=hs[n:] * v))

    by_head = lambda x: jnp.concatenate([jnp.where(m, x, 0.0) for m in c_["head_masks"]], axis=0)
    block_diag = lambda w: jnp.where(c_["same_block"], jnp.concatenate([w, w], axis=0), 0.0)
    grams = [solve_mm(jnp.concatenate([q["a_t"], q["r_t"]], axis=0),
                      jnp.concatenate([by_head(q["b_t"]), by_head(q["k_t"])], axis=0), nt=True)
             for q in pre]
    lows = [jnp.where(c_["strict_w"], gm[:n, :2 * n], 0.0) for gm in grams]
    m_aks = [jnp.where(c_["strict_w"], gm[:n, 2 * n:], 0.0) for gm in grams]
    for q, gm in zip(pre, grams):
        q["m_r"] = jnp.where(c_["incl_w2"], gm[n:], 0.0)
        q["v_heads"] = by_head(q["v"])
    from_v = [solve_mm(mk, q["v_heads"]) for mk, q in zip(m_aks, pre)]
    inv = [c_["eye_w"] + lo for lo in lows]
    if n > 2:
        pw = [solve_mm(lo, block_diag(lo)) for lo in lows]
        step = 2
        while 2 * step < n:
            both = [solve_mm(jnp.concatenate([t, p2], axis=0), block_diag(p2)) for t, p2 in zip(inv, pw)]
            inv = [t + bo[:n] for t, bo in zip(inv, both)]
            pw = [bo[n:] for bo in both]
            step *= 2
        inv = [t + solve_mm(t, block_diag(p2)) for t, p2 in zip(inv, pw)]
    for q, t, fv in zip(pre, inv, from_v):
        q["inv"], q["from_v"] = t, fv
    return pre


def _wkv_advance(pre, states, c_):
    n = pre[0]["v"].shape[0]
    inv_hd = 1.0 / RW_HEAD_DIM
    sum_g = lambda x: _mm_exact_rhs(x, c_["gsum"], WKV_GROUP_PIECES)
    solve_mm = lambda x, y: _mm(x, y, WKV_SOLVE_PASSES)
    by_head = lambda x: jnp.concatenate([jnp.where(m, x, 0.0) for m in c_["head_masks"]], axis=0)

    from_state = [_mm(jnp.concatenate([q["a_t"], q["r_t"]], axis=0), s, WKV_STATE_PASSES, nt=True)
                  for q, s in zip(pre, states)]
    sig_p = [solve_mm(q["inv"], by_head(fs[:n] + q["from_v"])) for q, fs in zip(pre, from_state)]
    y_p = [fs[n:] + solve_mm(q["m_r"], jnp.concatenate([by_head(sg), q["v_heads"]], axis=0))
           for q, fs, sg in zip(pre, from_state, sig_p)]
    upd = [_mm(jnp.concatenate([s, q["v"]], axis=0).T, q["bk_rest"], WKV_STATE_PASSES)
           for s, q in zip(sig_p, pre)]
    new_states = [s * q["decay"] + jnp.where(c_["same_head"], u, 0.0) for s, q, u in zip(states, pre, upd)]

    mean = [sum_g(y) * inv_hd for y in y_p]
    dev = [y - m for y, m in zip(y_p, mean)]
    var = [sum_g(dv * dv) * inv_hd for dv in dev]
    outs = []
    for q, dv, vr in zip(pre, dev, var):
        o = dv * lax.rsqrt(vr + RW_GN_EPS) * q["vec"][1:2, :] + q["vec"][2:3, :]
        outs.append((o + q["bonus"]) * q["g"])
    return outs, new_states


def _wkv_kernel(*refs, has_state0, n):
    if has_state0:
        (r_ref, lw_ref, k_ref, v_ref, kk_ref, a_ref, g_ref, vec_ref, s0_ref,
         o_ref, sout_ref, state_ref) = refs
    else:
        (r_ref, lw_ref, k_ref, v_ref, kk_ref, a_ref, g_ref, vec_ref,
         o_ref, sout_ref, state_ref) = refs
    c = pl.program_id(2)
    hd = RW_HEAD_DIM

    @pl.when(c == 0)
    def _():
        if has_state0:
            state_ref[...] = s0_ref[0]
        else:
            state_ref[...] = jnp.zeros_like(state_ref)

    tr = lax.broadcasted_iota(jnp.int32, (n, n), 0)
    tc = lax.broadcasted_iota(jnp.int32, (n, n), 1)
    tr2 = lax.broadcasted_iota(jnp.int32, (n, 2 * n), 0)
    tc2 = lax.rem(lax.broadcasted_iota(jnp.int32, (n, 2 * n), 1), n)
    tr4 = lax.broadcasted_iota(jnp.int32, (n, 4 * n), 0)
    tc4 = lax.rem(lax.broadcasted_iota(jnp.int32, (n, 4 * n), 1), n)
    br = lax.broadcasted_iota(jnp.int32, (2 * n, 2 * n), 0) // n
    bc = lax.broadcasted_iota(jnp.int32, (2 * n, 2 * n), 1) // n
    lane = lax.broadcasted_iota(jnp.int32, (n, LANES), 1)
    sr = lax.broadcasted_iota(jnp.int32, (LANES, LANES), 0) // hd
    sc = lax.broadcasted_iota(jnp.int32, (LANES, LANES), 1) // hd
    consts = dict(
        gsum=_group_ones(LANES, hd),
        tri_incl=jnp.where(tc <= tr, 1.0, 0.0).astype(BF16),
        strict_w=tc2 < tr2,
        incl_w2=tc4 <= tr4,
        eye_w=jnp.where(tc2 == tr2, 1.0, 0.0).astype(F32),
        same_block=br == bc,
        head_masks=[lane < hd, lane >= hd],
        same_head=sr == sc,
    )
    n_pairs = state_ref.shape[0]
    n_chunks = r_ref.shape[0] // n
    units = []
    for ci in range(n_chunks):
        rows = slice(ci * n, (ci + 1) * n)
        for p in range(n_pairs):
            sl = slice(p * LANES, (p + 1) * LANES)
            units.append(tuple(ref[rows, sl].astype(F32)
                               for ref in (r_ref, lw_ref, k_ref, v_ref, kk_ref, a_ref, g_ref)) + (vec_ref[:, sl],))
    pre = _wkv_prepare(units, consts)
    states = [state_ref[p] for p in range(n_pairs)]
    outs = []
    for ci in range(n_chunks):
        out, states = _wkv_advance(pre[ci * n_pairs:(ci + 1) * n_pairs], states, consts)
        outs.append(jnp.concatenate(out, axis=1))
    o_ref[...] = jnp.concatenate(outs, axis=0).astype(o_ref.dtype)
    new_states = jnp.stack(states)
    state_ref[...] = new_states
    sout_ref[0] = new_states


def _wkv(tok, batch, vec, state0):
    rows, d = tok[0].shape
    seq = rows // batch
    n = min(WKV_CHUNK, seq)
    span = n * min(WKV_CHUNKS_PER_STEP, seq // n)
    nc = seq // span
    pairs = d // LANES
    gp = WKV_PAIRS
    width = gp * LANES
    has_state0 = state0 is not None
    tok_spec = pl.BlockSpec((span, width), lambda b, p, c: (b * nc + c, p))
    state_spec = pl.BlockSpec((1, gp, LANES, LANES), lambda b, p, c: (b, p, 0, 0))
    ins = list(tok) + [vec]
    in_specs = [tok_spec] * 7 + [pl.BlockSpec((8, width), lambda b, p, c: (0, p))]
    if has_state0:
        ins.append(state0)
        in_specs.append(state_spec)
    return pl.pallas_call(
        functools.partial(_wkv_kernel, has_state0=has_state0, n=n),
        grid=(batch, pairs // gp, nc),
        in_specs=in_specs,
        out_specs=[tok_spec, state_spec],
        out_shape=[jax.ShapeDtypeStruct((rows, d), BF16),
                   jax.ShapeDtypeStruct((batch, pairs, LANES, LANES), F32)],
        scratch_shapes=[pltpu.VMEM((gp, LANES, LANES), F32)],
        compiler_params=_params("arbitrary", "arbitrary", "arbitrary"),
        name="wkv",
    )(*ins)


def _state_to_blockdiag(state):
    b, h, n, _ = state.shape
    s = state.reshape(b, h // 2, 2, n, n)
    z = jnp.zeros_like(s[:, :, 0])
    top = jnp.concatenate([s[:, :, 0], z], axis=-1)
    bot = jnp.concatenate([z, s[:, :, 1]], axis=-1)
    return jnp.concatenate([top, bot], axis=-2)


def _state_from_blockdiag(bd):
    b, p, n2, _ = bd.shape
    n = n2 // 2
    return jnp.stack([bd[:, :, :n, :n], bd[:, :, n:, n:]], axis=2).reshape(b, 2 * p, n, n)


def _trunk(x, batch, mem_k, mem_v, past_k, past_v, rw_state0, rw_shift0, w):
    depth = len(w["ffn_gu"])
    d = x.shape[-1]
    sb_k, sb_v, rw_states, rw_shifts = None, None, [], []
    n_sb = (depth + 1) // 2
    v_first = None
    for layer in range(depth):
        j = layer // 2
        if layer % 2 == 0:
            sb_k, sb_v, qkv = _sb_qkv(x, batch, w["norm_mix"][layer], w["sb_qkv"][j], j, n_sb, sb_k, sb_v)
            if past_k is None:
                o = _sb_attention_prompt(qkv, batch)
            else:
                o = _sb_attention_sample(qkv, batch, past_k, past_v, j)
            w_o = w["sb_o"][j]
        else:
            if rw_state0 is None:
                state0 = None
                shift0 = jnp.zeros((batch, d), F32)
            else:
                state0 = _state_to_blockdiag(rw_state0[j])
                shift0 = rw_shift0[j]
            tok, shift = _rw_proj(x, batch, w["norm_mix"][layer], shift0, w["rw"][j], v_first)
            if v_first is None:
                v_first = tok[3]
            o, state = _wkv(tok, batch, w["rw"][j]["out_vec"], state0)
            rw_states.append(_state_from_blockdiag(state))
            rw_shifts.append(shift)
            w_o = w["rw"][j]["w_o"]
        x = _xattn(o, w_o, x, w["norm_xattn"][layer], w["xa_q"][layer], w["xa_o"][layer],
                   mem_k[layer], mem_v[layer])
        final = w["norm_final"] if layer == depth - 1 else None
        x = _ffn(x, w["norm_ffn"][layer], w["ffn_gu"][layer], w["ffn_down"][layer], final)
    return x, sb_k, sb_v, jnp.stack(rw_states), jnp.stack(rw_shifts)


def kernel(x_prompt, x_sample, mem_prompt, cache_sb_k, cache_sb_v, state_rwkv_wkv, state_rwkv_shift,
           cache_mem_k, cache_mem_v, norm_mix, norm_xattn, norm_ffn, norm_mem, norm_final,
           sb_w_qkv, sb_w_o, rw_mu, rw_w_r, rw_w_k, rw_w_v, rw_w_o, rw_w0, rw_w1, rw_w2,
           rw_a0, rw_a1, rw_a2, rw_v0, rw_v1, rw_v2, rw_g1, rw_g2, rw_k_k, rw_k_a, rw_r_k,
           rw_lnx_w, rw_lnx_b, xa_w_q, xa_w_kv, xa_w_o, ffn_w_gate_up, ffn_w_down):
    batch, seq, d = x_prompt.shape
    dec_batch, dec_seq, _ = x_sample.shape
    depth = norm_mix.shape[0]
    n_rw = rw_w_r.shape[0]
    n_sb = sb_w_qkv.shape[0]
    n_mem = mem_prompt.shape[1]
    bf = lambda t: t.astype(BF16)

    rw = []
    for j in range(n_rw):
        pj = dict(mu=rw_mu[j], w_r=bf(rw_w_r[j]), w_k=bf(rw_w_k[j]), w_v=bf(rw_w_v[j]), w_o=bf(rw_w_o[j]),
                  w0=rw_w0[j], a0=rw_a0[j], k_k=rw_k_k[j], k_a=rw_k_a[j],
                  w1=bf(_pad_cols(rw_w1[j])), w2=bf(_pad_rows(rw_w2[j])),
                  a1=bf(_pad_cols(rw_a1[j])), a2=bf(_pad_rows(rw_a2[j])),
                  g1=bf(_pad_cols(rw_g1[j])), g2=bf(_pad_rows(rw_g2[j])))
        if j > 0:
            pj.update(v0=rw_v0[j - 1], v1=bf(_pad_cols(rw_v1[j - 1])), v2=bf(_pad_rows(rw_v2[j - 1])))
        zeros = jnp.zeros((d,), F32)
        pj["out_vec"] = jnp.stack([rw_r_k[j].reshape(d), rw_lnx_w[j], rw_lnx_b[j]] + [zeros] * 5)
        rw.append(pj)
    w = dict(norm_mix=norm_mix, norm_xattn=norm_xattn, norm_ffn=norm_ffn, norm_final=norm_final,
             sb_qkv=bf(sb_w_qkv), sb_o=bf(sb_w_o), rw=rw, xa_q=bf(xa_w_q), xa_o=bf(xa_w_o),
             ffn_gu=bf(ffn_w_gate_up), ffn_down=bf(ffn_w_down))

    mem_k, mem_v, mem_k_bf, mem_v_bf = _mem_kv(mem_prompt, norm_mem, bf(xa_w_kv))
    y_p, p_sb_k, p_sb_v, p_wkv, p_shift = _trunk(
        x_prompt.reshape(batch * seq, d), batch, mem_k_bf, mem_v_bf, None, None, None, None, w)

    y_s, s_sb_k, s_sb_v, s_wkv, s_shift = _trunk(
        x_sample.reshape(dec_batch * dec_seq, d), dec_batch,
        bf(cache_mem_k).reshape(depth, dec_batch, n_mem, d), bf(cache_mem_v).reshape(depth, dec_batch, n_mem, d),
        cache_sb_k.reshape(n_sb, dec_batch, -1, d), cache_sb_v.reshape(n_sb, dec_batch, -1, d),
        state_rwkv_wkv, state_rwkv_shift, w)

    hd = d // SB_HEADS
    return (y_p.reshape(batch, seq, d), y_s.reshape(dec_batch, dec_seq, d),
            p_sb_k.reshape(n_sb, batch, seq, SB_HEADS, hd), p_sb_v.reshape(n_sb, batch, seq, SB_HEADS, hd),
            p_wkv, p_shift,
            mem_k, mem_v,
            s_sb_k.reshape(n_sb, dec_batch, dec_seq, SB_HEADS, hd),
            s_sb_v.reshape(n_sb, dec_batch, dec_seq, SB_HEADS, hd),
            s_wkv, s_shift)
```

```python
import functools

import jax
import jax.numpy as jnp
from jax import lax
from jax.experimental import pallas as pl
from jax.experimental.pallas import tpu as pltpu

F32 = jnp.float32
BF16 = jnp.bfloat16

NORM_EPS = 1e-6
NEG_LOG2_E = -1.4426950408889634
RW_GN_EPS = 64e-5
SB_HEADS = 16
XA_HEADS = 4
RW_HEAD_DIM = 64
LANES = 128
ROW_TILE = 256
WIDE_ROW_TILE = 512
SB_BLOCK = 256
SB_PAIRS = 8
SB_WAVES = 4
SB_DECODE_WAVES = 2
SB_UNROLL = 2
SB_PAST_SPAN = 2048
WKV_CHUNK = 64
WKV_CHUNKS_PER_STEP = 4
WKV_PAIRS = 8
FFN_CHUNK = 256
VMEM_LIMIT_BYTES = 56 * 1024 * 1024


def _params(*sem):
    return pltpu.CompilerParams(dimension_semantics=sem, vmem_limit_bytes=VMEM_LIMIT_BYTES)


def _dot(a, b):
    return jnp.dot(a, b, preferred_element_type=F32)


def _dot_nt(a, b):
    return lax.dot_general(a, b, (((1,), (1,)), ((), ())), preferred_element_type=F32)


def _split(x, n):
    parts = []
    for i in range(n):
        p = x.astype(BF16)
        parts.append(p)
        if i + 1 < n:
            x = x - p.astype(F32)
    return parts


def _mm(a, b, passes, nt=False):
    dot = _dot_nt if nt else _dot
    if passes == 1:
        return dot(a.astype(BF16), b.astype(BF16))
    a_hi, a_lo = _split(a, 2)
    b_hi, b_lo = _split(b, 2)
    return dot(a_hi, b_hi) + (dot(a_hi, b_lo) + dot(a_lo, b_hi))


def _mm_exact_rhs(a, b_stacked, n):
    return _dot(jnp.concatenate(_split(a, n), axis=1), b_stacked)


def _mm_exact_lhs(a_repeated, b, n):
    return _dot(a_repeated, jnp.concatenate(_split(b, n), axis=0))


def _rms(x, gain):
    ms = jnp.mean(x * x, axis=-1, keepdims=True)
    return x * lax.rsqrt(ms + NORM_EPS) * gain


def _group_ones(n, group):
    r = lax.broadcasted_iota(jnp.int32, (n, n), 0) // group
    c = lax.broadcasted_iota(jnp.int32, (n, n), 1) // group
    return jnp.where(r == c, 1.0, 0.0).astype(BF16)


def _mem_kv_kernel(x_ref, g_ref, w_ref, k_ref, v_ref, kb_ref, vb_ref):
    tm, d = x_ref.shape
    heads, dh = k_ref.shape[-2:]
    h = _rms(x_ref[...], g_ref[0]).astype(BF16)
    kv = _dot(h, w_ref[0])
    k, v = kv[:, :d], kv[:, d:]
    k_ref[0, 0] = k.reshape(tm, heads, dh)
    v_ref[0, 0] = v.reshape(tm, heads, dh)
    kb_ref[0, 0] = k.astype(BF16)
    vb_ref[0, 0] = v.astype(BF16)


def _mem_kv(mem, gains, w_kv):
    depth = w_kv.shape[0]
    batch, n_mem, d = mem.shape
    tm = min(ROW_TILE, n_mem)
    nt = n_mem // tm
    dh = d // XA_HEADS
    assert n_mem % tm == 0 and dh % LANES == 0
    out = jax.ShapeDtypeStruct((depth, batch, n_mem, XA_HEADS, dh), F32)
    out_b = jax.ShapeDtypeStruct((depth, batch, n_mem, d), BF16)
    spec = pl.BlockSpec((1, 1, tm, XA_HEADS, dh), lambda l, i: (l, i // nt, i % nt, 0, 0))
    spec_b = pl.BlockSpec((1, 1, tm, d), lambda l, i: (l, i // nt, i % nt, 0))
    return pl.pallas_call(
        _mem_kv_kernel,
        grid=(depth, batch * nt),
        in_specs=[pl.BlockSpec((tm, d), lambda l, i: (i, 0)),
                  pl.BlockSpec((1, 1, d), lambda l, i: (l, 0, 0)),
                  pl.BlockSpec((1, d, 2 * d), lambda l, i: (l, 0, 0))],
        out_specs=[spec, spec, spec_b, spec_b],
        out_shape=[out, out, out_b, out_b],
        compiler_params=_params("arbitrary", "arbitrary"),
        name="mem_kv",
    )(mem.reshape(batch * n_mem, d), gains.reshape(depth, 1, d), w_kv)


def _split_heads(x, hd):
    tm, d = x.shape
    blocks = []
    for p in range(d // LANES):
        slab = x[:, p * LANES:(p + 1) * LANES]
        blocks += [slab, pltpu.roll(slab, LANES - hd, 1)]
    return jnp.concatenate(blocks, axis=1).reshape(tm, d // hd, LANES)[:, :, :hd]


def _sb_qkv_kernel(x_ref, g_ref, w_ref, *rest, q_scale, slot):
    k_ref, v_ref, qkv_ref = rest[-3:]
    d = x_ref.shape[-1]
    hd = k_ref.shape[-1]
    h = _rms(x_ref[...], g_ref[...]).astype(BF16)
    qkv = _dot(h, w_ref[...])
    qkv_ref[:, :d] = (qkv[:, :d] * q_scale).astype(BF16)
    qkv_ref[:, d:] = qkv[:, d:].astype(BF16)
    k3 = _split_heads(qkv[:, d:2 * d], hd)
    v3 = _split_heads(qkv[:, 2 * d:], hd)
    own = slot if k_ref.shape[0] > 1 else 0
    for s in range(k_ref.shape[0]):
        if s == own:
            k_ref[s, 0] = k3
            v_ref[s, 0] = v3
        else:
            k_ref[s, 0] = jnp.zeros(k3.shape, k_ref.dtype)
            v_ref[s, 0] = jnp.zeros(v3.shape, v_ref.dtype)


def _sb_qkv(x, batch, gain, w, slot, n_slots, k_all=None, v_all=None):
    rows, d = x.shape
    seq = rows // batch
    tm = min(ROW_TILE, seq)
    nt = seq // tm
    hd = d // SB_HEADS
    assert 2 * hd == LANES
    q_scale = float(hd ** -0.5)
    ins = [x, gain.reshape(1, d), w]
    in_specs = [pl.BlockSpec((tm, d), lambda i: (i, 0)),
                pl.BlockSpec((1, d), lambda i: (0, 0)),
                pl.BlockSpec(w.shape, lambda i: (0, 0))]
    if k_all is None:
        aliases = {}
        slot_spec = pl.BlockSpec((n_slots, 1, tm, SB_HEADS, hd), lambda i: (0, i // nt, i % nt, 0, 0))
    else:
        ins += [k_all, v_all]
        in_specs += [pl.BlockSpec(memory_space=pl.ANY)] * 2
        aliases = {3: 0, 4: 1}
        slot_spec = pl.BlockSpec((1, 1, tm, SB_HEADS, hd), lambda i: (slot, i // nt, i % nt, 0, 0))
    kv_shape = jax.ShapeDtypeStruct((n_slots, batch, seq, SB_HEADS, hd), F32)
    return pl.pallas_call(
        functools.partial(_sb_qkv_kernel, q_scale=q_scale, slot=slot),
        grid=(rows // tm,),
        in_specs=in_specs,
        out_specs=[slot_spec, slot_spec, pl.BlockSpec((tm, 3 * d), lambda i: (i, 0))],
        out_shape=[kv_shape, kv_shape, jax.ShapeDtypeStruct((rows, 3 * d), BF16)],
        input_output_aliases=aliases,
        compiler_params=_params("arbitrary"),
        name="sb_qkv",
    )(*ins)


def _sb_step(q2s, kv_blocks, carry, acc, neg_lower2, mask, n_waves):
    m = q2s[0].shape[0]
    groups = len(q2s)
    per = max(groups // n_waves, 1)
    waves = [range(c, min(c + per, groups)) for c in range(0, groups, per)]
    rows = [slice(w_[0] * m, (w_[-1] + 1) * m) for w_ in waves]
    units = [(b, c) for b in range(len(kv_blocks)) for c in range(len(waves))]

    def scores(b, c):
        return jnp.concatenate([_dot_nt(q2s[g], kv_blocks[b][0][g]) for g in waves[c]], axis=0)

    def softplus_pieces(c, z):
        sp = jnp.maximum(z, 0.0) + jnp.log(1.0 + jnp.exp2(jnp.abs(z) * NEG_LOG2_E))
        if mask is not None:
            sp = jnp.where(mask[rows[c]], sp, 0.0)
        return jnp.concatenate(_split(sp, 2), axis=1)

    def weights(c, z, from_s, later):
        w = jnp.exp(z + from_s + jnp.concatenate([later] * (z.shape[1] // LANES), axis=1))
        if mask is not None:
            w = jnp.where(mask[rows[c]], w, 0.0)
        return w.astype(BF16)

    def values(b, c, w):
        return jnp.concatenate([_dot(w[i * m:(i + 1) * m], kv_blocks[b][1][g])
                                for i, g in enumerate(waves[c])], axis=0)

    n = len(units)
    z, pieces, from_s, w = ([None] * n for _ in range(4))
    later = [carry[r] for r in rows]
    pv = [None] * len(waves)
    for tick in range(n + 4):
        for u, (b, c) in enumerate(units):
            stage = tick - u
            if stage == 0:
                z[u] = scores(b, c)
            elif stage == 1:
                pieces[u] = softplus_pieces(c, z[u])
            elif stage == 2:
                from_s[u] = _dot(pieces[u], neg_lower2)
            elif stage == 3:
                w[u] = weights(c, z[u], from_s[u], later[c])
                later[c] = later[c] + jnp.broadcast_to(from_s[u][:, 0:1], later[c].shape)
            elif stage == 4:
                out = values(b, c, w[u])
                pv[c] = out if pv[c] is None else pv[c] + out
    return jnp.concatenate(later, axis=0), acc + jnp.concatenate(pv, axis=0)


def _sb_consts(bk):
    r = lax.broadcasted_iota(jnp.int32, (2, bk, bk), 1).reshape(2 * bk, bk)
    c = lax.broadcasted_iota(jnp.int32, (2 * bk, bk), 1)
    return jnp.where(r >= c, -1.0, 0.0).astype(BF16)


def _sb_stack_heads(q_ref, q2_ref):
    tq = q_ref.shape[0]
    lane = lax.broadcasted_iota(jnp.int32, (tq, LANES), 1)
    for g in range(q2_ref.shape[0]):
        q = q_ref[:, g * LANES:(g + 1) * LANES]
        zero = jnp.zeros_like(q)
        q2_ref[g] = jnp.concatenate([jnp.where(lane < LANES // 2, q, zero),
                                     jnp.where(lane >= LANES // 2, q, zero)], axis=0)


def _sb_causal_mask(groups, tq, bk, t0=0):
    t = lax.broadcasted_iota(jnp.int32, (2 * groups, tq, bk), 1).reshape(2 * groups * tq, bk)
    s = lax.broadcasted_iota(jnp.int32, (2 * groups * tq, bk), 1)
    return s < t + t0


def _sb_write(o_ref, acc_ref):
    tq = o_ref.shape[0]
    lane = lax.broadcasted_iota(jnp.int32, (tq, LANES), 1)
    for g in range(acc_ref.shape[0] // (2 * tq)):
        o_ref[:, g * LANES:(g + 1) * LANES] = jnp.where(
            lane < LANES // 2, acc_ref[2 * g * tq:(2 * g + 1) * tq, :],
            acc_ref[(2 * g + 1) * tq:(2 * g + 2) * tq, :]).astype(o_ref.dtype)


def _sb_prompt_kernel(q_ref, k_ref, v_ref, o_ref, q2_ref, carry_ref, acc_ref):
    blk = q_ref.shape[0]
    groups = q2_ref.shape[0]
    i = pl.program_id(2)
    neg_lower2 = _sb_consts(blk)
    _sb_stack_heads(q_ref, q2_ref)
    lanes = [slice(g * LANES, (g + 1) * LANES) for g in range(groups)]

    def block(j):
        row0 = pl.multiple_of(j * blk, blk)
        return ([k_ref[pl.ds(row0, blk), sl] for sl in lanes], [v_ref[pl.ds(row0, blk), sl] for sl in lanes])

    def step(blocks, carry, acc, mask):
        return _sb_step([q2_ref[g] for g in range(groups)], blocks, carry, acc, neg_lower2, mask, SB_WAVES)

    half = blk // 2
    d0 = pl.multiple_of(i * blk, blk)
    zero = jnp.zeros((groups * blk, LANES), F32)
    halves = []
    for t0, keys, consts in ((half, blk, neg_lower2), (0, half, _sb_consts(half))):
        q2_half = [jnp.concatenate([q2_ref[g, t0:t0 + half, :], q2_ref[g, blk + t0:blk + t0 + half, :]], axis=0)
                   for g in range(groups)]
        own = ([k_ref[pl.ds(d0, keys), sl] for sl in lanes], [v_ref[pl.ds(d0, keys), sl] for sl in lanes])
        halves.append(_sb_step(q2_half, [own], zero, zero, consts, _sb_causal_mask(groups, half, keys, t0),
                               SB_WAVES))
    (carry_late, acc_late), (carry_early, acc_early) = halves
    for g in range(groups):
        for h in range(2):
            src = slice((2 * g + h) * half, (2 * g + h + 1) * half)
            dst = (2 * g + h) * blk
            for ref, early, late in ((carry_ref, carry_early, carry_late), (acc_ref, acc_early, acc_late)):
                ref[dst:dst + half, :] = early[src]
                ref[dst + half:dst + blk, :] = late[src]

    odd = i % SB_UNROLL

    def single(it, c):
        carry_ref[...], acc_ref[...] = step([block(i - 1 - it)], carry_ref[...], acc_ref[...], None)
        return c

    def multi(it, c):
        top = i - 1 - odd - it * SB_UNROLL
        carry_ref[...], acc_ref[...] = step([block(top - u) for u in range(SB_UNROLL)],
                                            carry_ref[...], acc_ref[...], None)
        return c

    lax.fori_loop(0, odd, single, 0)
    lax.fori_loop(0, i // SB_UNROLL, multi, 0)
    _sb_write(o_ref, acc_ref)


def _sb_attention_prompt(qkv, batch):
    rows, d3 = qkv.shape
    d = d3 // 3
    seq = rows // batch
    blk = min(SB_BLOCK, seq)
    nq = seq // blk
    width = SB_PAIRS * LANES
    ng = d // width
    return pl.pallas_call(
        _sb_prompt_kernel,
        grid=(batch, ng, nq),
        in_specs=[pl.BlockSpec((blk, width), lambda b, p, i: (b * nq + i, p)),
                  pl.BlockSpec((seq, width), lambda b, p, i: (b, ng + p)),
                  pl.BlockSpec((seq, width), lambda b, p, i: (b, 2 * ng + p))],
        out_specs=pl.BlockSpec((blk, width), lambda b, p, i: (b * nq + i, p)),
        out_shape=jax.ShapeDtypeStruct((rows, d), BF16),
        scratch_shapes=[pltpu.VMEM((SB_PAIRS, 2 * blk, LANES), BF16),
                        pltpu.VMEM((SB_PAIRS * 2 * blk, LANES), F32),
                        pltpu.VMEM((SB_PAIRS * 2 * blk, LANES), F32)],
        compiler_params=_params("arbitrary", "arbitrary", "arbitrary"),
        name="sb_attention_prompt",
    )(qkv, qkv, qkv)


def _sb_sample_kernel(q_ref, kn_ref, vn_ref, kp_ref, vp_ref, o_ref, q2_ref, carry_ref, acc_ref, *, blk):
    tq = q_ref.shape[0]
    groups = q2_ref.shape[0]
    c = pl.program_id(1)
    neg_lower2 = _sb_consts(blk)

    lanes = [slice(g * LANES, (g + 1) * LANES) for g in range(groups)]

    @pl.when(c == 0)
    def _():
        _sb_stack_heads(q_ref, q2_ref)
        zero = jnp.zeros(carry_ref.shape, F32)
        carry_ref[...], acc_ref[...] = _sb_step(
            [q2_ref[g] for g in range(groups)],
            [([kn_ref[0, :, sl] for sl in lanes], [vn_ref[0, :, sl] for sl in lanes])],
            zero, zero, neg_lower2, _sb_causal_mask(groups, tq, blk), SB_DECODE_WAVES)

    blocks = []
    for sb in reversed(range(kp_ref.shape[2] // blk)):
        rows = slice(sb * blk, (sb + 1) * blk)
        blocks.append(([kp_ref[0, 0, rows, sl].astype(BF16) for sl in lanes],
                       [vp_ref[0, 0, rows, sl].astype(BF16) for sl in lanes]))
    carry_ref[...], acc_ref[...] = _sb_step([q2_ref[g] for g in range(groups)], blocks,
                                            carry_ref[...], acc_ref[...], neg_lower2, None, SB_DECODE_WAVES)

    @pl.when(c == pl.num_programs(1) - 1)
    def _():
        _sb_write(o_ref, acc_ref)


def _sb_attention_sample(qkv, batch, past_k, past_v, layer):
    rows, d3 = qkv.shape
    d = d3 // 3
    seq = rows // batch
    blk = SB_BLOCK
    pairs = d // LANES
    past_len = past_k.shape[2]
    span = min(SB_PAST_SPAN, past_len)
    steps = past_len // span
    assert seq <= blk and past_len % span == 0 and span % blk == 0
    new = qkv[:, d:].reshape(batch, seq, 2 * d)
    new = jnp.pad(new, ((0, 0), (0, blk - seq), (0, 0)))
    return pl.pallas_call(
        functools.partial(_sb_sample_kernel, blk=blk),
        grid=(batch, steps),
        in_specs=[pl.BlockSpec((seq, d), lambda b, c: (b, 0)),
                  pl.BlockSpec((1, blk, d), lambda b, c: (b, 0, 0)),
                  pl.BlockSpec((1, blk, d), lambda b, c: (b, 0, 1)),
                  pl.BlockSpec((1, 1, span, d), lambda b, c: (layer, b, steps - 1 - c, 0)),
                  pl.BlockSpec((1, 1, span, d), lambda b, c: (layer, b, steps - 1 - c, 0))],
        out_specs=pl.BlockSpec((seq, d), lambda b, c: (b, 0)),
        out_shape=jax.ShapeDtypeStruct((rows, d), BF16),
        scratch_shapes=[pltpu.VMEM((pairs, 2 * seq, LANES), BF16),
                        pltpu.VMEM((pairs * 2 * seq, LANES), F32),
                        pltpu.VMEM((pairs * 2 * seq, LANES), F32)],
        compiler_params=_params("arbitrary", "arbitrary"),
        name="sb_attention_sample",
    )(qkv, new, new, past_k, past_v)


def _xattn_kernel(a_ref, wa_ref, x_ref, g_ref, wq_ref, wo_ref, mk_ref, mv_ref, o_ref, *, heads):
    x = x_ref[...] + _dot(a_ref[...], wa_ref[...])
    d = x.shape[-1]
    dh = d // heads
    h = _rms(x, g_ref[...]).astype(BF16)
    q = _dot(h, wq_ref[...]).astype(BF16)
    nb = mk_ref.shape[0]
    seq = x.shape[0] // nb
    per_seq = []
    for b in range(nb):
        rows = slice(b * seq, (b + 1) * seq)
        outs = []
        for hd in range(heads):
            sl = slice(hd * dh, (hd + 1) * dh)
            s = _dot_nt(q[rows, sl], mk_ref[b, :, sl]) * (dh ** -0.5)
            e = jnp.exp(s - jnp.max(s, axis=-1, keepdims=True))
            p = e / jnp.sum(e, axis=-1, keepdims=True)
            outs.append(_dot(p.astype(BF16), mv_ref[b, :, sl]).astype(BF16))
        per_seq.append(jnp.concatenate(outs, axis=-1))
    o = per_seq[0] if nb == 1 else jnp.concatenate(per_seq, axis=0)
    o_ref[...] = x + _dot(o, wo_ref[...])


def _xattn(a, wa, x, gain, wq, wo, mem_k, mem_v, in_place):
    rows, d = x.shape
    batch, n_mem, _ = mem_k.shape
    seq = rows // batch
    nb = min(batch, max(1, WIDE_ROW_TILE // seq))
    assert batch % nb == 0
    tm = nb * seq if nb > 1 else min(WIDE_ROW_TILE, seq)
    nt = 1 if nb > 1 else seq // tm
    row_spec = pl.BlockSpec((tm, d), lambda b, i: (b * nt + i, 0))
    weight_spec = pl.BlockSpec((d, d), lambda b, i: (0, 0))
    mem_spec = pl.BlockSpec((nb, n_mem, d), lambda b, i: (b, 0, 0))
    return pl.pallas_call(
        functools.partial(_xattn_kernel, heads=XA_HEADS),
        grid=(batch // nb, nt),
        in_specs=[row_spec, weight_spec, row_spec, pl.BlockSpec((1, d), lambda b, i: (0, 0)),
                  weight_spec, weight_spec, mem_spec, mem_spec],
        out_specs=row_spec,
        out_shape=jax.ShapeDtypeStruct((rows, d), F32),
        input_output_aliases={2: 0} if in_place else {},
        compiler_params=_params("arbitrary", "arbitrary"),
        name="xattn",
    )(a, wa, x, gain.reshape(1, d), wq, wo, mem_k, mem_v)


def _ffn_kernel(x_ref, g_ref, wgu_ref, wd_ref, o_ref, *, final_gain):
    x = x_ref[...]
    hidden = wd_ref.shape[0]
    h = _rms(x, g_ref[0:1, :]).astype(BF16)
    acc = x
    for c0 in range(0, hidden, FFN_CHUNK):
        c1 = min(c0 + FFN_CHUNK, hidden)
        gate = _dot(h, wgu_ref[:, c0:c1])
        up = _dot(h, wgu_ref[:, hidden + c0:hidden + c1])
        act = (gate * jax.nn.sigmoid(gate) * up).astype(BF16)
        acc = acc + _dot(act, wd_ref[c0:c1, :])
    if final_gain:
        acc = _rms(acc, g_ref[1:2, :])
    o_ref[...] = acc


def _ffn(x, gain, w_gate_up, w_down, final_gain=None):
    rows, d = x.shape
    tm = min(WIDE_ROW_TILE, rows)
    gains = jnp.stack([gain, gain if final_gain is None else final_gain])
    return pl.pallas_call(
        functools.partial(_ffn_kernel, final_gain=final_gain is not None),
        grid=(rows // tm,),
        in_specs=[pl.BlockSpec((tm, d), lambda i: (i, 0)),
                  pl.BlockSpec((2, d), lambda i: (0, 0)),
                  pl.BlockSpec(w_gate_up.shape, lambda i: (0, 0)),
                  pl.BlockSpec(w_down.shape, lambda i: (0, 0))],
        out_specs=pl.BlockSpec((tm, d), lambda i: (i, 0)),
        out_shape=jax.ShapeDtypeStruct((rows, d), F32),
        input_output_aliases={0: 0},
        compiler_params=_params("arbitrary"),
        name="ffn",
    )(x, gains, w_gate_up, w_down)


def _rw_proj_kernel(*refs, has_v_res):
    if has_v_res:
        (x_ref, gain_ref, shift0_ref, mu_ref, wr_ref, wk_ref, wv_ref, vec_ref,
         w1_ref, w2_ref, a1_ref, a2_ref, g1_ref, g2_ref, v1_ref, v2_ref, vfirst_ref,
         r_ref, lw_ref, k_ref, v_ref, kk_ref, a_ref, g_ref, shift_ref, prev_ref) = refs
    else:
        (x_ref, gain_ref, shift0_ref, mu_ref, wr_ref, wk_ref, wv_ref, vec_ref,
         w1_ref, w2_ref, a1_ref, a2_ref, g1_ref, g2_ref,
         r_ref, lw_ref, k_ref, v_ref, kk_ref, a_ref, g_ref, shift_ref, prev_ref) = refs
    i = pl.program_id(1)
    h = _rms(x_ref[...], gain_ref[...])
    tm = h.shape[0]

    @pl.when(i == 0)
    def _():
        prev_ref[...] = shift0_ref[0]

    row = lax.broadcasted_iota(jnp.int32, h.shape, 0)
    h_prev = jnp.where(row == 0, prev_ref[...], pltpu.roll(h, 1, 0))
    last = h[tm - 1:tm, :]
    prev_ref[...] = last
    shift_ref[0] = last

    xx = h_prev - h
    mixed = lambda j: (h + xx * mu_ref[j:j + 1, :]).astype(BF16)
    w0, a0, k_k, k_a, v0 = (vec_ref[j:j + 1, :] for j in range(5))

    r_ref[...] = _dot(mixed(0), wr_ref[...]).astype(r_ref.dtype)
    dec_in = _dot(mixed(1), w1_ref[...])
    k = _dot(mixed(2), wk_ref[...])
    dec = w0 + _dot(jnp.tanh(dec_in).astype(BF16), w2_ref[...])
    x_v = mixed(3)
    v = _dot(x_v, wv_ref[...])
    lw_ref[...] = -jnp.exp(-jax.nn.softplus(-dec) - 0.5)
    if has_v_res:
        mix = jax.nn.sigmoid(v0 + _dot(_dot(x_v, v1_ref[...]).astype(BF16), v2_ref[...]))
        v = v + (vfirst_ref[...] - v) * mix
    v_ref[...] = v
    a_in = _dot(mixed(4), a1_ref[...])
    g_in = _dot(mixed(5), g1_ref[...])
    a = jax.nn.sigmoid(a0 + _dot(a_in.astype(BF16), a2_ref[...]))
    kk_ref[...] = (k * k_k).astype(kk_ref.dtype)
    k_ref[...] = (k * (1.0 + (a - 1.0) * k_a)).astype(k_ref.dtype)
    a_ref[...] = a.astype(a_ref.dtype)
    g_ref[...] = _dot(jax.nn.sigmoid(g_in).astype(BF16), g2_ref[...]).astype(g_ref.dtype)


def _pad_cols(w, mult=LANES):
    pad = -w.shape[1] % mult
    return jnp.pad(w, ((0, 0), (0, pad))) if pad else w


def _pad_rows(w, mult=LANES):
    pad = -w.shape[0] % mult
    return jnp.pad(w, ((0, pad), (0, 0))) if pad else w


def _rw_proj(x, batch, gain, shift0, p, v_first):
    rows, d = x.shape
    seq = rows // batch
    tm = min(WIDE_ROW_TILE, seq)
    nt = seq // tm
    has_v_res = v_first is not None
    zeros = jnp.zeros((d,), F32)
    vec = jnp.stack([p["w0"], p["a0"], p["k_k"], p["k_a"], p["v0"] if has_v_res else zeros,
                     zeros, zeros, zeros])
    row_spec = pl.BlockSpec((tm, d), lambda b, i: (b * nt + i, 0))

    def full(arr):
        return pl.BlockSpec(arr.shape, lambda b, i: (0,) * arr.ndim, pipeline_mode=pl.Buffered(1))

    ins = [x, gain.reshape(1, d), shift0.reshape(batch, 1, d), p["mu"], p["w_r"], p["w_k"], p["w_v"], vec,
           p["w1"], p["w2"], p["a1"], p["a2"], p["g1"], p["g2"]]
    in_specs = [row_spec, full(ins[1]), pl.BlockSpec((1, 1, d), lambda b, i: (b, 0, 0))]
    in_specs += [full(a) for a in ins[3:]]
    if has_v_res:
        ins += [p["v1"], p["v2"], v_first]
        in_specs += [full(p["v1"]), full(p["v2"]), row_spec]
    tok = [jax.ShapeDtypeStruct((rows, d), dt) for dt in (BF16, F32, BF16, F32, BF16, BF16, BF16)]
    outs = pl.pallas_call(
        functools.partial(_rw_proj_kernel, has_v_res=has_v_res),
        grid=(batch, nt),
        in_specs=in_specs,
        out_specs=[row_spec] * 7 + [pl.BlockSpec((1, 1, d), lambda b, i: (b, 0, 0))],
        out_shape=tok + [jax.ShapeDtypeStruct((batch, 1, d), F32)],
        scratch_shapes=[pltpu.VMEM((1, d), F32)],
        compiler_params=_params("arbitrary", "arbitrary"),
        name="rw_proj",
    )(*ins)
    return outs[:7], outs[7].reshape(batch, d)


WKV_DECAY_PIECES = 3
WKV_GROUP_PIECES = 2
WKV_STATE_PASSES = 1
WKV_SOLVE_PASSES = 1


def _wkv_prepare(units, c_):
    n = units[0][0].shape[0]
    sum_l = lambda x: _mm_exact_lhs(c_["tri_incl"], x, WKV_DECAY_PIECES)
    sum_g = lambda x: _mm_exact_rhs(x, c_["gsum"], WKV_GROUP_PIECES)
    solve_mm = lambda x, y, nt=False: _mm(x, y, WKV_SOLVE_PASSES, nt=nt)

    cs = [sum_l(u[1]) for u in units]
    head_sums = [sum_g(jnp.concatenate([u[4] * u[4], u[0] * u[2] * u[7][0:1, :]], axis=0)) for u in units]
    yield
    pre = []
    for (r, lw, k, v, kk, a, g, vec), cs_u, hs in zip(units, cs, head_sums):
        cs_last = cs_u[n - 1:n, :]
        d_inv = jnp.exp(-cs_u)
        d_rest = jnp.exp(cs_last - cs_u)
        kkn = kk * lax.rsqrt(jnp.maximum(hs[:n], 1e-24))
        b = kkn * a
        pre.append(dict(a_t=-kkn * jnp.exp(cs_u - lw), r_t=r * jnp.exp(cs_u), b_t=b * d_inv, k_t=k * d_inv,
                        bk_rest=jnp.concatenate([b * d_rest, k * d_rest], axis=0), decay=jnp.exp(cs_last),
                        v=v, g=g, vec=vec, bonus=hs[n:] * v))
    yield

    by_head = lambda x: jnp.concatenate([jnp.where(m, x, 0.0) for m in c_["head_masks"]], axis=0)
    block_diag = lambda w: jnp.where(c_["same_block"], jnp.concatenate([w, w], axis=0), 0.0)
    grams = [solve_mm(jnp.concatenate([q["a_t"], q["r_t"]], axis=0),
                      jnp.concatenate([by_head(q["b_t"]), by_head(q["k_t"])], axis=0), nt=True)
             for q in pre]
    yield
    lows = [jnp.where(c_["strict_w"], gm[:n, :2 * n], 0.0) for gm in grams]
    m_aks = [jnp.where(c_["strict_w"], gm[:n, 2 * n:], 0.0) for gm in grams]
    for q, gm in zip(pre, grams):
        q["m_r"] = jnp.where(c_["incl_w2"], gm[n:], 0.0)
        q["v_heads"] = by_head(q["v"])
    from_v = [solve_mm(mk, q["v_heads"]) for mk, q in zip(m_aks, pre)]
    yield
    inv = [c_["eye_w"] + lo for lo in lows]
    if n > 2:
        pw = [solve_mm(lo, block_diag(lo)) for lo in lows]
        yield
        step = 2
        while 2 * step < n:
            both = [solve_mm(jnp.concatenate([t, p2], axis=0), block_diag(p2)) for t, p2 in zip(inv, pw)]
            inv = [t + bo[:n] for t, bo in zip(inv, both)]
            pw = [bo[n:] for bo in both]
            yield
            step *= 2
        inv = [t + solve_mm(t, block_diag(p2)) for t, p2 in zip(inv, pw)]
    for q, t, fv in zip(pre, inv, from_v):
        q["inv"], q["from_v"] = t, fv
    return pre


def _wkv_advance(pre, states, c_):
    n = pre[0]["v"].shape[0]
    inv_hd = 1.0 / RW_HEAD_DIM
    sum_g = lambda x: _mm_exact_rhs(x, c_["gsum"], WKV_GROUP_PIECES)
    solve_mm = lambda x, y: _mm(x, y, WKV_SOLVE_PASSES)
    by_head = lambda x: jnp.concatenate([jnp.where(m, x, 0.0) for m in c_["head_masks"]], axis=0)

    from_state = [_mm(jnp.concatenate([q["a_t"], q["r_t"]], axis=0), s, WKV_STATE_PASSES, nt=True)
                  for q, s in zip(pre, states)]
    yield
    sig_p = [solve_mm(q["inv"], by_head(fs[:n] + q["from_v"])) for q, fs in zip(pre, from_state)]
    yield
    y_p = [fs[n:] + solve_mm(q["m_r"], jnp.concatenate([by_head(sg), q["v_heads"]], axis=0))
           for q, fs, sg in zip(pre, from_state, sig_p)]
    upd = [_mm(jnp.concatenate([s, q["v"]], axis=0).T, q["bk_rest"], WKV_STATE_PASSES)
           for s, q in zip(sig_p, pre)]
    new_states = [s * q["decay"] + jnp.where(c_["same_head"], u, 0.0) for s, q, u in zip(states, pre, upd)]

    yield
    mean = [sum_g(y) * inv_hd for y in y_p]
    dev = [y - m for y, m in zip(y_p, mean)]
    yield
    var = [sum_g(dv * dv) * inv_hd for dv in dev]
    yield
    outs = []
    for q, dv, vr in zip(pre, dev, var):
        o = dv * lax.rsqrt(vr + RW_GN_EPS) * q["vec"][1:2, :] + q["vec"][2:3, :]
        outs.append((o + q["bonus"]) * q["g"])
    return outs, new_states


def _wkv_advance_chunks(pre_chunks, states, c_):
    outs = []
    for pre in pre_chunks:
        out, states = yield from _wkv_advance(pre, states, c_)
        outs.append(jnp.concatenate(out, axis=1))
    return outs, states


def _in_lock_step(*staged):
    results = [None] * len(staged)
    live = list(enumerate(staged))
    while live:
        still = []
        for i, gen in live:
            try:
                next(gen)
                still.append((i, gen))
            except StopIteration as done:
                results[i] = done.value
        live = still
    return results


def _wkv_kernel(*refs, has_state0, n):
    if has_state0:
        (r_ref, lw_ref, k_ref, v_ref, kk_ref, a_ref, g_ref, vec_ref, s0_ref,
         o_ref, sout_ref, state_ref) = refs
    else:
        (r_ref, lw_ref, k_ref, v_ref, kk_ref, a_ref, g_ref, vec_ref,
         o_ref, sout_ref, state_ref) = refs
    c = pl.program_id(2)
    hd = RW_HEAD_DIM

    @pl.when(c == 0)
    def _():
        if has_state0:
            state_ref[...] = s0_ref[0]
        else:
            state_ref[...] = jnp.zeros_like(state_ref)

    tr = lax.broadcasted_iota(jnp.int32, (n, n), 0)
    tc = lax.broadcasted_iota(jnp.int32, (n, n), 1)
    tr2 = lax.broadcasted_iota(jnp.int32, (n, 2 * n), 0)
    tc2 = lax.rem(lax.broadcasted_iota(jnp.int32, (n, 2 * n), 1), n)
    tr4 = lax.broadcasted_iota(jnp.int32, (n, 4 * n), 0)
    tc4 = lax.rem(lax.broadcasted_iota(jnp.int32, (n, 4 * n), 1), n)
    br = lax.broadcasted_iota(jnp.int32, (2 * n, 2 * n), 0) // n
    bc = lax.broadcasted_iota(jnp.int32, (2 * n, 2 * n), 1) // n
    lane = lax.broadcasted_iota(jnp.int32, (n, LANES), 1)
    sr = lax.broadcasted_iota(jnp.int32, (LANES, LANES), 0) // hd
    sc = lax.broadcasted_iota(jnp.int32, (LANES, LANES), 1) // hd
    consts = dict(
        gsum=jnp.concatenate([_group_ones(LANES, hd)] * WKV_GROUP_PIECES, axis=0),
        tri_incl=jnp.concatenate([jnp.where(tc <= tr, 1.0, 0.0).astype(BF16)] * WKV_DECAY_PIECES, axis=1),
        strict_w=tc2 < tr2,
        incl_w2=tc4 <= tr4,
        eye_w=jnp.where(tc2 == tr2, 1.0, 0.0).astype(F32),
        same_block=br == bc,
        head_masks=[lane < hd, lane >= hd],
        same_head=sr == sc,
    )
    n_pairs = state_ref.shape[0]
    n_chunks = r_ref.shape[0] // n

    def units(chunks):
        out = []
        for ci in chunks:
            rows = slice(ci * n, (ci + 1) * n)
            for p in range(n_pairs):
                sl = slice(p * LANES, (p + 1) * LANES)
                out.append(tuple(ref[rows, sl].astype(F32)
                                 for ref in (r_ref, lw_ref, k_ref, v_ref, kk_ref, a_ref, g_ref)) + (vec_ref[:, sl],))
        return out

    def per_chunk(pre):
        return [pre[i:i + n_pairs] for i in range(0, len(pre), n_pairs)]

    first = range((n_chunks + 1) // 2)
    second = range(len(first), n_chunks)
    states = [state_ref[p] for p in range(n_pairs)]
    (pre_first,) = _in_lock_step(_wkv_prepare(units(first), consts))
    if second:
        (outs, states), pre_second = _in_lock_step(
            _wkv_advance_chunks(per_chunk(pre_first), states, consts), _wkv_prepare(units(second), consts))
        ((outs_second, states),) = _in_lock_step(_wkv_advance_chunks(per_chunk(pre_second), states, consts))
        outs = outs + outs_second
    else:
        ((outs, states),) = _in_lock_step(_wkv_advance_chunks(per_chunk(pre_first), states, consts))
    o_ref[...] = jnp.concatenate(outs, axis=0).astype(o_ref.dtype)
    new_states = jnp.stack(states)
    state_ref[...] = new_states
    sout_ref[0] = new_states


def _wkv(tok, batch, vec, state0):
    rows, d = tok[0].shape
    seq = rows // batch
    n = min(WKV_CHUNK, seq)
    span = n * min(WKV_CHUNKS_PER_STEP, seq // n)
    nc = seq // span
    pairs = d // LANES
    gp = WKV_PAIRS
    width = gp * LANES
    has_state0 = state0 is not None
    tok_spec = pl.BlockSpec((span, width), lambda b, p, c: (b * nc + c, p))
    state_spec = pl.BlockSpec((1, gp, LANES, LANES), lambda b, p, c: (b, p, 0, 0))
    ins = list(tok) + [vec]
    in_specs = [tok_spec] * 7 + [pl.BlockSpec((8, width), lambda b, p, c: (0, p))]
    if has_state0:
        ins.append(state0)
        in_specs.append(state_spec)
    return pl.pallas_call(
        functools.partial(_wkv_kernel, has_state0=has_state0, n=n),
        grid=(batch, pairs // gp, nc),
        in_specs=in_specs,
        out_specs=[tok_spec, state_spec],
        out_shape=[jax.ShapeDtypeStruct((rows, d), BF16),
                   jax.ShapeDtypeStruct((batch, pairs, LANES, LANES), F32)],
        scratch_shapes=[pltpu.VMEM((gp, LANES, LANES), F32)],
        compiler_params=_params("arbitrary", "arbitrary", "arbitrary"),
        name="wkv",
    )(*ins)


def _state_to_blockdiag(state):
    b, h, n, _ = state.shape
    s = state.reshape(b, h // 2, 2, n, n)
    z = jnp.zeros_like(s[:, :, 0])
    top = jnp.concatenate([s[:, :, 0], z], axis=-1)
    bot = jnp.concatenate([z, s[:, :, 1]], axis=-1)
    return jnp.concatenate([top, bot], axis=-2)


def _state_from_blockdiag(bd):
    b, p, n2, _ = bd.shape
    n = n2 // 2
    return jnp.stack([bd[:, :, :n, :n], bd[:, :, n:, n:]], axis=2).reshape(b, 2 * p, n, n)


def _trunk(x, batch, mem_k, mem_v, past_k, past_v, rw_state0, rw_shift0, w):
    depth = len(w["ffn_gu"])
    d = x.shape[-1]
    sb_k, sb_v, rw_states, rw_shifts = None, None, [], []
    n_sb = (depth + 1) // 2
    v_first = None
    for layer in range(depth):
        j = layer // 2
        if layer % 2 == 0:
            sb_k, sb_v, qkv = _sb_qkv(x, batch, w["norm_mix"][layer], w["sb_qkv"][j], j, n_sb, sb_k, sb_v)
            if past_k is None:
                o = _sb_attention_prompt(qkv, batch)
            else:
                o = _sb_attention_sample(qkv, batch, past_k, past_v, j)
            w_o = w["sb_o"][j]
        else:
            if rw_state0 is None:
                state0 = None
                shift0 = jnp.zeros((batch, d), F32)
            else:
                state0 = _state_to_blockdiag(rw_state0[j])
                shift0 = rw_shift0[j]
            tok, shift = _rw_proj(x, batch, w["norm_mix"][layer], shift0, w["rw"][j], v_first)
            if v_first is None:
                v_first = tok[3]
            o, state = _wkv(tok, batch, w["rw"][j]["out_vec"], state0)
            rw_states.append(_state_from_blockdiag(state))
            rw_shifts.append(shift)
            w_o = w["rw"][j]["w_o"]
        x = _xattn(o, w_o, x, w["norm_xattn"][layer], w["xa_q"][layer], w["xa_o"][layer],
                   mem_k[layer], mem_v[layer], in_place=layer > 0)
        final = w["norm_final"] if layer == depth - 1 else None
        x = _ffn(x, w["norm_ffn"][layer], w["ffn_gu"][layer], w["ffn_down"][layer], final)
    return x, sb_k, sb_v, jnp.stack(rw_states), jnp.stack(rw_shifts)


def kernel(x_prompt, x_sample, mem_prompt, cache_sb_k, cache_sb_v, state_rwkv_wkv, state_rwkv_shift,
           cache_mem_k, cache_mem_v, norm_mix, norm_xattn, norm_ffn, norm_mem, norm_final,
           sb_w_qkv, sb_w_o, rw_mu, rw_w_r, rw_w_k, rw_w_v, rw_w_o, rw_w0, rw_w1, rw_w2,
           rw_a0, rw_a1, rw_a2, rw_v0, rw_v1, rw_v2, rw_g1, rw_g2, rw_k_k, rw_k_a, rw_r_k,
           rw_lnx_w, rw_lnx_b, xa_w_q, xa_w_kv, xa_w_o, ffn_w_gate_up, ffn_w_down):
    batch, seq, d = x_prompt.shape
    dec_batch, dec_seq, _ = x_sample.shape
    depth = norm_mix.shape[0]
    n_rw = rw_w_r.shape[0]
    n_sb = sb_w_qkv.shape[0]
    n_mem = mem_prompt.shape[1]
    bf = lambda t: t.astype(BF16)
    per_layer = lambda t: [bf(t[i]) for i in range(t.shape[0])]

    rw = []
    for j in range(n_rw):
        pj = dict(mu=rw_mu[j], w_r=bf(rw_w_r[j]), w_k=bf(rw_w_k[j]), w_v=bf(rw_w_v[j]), w_o=bf(rw_w_o[j]),
                  w0=rw_w0[j], a0=rw_a0[j], k_k=rw_k_k[j], k_a=rw_k_a[j],
                  w1=bf(_pad_cols(rw_w1[j])), w2=bf(_pad_rows(rw_w2[j])),
                  a1=bf(_pad_cols(rw_a1[j])), a2=bf(_pad_rows(rw_a2[j])),
                  g1=bf(_pad_cols(rw_g1[j])), g2=bf(_pad_rows(rw_g2[j])))
        if j > 0:
            pj.update(v0=rw_v0[j - 1], v1=bf(_pad_cols(rw_v1[j - 1])), v2=bf(_pad_rows(rw_v2[j - 1])))
        zeros = jnp.zeros((d,), F32)
        pj["out_vec"] = jnp.stack([rw_r_k[j].reshape(d), rw_lnx_w[j], rw_lnx_b[j]] + [zeros] * 5)
        rw.append(pj)
    w = dict(norm_mix=norm_mix, norm_xattn=norm_xattn, norm_ffn=norm_ffn, norm_final=norm_final,
             sb_qkv=per_layer(sb_w_qkv), sb_o=per_layer(sb_w_o), rw=rw,
             xa_q=per_layer(xa_w_q), xa_o=per_layer(xa_w_o),
             ffn_gu=per_layer(ffn_w_gate_up), ffn_down=per_layer(ffn_w_down))

    mem_k, mem_v, mem_k_bf, mem_v_bf = _mem_kv(mem_prompt, norm_mem, bf(xa_w_kv))
    y_p, p_sb_k, p_sb_v, p_wkv, p_shift = _trunk(
        x_prompt.reshape(batch * seq, d), batch, mem_k_bf, mem_v_bf, None, None, None, None, w)

    y_s, s_sb_k, s_sb_v, s_wkv, s_shift = _trunk(
        x_sample.reshape(dec_batch * dec_seq, d), dec_batch,
        bf(cache_mem_k).reshape(depth, dec_batch, n_mem, d), bf(cache_mem_v).reshape(depth, dec_batch, n_mem, d),
        cache_sb_k.reshape(n_sb, dec_batch, -1, d), cache_sb_v.reshape(n_sb, dec_batch, -1, d),
        state_rwkv_wkv, state_rwkv_shift, w)

    hd = d // SB_HEADS
    return (y_p.reshape(batch, seq, d), y_s.reshape(dec_batch, dec_seq, d),
            p_sb_k.reshape(n_sb, batch, seq, SB_HEADS, hd), p_sb_v.reshape(n_sb, batch, seq, SB_HEADS, hd),
            p_wkv, p_shift,
            mem_k, mem_v,
            s_sb_k.reshape(n_sb, dec_batch, dec_seq, SB_HEADS, hd),
            s_sb_v.reshape(n_sb, dec_batch, dec_seq, SB_HEADS, hd),
            s_wkv, s_shift)
```

```python
import functools

import jax
import jax.numpy as jnp
from jax import lax
from jax.experimental import pallas as pl
from jax.experimental.pallas import tpu as pltpu

F32 = jnp.float32
BF16 = jnp.bfloat16

NORM_EPS = 1e-6
NEG_LOG2_E = -1.4426950408889634
RW_GN_EPS = 64e-5
SB_HEADS = 16
XA_HEADS = 4
RW_HEAD_DIM = 64
LANES = 128
ROW_TILE = 256
WIDE_ROW_TILE = 512
SB_BLOCK = 256
SB_PAIRS = 8
SB_WAVES = 4
SB_DECODE_WAVES = 2
SB_UNROLL = 2
SB_PAST_SPAN = 2048
WKV_CHUNK = 64
WKV_CHUNKS_PER_STEP = 4
WKV_PAIRS = 8
FFN_CHUNK = 256
VMEM_LIMIT_BYTES = 56 * 1024 * 1024


def _params(*sem):
    return pltpu.CompilerParams(dimension_semantics=sem, vmem_limit_bytes=VMEM_LIMIT_BYTES)


def _dot(a, b):
    return jnp.dot(a, b, preferred_element_type=F32)


def _dot_nt(a, b):
    return lax.dot_general(a, b, (((1,), (1,)), ((), ())), preferred_element_type=F32)


def _split(x, n):
    parts = []
    for i in range(n):
        p = x.astype(BF16)
        parts.append(p)
        if i + 1 < n:
            x = x - p.astype(F32)
    return parts


def _mm(a, b, passes, nt=False):
    dot = _dot_nt if nt else _dot
    if passes == 1:
        return dot(a.astype(BF16), b.astype(BF16))
    a_hi, a_lo = _split(a, 2)
    b_hi, b_lo = _split(b, 2)
    return dot(a_hi, b_hi) + (dot(a_hi, b_lo) + dot(a_lo, b_hi))


def _mm_exact_rhs(a, b_stacked, n):
    return _dot(jnp.concatenate(_split(a, n), axis=1), b_stacked)


def _mm_exact_lhs(a_repeated, b, n):
    return _dot(a_repeated, jnp.concatenate(_split(b, n), axis=0))


def _rms(x, gain):
    ms = jnp.mean(x * x, axis=-1, keepdims=True)
    return x * lax.rsqrt(ms + NORM_EPS) * gain


def _group_ones(n, group):
    r = lax.broadcasted_iota(jnp.int32, (n, n), 0) // group
    c = lax.broadcasted_iota(jnp.int32, (n, n), 1) // group
    return jnp.where(r == c, 1.0, 0.0).astype(BF16)


def _mem_kv_kernel(x_ref, g_ref, w_ref, k_ref, v_ref, kb_ref, vb_ref):
    tm, d = x_ref.shape
    heads, dh = k_ref.shape[-2:]
    h = _rms(x_ref[...], g_ref[0]).astype(BF16)
    kv = _dot(h, w_ref[0])
    k, v = kv[:, :d], kv[:, d:]
    k_ref[0, 0] = k.reshape(tm, heads, dh)
    v_ref[0, 0] = v.reshape(tm, heads, dh)
    kb_ref[0, 0] = k.astype(BF16)
    vb_ref[0, 0] = v.astype(BF16)


def _mem_kv(mem, gains, w_kv):
    depth = w_kv.shape[0]
    batch, n_mem, d = mem.shape
    tm = min(ROW_TILE, n_mem)
    nt = n_mem // tm
    dh = d // XA_HEADS
    assert n_mem % tm == 0 and dh % LANES == 0
    out = jax.ShapeDtypeStruct((depth, batch, n_mem, XA_HEADS, dh), F32)
    out_b = jax.ShapeDtypeStruct((depth, batch, n_mem, d), BF16)
    spec = pl.BlockSpec((1, 1, tm, XA_HEADS, dh), lambda l, i: (l, i // nt, i % nt, 0, 0))
    spec_b = pl.BlockSpec((1, 1, tm, d), lambda l, i: (l, i // nt, i % nt, 0))
    return pl.pallas_call(
        _mem_kv_kernel,
        grid=(depth, batch * nt),
        in_specs=[pl.BlockSpec((tm, d), lambda l, i: (i, 0)),
                  pl.BlockSpec((1, 1, d), lambda l, i: (l, 0, 0)),
                  pl.BlockSpec((1, d, 2 * d), lambda l, i: (l, 0, 0))],
        out_specs=[spec, spec, spec_b, spec_b],
        out_shape=[out, out, out_b, out_b],
        compiler_params=_params("arbitrary", "arbitrary"),
        name="mem_kv",
    )(mem.reshape(batch * n_mem, d), gains.reshape(depth, 1, d), w_kv)


def _split_heads(x, hd):
    tm, d = x.shape
    blocks = []
    for p in range(d // LANES):
        slab = x[:, p * LANES:(p + 1) * LANES]
        blocks += [slab, pltpu.roll(slab, LANES - hd, 1)]
    return jnp.concatenate(blocks, axis=1).reshape(tm, d // hd, LANES)[:, :, :hd]


def _sb_qkv_kernel(x_ref, g_ref, w_ref, *rest, q_scale, slot):
    k_ref, v_ref, qkv_ref = rest[-3:]
    d = x_ref.shape[-1]
    hd = k_ref.shape[-1]
    h = _rms(x_ref[...], g_ref[...]).astype(BF16)
    qkv = _dot(h, w_ref[...])
    qkv_ref[:, :d] = (qkv[:, :d] * q_scale).astype(BF16)
    qkv_ref[:, d:] = qkv[:, d:].astype(BF16)
    k3 = _split_heads(qkv[:, d:2 * d], hd)
    v3 = _split_heads(qkv[:, 2 * d:], hd)
    own = slot if k_ref.shape[0] > 1 else 0
    for s in range(k_ref.shape[0]):
        if s == own:
            k_ref[s, 0] = k3
            v_ref[s, 0] = v3
        else:
            k_ref[s, 0] = jnp.zeros(k3.shape, k_ref.dtype)
            v_ref[s, 0] = jnp.zeros(v3.shape, v_ref.dtype)


def _sb_qkv(x, batch, gain, w, slot, n_slots, k_all=None, v_all=None):
    rows, d = x.shape
    seq = rows // batch
    tm = min(ROW_TILE, seq)
    nt = seq // tm
    hd = d // SB_HEADS
    assert 2 * hd == LANES
    q_scale = float(hd ** -0.5)
    ins = [x, gain.reshape(1, d), w]
    in_specs = [pl.BlockSpec((tm, d), lambda i: (i, 0)),
                pl.BlockSpec((1, d), lambda i: (0, 0)),
                pl.BlockSpec(w.shape, lambda i: (0, 0))]
    if k_all is None:
        aliases = {}
        slot_spec = pl.BlockSpec((n_slots, 1, tm, SB_HEADS, hd), lambda i: (0, i // nt, i % nt, 0, 0))
    else:
        ins += [k_all, v_all]
        in_specs += [pl.BlockSpec(memory_space=pl.ANY)] * 2
        aliases = {3: 0, 4: 1}
        slot_spec = pl.BlockSpec((1, 1, tm, SB_HEADS, hd), lambda i: (slot, i // nt, i % nt, 0, 0))
    kv_shape = jax.ShapeDtypeStruct((n_slots, batch, seq, SB_HEADS, hd), F32)
    return pl.pallas_call(
        functools.partial(_sb_qkv_kernel, q_scale=q_scale, slot=slot),
        grid=(rows // tm,),
        in_specs=in_specs,
        out_specs=[slot_spec, slot_spec, pl.BlockSpec((tm, 3 * d), lambda i: (i, 0))],
        out_shape=[kv_shape, kv_shape, jax.ShapeDtypeStruct((rows, 3 * d), BF16)],
        input_output_aliases=aliases,
        compiler_params=_params("arbitrary"),
        name="sb_qkv",
    )(*ins)


def _sb_step(q2s, kv_blocks, carry, acc, neg_lower2, mask, n_waves):
    m = q2s[0].shape[0]
    groups = len(q2s)
    per = max(groups // n_waves, 1)
    waves = [range(c, min(c + per, groups)) for c in range(0, groups, per)]
    rows = [slice(w_[0] * m, (w_[-1] + 1) * m) for w_ in waves]
    units = [(b, c) for b in range(len(kv_blocks)) for c in range(len(waves))]

    def scores(b, c):
        return jnp.concatenate([_dot_nt(q2s[g], kv_blocks[b][0][g]) for g in waves[c]], axis=0)

    def softplus_pieces(c, z):
        sp = jnp.maximum(z, 0.0) + jnp.log(1.0 + jnp.exp2(jnp.abs(z) * NEG_LOG2_E))
        if mask is not None:
            sp = jnp.where(mask[rows[c]], sp, 0.0)
        return jnp.concatenate(_split(sp, 2), axis=1)

    def weights(c, z, from_s, later):
        w = jnp.exp(z + from_s + jnp.concatenate([later] * (z.shape[1] // LANES), axis=1))
        if mask is not None:
            w = jnp.where(mask[rows[c]], w, 0.0)
        return w.astype(BF16)

    def values(b, c, w):
        return jnp.concatenate([_dot(w[i * m:(i + 1) * m], kv_blocks[b][1][g])
                                for i, g in enumerate(waves[c])], axis=0)

    n = len(units)
    z, pieces, from_s, w = ([None] * n for _ in range(4))
    later = [carry[r] for r in rows]
    pv = [None] * len(waves)
    for tick in range(n + 4):
        for u, (b, c) in enumerate(units):
            stage = tick - u
            if stage == 0:
                z[u] = scores(b, c)
            elif stage == 1:
                pieces[u] = softplus_pieces(c, z[u])
            elif stage == 2:
                from_s[u] = _dot(pieces[u], neg_lower2)
            elif stage == 3:
                w[u] = weights(c, z[u], from_s[u], later[c])
                later[c] = later[c] + jnp.broadcast_to(from_s[u][:, 0:1], later[c].shape)
            elif stage == 4:
                out = values(b, c, w[u])
                pv[c] = out if pv[c] is None else pv[c] + out
    return jnp.concatenate(later, axis=0), acc + jnp.concatenate(pv, axis=0)


def _sb_consts(bk):
    r = lax.broadcasted_iota(jnp.int32, (2, bk, bk), 1).reshape(2 * bk, bk)
    c = lax.broadcasted_iota(jnp.int32, (2 * bk, bk), 1)
    return jnp.where(r >= c, -1.0, 0.0).astype(BF16)


def _sb_stack_heads(q_ref, q2_ref):
    tq = q_ref.shape[0]
    lane = lax.broadcasted_iota(jnp.int32, (tq, LANES), 1)
    for g in range(q2_ref.shape[0]):
        q = q_ref[:, g * LANES:(g + 1) * LANES]
        zero = jnp.zeros_like(q)
        q2_ref[g] = jnp.concatenate([jnp.where(lane < LANES // 2, q, zero),
                                     jnp.where(lane >= LANES // 2, q, zero)], axis=0)


def _sb_causal_mask(groups, tq, bk, t0=0):
    t = lax.broadcasted_iota(jnp.int32, (2 * groups, tq, bk), 1).reshape(2 * groups * tq, bk)
    s = lax.broadcasted_iota(jnp.int32, (2 * groups * tq, bk), 1)
    return s < t + t0


def _sb_write(o_ref, acc_ref):
    tq = o_ref.shape[0]
    lane = lax.broadcasted_iota(jnp.int32, (tq, LANES), 1)
    for g in range(acc_ref.shape[0] // (2 * tq)):
        o_ref[:, g * LANES:(g + 1) * LANES] = jnp.where(
            lane < LANES // 2, acc_ref[2 * g * tq:(2 * g + 1) * tq, :],
            acc_ref[(2 * g + 1) * tq:(2 * g + 2) * tq, :]).astype(o_ref.dtype)


def _sb_prompt_kernel(q_ref, k_ref, v_ref, o_ref, q2_ref, carry_ref, acc_ref):
    blk = q_ref.shape[0]
    groups = q2_ref.shape[0]
    i = pl.program_id(2)
    neg_lower2 = _sb_consts(blk)
    _sb_stack_heads(q_ref, q2_ref)
    lanes = [slice(g * LANES, (g + 1) * LANES) for g in range(groups)]

    def block(j):
        row0 = pl.multiple_of(j * blk, blk)
        return ([k_ref[pl.ds(row0, blk), sl] for sl in lanes], [v_ref[pl.ds(row0, blk), sl] for sl in lanes])

    def step(blocks, carry, acc, mask):
        return _sb_step([q2_ref[g] for g in range(groups)], blocks, carry, acc, neg_lower2, mask, SB_WAVES)

    half = blk // 2
    d0 = pl.multiple_of(i * blk, blk)
    zero = jnp.zeros((groups * blk, LANES), F32)
    halves = []
    for t0, keys, consts in ((half, blk, neg_lower2), (0, half, _sb_consts(half))):
        q2_half = [jnp.concatenate([q2_ref[g, t0:t0 + half, :], q2_ref[g, blk + t0:blk + t0 + half, :]], axis=0)
                   for g in range(groups)]
        own = ([k_ref[pl.ds(d0, keys), sl] for sl in lanes], [v_ref[pl.ds(d0, keys), sl] for sl in lanes])
        halves.append(_sb_step(q2_half, [own], zero, zero, consts, _sb_causal_mask(groups, half, keys, t0),
                               SB_WAVES))
    (carry_late, acc_late), (carry_early, acc_early) = halves
    for g in range(groups):
        for h in range(2):
            src = slice((2 * g + h) * half, (2 * g + h + 1) * half)
            dst = (2 * g + h) * blk
            for ref, early, late in ((carry_ref, carry_early, carry_late), (acc_ref, acc_early, acc_late)):
                ref[dst:dst + half, :] = early[src]
                ref[dst + half:dst + blk, :] = late[src]

    odd = i % SB_UNROLL

    def single(it, c):
        carry_ref[...], acc_ref[...] = step([block(i - 1 - it)], carry_ref[...], acc_ref[...], None)
        return c

    def multi(it, c):
        top = i - 1 - odd - it * SB_UNROLL
        carry_ref[...], acc_ref[...] = step([block(top - u) for u in range(SB_UNROLL)],
                                            carry_ref[...], acc_ref[...], None)
        return c

    lax.fori_loop(0, odd, single, 0)
    lax.fori_loop(0, i // SB_UNROLL, multi, 0)
    _sb_write(o_ref, acc_ref)


def _sb_attention_prompt(qkv, batch):
    rows, d3 = qkv.shape
    d = d3 // 3
    seq = rows // batch
    blk = min(SB_BLOCK, seq)
    nq = seq // blk
    width = SB_PAIRS * LANES
    ng = d // width
    return pl.pallas_call(
        _sb_prompt_kernel,
        grid=(batch, ng, nq),
        in_specs=[pl.BlockSpec((blk, width), lambda b, p, i: (b * nq + i, p)),
                  pl.BlockSpec((seq, width), lambda b, p, i: (b, ng + p)),
                  pl.BlockSpec((seq, width), lambda b, p, i: (b, 2 * ng + p))],
        out_specs=pl.BlockSpec((blk, width), lambda b, p, i: (b * nq + i, p)),
        out_shape=jax.ShapeDtypeStruct((rows, d), BF16),
        scratch_shapes=[pltpu.VMEM((SB_PAIRS, 2 * blk, LANES), BF16),
                        pltpu.VMEM((SB_PAIRS * 2 * blk, LANES), F32),
                        pltpu.VMEM((SB_PAIRS * 2 * blk, LANES), F32)],
        compiler_params=_params("arbitrary", "arbitrary", "arbitrary"),
        name="sb_attention_prompt",
    )(qkv, qkv, qkv)


def _sb_sample_kernel(q_ref, kn_ref, vn_ref, kp_ref, vp_ref, o_ref, q2_ref, carry_ref, acc_ref, *, blk):
    tq = q_ref.shape[0]
    groups = q2_ref.shape[0]
    c = pl.program_id(1)
    neg_lower2 = _sb_consts(blk)

    lanes = [slice(g * LANES, (g + 1) * LANES) for g in range(groups)]

    @pl.when(c == 0)
    def _():
        _sb_stack_heads(q_ref, q2_ref)
        zero = jnp.zeros(carry_ref.shape, F32)
        carry_ref[...], acc_ref[...] = _sb_step(
            [q2_ref[g] for g in range(groups)],
            [([kn_ref[0, :, sl] for sl in lanes], [vn_ref[0, :, sl] for sl in lanes])],
            zero, zero, neg_lower2, _sb_causal_mask(groups, tq, blk), SB_DECODE_WAVES)

    blocks = []
    for sb in reversed(range(kp_ref.shape[2] // blk)):
        rows = slice(sb * blk, (sb + 1) * blk)
        blocks.append(([kp_ref[0, 0, rows, sl].astype(BF16) for sl in lanes],
                       [vp_ref[0, 0, rows, sl].astype(BF16) for sl in lanes]))
    carry_ref[...], acc_ref[...] = _sb_step([q2_ref[g] for g in range(groups)], blocks,
                                            carry_ref[...], acc_ref[...], neg_lower2, None, SB_DECODE_WAVES)

    @pl.when(c == pl.num_programs(1) - 1)
    def _():
        _sb_write(o_ref, acc_ref)


def _sb_attention_sample(qkv, batch, past_k, past_v, layer):
    rows, d3 = qkv.shape
    d = d3 // 3
    seq = rows // batch
    blk = SB_BLOCK
    pairs = d // LANES
    past_len = past_k.shape[2]
    span = min(SB_PAST_SPAN, past_len)
    steps = past_len // span
    assert seq <= blk and past_len % span == 0 and span % blk == 0
    new = qkv[:, d:].reshape(batch, seq, 2 * d)
    new = jnp.pad(new, ((0, 0), (0, blk - seq), (0, 0)))
    return pl.pallas_call(
        functools.partial(_sb_sample_kernel, blk=blk),
        grid=(batch, steps),
        in_specs=[pl.BlockSpec((seq, d), lambda b, c: (b, 0)),
                  pl.BlockSpec((1, blk, d), lambda b, c: (b, 0, 0)),
                  pl.BlockSpec((1, blk, d), lambda b, c: (b, 0, 1)),
                  pl.BlockSpec((1, 1, span, d), lambda b, c: (layer, b, steps - 1 - c, 0)),
                  pl.BlockSpec((1, 1, span, d), lambda b, c: (layer, b, steps - 1 - c, 0))],
        out_specs=pl.BlockSpec((seq, d), lambda b, c: (b, 0)),
        out_shape=jax.ShapeDtypeStruct((rows, d), BF16),
        scratch_shapes=[pltpu.VMEM((pairs, 2 * seq, LANES), BF16),
                        pltpu.VMEM((pairs * 2 * seq, LANES), F32),
                        pltpu.VMEM((pairs * 2 * seq, LANES), F32)],
        compiler_params=_params("arbitrary", "arbitrary"),
        name="sb_attention_sample",
    )(qkv, new, new, past_k, past_v)


def _xattn_kernel(a_ref, wa_ref, x_ref, g_ref, wq_ref, wo_ref, mk_ref, mv_ref, o_ref, *, heads):
    x = x_ref[...] + _dot(a_ref[...], wa_ref[...])
    d = x.shape[-1]
    dh = d // heads
    h = _rms(x, g_ref[...]).astype(BF16)
    q = _dot(h, wq_ref[0]).astype(BF16)
    nb = mk_ref.shape[1]
    seq = x.shape[0] // nb
    per_seq = []
    for b in range(nb):
        rows = slice(b * seq, (b + 1) * seq)
        outs = []
        for hd in range(heads):
            sl = slice(hd * dh, (hd + 1) * dh)
            s = _dot_nt(q[rows, sl], mk_ref[0, b, :, sl]) * (dh ** -0.5)
            e = jnp.exp(s - jnp.max(s, axis=-1, keepdims=True))
            p = e / jnp.sum(e, axis=-1, keepdims=True)
            outs.append(_dot(p.astype(BF16), mv_ref[0, b, :, sl]).astype(BF16))
        per_seq.append(jnp.concatenate(outs, axis=-1))
    o = per_seq[0] if nb == 1 else jnp.concatenate(per_seq, axis=0)
    o_ref[...] = x + _dot(o, wo_ref[0])


def _xattn(a, wa, x, gain, wq, wo, mem_k, mem_v, layer, in_place):
    rows, d = x.shape
    _, batch, n_mem, _ = mem_k.shape
    seq = rows // batch
    nb = min(batch, max(1, WIDE_ROW_TILE // seq))
    assert batch % nb == 0
    tm = nb * seq if nb > 1 else min(WIDE_ROW_TILE, seq)
    nt = 1 if nb > 1 else seq // tm
    row_spec = pl.BlockSpec((tm, d), lambda b, i: (b * nt + i, 0))
    weight_spec = pl.BlockSpec((1, d, d), lambda b, i: (layer, 0, 0))
    mem_spec = pl.BlockSpec((1, nb, n_mem, d), lambda b, i: (layer, b, 0, 0))
    return pl.pallas_call(
        functools.partial(_xattn_kernel, heads=XA_HEADS),
        grid=(batch // nb, nt),
        in_specs=[row_spec, pl.BlockSpec((d, d), lambda b, i: (0, 0)), row_spec,
                  pl.BlockSpec((1, d), lambda b, i: (0, 0)), weight_spec, weight_spec, mem_spec, mem_spec],
        out_specs=row_spec,
        out_shape=jax.ShapeDtypeStruct((rows, d), F32),
        input_output_aliases={2: 0} if in_place else {},
        compiler_params=_params("arbitrary", "arbitrary"),
        name="xattn",
    )(a, wa, x, gain.reshape(1, d), wq, wo, mem_k, mem_v)


def _ffn_kernel(x_ref, g_ref, wgu_ref, wd_ref, o_ref, *, final_gain):
    x = x_ref[...]
    hidden = wd_ref.shape[1]
    h = _rms(x, g_ref[0:1, :]).astype(BF16)
    acc = x
    for c0 in range(0, hidden, FFN_CHUNK):
        c1 = min(c0 + FFN_CHUNK, hidden)
        gate = _dot(h, wgu_ref[0, :, c0:c1])
        up = _dot(h, wgu_ref[0, :, hidden + c0:hidden + c1])
        act = (gate * jax.nn.sigmoid(gate) * up).astype(BF16)
        acc = acc + _dot(act, wd_ref[0, c0:c1, :])
    if final_gain:
        acc = _rms(acc, g_ref[1:2, :])
    o_ref[...] = acc


def _ffn(x, gain, w_gate_up, w_down, layer, final_gain=None):
    rows, d = x.shape
    tm = min(WIDE_ROW_TILE, rows)
    gains = jnp.stack([gain, gain if final_gain is None else final_gain])
    return pl.pallas_call(
        functools.partial(_ffn_kernel, final_gain=final_gain is not None),
        grid=(rows // tm,),
        in_specs=[pl.BlockSpec((tm, d), lambda i: (i, 0)),
                  pl.BlockSpec((2, d), lambda i: (0, 0)),
                  pl.BlockSpec((1,) + w_gate_up.shape[1:], lambda i: (layer, 0, 0)),
                  pl.BlockSpec((1,) + w_down.shape[1:], lambda i: (layer, 0, 0))],
        out_specs=pl.BlockSpec((tm, d), lambda i: (i, 0)),
        out_shape=jax.ShapeDtypeStruct((rows, d), F32),
        input_output_aliases={0: 0},
        compiler_params=_params("arbitrary"),
        name="ffn",
    )(x, gains, w_gate_up, w_down)


def _rw_proj_kernel(*refs, has_v_res):
    if has_v_res:
        (x_ref, gain_ref, shift0_ref, mu_ref, wr_ref, wk_ref, wv_ref, vec_ref,
         w1_ref, w2_ref, a1_ref, a2_ref, g1_ref, g2_ref, v1_ref, v2_ref, vfirst_ref,
         r_ref, lw_ref, k_ref, v_ref, kk_ref, a_ref, g_ref, shift_ref, prev_ref) = refs
    else:
        (x_ref, gain_ref, shift0_ref, mu_ref, wr_ref, wk_ref, wv_ref, vec_ref,
         w1_ref, w2_ref, a1_ref, a2_ref, g1_ref, g2_ref,
         r_ref, lw_ref, k_ref, v_ref, kk_ref, a_ref, g_ref, shift_ref, prev_ref) = refs
    i = pl.program_id(1)
    h = _rms(x_ref[...], gain_ref[...])
    tm = h.shape[0]

    @pl.when(i == 0)
    def _():
        prev_ref[...] = shift0_ref[0]

    row = lax.broadcasted_iota(jnp.int32, h.shape, 0)
    h_prev = jnp.where(row == 0, prev_ref[...], pltpu.roll(h, 1, 0))
    last = h[tm - 1:tm, :]
    prev_ref[...] = last
    shift_ref[0] = last

    xx = h_prev - h
    mixed = lambda j: (h + xx * mu_ref[j:j + 1, :]).astype(BF16)
    w0, a0, k_k, k_a, v0 = (vec_ref[j:j + 1, :] for j in range(5))

    r_ref[...] = _dot(mixed(0), wr_ref[...]).astype(r_ref.dtype)
    dec_in = _dot(mixed(1), w1_ref[...])
    k = _dot(mixed(2), wk_ref[...])
    dec = w0 + _dot(jnp.tanh(dec_in).astype(BF16), w2_ref[...])
    x_v = mixed(3)
    v = _dot(x_v, wv_ref[...])
    lw_ref[...] = -jnp.exp(-jax.nn.softplus(-dec) - 0.5)
    if has_v_res:
        mix = jax.nn.sigmoid(v0 + _dot(_dot(x_v, v1_ref[...]).astype(BF16), v2_ref[...]))
        v = v + (vfirst_ref[...] - v) * mix
    v_ref[...] = v
    a_in = _dot(mixed(4), a1_ref[...])
    g_in = _dot(mixed(5), g1_ref[...])
    a = jax.nn.sigmoid(a0 + _dot(a_in.astype(BF16), a2_ref[...]))
    kk_ref[...] = (k * k_k).astype(kk_ref.dtype)
    k_ref[...] = (k * (1.0 + (a - 1.0) * k_a)).astype(k_ref.dtype)
    a_ref[...] = a.astype(a_ref.dtype)
    g_ref[...] = _dot(jax.nn.sigmoid(g_in).astype(BF16), g2_ref[...]).astype(g_ref.dtype)


def _pad_cols(w, mult=LANES):
    pad = -w.shape[1] % mult
    return jnp.pad(w, ((0, 0), (0, pad))) if pad else w


def _pad_rows(w, mult=LANES):
    pad = -w.shape[0] % mult
    return jnp.pad(w, ((0, pad), (0, 0))) if pad else w


def _rw_proj(x, batch, gain, shift0, p, v_first):
    rows, d = x.shape
    seq = rows // batch
    tm = min(WIDE_ROW_TILE, seq)
    nt = seq // tm
    has_v_res = v_first is not None
    zeros = jnp.zeros((d,), F32)
    vec = jnp.stack([p["w0"], p["a0"], p["k_k"], p["k_a"], p["v0"] if has_v_res else zeros,
                     zeros, zeros, zeros])
    row_spec = pl.BlockSpec((tm, d), lambda b, i: (b * nt + i, 0))

    def full(arr):
        return pl.BlockSpec(arr.shape, lambda b, i: (0,) * arr.ndim, pipeline_mode=pl.Buffered(1))

    ins = [x, gain.reshape(1, d), shift0.reshape(batch, 1, d), p["mu"], p["w_r"], p["w_k"], p["w_v"], vec,
           p["w1"], p["w2"], p["a1"], p["a2"], p["g1"], p["g2"]]
    in_specs = [row_spec, full(ins[1]), pl.BlockSpec((1, 1, d), lambda b, i: (b, 0, 0))]
    in_specs += [full(a) for a in ins[3:]]
    if has_v_res:
        ins += [p["v1"], p["v2"], v_first]
        in_specs += [full(p["v1"]), full(p["v2"]), row_spec]
    tok = [jax.ShapeDtypeStruct((rows, d), dt) for dt in (BF16, F32, BF16, F32, BF16, BF16, BF16)]
    outs = pl.pallas_call(
        functools.partial(_rw_proj_kernel, has_v_res=has_v_res),
        grid=(batch, nt),
        in_specs=in_specs,
        out_specs=[row_spec] * 7 + [pl.BlockSpec((1, 1, d), lambda b, i: (b, 0, 0))],
        out_shape=tok + [jax.ShapeDtypeStruct((batch, 1, d), F32)],
        scratch_shapes=[pltpu.VMEM((1, d), F32)],
        compiler_params=_params("arbitrary", "arbitrary"),
        name="rw_proj",
    )(*ins)
    return outs[:7], outs[7].reshape(batch, d)


WKV_DECAY_PIECES = 3
WKV_GROUP_PIECES = 2
WKV_STATE_PASSES = 1
WKV_SOLVE_PASSES = 1


def _wkv_prepare(units, c_):
    n = units[0][0].shape[0]
    sum_l = lambda x: _mm_exact_lhs(c_["tri_incl"], x, WKV_DECAY_PIECES)
    sum_g = lambda x: _mm_exact_rhs(x, c_["gsum"], WKV_GROUP_PIECES)
    solve_mm = lambda x, y, nt=False: _mm(x, y, WKV_SOLVE_PASSES, nt=nt)

    cs = [sum_l(u[1]) for u in units]
    head_sums = [sum_g(jnp.concatenate([u[4] * u[4], u[0] * u[2] * u[7][0:1, :]], axis=0)) for u in units]
    yield
    pre = []
    for (r, lw, k, v, kk, a, g, vec), cs_u, hs in zip(units, cs, head_sums):
        cs_last = cs_u[n - 1:n, :]
        d_inv = jnp.exp(-cs_u)
        d_rest = jnp.exp(cs_last - cs_u)
        kkn = kk * lax.rsqrt(jnp.maximum(hs[:n], 1e-24))
        b = kkn * a
        pre.append(dict(a_t=-kkn * jnp.exp(cs_u - lw), r_t=r * jnp.exp(cs_u), b_t=b * d_inv, k_t=k * d_inv,
                        bk_rest=jnp.concatenate([b * d_rest, k * d_rest], axis=0), decay=jnp.exp(cs_last),
                        v=v, g=g, vec=vec, bonus=hs[n:] * v))
    yield

    by_head = lambda x: jnp.concatenate([jnp.where(m, x, 0.0) for m in c_["head_masks"]], axis=0)
    block_diag = lambda w: jnp.where(c_["same_block"], jnp.concatenate([w, w], axis=0), 0.0)
    grams = [solve_mm(jnp.concatenate([q["a_t"], q["r_t"]], axis=0),
                      jnp.concatenate([by_head(q["b_t"]), by_head(q["k_t"])], axis=0), nt=True)
             for q in pre]
    yield
    lows = [jnp.where(c_["strict_w"], gm[:n, :2 * n], 0.0) for gm in grams]
    m_aks = [jnp.where(c_["strict_w"], gm[:n, 2 * n:], 0.0) for gm in grams]
    for q, gm in zip(pre, grams):
        q["m_r"] = jnp.where(c_["incl_w2"], gm[n:], 0.0)
        q["v_heads"] = by_head(q["v"])
    from_v = [solve_mm(mk, q["v_heads"]) for mk, q in zip(m_aks, pre)]
    yield
    inv = [c_["eye_w"] + lo for lo in lows]
    if n > 2:
        pw = [solve_mm(lo, block_diag(lo)) for lo in lows]
        yield
        step = 2
        while 2 * step < n:
            both = [solve_mm(jnp.concatenate([t, p2], axis=0), block_diag(p2)) for t, p2 in zip(inv, pw)]
            inv = [t + bo[:n] for t, bo in zip(inv, both)]
            pw = [bo[n:] for bo in both]
            yield
            step *= 2
        inv = [t + solve_mm(t, block_diag(p2)) for t, p2 in zip(inv, pw)]
    for q, t, fv in zip(pre, inv, from_v):
        q["inv"], q["from_v"] = t, fv
    return pre


def _wkv_advance(pre, states, c_):
    n = pre[0]["v"].shape[0]
    inv_hd = 1.0 / RW_HEAD_DIM
    sum_g = lambda x: _mm_exact_rhs(x, c_["gsum"], WKV_GROUP_PIECES)
    solve_mm = lambda x, y: _mm(x, y, WKV_SOLVE_PASSES)
    by_head = lambda x: jnp.concatenate([jnp.where(m, x, 0.0) for m in c_["head_masks"]], axis=0)

    from_state = [_mm(jnp.concatenate([q["a_t"], q["r_t"]], axis=0), s, WKV_STATE_PASSES, nt=True)
                  for q, s in zip(pre, states)]
    yield
    sig_p = [solve_mm(q["inv"], by_head(fs[:n] + q["from_v"])) for q, fs in zip(pre, from_state)]
    yield
    y_p = [fs[n:] + solve_mm(q["m_r"], jnp.concatenate([by_head(sg), q["v_heads"]], axis=0))
           for q, fs, sg in zip(pre, from_state, sig_p)]
    upd = [_mm(jnp.concatenate([s, q["v"]], axis=0).T, q["bk_rest"], WKV_STATE_PASSES)
           for s, q in zip(sig_p, pre)]
    new_states = [s * q["decay"] + jnp.where(c_["same_head"], u, 0.0) for s, q, u in zip(states, pre, upd)]

    yield
    mean = [sum_g(y) * inv_hd for y in y_p]
    dev = [y - m for y, m in zip(y_p, mean)]
    yield
    var = [sum_g(dv * dv) * inv_hd for dv in dev]
    yield
    outs = []
    for q, dv, vr in zip(pre, dev, var):
        o = dv * lax.rsqrt(vr + RW_GN_EPS) * q["vec"][1:2, :] + q["vec"][2:3, :]
        outs.append((o + q["bonus"]) * q["g"])
    return outs, new_states


def _wkv_advance_chunks(pre_chunks, states, c_):
    outs = []
    for pre in pre_chunks:
        out, states = yield from _wkv_advance(pre, states, c_)
        outs.append(jnp.concatenate(out, axis=1))
    return outs, states


def _in_lock_step(*staged):
    results = [None] * len(staged)
    live = list(enumerate(staged))
    while live:
        still = []
        for i, gen in live:
            try:
                next(gen)
                still.append((i, gen))
            except StopIteration as done:
                results[i] = done.value
        live = still
    return results


def _wkv_kernel(*refs, has_state0, n):
    if has_state0:
        (r_ref, lw_ref, k_ref, v_ref, kk_ref, a_ref, g_ref, vec_ref, s0_ref,
         o_ref, sout_ref, state_ref) = refs
    else:
        (r_ref, lw_ref, k_ref, v_ref, kk_ref, a_ref, g_ref, vec_ref,
         o_ref, sout_ref, state_ref) = refs
    c = pl.program_id(2)
    hd = RW_HEAD_DIM

    @pl.when(c == 0)
    def _():
        if has_state0:
            state_ref[...] = s0_ref[0]
        else:
            state_ref[...] = jnp.zeros_like(state_ref)

    tr = lax.broadcasted_iota(jnp.int32, (n, n), 0)
    tc = lax.broadcasted_iota(jnp.int32, (n, n), 1)
    tr2 = lax.broadcasted_iota(jnp.int32, (n, 2 * n), 0)
    tc2 = lax.rem(lax.broadcasted_iota(jnp.int32, (n, 2 * n), 1), n)
    tr4 = lax.broadcasted_iota(jnp.int32, (n, 4 * n), 0)
    tc4 = lax.rem(lax.broadcasted_iota(jnp.int32, (n, 4 * n), 1), n)
    br = lax.broadcasted_iota(jnp.int32, (2 * n, 2 * n), 0) // n
    bc = lax.broadcasted_iota(jnp.int32, (2 * n, 2 * n), 1) // n
    lane = lax.broadcasted_iota(jnp.int32, (n, LANES), 1)
    sr = lax.broadcasted_iota(jnp.int32, (LANES, LANES), 0) // hd
    sc = lax.broadcasted_iota(jnp.int32, (LANES, LANES), 1) // hd
    consts = dict(
        gsum=jnp.concatenate([_group_ones(LANES, hd)] * WKV_GROUP_PIECES, axis=0),
        tri_incl=jnp.concatenate([jnp.where(tc <= tr, 1.0, 0.0).astype(BF16)] * WKV_DECAY_PIECES, axis=1),
        strict_w=tc2 < tr2,
        incl_w2=tc4 <= tr4,
        eye_w=jnp.where(tc2 == tr2, 1.0, 0.0).astype(F32),
        same_block=br == bc,
        head_masks=[lane < hd, lane >= hd],
        same_head=sr == sc,
    )
    n_pairs = state_ref.shape[0]
    n_chunks = r_ref.shape[0] // n

    def units(chunks):
        out = []
        for ci in chunks:
            rows = slice(ci * n, (ci + 1) * n)
            for p in range(n_pairs):
                sl = slice(p * LANES, (p + 1) * LANES)
                out.append(tuple(ref[rows, sl].astype(F32)
                                 for ref in (r_ref, lw_ref, k_ref, v_ref, kk_ref, a_ref, g_ref)) + (vec_ref[:, sl],))
        return out

    def per_chunk(pre):
        return [pre[i:i + n_pairs] for i in range(0, len(pre), n_pairs)]

    first = range((n_chunks + 1) // 2)
    second = range(len(first), n_chunks)
    states = [state_ref[p] for p in range(n_pairs)]
    (pre_first,) = _in_lock_step(_wkv_prepare(units(first), consts))
    if second:
        (outs, states), pre_second = _in_lock_step(
            _wkv_advance_chunks(per_chunk(pre_first), states, consts), _wkv_prepare(units(second), consts))
        ((outs_second, states),) = _in_lock_step(_wkv_advance_chunks(per_chunk(pre_second), states, consts))
        outs = outs + outs_second
    else:
        ((outs, states),) = _in_lock_step(_wkv_advance_chunks(per_chunk(pre_first), states, consts))
    o_ref[...] = jnp.concatenate(outs, axis=0).astype(o_ref.dtype)
    new_states = jnp.stack(states)
    state_ref[...] = new_states
    sout_ref[0] = new_states


def _wkv(tok, batch, vec, state0):
    rows, d = tok[0].shape
    seq = rows // batch
    n = min(WKV_CHUNK, seq)
    span = n * min(WKV_CHUNKS_PER_STEP, seq // n)
    nc = seq // span
    pairs = d // LANES
    gp = WKV_PAIRS
    width = gp * LANES
    has_state0 = state0 is not None
    tok_spec = pl.BlockSpec((span, width), lambda b, p, c: (b * nc + c, p))
    state_spec = pl.BlockSpec((1, gp, LANES, LANES), lambda b, p, c: (b, p, 0, 0))
    ins = list(tok) + [vec]
    in_specs = [tok_spec] * 7 + [pl.BlockSpec((8, width), lambda b, p, c: (0, p))]
    if has_state0:
        ins.append(state0)
        in_specs.append(state_spec)
    return pl.pallas_call(
        functools.partial(_wkv_kernel, has_state0=has_state0, n=n),
        grid=(batch, pairs // gp, nc),
        in_specs=in_specs,
        out_specs=[tok_spec, state_spec],
        out_shape=[jax.ShapeDtypeStruct((rows, d), BF16),
                   jax.ShapeDtypeStruct((batch, pairs, LANES, LANES), F32)],
        scratch_shapes=[pltpu.VMEM((gp, LANES, LANES), F32)],
        compiler_params=_params("arbitrary", "arbitrary", "arbitrary"),
        name="wkv",
    )(*ins)


def _state_to_blockdiag(state):
    b, h, n, _ = state.shape
    s = state.reshape(b, h // 2, 2, n, n)
    z = jnp.zeros_like(s[:, :, 0])
    top = jnp.concatenate([s[:, :, 0], z], axis=-1)
    bot = jnp.concatenate([z, s[:, :, 1]], axis=-1)
    return jnp.concatenate([top, bot], axis=-2)


def _state_from_blockdiag(bd):
    b, p, n2, _ = bd.shape
    n = n2 // 2
    return jnp.stack([bd[:, :, :n, :n], bd[:, :, n:, n:]], axis=2).reshape(b, 2 * p, n, n)


def _trunk(x, batch, mem_k, mem_v, past_k, past_v, rw_state0, rw_shift0, w):
    depth = w["ffn_gu"].shape[0]
    d = x.shape[-1]
    sb_k, sb_v, rw_states, rw_shifts = None, None, [], []
    n_sb = (depth + 1) // 2
    v_first = None
    for layer in range(depth):
        j = layer // 2
        if layer % 2 == 0:
            sb_k, sb_v, qkv = _sb_qkv(x, batch, w["norm_mix"][layer], w["sb_qkv"][j], j, n_sb, sb_k, sb_v)
            if past_k is None:
                o = _sb_attention_prompt(qkv, batch)
            else:
                o = _sb_attention_sample(qkv, batch, past_k, past_v, j)
            w_o = w["sb_o"][j]
        else:
            if rw_state0 is None:
                state0 = None
                shift0 = jnp.zeros((batch, d), F32)
            else:
                state0 = _state_to_blockdiag(rw_state0[j])
                shift0 = rw_shift0[j]
            tok, shift = _rw_proj(x, batch, w["norm_mix"][layer], shift0, w["rw"][j], v_first)
            if v_first is None:
                v_first = tok[3]
            o, state = _wkv(tok, batch, w["rw"][j]["out_vec"], state0)
            rw_states.append(_state_from_blockdiag(state))
            rw_shifts.append(shift)
            w_o = w["rw"][j]["w_o"]
        x = _xattn(o, w_o, x, w["norm_xattn"][layer], w["xa_q"], w["xa_o"], mem_k, mem_v, layer,
                   in_place=layer > 0)
        final = w["norm_final"] if layer == depth - 1 else None
        x = _ffn(x, w["norm_ffn"][layer], w["ffn_gu"], w["ffn_down"], layer, final)
    return x, sb_k, sb_v, jnp.stack(rw_states), jnp.stack(rw_shifts)


def kernel(x_prompt, x_sample, mem_prompt, cache_sb_k, cache_sb_v, state_rwkv_wkv, state_rwkv_shift,
           cache_mem_k, cache_mem_v, norm_mix, norm_xattn, norm_ffn, norm_mem, norm_final,
           sb_w_qkv, sb_w_o, rw_mu, rw_w_r, rw_w_k, rw_w_v, rw_w_o, rw_w0, rw_w1, rw_w2,
           rw_a0, rw_a1, rw_a2, rw_v0, rw_v1, rw_v2, rw_g1, rw_g2, rw_k_k, rw_k_a, rw_r_k,
           rw_lnx_w, rw_lnx_b, xa_w_q, xa_w_kv, xa_w_o, ffn_w_gate_up, ffn_w_down):
    batch, seq, d = x_prompt.shape
    dec_batch, dec_seq, _ = x_sample.shape
    depth = norm_mix.shape[0]
    n_rw = rw_w_r.shape[0]
    n_sb = sb_w_qkv.shape[0]
    n_mem = mem_prompt.shape[1]
    bf = lambda t: t.astype(BF16)
    per_layer = lambda t: [bf(t[i]) for i in range(t.shape[0])]

    rw = []
    for j in range(n_rw):
        pj = dict(mu=rw_mu[j], w_r=bf(rw_w_r[j]), w_k=bf(rw_w_k[j]), w_v=bf(rw_w_v[j]), w_o=bf(rw_w_o[j]),
                  w0=rw_w0[j], a0=rw_a0[j], k_k=rw_k_k[j], k_a=rw_k_a[j],
                  w1=bf(_pad_cols(rw_w1[j])), w2=bf(_pad_rows(rw_w2[j])),
                  a1=bf(_pad_cols(rw_a1[j])), a2=bf(_pad_rows(rw_a2[j])),
                  g1=bf(_pad_cols(rw_g1[j])), g2=bf(_pad_rows(rw_g2[j])))
        if j > 0:
            pj.update(v0=rw_v0[j - 1], v1=bf(_pad_cols(rw_v1[j - 1])), v2=bf(_pad_rows(rw_v2[j - 1])))
        zeros = jnp.zeros((d,), F32)
        pj["out_vec"] = jnp.stack([rw_r_k[j].reshape(d), rw_lnx_w[j], rw_lnx_b[j]] + [zeros] * 5)
        rw.append(pj)
    w = dict(norm_mix=norm_mix, norm_xattn=norm_xattn, norm_ffn=norm_ffn, norm_final=norm_final,
             sb_qkv=per_layer(sb_w_qkv), sb_o=per_layer(sb_w_o), rw=rw,
             xa_q=bf(xa_w_q), xa_o=bf(xa_w_o), ffn_gu=bf(ffn_w_gate_up), ffn_down=bf(ffn_w_down))

    mem_k, mem_v, mem_k_bf, mem_v_bf = _mem_kv(mem_prompt, norm_mem, bf(xa_w_kv))
    y_p, p_sb_k, p_sb_v, p_wkv, p_shift = _trunk(
        x_prompt.reshape(batch * seq, d), batch, mem_k_bf, mem_v_bf, None, None, None, None, w)

    y_s, s_sb_k, s_sb_v, s_wkv, s_shift = _trunk(
        x_sample.reshape(dec_batch * dec_seq, d), dec_batch,
        bf(cache_mem_k).reshape(depth, dec_batch, n_mem, d), bf(cache_mem_v).reshape(depth, dec_batch, n_mem, d),
        cache_sb_k.reshape(n_sb, dec_batch, -1, d), cache_sb_v.reshape(n_sb, dec_batch, -1, d),
        state_rwkv_wkv, state_rwkv_shift, w)

    hd = d // SB_HEADS
    return (y_p.reshape(batch, seq, d), y_s.reshape(dec_batch, dec_seq, d),
            p_sb_k.reshape(n_sb, batch, seq, SB_HEADS, hd), p_sb_v.reshape(n_sb, batch, seq, SB_HEADS, hd),
            p_wkv, p_shift,
            mem_k, mem_v,
            s_sb_k.reshape(n_sb, dec_batch, dec_seq, SB_HEADS, hd),
            s_sb_v.reshape(n_sb, dec_batch, dec_seq, SB_HEADS, hd),
            s_wkv, s_shift)
```

```python
import functools

import jax
import jax.numpy as jnp
from jax import lax
from jax.experimental import pallas as pl
from jax.experimental.pallas import tpu as pltpu

F32 = jnp.float32
BF16 = jnp.bfloat16

NORM_EPS = 1e-6
NEG_LOG2_E = -1.4426950408889634
RW_GN_EPS = 64e-5
SB_HEADS = 16
XA_HEADS = 4
RW_HEAD_DIM = 64
LANES = 128
ROW_TILE = 256
WIDE_ROW_TILE = 512
SB_BLOCK = 256
SB_PAIRS = 8
SB_WAVES = 4
SB_DECODE_WAVES = 2
SB_UNROLL = 2
SB_PAST_SPAN = 2048
WKV_CHUNK = 64
WKV_CHUNKS_PER_STEP = 4
WKV_PAIRS = 8
FFN_CHUNK = 256
VMEM_LIMIT_BYTES = 56 * 1024 * 1024


def _params(*sem):
    return pltpu.CompilerParams(dimension_semantics=sem, vmem_limit_bytes=VMEM_LIMIT_BYTES)


def _dot(a, b):
    return jnp.dot(a, b, preferred_element_type=F32)


def _dot_nt(a, b):
    return lax.dot_general(a, b, (((1,), (1,)), ((), ())), preferred_element_type=F32)


def _split(x, n):
    parts = []
    for i in range(n):
        p = x.astype(BF16)
        parts.append(p)
        if i + 1 < n:
            x = x - p.astype(F32)
    return parts


def _mm(a, b, passes, nt=False):
    dot = _dot_nt if nt else _dot
    if passes == 1:
        return dot(a.astype(BF16), b.astype(BF16))
    a_hi, a_lo = _split(a, 2)
    b_hi, b_lo = _split(b, 2)
    return dot(a_hi, b_hi) + (dot(a_hi, b_lo) + dot(a_lo, b_hi))


def _mm_exact_rhs(a, b_stacked, n):
    return _dot(jnp.concatenate(_split(a, n), axis=1), b_stacked)


def _mm_exact_lhs(a_repeated, b, n):
    return _dot(a_repeated, jnp.concatenate(_split(b, n), axis=0))


def _rms(x, gain):
    ms = jnp.mean(x * x, axis=-1, keepdims=True)
    return x * lax.rsqrt(ms + NORM_EPS) * gain


def _group_ones(n, group):
    r = lax.broadcasted_iota(jnp.int32, (n, n), 0) // group
    c = lax.broadcasted_iota(jnp.int32, (n, n), 1) // group
    return jnp.where(r == c, 1.0, 0.0).astype(BF16)


def _mem_kv_kernel(x_ref, g_ref, w_ref, k_ref, v_ref, kb_ref, vb_ref):
    tm, d = x_ref.shape
    heads, dh = k_ref.shape[-2:]
    h = _rms(x_ref[...], g_ref[0]).astype(BF16)
    kv = _dot(h, w_ref[0])
    k, v = kv[:, :d], kv[:, d:]
    k_ref[0, 0] = k.reshape(tm, heads, dh)
    v_ref[0, 0] = v.reshape(tm, heads, dh)
    kb_ref[0, 0] = k.astype(BF16)
    vb_ref[0, 0] = v.astype(BF16)


def _mem_kv(mem, gains, w_kv):
    depth = w_kv.shape[0]
    batch, n_mem, d = mem.shape
    tm = min(ROW_TILE, n_mem)
    nt = n_mem // tm
    dh = d // XA_HEADS
    assert n_mem % tm == 0 and dh % LANES == 0
    out = jax.ShapeDtypeStruct((depth, batch, n_mem, XA_HEADS, dh), F32)
    out_b = jax.ShapeDtypeStruct((depth, batch, n_mem, d), BF16)
    spec = pl.BlockSpec((1, 1, tm, XA_HEADS, dh), lambda l, i: (l, i // nt, i % nt, 0, 0))
    spec_b = pl.BlockSpec((1, 1, tm, d), lambda l, i: (l, i // nt, i % nt, 0))
    return pl.pallas_call(
        _mem_kv_kernel,
        grid=(depth, batch * nt),
        in_specs=[pl.BlockSpec((tm, d), lambda l, i: (i, 0)),
                  pl.BlockSpec((1, 1, d), lambda l, i: (l, 0, 0)),
                  pl.BlockSpec((1, d, 2 * d), lambda l, i: (l, 0, 0))],
        out_specs=[spec, spec, spec_b, spec_b],
        out_shape=[out, out, out_b, out_b],
        compiler_params=_params("arbitrary", "arbitrary"),
        name="mem_kv",
    )(mem.reshape(batch * n_mem, d), gains.reshape(depth, 1, d), w_kv)


def _split_heads(x, hd):
    tm, d = x.shape
    blocks = []
    for p in range(d // LANES):
        slab = x[:, p * LANES:(p + 1) * LANES]
        blocks += [slab, pltpu.roll(slab, LANES - hd, 1)]
    return jnp.concatenate(blocks, axis=1).reshape(tm, d // hd, LANES)[:, :, :hd]


def _sb_qkv_kernel(x_ref, g_ref, w_ref, *rest, q_scale, slot):
    k_ref, v_ref, qkv_ref = rest[-3:]
    d = x_ref.shape[-1]
    hd = k_ref.shape[-1]
    h = _rms(x_ref[...], g_ref[...]).astype(BF16)
    qkv = _dot(h, w_ref[...])
    qkv_ref[:, :d] = (qkv[:, :d] * q_scale).astype(BF16)
    qkv_ref[:, d:] = qkv[:, d:].astype(BF16)
    k3 = _split_heads(qkv[:, d:2 * d], hd)
    v3 = _split_heads(qkv[:, 2 * d:], hd)
    own = slot if k_ref.shape[0] > 1 else 0
    for s in range(k_ref.shape[0]):
        if s == own:
            k_ref[s, 0] = k3
            v_ref[s, 0] = v3
        else:
            k_ref[s, 0] = jnp.zeros(k3.shape, k_ref.dtype)
            v_ref[s, 0] = jnp.zeros(v3.shape, v_ref.dtype)


def _sb_qkv(x, batch, gain, w, slot, n_slots, k_all=None, v_all=None):
    rows, d = x.shape
    seq = rows // batch
    tm = min(ROW_TILE, seq)
    nt = seq // tm
    hd = d // SB_HEADS
    assert 2 * hd == LANES
    q_scale = float(hd ** -0.5)
    ins = [x, gain.reshape(1, d), w]
    in_specs = [pl.BlockSpec((tm, d), lambda i: (i, 0)),
                pl.BlockSpec((1, d), lambda i: (0, 0)),
                pl.BlockSpec(w.shape, lambda i: (0, 0))]
    if k_all is None:
        aliases = {}
        slot_spec = pl.BlockSpec((n_slots, 1, tm, SB_HEADS, hd), lambda i: (0, i // nt, i % nt, 0, 0))
    else:
        ins += [k_all, v_all]
        in_specs += [pl.BlockSpec(memory_space=pl.ANY)] * 2
        aliases = {3: 0, 4: 1}
        slot_spec = pl.BlockSpec((1, 1, tm, SB_HEADS, hd), lambda i: (slot, i // nt, i % nt, 0, 0))
    kv_shape = jax.ShapeDtypeStruct((n_slots, batch, seq, SB_HEADS, hd), F32)
    return pl.pallas_call(
        functools.partial(_sb_qkv_kernel, q_scale=q_scale, slot=slot),
        grid=(rows // tm,),
        in_specs=in_specs,
        out_specs=[slot_spec, slot_spec, pl.BlockSpec((tm, 3 * d), lambda i: (i, 0))],
        out_shape=[kv_shape, kv_shape, jax.ShapeDtypeStruct((rows, 3 * d), BF16)],
        input_output_aliases=aliases,
        compiler_params=_params("arbitrary"),
        name="sb_qkv",
    )(*ins)


def _sb_step(q2s, kv_blocks, carry, acc, neg_lower2, mask, n_waves):
    m = q2s[0].shape[0]
    groups = len(q2s)
    per = max(groups // n_waves, 1)
    waves = [range(c, min(c + per, groups)) for c in range(0, groups, per)]
    rows = [slice(w_[0] * m, (w_[-1] + 1) * m) for w_ in waves]
    units = [(b, c) for b in range(len(kv_blocks)) for c in range(len(waves))]

    def scores(b, c):
        return jnp.concatenate([_dot_nt(q2s[g], kv_blocks[b][0][g]) for g in waves[c]], axis=0)

    def softplus_pieces(c, z):
        sp = jnp.maximum(z, 0.0) + jnp.log(1.0 + jnp.exp2(jnp.abs(z) * NEG_LOG2_E))
        if mask is not None:
            sp = jnp.where(mask[rows[c]], sp, 0.0)
        return jnp.concatenate(_split(sp, 2), axis=1)

    def weights(c, z, from_s, later):
        w = jnp.exp(z + from_s + jnp.concatenate([later] * (z.shape[1] // LANES), axis=1))
        if mask is not None:
            w = jnp.where(mask[rows[c]], w, 0.0)
        return w.astype(BF16)

    def values(b, c, w):
        return jnp.concatenate([_dot(w[i * m:(i + 1) * m], kv_blocks[b][1][g])
                                for i, g in enumerate(waves[c])], axis=0)

    n = len(units)
    z, pieces, from_s, w = ([None] * n for _ in range(4))
    later = [carry[r] for r in rows]
    pv = [None] * len(waves)
    for tick in range(n + 4):
        for u, (b, c) in enumerate(units):
            stage = tick - u
            if stage == 0:
                z[u] = scores(b, c)
            elif stage == 1:
                pieces[u] = softplus_pieces(c, z[u])
            elif stage == 2:
                from_s[u] = _dot(pieces[u], neg_lower2)
            elif stage == 3:
                w[u] = weights(c, z[u], from_s[u], later[c])
                later[c] = later[c] + jnp.broadcast_to(from_s[u][:, 0:1], later[c].shape)
            elif stage == 4:
                out = values(b, c, w[u])
                pv[c] = out if pv[c] is None else pv[c] + out
    return jnp.concatenate(later, axis=0), acc + jnp.concatenate(pv, axis=0)


def _sb_consts(bk):
    r = lax.broadcasted_iota(jnp.int32, (2, bk, bk), 1).reshape(2 * bk, bk)
    c = lax.broadcasted_iota(jnp.int32, (2 * bk, bk), 1)
    return jnp.where(r >= c, -1.0, 0.0).astype(BF16)


def _sb_stack_heads(q_ref, q2_ref):
    tq = q_ref.shape[0]
    lane = lax.broadcasted_iota(jnp.int32, (tq, LANES), 1)
    for g in range(q2_ref.shape[0]):
        q = q_ref[:, g * LANES:(g + 1) * LANES]
        zero = jnp.zeros_like(q)
        q2_ref[g] = jnp.concatenate([jnp.where(lane < LANES // 2, q, zero),
                                     jnp.where(lane >= LANES // 2, q, zero)], axis=0)


def _sb_causal_mask(groups, tq, bk, t0=0):
    t = lax.broadcasted_iota(jnp.int32, (2 * groups, tq, bk), 1).reshape(2 * groups * tq, bk)
    s = lax.broadcasted_iota(jnp.int32, (2 * groups * tq, bk), 1)
    return s < t + t0


def _sb_write(o_ref, acc_ref):
    tq = o_ref.shape[0]
    lane = lax.broadcasted_iota(jnp.int32, (tq, LANES), 1)
    for g in range(acc_ref.shape[0] // (2 * tq)):
        o_ref[:, g * LANES:(g + 1) * LANES] = jnp.where(
            lane < LANES // 2, acc_ref[2 * g * tq:(2 * g + 1) * tq, :],
            acc_ref[(2 * g + 1) * tq:(2 * g + 2) * tq, :]).astype(o_ref.dtype)


def _sb_prompt_kernel(q_ref, k_ref, v_ref, o_ref, q2_ref, carry_ref, acc_ref):
    blk = q_ref.shape[0]
    groups = q2_ref.shape[0]
    i = pl.program_id(2)
    neg_lower2 = _sb_consts(blk)
    _sb_stack_heads(q_ref, q2_ref)
    lanes = [slice(g * LANES, (g + 1) * LANES) for g in range(groups)]

    def block(j):
        row0 = pl.multiple_of(j * blk, blk)
        return ([k_ref[pl.ds(row0, blk), sl] for sl in lanes], [v_ref[pl.ds(row0, blk), sl] for sl in lanes])

    def step(blocks, carry, acc, mask):
        return _sb_step([q2_ref[g] for g in range(groups)], blocks, carry, acc, neg_lower2, mask, SB_WAVES)

    half = blk // 2
    d0 = pl.multiple_of(i * blk, blk)
    zero = jnp.zeros((groups * blk, LANES), F32)
    halves = []
    for t0, keys, consts in ((half, blk, neg_lower2), (0, half, _sb_consts(half))):
        q2_half = [jnp.concatenate([q2_ref[g, t0:t0 + half, :], q2_ref[g, blk + t0:blk + t0 + half, :]], axis=0)
                   for g in range(groups)]
        own = ([k_ref[pl.ds(d0, keys), sl] for sl in lanes], [v_ref[pl.ds(d0, keys), sl] for sl in lanes])
        halves.append(_sb_step(q2_half, [own], zero, zero, consts, _sb_causal_mask(groups, half, keys, t0),
                               SB_WAVES))
    (carry_late, acc_late), (carry_early, acc_early) = halves
    for g in range(groups):
        for h in range(2):
            src = slice((2 * g + h) * half, (2 * g + h + 1) * half)
            dst = (2 * g + h) * blk
            for ref, early, late in ((carry_ref, carry_early, carry_late), (acc_ref, acc_early, acc_late)):
                ref[dst:dst + half, :] = early[src]
                ref[dst + half:dst + blk, :] = late[src]

    odd = i % SB_UNROLL

    def single(it, c):
        carry_ref[...], acc_ref[...] = step([block(i - 1 - it)], carry_ref[...], acc_ref[...], None)
        return c

    def multi(it, c):
        top = i - 1 - odd - it * SB_UNROLL
        carry_ref[...], acc_ref[...] = step([block(top - u) for u in range(SB_UNROLL)],
                                            carry_ref[...], acc_ref[...], None)
        return c

    lax.fori_loop(0, odd, single, 0)
    lax.fori_loop(0, i // SB_UNROLL, multi, 0)
    _sb_write(o_ref, acc_ref)


def _sb_attention_prompt(qkv, batch):
    rows, d3 = qkv.shape
    d = d3 // 3
    seq = rows // batch
    blk = min(SB_BLOCK, seq)
    nq = seq // blk
    width = SB_PAIRS * LANES
    ng = d // width
    return pl.pallas_call(
        _sb_prompt_kernel,
        grid=(batch, ng, nq),
        in_specs=[pl.BlockSpec((blk, width), lambda b, p, i: (b * nq + i, p)),
                  pl.BlockSpec((seq, width), lambda b, p, i: (b, ng + p)),
                  pl.BlockSpec((seq, width), lambda b, p, i: (b, 2 * ng + p))],
        out_specs=pl.BlockSpec((blk, width), lambda b, p, i: (b * nq + i, p)),
        out_shape=jax.ShapeDtypeStruct((rows, d), BF16),
        scratch_shapes=[pltpu.VMEM((SB_PAIRS, 2 * blk, LANES), BF16),
                        pltpu.VMEM((SB_PAIRS * 2 * blk, LANES), F32),
                        pltpu.VMEM((SB_PAIRS * 2 * blk, LANES), F32)],
        compiler_params=_params("arbitrary", "arbitrary", "arbitrary"),
        name="sb_attention_prompt",
    )(qkv, qkv, qkv)


def _sb_sample_kernel(q_ref, kn_ref, vn_ref, kp_ref, vp_ref, o_ref, q2_ref, carry_ref, acc_ref, *, blk):
    tq = q_ref.shape[0]
    groups = q2_ref.shape[0]
    c = pl.program_id(1)
    neg_lower2 = _sb_consts(blk)

    lanes = [slice(g * LANES, (g + 1) * LANES) for g in range(groups)]

    @pl.when(c == 0)
    def _():
        _sb_stack_heads(q_ref, q2_ref)
        zero = jnp.zeros(carry_ref.shape, F32)
        carry_ref[...], acc_ref[...] = _sb_step(
            [q2_ref[g] for g in range(groups)],
            [([kn_ref[0, :, sl] for sl in lanes], [vn_ref[0, :, sl] for sl in lanes])],
            zero, zero, neg_lower2, _sb_causal_mask(groups, tq, blk), SB_DECODE_WAVES)

    blocks = []
    for sb in reversed(range(kp_ref.shape[2] // blk)):
        rows = slice(sb * blk, (sb + 1) * blk)
        blocks.append(([kp_ref[0, 0, rows, sl].astype(BF16) for sl in lanes],
                       [vp_ref[0, 0, rows, sl].astype(BF16) for sl in lanes]))
    carry_ref[...], acc_ref[...] = _sb_step([q2_ref[g] for g in range(groups)], blocks,
                                            carry_ref[...], acc_ref[...], neg_lower2, None, SB_DECODE_WAVES)

    @pl.when(c == pl.num_programs(1) - 1)
    def _():
        _sb_write(o_ref, acc_ref)


def _sb_attention_sample(qkv, batch, past_k, past_v, layer):
    rows, d3 = qkv.shape
    d = d3 // 3
    seq = rows // batch
    blk = SB_BLOCK
    pairs = d // LANES
    past_len = past_k.shape[2]
    span = min(SB_PAST_SPAN, past_len)
    steps = past_len // span
    assert seq <= blk and past_len % span == 0 and span % blk == 0
    new = qkv[:, d:].reshape(batch, seq, 2 * d)
    new = jnp.pad(new, ((0, 0), (0, blk - seq), (0, 0)))
    return pl.pallas_call(
        functools.partial(_sb_sample_kernel, blk=blk),
        grid=(batch, steps),
        in_specs=[pl.BlockSpec((seq, d), lambda b, c: (b, 0)),
                  pl.BlockSpec((1, blk, d), lambda b, c: (b, 0, 0)),
                  pl.BlockSpec((1, blk, d), lambda b, c: (b, 0, 1)),
                  pl.BlockSpec((1, 1, span, d), lambda b, c: (layer, b, steps - 1 - c, 0)),
                  pl.BlockSpec((1, 1, span, d), lambda b, c: (layer, b, steps - 1 - c, 0))],
        out_specs=pl.BlockSpec((seq, d), lambda b, c: (b, 0)),
        out_shape=jax.ShapeDtypeStruct((rows, d), BF16),
        scratch_shapes=[pltpu.VMEM((pairs, 2 * seq, LANES), BF16),
                        pltpu.VMEM((pairs * 2 * seq, LANES), F32),
                        pltpu.VMEM((pairs * 2 * seq, LANES), F32)],
        compiler_params=_params("arbitrary", "arbitrary"),
        name="sb_attention_sample",
    )(qkv, new, new, past_k, past_v)


def _xattn_kernel(a_ref, wa_ref, x_ref, g_ref, wq_ref, wo_ref, mk_ref, mv_ref, o_ref, *, heads):
    x = x_ref[...] + _dot(a_ref[...], wa_ref[...])
    d = x.shape[-1]
    dh = d // heads
    h = _rms(x, g_ref[...]).astype(BF16)
    q = _dot(h, wq_ref[0]).astype(BF16)
    nb = mk_ref.shape[1]
    seq = x.shape[0] // nb
    per_seq = []
    for b in range(nb):
        rows = slice(b * seq, (b + 1) * seq)
        outs = []
        for hd in range(heads):
            sl = slice(hd * dh, (hd + 1) * dh)
            s = _dot_nt(q[rows, sl], mk_ref[0, b, :, sl]) * (dh ** -0.5)
            e = jnp.exp(s - jnp.max(s, axis=-1, keepdims=True))
            p = e / jnp.sum(e, axis=-1, keepdims=True)
            outs.append(_dot(p.astype(BF16), mv_ref[0, b, :, sl]).astype(BF16))
        per_seq.append(jnp.concatenate(outs, axis=-1))
    o = per_seq[0] if nb == 1 else jnp.concatenate(per_seq, axis=0)
    o_ref[...] = x + _dot(o, wo_ref[0])


def _xattn(a, wa, x, gain, wq, wo, mem_k, mem_v, layer, in_place):
    rows, d = x.shape
    _, batch, n_mem, _ = mem_k.shape
    seq = rows // batch
    nb = min(batch, max(1, WIDE_ROW_TILE // seq))
    assert batch % nb == 0
    tm = nb * seq if nb > 1 else min(WIDE_ROW_TILE, seq)
    nt = 1 if nb > 1 else seq // tm
    row_spec = pl.BlockSpec((tm, d), lambda b, i: (b * nt + i, 0))
    weight_spec = pl.BlockSpec((1, d, d), lambda b, i: (layer, 0, 0))
    mem_spec = pl.BlockSpec((1, nb, n_mem, d), lambda b, i: (layer, b, 0, 0))
    return pl.pallas_call(
        functools.partial(_xattn_kernel, heads=XA_HEADS),
        grid=(batch // nb, nt),
        in_specs=[row_spec, pl.BlockSpec((d, d), lambda b, i: (0, 0)), row_spec,
                  pl.BlockSpec((1, d), lambda b, i: (0, 0)), weight_spec, weight_spec, mem_spec, mem_spec],
        out_specs=row_spec,
        out_shape=jax.ShapeDtypeStruct((rows, d), F32),
        input_output_aliases={2: 0} if in_place else {},
        compiler_params=_params("arbitrary", "arbitrary"),
        name="xattn",
    )(a, wa, x, gain.reshape(1, d), wq, wo, mem_k, mem_v)


def _ffn_kernel(x_ref, g_ref, wgu_ref, wd_ref, o_ref, *, final_gain):
    x = x_ref[...]
    hidden = wd_ref.shape[1]
    h = _rms(x, g_ref[0:1, :]).astype(BF16)
    acc = x
    for c0 in range(0, hidden, FFN_CHUNK):
        c1 = min(c0 + FFN_CHUNK, hidden)
        gate = _dot(h, wgu_ref[0, :, c0:c1])
        up = _dot(h, wgu_ref[0, :, hidden + c0:hidden + c1])
        act = (gate * jax.nn.sigmoid(gate) * up).astype(BF16)
        acc = acc + _dot(act, wd_ref[0, c0:c1, :])
    if final_gain:
        acc = _rms(acc, g_ref[1:2, :])
    o_ref[...] = acc


def _ffn(x, gain, w_gate_up, w_down, layer, final_gain=None):
    rows, d = x.shape
    tm = min(WIDE_ROW_TILE, rows)
    gains = jnp.stack([gain, gain if final_gain is None else final_gain])
    return pl.pallas_call(
        functools.partial(_ffn_kernel, final_gain=final_gain is not None),
        grid=(rows // tm,),
        in_specs=[pl.BlockSpec((tm, d), lambda i: (i, 0)),
                  pl.BlockSpec((2, d), lambda i: (0, 0)),
                  pl.BlockSpec((1,) + w_gate_up.shape[1:], lambda i: (layer, 0, 0)),
                  pl.BlockSpec((1,) + w_down.shape[1:], lambda i: (layer, 0, 0))],
        out_specs=pl.BlockSpec((tm, d), lambda i: (i, 0)),
        out_shape=jax.ShapeDtypeStruct((rows, d), F32),
        input_output_aliases={0: 0},
        compiler_params=_params("arbitrary"),
        name="ffn",
    )(x, gains, w_gate_up, w_down)


def _rw_proj_kernel(*refs, has_v_res):
    if has_v_res:
        (x_ref, gain_ref, shift0_ref, mu_ref, wr_ref, wk_ref, wv_ref, vec_ref,
         w1_ref, w2_ref, a1_ref, a2_ref, g1_ref, g2_ref, v1_ref, v2_ref, vfirst_ref,
         r_ref, lw_ref, k_ref, v_ref, kk_ref, a_ref, g_ref, shift_ref, prev_ref) = refs
    else:
        (x_ref, gain_ref, shift0_ref, mu_ref, wr_ref, wk_ref, wv_ref, vec_ref,
         w1_ref, w2_ref, a1_ref, a2_ref, g1_ref, g2_ref,
         r_ref, lw_ref, k_ref, v_ref, kk_ref, a_ref, g_ref, shift_ref, prev_ref) = refs
    i = pl.program_id(1)
    h = _rms(x_ref[...], gain_ref[...])
    tm = h.shape[0]

    @pl.when(i == 0)
    def _():
        prev_ref[...] = shift0_ref[0]

    row = lax.broadcasted_iota(jnp.int32, h.shape, 0)
    h_prev = jnp.where(row == 0, prev_ref[...], pltpu.roll(h, 1, 0))
    last = h[tm - 1:tm, :]
    prev_ref[...] = last
    shift_ref[0] = last

    xx = h_prev - h
    mixed = lambda j: (h + xx * mu_ref[j:j + 1, :]).astype(BF16)
    w0, a0, k_k, k_a, v0 = (vec_ref[j:j + 1, :] for j in range(5))

    r_ref[...] = _dot(mixed(0), wr_ref[...]).astype(r_ref.dtype)
    dec_in = _dot(mixed(1), w1_ref[...])
    k = _dot(mixed(2), wk_ref[...])
    dec = w0 + _dot(jnp.tanh(dec_in).astype(BF16), w2_ref[...])
    x_v = mixed(3)
    v = _dot(x_v, wv_ref[...])
    lw_ref[...] = -jnp.exp(-jax.nn.softplus(-dec) - 0.5)
    if has_v_res:
        mix = jax.nn.sigmoid(v0 + _dot(_dot(x_v, v1_ref[...]).astype(BF16), v2_ref[...]))
        v = v + (vfirst_ref[...] - v) * mix
    v_ref[...] = v
    a_in = _dot(mixed(4), a1_ref[...])
    g_in = _dot(mixed(5), g1_ref[...])
    a = jax.nn.sigmoid(a0 + _dot(a_in.astype(BF16), a2_ref[...]))
    kk_ref[...] = (k * k_k).astype(kk_ref.dtype)
    k_ref[...] = (k * (1.0 + (a - 1.0) * k_a)).astype(k_ref.dtype)
    a_ref[...] = a.astype(a_ref.dtype)
    g_ref[...] = _dot(jax.nn.sigmoid(g_in).astype(BF16), g2_ref[...]).astype(g_ref.dtype)


def _pad_cols(w, mult=LANES):
    pad = -w.shape[1] % mult
    return jnp.pad(w, ((0, 0), (0, pad))) if pad else w


def _pad_rows(w, mult=LANES):
    pad = -w.shape[0] % mult
    return jnp.pad(w, ((0, pad), (0, 0))) if pad else w


def _rw_proj(x, batch, gain, shift0, p, v_first):
    rows, d = x.shape
    seq = rows // batch
    tm = min(WIDE_ROW_TILE, seq)
    nt = seq // tm
    has_v_res = v_first is not None
    zeros = jnp.zeros((d,), F32)
    vec = jnp.stack([p["w0"], p["a0"], p["k_k"], p["k_a"], p["v0"] if has_v_res else zeros,
                     zeros, zeros, zeros])
    row_spec = pl.BlockSpec((tm, d), lambda b, i: (b * nt + i, 0))

    def full(arr):
        return pl.BlockSpec(arr.shape, lambda b, i: (0,) * arr.ndim, pipeline_mode=pl.Buffered(1))

    ins = [x, gain.reshape(1, d), shift0.reshape(batch, 1, d), p["mu"], p["w_r"], p["w_k"], p["w_v"], vec,
           p["w1"], p["w2"], p["a1"], p["a2"], p["g1"], p["g2"]]
    in_specs = [row_spec, full(ins[1]), pl.BlockSpec((1, 1, d), lambda b, i: (b, 0, 0))]
    in_specs += [full(a) for a in ins[3:]]
    if has_v_res:
        ins += [p["v1"], p["v2"], v_first]
        in_specs += [full(p["v1"]), full(p["v2"]), row_spec]
    tok = [jax.ShapeDtypeStruct((rows, d), dt) for dt in (BF16, F32, BF16, F32, BF16, BF16, BF16)]
    outs = pl.pallas_call(
        functools.partial(_rw_proj_kernel, has_v_res=has_v_res),
        grid=(batch, nt),
        in_specs=in_specs,
        out_specs=[row_spec] * 7 + [pl.BlockSpec((1, 1, d), lambda b, i: (b, 0, 0))],
        out_shape=tok + [jax.ShapeDtypeStruct((batch, 1, d), F32)],
        scratch_shapes=[pltpu.VMEM((1, d), F32)],
        compiler_params=_params("arbitrary", "arbitrary"),
        name="rw_proj",
    )(*ins)
    return outs[:7], outs[7].reshape(batch, d)


WKV_BASE_BLOCK = 4
WKV_DECAY_PIECES = 3
WKV_GROUP_PIECES = 2
WKV_STATE_PASSES = 1
WKV_SOLVE_PASSES = 1


def _wkv_prepare(units, c_):
    n = units[0][0].shape[0]
    sum_l = lambda x: _mm_exact_lhs(c_["tri_incl"], x, WKV_DECAY_PIECES)
    sum_g = lambda x: _mm_exact_rhs(x, c_["gsum"], WKV_GROUP_PIECES)
    solve_mm = lambda x, y, nt=False: _mm(x, y, WKV_SOLVE_PASSES, nt=nt)

    cs = [sum_l(u[1]) for u in units]
    head_sums = [sum_g(jnp.concatenate([u[4] * u[4], u[0] * u[2] * u[7][0:1, :]], axis=0)) for u in units]
    yield
    pre = []
    for (r, lw, k, v, kk, a, g, vec), cs_u, hs in zip(units, cs, head_sums):
        cs_last = cs_u[n - 1:n, :]
        d_inv = jnp.exp(-cs_u)
        d_rest = jnp.exp(cs_last - cs_u)
        kkn = kk * lax.rsqrt(jnp.maximum(hs[:n], 1e-24))
        b = kkn * a
        pre.append(dict(a_t=-kkn * jnp.exp(cs_u - lw), r_t=r * jnp.exp(cs_u), b_t=b * d_inv, k_t=k * d_inv,
                        bk_rest=jnp.concatenate([b * d_rest, k * d_rest], axis=0), decay=jnp.exp(cs_last),
                        v=v, g=g, vec=vec, bonus=hs[n:] * v))
    yield

    by_head = lambda x: jnp.concatenate([jnp.where(m, x, 0.0) for m in c_["head_masks"]], axis=0)
    block_diag = lambda w: jnp.where(c_["same_block"], jnp.concatenate([w, w], axis=0), 0.0)
    grams = [solve_mm(jnp.concatenate([q["a_t"], q["r_t"]], axis=0),
                      jnp.concatenate([by_head(q["b_t"]), by_head(q["k_t"])], axis=0), nt=True)
             for q in pre]
    yield
    lows = [jnp.where(c_["strict_w"], gm[:n, :2 * n], 0.0) for gm in grams]
    m_aks = [jnp.where(c_["strict_w"], gm[:n, 2 * n:], 0.0) for gm in grams]
    for q, gm in zip(pre, grams):
        q["m_r"] = jnp.where(c_["incl_w2"], gm[n:], 0.0)
        q["v_heads"] = by_head(q["v"])
    from_v = [solve_mm(mk, q["v_heads"]) for mk, q in zip(m_aks, pre)]
    yield
    diag = [jnp.where(c_["diag_block"], lo, 0.0) for lo in lows]
    inv = [c_["eye_w"] + dg for dg in diag]
    sq = [solve_mm(dg, block_diag(dg)) for dg in diag]
    yield
    inv = [t + solve_mm(t, block_diag(p2)) for t, p2 in zip(inv, sq)]
    yield
    for below in c_["below_blocks"]:
        half = [solve_mm(t, block_diag(jnp.where(below, lo, 0.0))) for t, lo in zip(inv, lows)]
        yield
        inv = [t + solve_mm(hf, block_diag(t)) for t, hf in zip(inv, half)]
        yield
    for q, t, fv in zip(pre, inv, from_v):
        q["inv"], q["from_v"] = t, fv
    return pre


def _wkv_advance(pre, states, c_):
    n = pre[0]["v"].shape[0]
    inv_hd = 1.0 / RW_HEAD_DIM
    sum_g = lambda x: _mm_exact_rhs(x, c_["gsum"], WKV_GROUP_PIECES)
    solve_mm = lambda x, y: _mm(x, y, WKV_SOLVE_PASSES)
    by_head = lambda x: jnp.concatenate([jnp.where(m, x, 0.0) for m in c_["head_masks"]], axis=0)

    from_state = [_mm(jnp.concatenate([q["a_t"], q["r_t"]], axis=0), s, WKV_STATE_PASSES, nt=True)
                  for q, s in zip(pre, states)]
    yield
    sig_p = [solve_mm(q["inv"], by_head(fs[:n] + q["from_v"])) for q, fs in zip(pre, from_state)]
    yield
    y_p = [fs[n:] + solve_mm(q["m_r"], jnp.concatenate([by_head(sg), q["v_heads"]], axis=0))
           for q, fs, sg in zip(pre, from_state, sig_p)]
    upd = [_mm(jnp.concatenate([s, q["v"]], axis=0).T, q["bk_rest"], WKV_STATE_PASSES)
           for s, q in zip(sig_p, pre)]
    new_states = [s * q["decay"] + jnp.where(c_["same_head"], u, 0.0) for s, q, u in zip(states, pre, upd)]

    yield
    mean = [sum_g(y) * inv_hd for y in y_p]
    dev = [y - m for y, m in zip(y_p, mean)]
    yield
    var = [sum_g(dv * dv) * inv_hd for dv in dev]
    yield
    outs = []
    for q, dv, vr in zip(pre, dev, var):
        o = dv * lax.rsqrt(vr + RW_GN_EPS) * q["vec"][1:2, :] + q["vec"][2:3, :]
        outs.append((o + q["bonus"]) * q["g"])
    return outs, new_states


def _wkv_advance_chunks(pre_chunks, states, c_):
    outs = []
    for pre in pre_chunks:
        out, states = yield from _wkv_advance(pre, states, c_)
        outs.append(jnp.concatenate(out, axis=1))
    return outs, states


def _in_lock_step(*staged):
    results = [None] * len(staged)
    live = list(enumerate(staged))
    while live:
        still = []
        for i, gen in live:
            try:
                next(gen)
                still.append((i, gen))
            except StopIteration as done:
                results[i] = done.value
        live = still
    return results


def _wkv_kernel(*refs, has_state0, n):
    if has_state0:
        (r_ref, lw_ref, k_ref, v_ref, kk_ref, a_ref, g_ref, vec_ref, s0_ref,
         o_ref, sout_ref, state_ref) = refs
    else:
        (r_ref, lw_ref, k_ref, v_ref, kk_ref, a_ref, g_ref, vec_ref,
         o_ref, sout_ref, state_ref) = refs
    c = pl.program_id(2)
    hd = RW_HEAD_DIM

    @pl.when(c == 0)
    def _():
        if has_state0:
            state_ref[...] = s0_ref[0]
        else:
            state_ref[...] = jnp.zeros_like(state_ref)

    tr = lax.broadcasted_iota(jnp.int32, (n, n), 0)
    tc = lax.broadcasted_iota(jnp.int32, (n, n), 1)
    tr2 = lax.broadcasted_iota(jnp.int32, (n, 2 * n), 0)
    tc2 = lax.rem(lax.broadcasted_iota(jnp.int32, (n, 2 * n), 1), n)
    tr4 = lax.broadcasted_iota(jnp.int32, (n, 4 * n), 0)
    tc4 = lax.rem(lax.broadcasted_iota(jnp.int32, (n, 4 * n), 1), n)
    br = lax.broadcasted_iota(jnp.int32, (2 * n, 2 * n), 0) // n
    bc = lax.broadcasted_iota(jnp.int32, (2 * n, 2 * n), 1) // n
    lane = lax.broadcasted_iota(jnp.int32, (n, LANES), 1)
    sr = lax.broadcasted_iota(jnp.int32, (LANES, LANES), 0) // hd
    sc = lax.broadcasted_iota(jnp.int32, (LANES, LANES), 1) // hd
    sizes = [WKV_BASE_BLOCK << i for i in range(n.bit_length()) if (WKV_BASE_BLOCK << i) < n]
    below_blocks = []
    for s in sizes:
        u, v = tr2 // s, tc2 // s
        below_blocks.append(jnp.where((u & 1) == 1, u - 1, -1) == v)
    consts = dict(
        diag_block=(tr2 // WKV_BASE_BLOCK) == (tc2 // WKV_BASE_BLOCK),
        below_blocks=below_blocks,
        gsum=jnp.concatenate([_group_ones(LANES, hd)] * WKV_GROUP_PIECES, axis=0),
        tri_incl=jnp.concatenate([jnp.where(tc <= tr, 1.0, 0.0).astype(BF16)] * WKV_DECAY_PIECES, axis=1),
        strict_w=tc2 < tr2,
        incl_w2=tc4 <= tr4,
        eye_w=jnp.where(tc2 == tr2, 1.0, 0.0).astype(F32),
        same_block=br == bc,
        head_masks=[lane < hd, lane >= hd],
        same_head=sr == sc,
    )
    n_pairs = state_ref.shape[0]
    n_chunks = r_ref.shape[0] // n

    def units(chunks):
        out = []
        for ci in chunks:
            rows = slice(ci * n, (ci + 1) * n)
            for p in range(n_pairs):
                sl = slice(p * LANES, (p + 1) * LANES)
                out.append(tuple(ref[rows, sl].astype(F32)
                                 for ref in (r_ref, lw_ref, k_ref, v_ref, kk_ref, a_ref, g_ref)) + (vec_ref[:, sl],))
        return out

    def per_chunk(pre):
        return [pre[i:i + n_pairs] for i in range(0, len(pre), n_pairs)]

    first = range((n_chunks + 1) // 2)
    second = range(len(first), n_chunks)
    states = [state_ref[p] for p in range(n_pairs)]
    (pre_first,) = _in_lock_step(_wkv_prepare(units(first), consts))
    if second:
        (outs, states), pre_second = _in_lock_step(
            _wkv_advance_chunks(per_chunk(pre_first), states, consts), _wkv_prepare(units(second), consts))
        ((outs_second, states),) = _in_lock_step(_wkv_advance_chunks(per_chunk(pre_second), states, consts))
        outs = outs + outs_second
    else:
        ((outs, states),) = _in_lock_step(_wkv_advance_chunks(per_chunk(pre_first), states, consts))
    o_ref[...] = jnp.concatenate(outs, axis=0).astype(o_ref.dtype)
    new_states = jnp.stack(states)
    state_ref[...] = new_states
    sout_ref[0] = new_states


def _wkv(tok, batch, vec, state0):
    rows, d = tok[0].shape
    seq = rows // batch
    n = min(WKV_CHUNK, seq)
    span = n * min(WKV_CHUNKS_PER_STEP, seq // n)
    nc = seq // span
    pairs = d // LANES
    gp = WKV_PAIRS
    width = gp * LANES
    has_state0 = state0 is not None
    tok_spec = pl.BlockSpec((span, width), lambda b, p, c: (b * nc + c, p))
    state_spec = pl.BlockSpec((1, gp, LANES, LANES), lambda b, p, c: (b, p, 0, 0))
    ins = list(tok) + [vec]
    in_specs = [tok_spec] * 7 + [pl.BlockSpec((8, width), lambda b, p, c: (0, p))]
    if has_state0:
        ins.append(state0)
        in_specs.append(state_spec)
    return pl.pallas_call(
        functools.partial(_wkv_kernel, has_state0=has_state0, n=n),
        grid=(batch, pairs // gp, nc),
        in_specs=in_specs,
        out_specs=[tok_spec, state_spec],
        out_shape=[jax.ShapeDtypeStruct((rows, d), BF16),
                   jax.ShapeDtypeStruct((batch, pairs, LANES, LANES), F32)],
        scratch_shapes=[pltpu.VMEM((gp, LANES, LANES), F32)],
        compiler_params=_params("arbitrary", "arbitrary", "arbitrary"),
        name="wkv",
    )(*ins)


def _state_to_blockdiag(state):
    b, h, n, _ = state.shape
    s = state.reshape(b, h // 2, 2, n, n)
    z = jnp.zeros_like(s[:, :, 0])
    top = jnp.concatenate([s[:, :, 0], z], axis=-1)
    bot = jnp.concatenate([z, s[:, :, 1]], axis=-1)
    return jnp.concatenate([top, bot], axis=-2)


def _state_from_blockdiag(bd):
    b, p, n2, _ = bd.shape
    n = n2 // 2
    return jnp.stack([bd[:, :, :n, :n], bd[:, :, n:, n:]], axis=2).reshape(b, 2 * p, n, n)


def _trunk(x, batch, mem_k, mem_v, past_k, past_v, rw_state0, rw_shift0, w):
    depth = w["ffn_gu"].shape[0]
    d = x.shape[-1]
    sb_k, sb_v, rw_states, rw_shifts = None, None, [], []
    n_sb = (depth + 1) // 2
    v_first = None
    for layer in range(depth):
        j = layer // 2
        if layer % 2 == 0:
            sb_k, sb_v, qkv = _sb_qkv(x, batch, w["norm_mix"][layer], w["sb_qkv"][j], j, n_sb, sb_k, sb_v)
            if past_k is None:
                o = _sb_attention_prompt(qkv, batch)
            else:
                o = _sb_attention_sample(qkv, batch, past_k, past_v, j)
            w_o = w["sb_o"][j]
        else:
            if rw_state0 is None:
                state0 = None
                shift0 = jnp.zeros((batch, d), F32)
            else:
                state0 = _state_to_blockdiag(rw_state0[j])
                shift0 = rw_shift0[j]
            tok, shift = _rw_proj(x, batch, w["norm_mix"][layer], shift0, w["rw"][j], v_first)
            if v_first is None:
                v_first = tok[3]
            o, state = _wkv(tok, batch, w["rw"][j]["out_vec"], state0)
            rw_states.append(_state_from_blockdiag(state))
            rw_shifts.append(shift)
            w_o = w["rw"][j]["w_o"]
        x = _xattn(o, w_o, x, w["norm_xattn"][layer], w["xa_q"], w["xa_o"], mem_k, mem_v, layer,
                   in_place=layer > 0)
        final = w["norm_final"] if layer == depth - 1 else None
        x = _ffn(x, w["norm_ffn"][layer], w["ffn_gu"], w["ffn_down"], layer, final)
    return x, sb_k, sb_v, jnp.stack(rw_states), jnp.stack(rw_shifts)


def kernel(x_prompt, x_sample, mem_prompt, cache_sb_k, cache_sb_v, state_rwkv_wkv, state_rwkv_shift,
           cache_mem_k, cache_mem_v, norm_mix, norm_xattn, norm_ffn, norm_mem, norm_final,
           sb_w_qkv, sb_w_o, rw_mu, rw_w_r, rw_w_k, rw_w_v, rw_w_o, rw_w0, rw_w1, rw_w2,
           rw_a0, rw_a1, rw_a2, rw_v0, rw_v1, rw_v2, rw_g1, rw_g2, rw_k_k, rw_k_a, rw_r_k,
           rw_lnx_w, rw_lnx_b, xa_w_q, xa_w_kv, xa_w_o, ffn_w_gate_up, ffn_w_down):
    batch, seq, d = x_prompt.shape
    dec_batch, dec_seq, _ = x_sample.shape
    depth = norm_mix.shape[0]
    n_rw = rw_w_r.shape[0]
    n_sb = sb_w_qkv.shape[0]
    n_mem = mem_prompt.shape[1]
    bf = lambda t: t.astype(BF16)
    per_layer = lambda t: [bf(t[i]) for i in range(t.shape[0])]

    rw = []
    for j in range(n_rw):
        pj = dict(mu=rw_mu[j], w_r=bf(rw_w_r[j]), w_k=bf(rw_w_k[j]), w_v=bf(rw_w_v[j]), w_o=bf(rw_w_o[j]),
                  w0=rw_w0[j], a0=rw_a0[j], k_k=rw_k_k[j], k_a=rw_k_a[j],
                  w1=bf(_pad_cols(rw_w1[j])), w2=bf(_pad_rows(rw_w2[j])),
                  a1=bf(_pad_cols(rw_a1[j])), a2=bf(_pad_rows(rw_a2[j])),
                  g1=bf(_pad_cols(rw_g1[j])), g2=bf(_pad_rows(rw_g2[j])))
        if j > 0:
            pj.update(v0=rw_v0[j - 1], v1=bf(_pad_cols(rw_v1[j - 1])), v2=bf(_pad_rows(rw_v2[j - 1])))
        zeros = jnp.zeros((d,), F32)
        pj["out_vec"] = jnp.stack([rw_r_k[j].reshape(d), rw_lnx_w[j], rw_lnx_b[j]] + [zeros] * 5)
        rw.append(pj)
    w = dict(norm_mix=norm_mix, norm_xattn=norm_xattn, norm_ffn=norm_ffn, norm_final=norm_final,
             sb_qkv=per_layer(sb_w_qkv), sb_o=per_layer(sb_w_o), rw=rw,
             xa_q=bf(xa_w_q), xa_o=bf(xa_w_o), ffn_gu=bf(ffn_w_gate_up), ffn_down=bf(ffn_w_down))

    mem_k, mem_v, mem_k_bf, mem_v_bf = _mem_kv(mem_prompt, norm_mem, bf(xa_w_kv))
    y_p, p_sb_k, p_sb_v, p_wkv, p_shift = _trunk(
        x_prompt.reshape(batch * seq, d), batch, mem_k_bf, mem_v_bf, None, None, None, None, w)

    y_s, s_sb_k, s_sb_v, s_wkv, s_shift = _trunk(
        x_sample.reshape(dec_batch * dec_seq, d), dec_batch,
        bf(cache_mem_k).reshape(depth, dec_batch, n_mem, d), bf(cache_mem_v).reshape(depth, dec_batch, n_mem, d),
        cache_sb_k.reshape(n_sb, dec_batch, -1, d), cache_sb_v.reshape(n_sb, dec_batch, -1, d),
        state_rwkv_wkv, state_rwkv_shift, w)

    hd = d // SB_HEADS
    return (y_p.reshape(batch, seq, d), y_s.reshape(dec_batch, dec_seq, d),
            p_sb_k.reshape(n_sb, batch, seq, SB_HEADS, hd), p_sb_v.reshape(n_sb, batch, seq, SB_HEADS, hd),
            p_wkv, p_shift,
            mem_k, mem_v,
            s_sb_k.reshape(n_sb, dec_batch, dec_seq, SB_HEADS, hd),
            s_sb_v.reshape(n_sb, dec_batch, dec_seq, SB_HEADS, hd),
            s_wkv, s_shift)
```

```python
import functools

import jax
import jax.numpy as jnp
from jax import lax
from jax.experimental import pallas as pl
from jax.experimental.pallas import tpu as pltpu

F32 = jnp.float32
BF16 = jnp.bfloat16

NORM_EPS = 1e-6
NEG_LOG2_E = -1.4426950408889634
RW_GN_EPS = 64e-5
SB_HEADS = 16
XA_HEADS = 4
RW_HEAD_DIM = 64
LANES = 128
ROW_TILE = 256
WIDE_ROW_TILE = 512
XATTN_ROW_TILE = 1024
SB_BLOCK = 256
SB_PAIRS = 8
SB_WAVES = 4
SB_DECODE_WAVES = 2
SB_UNROLL = 2
SB_PAST_SPAN = 2048
WKV_CHUNK = 64
WKV_CHUNKS_PER_STEP = 4
WKV_PAIRS = 8
FFN_CHUNK = 256
VMEM_LIMIT_BYTES = 56 * 1024 * 1024


def _params(*sem):
    return pltpu.CompilerParams(dimension_semantics=sem, vmem_limit_bytes=VMEM_LIMIT_BYTES)


def _dot(a, b):
    return jnp.dot(a, b, preferred_element_type=F32)


def _dot_nt(a, b):
    return lax.dot_general(a, b, (((1,), (1,)), ((), ())), preferred_element_type=F32)


def _split(x, n):
    parts = []
    for i in range(n):
        p = x.astype(BF16)
        parts.append(p)
        if i + 1 < n:
            x = x - p.astype(F32)
    return parts


def _mm(a, b, passes, nt=False):
    dot = _dot_nt if nt else _dot
    if passes == 1:
        return dot(a.astype(BF16), b.astype(BF16))
    a_hi, a_lo = _split(a, 2)
    b_hi, b_lo = _split(b, 2)
    return dot(a_hi, b_hi) + (dot(a_hi, b_lo) + dot(a_lo, b_hi))


def _mm_exact_rhs(a, b_stacked, n):
    return _dot(jnp.concatenate(_split(a, n), axis=1), b_stacked)


def _mm_exact_lhs(a_repeated, b, n):
    return _dot(a_repeated, jnp.concatenate(_split(b, n), axis=0))


def _rms(x, gain):
    ms = jnp.mean(x * x, axis=-1, keepdims=True)
    return x * lax.rsqrt(ms + NORM_EPS) * gain


def _group_ones(n, group):
    r = lax.broadcasted_iota(jnp.int32, (n, n), 0) // group
    c = lax.broadcasted_iota(jnp.int32, (n, n), 1) // group
    return jnp.where(r == c, 1.0, 0.0).astype(BF16)


def _mem_kv_kernel(x_ref, g_ref, w_ref, k_ref, v_ref, kb_ref, vb_ref):
    tm, d = x_ref.shape
    heads, dh = k_ref.shape[-2:]
    h = _rms(x_ref[...], g_ref[0]).astype(BF16)
    kv = _dot(h, w_ref[0])
    k, v = kv[:, :d], kv[:, d:]
    k_ref[0, 0] = k.reshape(tm, heads, dh)
    v_ref[0, 0] = v.reshape(tm, heads, dh)
    kb_ref[0, 0] = k.astype(BF16)
    vb_ref[0, 0] = v.astype(BF16)


def _mem_kv(mem, gains, w_kv):
    depth = w_kv.shape[0]
    batch, n_mem, d = mem.shape
    tm = min(ROW_TILE, n_mem)
    nt = n_mem // tm
    dh = d // XA_HEADS
    assert n_mem % tm == 0 and dh % LANES == 0
    out = jax.ShapeDtypeStruct((depth, batch, n_mem, XA_HEADS, dh), F32)
    out_b = jax.ShapeDtypeStruct((depth, batch, n_mem, d), BF16)
    spec = pl.BlockSpec((1, 1, tm, XA_HEADS, dh), lambda l, i: (l, i // nt, i % nt, 0, 0))
    spec_b = pl.BlockSpec((1, 1, tm, d), lambda l, i: (l, i // nt, i % nt, 0))
    return pl.pallas_call(
        _mem_kv_kernel,
        grid=(depth, batch * nt),
        in_specs=[pl.BlockSpec((tm, d), lambda l, i: (i, 0)),
                  pl.BlockSpec((1, 1, d), lambda l, i: (l, 0, 0)),
                  pl.BlockSpec((1, d, 2 * d), lambda l, i: (l, 0, 0))],
        out_specs=[spec, spec, spec_b, spec_b],
        out_shape=[out, out, out_b, out_b],
        compiler_params=_params("arbitrary", "arbitrary"),
        name="mem_kv",
    )(mem.reshape(batch * n_mem, d), gains.reshape(depth, 1, d), w_kv)


def _split_heads(x, hd):
    tm, d = x.shape
    blocks = []
    for p in range(d // LANES):
        slab = x[:, p * LANES:(p + 1) * LANES]
        blocks += [slab, pltpu.roll(slab, LANES - hd, 1)]
    return jnp.concatenate(blocks, axis=1).reshape(tm, d // hd, LANES)[:, :, :hd]


def _sb_qkv_kernel(x_ref, g_ref, w_ref, *rest, q_scale, slot):
    k_ref, v_ref, qkv_ref = rest[-3:]
    d = x_ref.shape[-1]
    hd = k_ref.shape[-1]
    h = _rms(x_ref[...], g_ref[...]).astype(BF16)
    own = slot if k_ref.shape[0] > 1 else 0
    for s in range(k_ref.shape[0]):
        if s != own:
            k_ref[s, 0] = jnp.zeros(k_ref.shape[2:], k_ref.dtype)
            v_ref[s, 0] = jnp.zeros(v_ref.shape[2:], v_ref.dtype)
    k = _dot(h, w_ref[:, d:2 * d])
    qkv_ref[:, d:2 * d] = k.astype(BF16)
    k_ref[own, 0] = _split_heads(k, hd)
    v = _dot(h, w_ref[:, 2 * d:])
    qkv_ref[:, 2 * d:] = v.astype(BF16)
    v_ref[own, 0] = _split_heads(v, hd)
    qkv_ref[:, :d] = (_dot(h, w_ref[:, :d]) * q_scale).astype(BF16)


def _sb_qkv(x, batch, gain, w, slot, n_slots, k_all=None, v_all=None):
    rows, d = x.shape
    seq = rows // batch
    tm = min(ROW_TILE, seq)
    nt = seq // tm
    hd = d // SB_HEADS
    assert 2 * hd == LANES
    q_scale = float(hd ** -0.5)
    ins = [x, gain.reshape(1, d), w]
    in_specs = [pl.BlockSpec((tm, d), lambda i: (i, 0)),
                pl.BlockSpec((1, d), lambda i: (0, 0)),
                pl.BlockSpec(w.shape, lambda i: (0, 0))]
    if k_all is None:
        aliases = {}
        slot_spec = pl.BlockSpec((n_slots, 1, tm, SB_HEADS, hd), lambda i: (0, i // nt, i % nt, 0, 0))
    else:
        ins += [k_all, v_all]
        in_specs += [pl.BlockSpec(memory_space=pl.ANY)] * 2
        aliases = {3: 0, 4: 1}
        slot_spec = pl.BlockSpec((1, 1, tm, SB_HEADS, hd), lambda i: (slot, i // nt, i % nt, 0, 0))
    kv_shape = jax.ShapeDtypeStruct((n_slots, batch, seq, SB_HEADS, hd), F32)
    return pl.pallas_call(
        functools.partial(_sb_qkv_kernel, q_scale=q_scale, slot=slot),
        grid=(rows // tm,),
        in_specs=in_specs,
        out_specs=[slot_spec, slot_spec, pl.BlockSpec((tm, 3 * d), lambda i: (i, 0))],
        out_shape=[kv_shape, kv_shape, jax.ShapeDtypeStruct((rows, 3 * d), BF16)],
        input_output_aliases=aliases,
        compiler_params=_params("arbitrary"),
        name="sb_qkv",
    )(*ins)


def _sb_step(q2s, kv_blocks, carry, acc, neg_lower2, mask, n_waves):
    m = q2s[0].shape[0]
    groups = len(q2s)
    per = max(groups // n_waves, 1)
    waves = [range(c, min(c + per, groups)) for c in range(0, groups, per)]
    rows = [slice(w_[0] * m, (w_[-1] + 1) * m) for w_ in waves]
    units = [(b, c) for b in range(len(kv_blocks)) for c in range(len(waves))]

    def scores(b, c):
        return jnp.concatenate([_dot_nt(q2s[g], kv_blocks[b][0][g]) for g in waves[c]], axis=0)

    def softplus_pieces(c, z):
        sp = jnp.maximum(z, 0.0) + jnp.log(1.0 + jnp.exp2(jnp.abs(z) * NEG_LOG2_E))
        if mask is not None:
            sp = jnp.where(mask[rows[c]], sp, 0.0)
        return jnp.concatenate(_split(sp, 2), axis=1)

    def weights(c, z, from_s, later):
        w = jnp.exp(z + from_s + jnp.concatenate([later] * (z.shape[1] // LANES), axis=1))
        if mask is not None:
            w = jnp.where(mask[rows[c]], w, 0.0)
        return w.astype(BF16)

    def values(b, c, w):
        return jnp.concatenate([_dot(w[i * m:(i + 1) * m], kv_blocks[b][1][g])
                                for i, g in enumerate(waves[c])], axis=0)

    n = len(units)
    z, pieces, from_s, w = ([None] * n for _ in range(4))
    later = [carry[r] for r in rows]
    pv = [None] * len(waves)
    for tick in range(n + 4):
        for u, (b, c) in enumerate(units):
            stage = tick - u
            if stage == 0:
                z[u] = scores(b, c)
            elif stage == 1:
                pieces[u] = softplus_pieces(c, z[u])
            elif stage == 2:
                from_s[u] = _dot(pieces[u], neg_lower2)
            elif stage == 3:
                w[u] = weights(c, z[u], from_s[u], later[c])
                later[c] = later[c] + jnp.broadcast_to(from_s[u][:, 0:1], later[c].shape)
            elif stage == 4:
                out = values(b, c, w[u])
                pv[c] = out if pv[c] is None else pv[c] + out
    return jnp.concatenate(later, axis=0), acc + jnp.concatenate(pv, axis=0)


def _sb_consts(bk):
    r = lax.broadcasted_iota(jnp.int32, (2, bk, bk), 1).reshape(2 * bk, bk)
    c = lax.broadcasted_iota(jnp.int32, (2 * bk, bk), 1)
    return jnp.where(r >= c, -1.0, 0.0).astype(BF16)


def _sb_stack_heads(q_ref, q2_ref):
    tq = q_ref.shape[0]
    lane = lax.broadcasted_iota(jnp.int32, (tq, LANES), 1)
    for g in range(q2_ref.shape[0]):
        q = q_ref[:, g * LANES:(g + 1) * LANES]
        zero = jnp.zeros_like(q)
        q2_ref[g] = jnp.concatenate([jnp.where(lane < LANES // 2, q, zero),
                                     jnp.where(lane >= LANES // 2, q, zero)], axis=0)


def _sb_causal_mask(groups, tq, bk, t0=0):
    t = lax.broadcasted_iota(jnp.int32, (2 * groups, tq, bk), 1).reshape(2 * groups * tq, bk)
    s = lax.broadcasted_iota(jnp.int32, (2 * groups * tq, bk), 1)
    return s < t + t0


def _sb_write(o_ref, acc_ref):
    tq = o_ref.shape[0]
    lane = lax.broadcasted_iota(jnp.int32, (tq, LANES), 1)
    for g in range(acc_ref.shape[0] // (2 * tq)):
        o_ref[:, g * LANES:(g + 1) * LANES] = jnp.where(
            lane < LANES // 2, acc_ref[2 * g * tq:(2 * g + 1) * tq, :],
            acc_ref[(2 * g + 1) * tq:(2 * g + 2) * tq, :]).astype(o_ref.dtype)


def _sb_prompt_kernel(q_ref, k_ref, v_ref, o_ref, q2_ref, carry_ref, acc_ref):
    blk = q_ref.shape[0]
    groups = q2_ref.shape[0]
    i = pl.program_id(2)
    neg_lower2 = _sb_consts(blk)
    _sb_stack_heads(q_ref, q2_ref)
    lanes = [slice(g * LANES, (g + 1) * LANES) for g in range(groups)]

    def block(j):
        row0 = pl.multiple_of(j * blk, blk)
        return ([k_ref[pl.ds(row0, blk), sl] for sl in lanes], [v_ref[pl.ds(row0, blk), sl] for sl in lanes])

    def step(blocks, carry, acc, mask):
        return _sb_step([q2_ref[g] for g in range(groups)], blocks, carry, acc, neg_lower2, mask, SB_WAVES)

    half = blk // 2
    d0 = pl.multiple_of(i * blk, blk)
    zero = jnp.zeros((groups * blk, LANES), F32)
    halves = []
    for t0, keys, consts in ((half, blk, neg_lower2), (0, half, _sb_consts(half))):
        q2_half = [jnp.concatenate([q2_ref[g, t0:t0 + half, :], q2_ref[g, blk + t0:blk + t0 + half, :]], axis=0)
                   for g in range(groups)]
        own = ([k_ref[pl.ds(d0, keys), sl] for sl in lanes], [v_ref[pl.ds(d0, keys), sl] for sl in lanes])
        halves.append(_sb_step(q2_half, [own], zero, zero, consts, _sb_causal_mask(groups, half, keys, t0),
                               SB_WAVES))
    (carry_late, acc_late), (carry_early, acc_early) = halves
    for g in range(groups):
        for h in range(2):
            src = slice((2 * g + h) * half, (2 * g + h + 1) * half)
            dst = (2 * g + h) * blk
            for ref, early, late in ((carry_ref, carry_early, carry_late), (acc_ref, acc_early, acc_late)):
                ref[dst:dst + half, :] = early[src]
                ref[dst + half:dst + blk, :] = late[src]

    odd = i % SB_UNROLL

    def single(it, c):
        carry_ref[...], acc_ref[...] = step([block(i - 1 - it)], carry_ref[...], acc_ref[...], None)
        return c

    def multi(it, c):
        top = i - 1 - odd - it * SB_UNROLL
        carry_ref[...], acc_ref[...] = step([block(top - u) for u in range(SB_UNROLL)],
                                            carry_ref[...], acc_ref[...], None)
        return c

    lax.fori_loop(0, odd, single, 0)
    lax.fori_loop(0, i // SB_UNROLL, multi, 0)
    _sb_write(o_ref, acc_ref)


def _sb_attention_prompt(qkv, batch):
    rows, d3 = qkv.shape
    d = d3 // 3
    seq = rows // batch
    blk = min(SB_BLOCK, seq)
    nq = seq // blk
    width = SB_PAIRS * LANES
    ng = d // width
    return pl.pallas_call(
        _sb_prompt_kernel,
        grid=(batch, ng, nq),
        in_specs=[pl.BlockSpec((blk, width), lambda b, p, i: (b * nq + i, p)),
                  pl.BlockSpec((seq, width), lambda b, p, i: (b, ng + p)),
                  pl.BlockSpec((seq, width), lambda b, p, i: (b, 2 * ng + p))],
        out_specs=pl.BlockSpec((blk, width), lambda b, p, i: (b * nq + i, p)),
        out_shape=jax.ShapeDtypeStruct((rows, d), BF16),
        scratch_shapes=[pltpu.VMEM((SB_PAIRS, 2 * blk, LANES), BF16),
                        pltpu.VMEM((SB_PAIRS * 2 * blk, LANES), F32),
                        pltpu.VMEM((SB_PAIRS * 2 * blk, LANES), F32)],
        compiler_params=_params("arbitrary", "arbitrary", "arbitrary"),
        name="sb_attention_prompt",
    )(qkv, qkv, qkv)


def _sb_sample_kernel(q_ref, kn_ref, vn_ref, kp_ref, vp_ref, o_ref, q2_ref, carry_ref, acc_ref, *, blk):
    tq = q_ref.shape[0]
    groups = q2_ref.shape[0]
    c = pl.program_id(1)
    neg_lower2 = _sb_consts(blk)

    lanes = [slice(g * LANES, (g + 1) * LANES) for g in range(groups)]

    @pl.when(c == 0)
    def _():
        _sb_stack_heads(q_ref, q2_ref)
        zero = jnp.zeros(carry_ref.shape, F32)
        carry_ref[...], acc_ref[...] = _sb_step(
            [q2_ref[g] for g in range(groups)],
            [([kn_ref[0, :, sl] for sl in lanes], [vn_ref[0, :, sl] for sl in lanes])],
            zero, zero, neg_lower2, _sb_causal_mask(groups, tq, blk), SB_DECODE_WAVES)

    blocks = []
    for sb in reversed(range(kp_ref.shape[2] // blk)):
        rows = slice(sb * blk, (sb + 1) * blk)
        blocks.append(([kp_ref[0, 0, rows, sl].astype(BF16) for sl in lanes],
                       [vp_ref[0, 0, rows, sl].astype(BF16) for sl in lanes]))
    carry_ref[...], acc_ref[...] = _sb_step([q2_ref[g] for g in range(groups)], blocks,
                                            carry_ref[...], acc_ref[...], neg_lower2, None, SB_DECODE_WAVES)

    @pl.when(c == pl.num_programs(1) - 1)
    def _():
        _sb_write(o_ref, acc_ref)


def _sb_attention_sample(qkv, batch, past_k, past_v, layer):
    rows, d3 = qkv.shape
    d = d3 // 3
    seq = rows // batch
    blk = SB_BLOCK
    pairs = d // LANES
    past_len = past_k.shape[2]
    span = min(SB_PAST_SPAN, past_len)
    steps = past_len // span
    assert seq <= blk and past_len % span == 0 and span % blk == 0
    new = qkv[:, d:].reshape(batch, seq, 2 * d)
    new = jnp.pad(new, ((0, 0), (0, blk - seq), (0, 0)))
    return pl.pallas_call(
        functools.partial(_sb_sample_kernel, blk=blk),
        grid=(batch, steps),
        in_specs=[pl.BlockSpec((seq, d), lambda b, c: (b, 0)),
                  pl.BlockSpec((1, blk, d), lambda b, c: (b, 0, 0)),
                  pl.BlockSpec((1, blk, d), lambda b, c: (b, 0, 1)),
                  pl.BlockSpec((1, 1, span, d), lambda b, c: (layer, b, steps - 1 - c, 0)),
                  pl.BlockSpec((1, 1, span, d), lambda b, c: (layer, b, steps - 1 - c, 0))],
        out_specs=pl.BlockSpec((seq, d), lambda b, c: (b, 0)),
        out_shape=jax.ShapeDtypeStruct((rows, d), BF16),
        scratch_shapes=[pltpu.VMEM((pairs, 2 * seq, LANES), BF16),
                        pltpu.VMEM((pairs * 2 * seq, LANES), F32),
                        pltpu.VMEM((pairs * 2 * seq, LANES), F32)],
        compiler_params=_params("arbitrary", "arbitrary"),
        name="sb_attention_sample",
    )(qkv, new, new, past_k, past_v)


def _xattn_kernel(a_ref, wa_ref, x_ref, g_ref, wq_ref, wo_ref, mk_ref, mv_ref, o_ref, *, heads):
    x = x_ref[...] + _dot(a_ref[...], wa_ref[...])
    d = x.shape[-1]
    dh = d // heads
    h = _rms(x, g_ref[...]).astype(BF16)
    q = _dot(h, wq_ref[0]).astype(BF16)
    nb = mk_ref.shape[1]
    seq = x.shape[0] // nb
    per_seq = []
    for b in range(nb):
        rows = slice(b * seq, (b + 1) * seq)
        outs = []
        for hd in range(heads):
            sl = slice(hd * dh, (hd + 1) * dh)
            s = _dot_nt(q[rows, sl], mk_ref[0, b, :, sl]) * (dh ** -0.5)
            e = jnp.exp(s - jnp.max(s, axis=-1, keepdims=True))
            p = e / jnp.sum(e, axis=-1, keepdims=True)
            outs.append(_dot(p.astype(BF16), mv_ref[0, b, :, sl]).astype(BF16))
        per_seq.append(jnp.concatenate(outs, axis=-1))
    o = per_seq[0] if nb == 1 else jnp.concatenate(per_seq, axis=0)
    o_ref[...] = x + _dot(o, wo_ref[0])


def _xattn(a, wa, x, gain, wq, wo, mem_k, mem_v, layer, in_place):
    rows, d = x.shape
    _, batch, n_mem, _ = mem_k.shape
    seq = rows // batch
    nb = min(batch, max(1, WIDE_ROW_TILE // seq))
    assert batch % nb == 0
    tm = nb * seq if nb > 1 else min(XATTN_ROW_TILE, seq)
    nt = 1 if nb > 1 else seq // tm
    row_spec = pl.BlockSpec((tm, d), lambda b, i: (b * nt + i, 0))
    weight_spec = pl.BlockSpec((1, d, d), lambda b, i: (layer, 0, 0))
    mem_spec = pl.BlockSpec((1, nb, n_mem, d), lambda b, i: (layer, b, 0, 0))
    return pl.pallas_call(
        functools.partial(_xattn_kernel, heads=XA_HEADS),
        grid=(batch // nb, nt),
        in_specs=[row_spec, pl.BlockSpec((d, d), lambda b, i: (0, 0)), row_spec,
                  pl.BlockSpec((1, d), lambda b, i: (0, 0)), weight_spec, weight_spec, mem_spec, mem_spec],
        out_specs=row_spec,
        out_shape=jax.ShapeDtypeStruct((rows, d), F32),
        input_output_aliases={2: 0} if in_place else {},
        compiler_params=_params("arbitrary", "arbitrary"),
        name="xattn",
    )(a, wa, x, gain.reshape(1, d), wq, wo, mem_k, mem_v)


def _ffn_kernel(x_ref, g_ref, wgu_ref, wd_ref, o_ref, *, final_gain):
    x = x_ref[...]
    hidden = wd_ref.shape[1]
    h = _rms(x, g_ref[0:1, :]).astype(BF16)
    acc = x
    for c0 in range(0, hidden, FFN_CHUNK):
        c1 = min(c0 + FFN_CHUNK, hidden)
        gate = _dot(h, wgu_ref[0, :, c0:c1])
        up = _dot(h, wgu_ref[0, :, hidden + c0:hidden + c1])
        act = (gate * jax.nn.sigmoid(gate) * up).astype(BF16)
        acc = acc + _dot(act, wd_ref[0, c0:c1, :])
    if final_gain:
        acc = _rms(acc, g_ref[1:2, :])
    o_ref[...] = acc


def _ffn(x, gain, w_gate_up, w_down, layer, final_gain=None):
    rows, d = x.shape
    tm = min(WIDE_ROW_TILE, rows)
    gains = jnp.stack([gain, gain if final_gain is None else final_gain])
    return pl.pallas_call(
        functools.partial(_ffn_kernel, final_gain=final_gain is not None),
        grid=(rows // tm,),
        in_specs=[pl.BlockSpec((tm, d), lambda i: (i, 0)),
                  pl.BlockSpec((2, d), lambda i: (0, 0)),
                  pl.BlockSpec((1,) + w_gate_up.shape[1:], lambda i: (layer, 0, 0)),
                  pl.BlockSpec((1,) + w_down.shape[1:], lambda i: (layer, 0, 0))],
        out_specs=pl.BlockSpec((tm, d), lambda i: (i, 0)),
        out_shape=jax.ShapeDtypeStruct((rows, d), F32),
        input_output_aliases={0: 0},
        compiler_params=_params("arbitrary"),
        name="ffn",
    )(x, gains, w_gate_up, w_down)


def _rw_proj_kernel(*refs, has_v_res):
    if has_v_res:
        (x_ref, gain_ref, shift0_ref, mu_ref, wr_ref, wk_ref, wv_ref, vec_ref,
         w1_ref, w2_ref, a1_ref, a2_ref, g1_ref, g2_ref, v1_ref, v2_ref, vfirst_ref,
         r_ref, lw_ref, k_ref, v_ref, kk_ref, a_ref, g_ref, shift_ref, prev_ref) = refs
    else:
        (x_ref, gain_ref, shift0_ref, mu_ref, wr_ref, wk_ref, wv_ref, vec_ref,
         w1_ref, w2_ref, a1_ref, a2_ref, g1_ref, g2_ref,
         r_ref, lw_ref, k_ref, v_ref, kk_ref, a_ref, g_ref, shift_ref, prev_ref) = refs
    i = pl.program_id(1)
    h = _rms(x_ref[...], gain_ref[...])
    tm = h.shape[0]

    @pl.when(i == 0)
    def _():
        prev_ref[...] = shift0_ref[0]

    row = lax.broadcasted_iota(jnp.int32, h.shape, 0)
    h_prev = jnp.where(row == 0, prev_ref[...], pltpu.roll(h, 1, 0))
    last = h[tm - 1:tm, :]
    prev_ref[...] = last
    shift_ref[0] = last

    xx = h_prev - h
    mixed = lambda j: (h + xx * mu_ref[j:j + 1, :]).astype(BF16)
    w0, a0, k_k, k_a, v0 = (vec_ref[j:j + 1, :] for j in range(5))

    r_ref[...] = _dot(mixed(0), wr_ref[...]).astype(r_ref.dtype)
    dec_in = _dot(mixed(1), w1_ref[...])
    k = _dot(mixed(2), wk_ref[...])
    dec = w0 + _dot(jnp.tanh(dec_in).astype(BF16), w2_ref[...])
    x_v = mixed(3)
    v = _dot(x_v, wv_ref[...])
    lw_ref[...] = -jnp.exp(-jax.nn.softplus(-dec) - 0.5)
    if has_v_res:
        mix = jax.nn.sigmoid(v0 + _dot(_dot(x_v, v1_ref[...]).astype(BF16), v2_ref[...]))
        v = v + (vfirst_ref[...] - v) * mix
    v_ref[...] = v
    a_in = _dot(mixed(4), a1_ref[...])
    g_in = _dot(mixed(5), g1_ref[...])
    a = jax.nn.sigmoid(a0 + _dot(a_in.astype(BF16), a2_ref[...]))
    kk_ref[...] = (k * k_k).astype(kk_ref.dtype)
    k_ref[...] = (k * (1.0 + (a - 1.0) * k_a)).astype(k_ref.dtype)
    a_ref[...] = a.astype(a_ref.dtype)
    g_ref[...] = _dot(jax.nn.sigmoid(g_in).astype(BF16), g2_ref[...]).astype(g_ref.dtype)


def _pad_cols(w, mult=LANES):
    pad = -w.shape[1] % mult
    return jnp.pad(w, ((0, 0), (0, pad))) if pad else w


def _pad_rows(w, mult=LANES):
    pad = -w.shape[0] % mult
    return jnp.pad(w, ((0, pad), (0, 0))) if pad else w


def _rw_proj(x, batch, gain, shift0, p, v_first):
    rows, d = x.shape
    seq = rows // batch
    tm = min(WIDE_ROW_TILE, seq)
    nt = seq // tm
    has_v_res = v_first is not None
    zeros = jnp.zeros((d,), F32)
    vec = jnp.stack([p["w0"], p["a0"], p["k_k"], p["k_a"], p["v0"] if has_v_res else zeros,
                     zeros, zeros, zeros])
    row_spec = pl.BlockSpec((tm, d), lambda b, i: (b * nt + i, 0))

    def full(arr):
        return pl.BlockSpec(arr.shape, lambda b, i: (0,) * arr.ndim, pipeline_mode=pl.Buffered(1))

    ins = [x, gain.reshape(1, d), shift0.reshape(batch, 1, d), p["mu"], p["w_r"], p["w_k"], p["w_v"], vec,
           p["w1"], p["w2"], p["a1"], p["a2"], p["g1"], p["g2"]]
    in_specs = [row_spec, full(ins[1]), pl.BlockSpec((1, 1, d), lambda b, i: (b, 0, 0))]
    in_specs += [full(a) for a in ins[3:]]
    if has_v_res:
        ins += [p["v1"], p["v2"], v_first]
        in_specs += [full(p["v1"]), full(p["v2"]), row_spec]
    tok = [jax.ShapeDtypeStruct((rows, d), dt) for dt in (BF16, F32, BF16, F32, BF16, BF16, BF16)]
    outs = pl.pallas_call(
        functools.partial(_rw_proj_kernel, has_v_res=has_v_res),
        grid=(batch, nt),
        in_specs=in_specs,
        out_specs=[row_spec] * 7 + [pl.BlockSpec((1, 1, d), lambda b, i: (b, 0, 0))],
        out_shape=tok + [jax.ShapeDtypeStruct((batch, 1, d), F32)],
        scratch_shapes=[pltpu.VMEM((1, d), F32)],
        compiler_params=_params("arbitrary", "arbitrary"),
        name="rw_proj",
    )(*ins)
    return outs[:7], outs[7].reshape(batch, d)


WKV_BASE_BLOCK = 4
WKV_DECAY_PIECES = 3
WKV_GROUP_PIECES = 2
WKV_STATE_PASSES = 1
WKV_SOLVE_PASSES = 1


def _wkv_prepare(units, c_):
    n = units[0][0].shape[0]
    sum_l = lambda x: _mm_exact_lhs(c_["tri_incl"], x, WKV_DECAY_PIECES)
    sum_g = lambda x: _mm_exact_rhs(x, c_["gsum"], WKV_GROUP_PIECES)
    solve_mm = lambda x, y, nt=False: _mm(x, y, WKV_SOLVE_PASSES, nt=nt)

    cs = [sum_l(u[1]) for u in units]
    head_sums = [sum_g(jnp.concatenate([u[4] * u[4], u[0] * u[2] * u[7][0:1, :]], axis=0)) for u in units]
    yield
    pre = []
    for (r, lw, k, v, kk, a, g, vec), cs_u, hs in zip(units, cs, head_sums):
        cs_last = cs_u[n - 1:n, :]
        d_inv = jnp.exp(-cs_u)
        d_rest = jnp.exp(cs_last - cs_u)
        kkn = kk * lax.rsqrt(jnp.maximum(hs[:n], 1e-24))
        b = kkn * a
        pre.append(dict(a_t=-kkn * jnp.exp(cs_u - lw), r_t=r * jnp.exp(cs_u), b_t=b * d_inv, k_t=k * d_inv,
                        bk_rest=jnp.concatenate([b * d_rest, k * d_rest], axis=0), decay=jnp.exp(cs_last),
                        v=v, g=g, vec=vec, bonus=hs[n:] * v))
    yield

    by_head = lambda x: jnp.concatenate([jnp.where(m, x, 0.0) for m in c_["head_masks"]], axis=0)
    block_diag = lambda w: jnp.where(c_["same_block"], jnp.concatenate([w, w], axis=0), 0.0)
    grams = [solve_mm(jnp.concatenate([q["a_t"], q["r_t"]], axis=0),
                      jnp.concatenate([by_head(q["b_t"]), by_head(q["k_t"])], axis=0), nt=True)
             for q in pre]
    yield
    lows = [jnp.where(c_["strict_w"], gm[:n, :2 * n], 0.0) for gm in grams]
    m_aks = [jnp.where(c_["strict_w"], gm[:n, 2 * n:], 0.0) for gm in grams]
    for q, gm in zip(pre, grams):
        q["m_r"] = jnp.where(c_["incl_w2"], gm[n:], 0.0)
        q["v_heads"] = by_head(q["v"])
    from_v = [solve_mm(mk, q["v_heads"]) for mk, q in zip(m_aks, pre)]
    yield
    diag = [jnp.where(c_["diag_block"], lo, 0.0) for lo in lows]
    inv = [c_["eye_w"] + dg for dg in diag]
    sq = [solve_mm(dg, block_diag(dg)) for dg in diag]
    yield
    inv = [t + solve_mm(t, block_diag(p2)) for t, p2 in zip(inv, sq)]
    yield
    for below in c_["below_blocks"]:
        half = [solve_mm(t, block_diag(jnp.where(below, lo, 0.0))) for t, lo in zip(inv, lows)]
        yield
        inv = [t + solve_mm(hf, block_diag(t)) for t, hf in zip(inv, half)]
        yield
    for q, t, fv in zip(pre, inv, from_v):
        q["inv"], q["from_v"] = t, fv
    return pre


def _wkv_advance(pre, states, c_):
    n = pre[0]["v"].shape[0]
    inv_hd = 1.0 / RW_HEAD_DIM
    sum_g = lambda x: _mm_exact_rhs(x, c_["gsum"], WKV_GROUP_PIECES)
    solve_mm = lambda x, y: _mm(x, y, WKV_SOLVE_PASSES)
    by_head = lambda x: jnp.concatenate([jnp.where(m, x, 0.0) for m in c_["head_masks"]], axis=0)

    from_state = [_mm(jnp.concatenate([q["a_t"], q["r_t"]], axis=0), s, WKV_STATE_PASSES, nt=True)
                  for q, s in zip(pre, states)]
    yield
    sig_p = [solve_mm(q["inv"], by_head(fs[:n] + q["from_v"])) for q, fs in zip(pre, from_state)]
    yield
    y_p = [fs[n:] + solve_mm(q["m_r"], jnp.concatenate([by_head(sg), q["v_heads"]], axis=0))
           for q, fs, sg in zip(pre, from_state, sig_p)]
    upd = [_mm(jnp.concatenate([s, q["v"]], axis=0).T, q["bk_rest"], WKV_STATE_PASSES)
           for s, q in zip(sig_p, pre)]
    new_states = [s * q["decay"] + jnp.where(c_["same_head"], u, 0.0) for s, q, u in zip(states, pre, upd)]

    yield
    mean = [sum_g(y) * inv_hd for y in y_p]
    dev = [y - m for y, m in zip(y_p, mean)]
    yield
    var = [sum_g(dv * dv) * inv_hd for dv in dev]
    yield
    outs = []
    for q, dv, vr in zip(pre, dev, var):
        o = dv * lax.rsqrt(vr + RW_GN_EPS) * q["vec"][1:2, :] + q["vec"][2:3, :]
        outs.append((o + q["bonus"]) * q["g"])
    return outs, new_states


def _wkv_advance_chunks(pre_chunks, states, c_):
    outs = []
    for pre in pre_chunks:
        out, states = yield from _wkv_advance(pre, states, c_)
        outs.append(jnp.concatenate(out, axis=1))
    return outs, states


def _in_lock_step(*staged):
    results = [None] * len(staged)
    live = list(enumerate(staged))
    while live:
        still = []
        for i, gen in live:
            try:
                next(gen)
                still.append((i, gen))
            except StopIteration as done:
                results[i] = done.value
        live = still
    return results


def _wkv_kernel(*refs, has_state0, n):
    if has_state0:
        (r_ref, lw_ref, k_ref, v_ref, kk_ref, a_ref, g_ref, vec_ref, s0_ref,
         o_ref, sout_ref, state_ref) = refs
    else:
        (r_ref, lw_ref, k_ref, v_ref, kk_ref, a_ref, g_ref, vec_ref,
         o_ref, sout_ref, state_ref) = refs
    c = pl.program_id(2)
    hd = RW_HEAD_DIM

    @pl.when(c == 0)
    def _():
        if has_state0:
            state_ref[...] = s0_ref[0]
        else:
            state_ref[...] = jnp.zeros_like(state_ref)

    tr = lax.broadcasted_iota(jnp.int32, (n, n), 0)
    tc = lax.broadcasted_iota(jnp.int32, (n, n), 1)
    tr2 = lax.broadcasted_iota(jnp.int32, (n, 2 * n), 0)
    tc2 = lax.rem(lax.broadcasted_iota(jnp.int32, (n, 2 * n), 1), n)
    tr4 = lax.broadcasted_iota(jnp.int32, (n, 4 * n), 0)
    tc4 = lax.rem(lax.broadcasted_iota(jnp.int32, (n, 4 * n), 1), n)
    br = lax.broadcasted_iota(jnp.int32, (2 * n, 2 * n), 0) // n
    bc = lax.broadcasted_iota(jnp.int32, (2 * n, 2 * n), 1) // n
    lane = lax.broadcasted_iota(jnp.int32, (n, LANES), 1)
    sr = lax.broadcasted_iota(jnp.int32, (LANES, LANES), 0) // hd
    sc = lax.broadcasted_iota(jnp.int32, (LANES, LANES), 1) // hd
    sizes = [WKV_BASE_BLOCK << i for i in range(n.bit_length()) if (WKV_BASE_BLOCK << i) < n]
    below_blocks = []
    for s in sizes:
        u, v = tr2 // s, tc2 // s
        below_blocks.append(jnp.where((u & 1) == 1, u - 1, -1) == v)
    consts = dict(
        diag_block=(tr2 // WKV_BASE_BLOCK) == (tc2 // WKV_BASE_BLOCK),
        below_blocks=below_blocks,
        gsum=jnp.concatenate([_group_ones(LANES, hd)] * WKV_GROUP_PIECES, axis=0),
        tri_incl=jnp.concatenate([jnp.where(tc <= tr, 1.0, 0.0).astype(BF16)] * WKV_DECAY_PIECES, axis=1),
        strict_w=tc2 < tr2,
        incl_w2=tc4 <= tr4,
        eye_w=jnp.where(tc2 == tr2, 1.0, 0.0).astype(F32),
        same_block=br == bc,
        head_masks=[lane < hd, lane >= hd],
        same_head=sr == sc,
    )
    n_pairs = state_ref.shape[0]
    n_chunks = r_ref.shape[0] // n

    def units(chunks):
        out = []
        for ci in chunks:
            rows = slice(ci * n, (ci + 1) * n)
            for p in range(n_pairs):
                sl = slice(p * LANES, (p + 1) * LANES)
                out.append(tuple(ref[rows, sl].astype(F32)
                                 for ref in (r_ref, lw_ref, k_ref, v_ref, kk_ref, a_ref, g_ref)) + (vec_ref[:, sl],))
        return out

    def per_chunk(pre):
        return [pre[i:i + n_pairs] for i in range(0, len(pre), n_pairs)]

    first = range((n_chunks + 1) // 2)
    second = range(len(first), n_chunks)
    states = [state_ref[p] for p in range(n_pairs)]
    (pre_first,) = _in_lock_step(_wkv_prepare(units(first), consts))
    if second:
        (outs, states), pre_second = _in_lock_step(
            _wkv_advance_chunks(per_chunk(pre_first), states, consts), _wkv_prepare(units(second), consts))
        ((outs_second, states),) = _in_lock_step(_wkv_advance_chunks(per_chunk(pre_second), states, consts))
        outs = outs + outs_second
    else:
        ((outs, states),) = _in_lock_step(_wkv_advance_chunks(per_chunk(pre_first), states, consts))
    o_ref[...] = jnp.concatenate(outs, axis=0).astype(o_ref.dtype)
    new_states = jnp.stack(states)
    state_ref[...] = new_states
    sout_ref[0] = new_states


def _wkv(tok, batch, vec, state0):
    rows, d = tok[0].shape
    seq = rows // batch
    n = min(WKV_CHUNK, seq)
    span = n * min(WKV_CHUNKS_PER_STEP, seq // n)
    nc = seq // span
    pairs = d // LANES
    gp = WKV_PAIRS
    width = gp * LANES
    has_state0 = state0 is not None
    tok_spec = pl.BlockSpec((span, width), lambda b, p, c: (b * nc + c, p))
    state_spec = pl.BlockSpec((1, gp, LANES, LANES), lambda b, p, c: (b, p, 0, 0))
    ins = list(tok) + [vec]
    in_specs = [tok_spec] * 7 + [pl.BlockSpec((8, width), lambda b, p, c: (0, p))]
    if has_state0:
        ins.append(state0)
        in_specs.append(state_spec)
    return pl.pallas_call(
        functools.partial(_wkv_kernel, has_state0=has_state0, n=n),
        grid=(batch, pairs // gp, nc),
        in_specs=in_specs,
        out_specs=[tok_spec, state_spec],
        out_shape=[jax.ShapeDtypeStruct((rows, d), BF16),
                   jax.ShapeDtypeStruct((batch, pairs, LANES, LANES), F32)],
        scratch_shapes=[pltpu.VMEM((gp, LANES, LANES), F32)],
        compiler_params=_params("arbitrary", "arbitrary", "arbitrary"),
        name="wkv",
    )(*ins)


def _state_to_blockdiag(state):
    b, h, n, _ = state.shape
    s = state.reshape(b, h // 2, 2, n, n)
    z = jnp.zeros_like(s[:, :, 0])
    top = jnp.concatenate([s[:, :, 0], z], axis=-1)
    bot = jnp.concatenate([z, s[:, :, 1]], axis=-1)
    return jnp.concatenate([top, bot], axis=-2)


def _state_from_blockdiag(bd):
    b, p, n2, _ = bd.shape
    n = n2 // 2
    return jnp.stack([bd[:, :, :n, :n], bd[:, :, n:, n:]], axis=2).reshape(b, 2 * p, n, n)


def _trunk(x, batch, mem_k, mem_v, past_k, past_v, rw_state0, rw_shift0, w):
    depth = w["ffn_gu"].shape[0]
    d = x.shape[-1]
    sb_k, sb_v, rw_states, rw_shifts = None, None, [], []
    n_sb = (depth + 1) // 2
    v_first = None
    for layer in range(depth):
        j = layer // 2
        if layer % 2 == 0:
            sb_k, sb_v, qkv = _sb_qkv(x, batch, w["norm_mix"][layer], w["sb_qkv"][j], j, n_sb, sb_k, sb_v)
            if past_k is None:
                o = _sb_attention_prompt(qkv, batch)
            else:
                o = _sb_attention_sample(qkv, batch, past_k, past_v, j)
            w_o = w["sb_o"][j]
        else:
            if rw_state0 is None:
                state0 = None
                shift0 = jnp.zeros((batch, d), F32)
            else:
                state0 = _state_to_blockdiag(rw_state0[j])
                shift0 = rw_shift0[j]
            tok, shift = _rw_proj(x, batch, w["norm_mix"][layer], shift0, w["rw"][j], v_first)
            if v_first is None:
                v_first = tok[3]
            o, state = _wkv(tok, batch, w["rw"][j]["out_vec"], state0)
            rw_states.append(_state_from_blockdiag(state))
            rw_shifts.append(shift)
            w_o = w["rw"][j]["w_o"]
        x = _xattn(o, w_o, x, w["norm_xattn"][layer], w["xa_q"], w["xa_o"], mem_k, mem_v, layer,
                   in_place=layer > 0)
        final = w["norm_final"] if layer == depth - 1 else None
        x = _ffn(x, w["norm_ffn"][layer], w["ffn_gu"], w["ffn_down"], layer, final)
    return x, sb_k, sb_v, jnp.stack(rw_states), jnp.stack(rw_shifts)


def kernel(x_prompt, x_sample, mem_prompt, cache_sb_k, cache_sb_v, state_rwkv_wkv, state_rwkv_shift,
           cache_mem_k, cache_mem_v, norm_mix, norm_xattn, norm_ffn, norm_mem, norm_final,
           sb_w_qkv, sb_w_o, rw_mu, rw_w_r, rw_w_k, rw_w_v, rw_w_o, rw_w0, rw_w1, rw_w2,
           rw_a0, rw_a1, rw_a2, rw_v0, rw_v1, rw_v2, rw_g1, rw_g2, rw_k_k, rw_k_a, rw_r_k,
           rw_lnx_w, rw_lnx_b, xa_w_q, xa_w_kv, xa_w_o, ffn_w_gate_up, ffn_w_down):
    batch, seq, d = x_prompt.shape
    dec_batch, dec_seq, _ = x_sample.shape
    depth = norm_mix.shape[0]
    n_rw = rw_w_r.shape[0]
    n_sb = sb_w_qkv.shape[0]
    n_mem = mem_prompt.shape[1]
    bf = lambda t: t.astype(BF16)
    per_layer = lambda t: [bf(t[i]) for i in range(t.shape[0])]

    rw = []
    for j in range(n_rw):
        pj = dict(mu=rw_mu[j], w_r=bf(rw_w_r[j]), w_k=bf(rw_w_k[j]), w_v=bf(rw_w_v[j]), w_o=bf(rw_w_o[j]),
                  w0=rw_w0[j], a0=rw_a0[j], k_k=rw_k_k[j], k_a=rw_k_a[j],
                  w1=bf(_pad_cols(rw_w1[j])), w2=bf(_pad_rows(rw_w2[j])),
                  a1=bf(_pad_cols(rw_a1[j])), a2=bf(_pad_rows(rw_a2[j])),
                  g1=bf(_pad_cols(rw_g1[j])), g2=bf(_pad_rows(rw_g2[j])))
        if j > 0:
            pj.update(v0=rw_v0[j - 1], v1=bf(_pad_cols(rw_v1[j - 1])), v2=bf(_pad_rows(rw_v2[j - 1])))
        zeros = jnp.zeros((d,), F32)
        pj["out_vec"] = jnp.stack([rw_r_k[j].reshape(d), rw_lnx_w[j], rw_lnx_b[j]] + [zeros] * 5)
        rw.append(pj)
    w = dict(norm_mix=norm_mix, norm_xattn=norm_xattn, norm_ffn=norm_ffn, norm_final=norm_final,
             sb_qkv=per_layer(sb_w_qkv), sb_o=per_layer(sb_w_o), rw=rw,
             xa_q=bf(xa_w_q), xa_o=bf(xa_w_o), ffn_gu=bf(ffn_w_gate_up), ffn_down=bf(ffn_w_down))

    mem_k, mem_v, mem_k_bf, mem_v_bf = _mem_kv(mem_prompt, norm_mem, bf(xa_w_kv))
    y_p, p_sb_k, p_sb_v, p_wkv, p_shift = _trunk(
        x_prompt.reshape(batch * seq, d), batch, mem_k_bf, mem_v_bf, None, None, None, None, w)

    y_s, s_sb_k, s_sb_v, s_wkv, s_shift = _trunk(
        x_sample.reshape(dec_batch * dec_seq, d), dec_batch,
        bf(cache_mem_k).reshape(depth, dec_batch, n_mem, d), bf(cache_mem_v).reshape(depth, dec_batch, n_mem, d),
        cache_sb_k.reshape(n_sb, dec_batch, -1, d), cache_sb_v.reshape(n_sb, dec_batch, -1, d),
        state_rwkv_wkv, state_rwkv_shift, w)

    hd = d // SB_HEADS
    return (y_p.reshape(batch, seq, d), y_s.reshape(dec_batch, dec_seq, d),
            p_sb_k.reshape(n_sb, batch, seq, SB_HEADS, hd), p_sb_v.reshape(n_sb, batch, seq, SB_HEADS, hd),
            p_wkv, p_shift,
            mem_k, mem_v,
            s_sb_k.reshape(n_sb, dec_batch, dec_seq, SB_HEADS, hd),
            s_sb_v.reshape(n_sb, dec_batch, dec_seq, SB_HEADS, hd),
            s_wkv, s_shift)
```

```python
import functools

import jax
import jax.numpy as jnp
from jax import lax
from jax.experimental import pallas as pl
from jax.experimental.pallas import tpu as pltpu

F32 = jnp.float32
BF16 = jnp.bfloat16

NORM_EPS = 1e-6
NEG_LOG2_E = -1.4426950408889634
RW_GN_EPS = 64e-5
SB_HEADS = 16
XA_HEADS = 4
RW_HEAD_DIM = 64
LANES = 128
ROW_TILE = 256
WIDE_ROW_TILE = 512
XATTN_ROW_TILE = 1024
SB_BLOCK = 256
SB_PAIRS = 8
SB_WAVES = 4
SB_DECODE_WAVES = 2
SB_UNROLL = 2
SB_PAST_SPAN = 2048
WKV_CHUNK = 64
WKV_CHUNKS_PER_STEP = 4
WKV_PAIRS = 8
FFN_CHUNK = 256
VMEM_LIMIT_BYTES = 56 * 1024 * 1024


def _params(*sem):
    return pltpu.CompilerParams(dimension_semantics=sem, vmem_limit_bytes=VMEM_LIMIT_BYTES)


def _dot(a, b):
    return jnp.dot(a, b, preferred_element_type=F32)


def _dot_nt(a, b):
    return lax.dot_general(a, b, (((1,), (1,)), ((), ())), preferred_element_type=F32)


def _split(x, n):
    parts = []
    for i in range(n):
        p = x.astype(BF16)
        parts.append(p)
        if i + 1 < n:
            x = x - p.astype(F32)
    return parts


def _mm(a, b, passes, nt=False):
    dot = _dot_nt if nt else _dot
    if passes == 1:
        return dot(a.astype(BF16), b.astype(BF16))
    a_hi, a_lo = _split(a, 2)
    b_hi, b_lo = _split(b, 2)
    return dot(a_hi, b_hi) + (dot(a_hi, b_lo) + dot(a_lo, b_hi))


def _mm_exact_rhs(a, b_stacked, n):
    return _dot(jnp.concatenate(_split(a, n), axis=1), b_stacked)


def _mm_exact_lhs(a_repeated, b, n):
    return _dot(a_repeated, jnp.concatenate(_split(b, n), axis=0))


def _rms(x, gain):
    ms = jnp.mean(x * x, axis=-1, keepdims=True)
    return x * lax.rsqrt(ms + NORM_EPS) * gain


def _group_ones(n, group):
    r = lax.broadcasted_iota(jnp.int32, (n, n), 0) // group
    c = lax.broadcasted_iota(jnp.int32, (n, n), 1) // group
    return jnp.where(r == c, 1.0, 0.0).astype(BF16)


def _mem_kv_kernel(x_ref, g_ref, w_ref, k_ref, v_ref, kb_ref, vb_ref):
    tm, d = x_ref.shape
    heads, dh = k_ref.shape[-2:]
    h = _rms(x_ref[...], g_ref[0]).astype(BF16)
    kv = _dot(h, w_ref[0])
    k, v = kv[:, :d], kv[:, d:]
    k_ref[0, 0] = k.reshape(tm, heads, dh)
    v_ref[0, 0] = v.reshape(tm, heads, dh)
    kb_ref[0, 0] = k.astype(BF16)
    vb_ref[0, 0] = v.astype(BF16)


def _mem_kv(mem, gains, w_kv):
    depth = w_kv.shape[0]
    batch, n_mem, d = mem.shape
    tm = min(ROW_TILE, n_mem)
    nt = n_mem // tm
    dh = d // XA_HEADS
    assert n_mem % tm == 0 and dh % LANES == 0
    out = jax.ShapeDtypeStruct((depth, batch, n_mem, XA_HEADS, dh), F32)
    out_b = jax.ShapeDtypeStruct((depth, batch, n_mem, d), BF16)
    spec = pl.BlockSpec((1, 1, tm, XA_HEADS, dh), lambda l, i: (l, i // nt, i % nt, 0, 0))
    spec_b = pl.BlockSpec((1, 1, tm, d), lambda l, i: (l, i // nt, i % nt, 0))
    return pl.pallas_call(
        _mem_kv_kernel,
        grid=(depth, batch * nt),
        in_specs=[pl.BlockSpec((tm, d), lambda l, i: (i, 0)),
                  pl.BlockSpec((1, 1, d), lambda l, i: (l, 0, 0)),
                  pl.BlockSpec((1, d, 2 * d), lambda l, i: (l, 0, 0))],
        out_specs=[spec, spec, spec_b, spec_b],
        out_shape=[out, out, out_b, out_b],
        compiler_params=_params("arbitrary", "arbitrary"),
        name="mem_kv",
    )(mem.reshape(batch * n_mem, d), gains.reshape(depth, 1, d), w_kv)


def _split_heads(x, hd):
    tm, d = x.shape
    blocks = []
    for p in range(d // LANES):
        slab = x[:, p * LANES:(p + 1) * LANES]
        blocks += [slab, pltpu.roll(slab, LANES - hd, 1)]
    return jnp.concatenate(blocks, axis=1).reshape(tm, d // hd, LANES)[:, :, :hd]


def _sb_qkv_kernel(x_ref, g_ref, w_ref, *rest, q_scale, slot):
    k_ref, v_ref, qkv_ref = rest[-3:]
    d = x_ref.shape[-1]
    hd = k_ref.shape[-1]
    h = _rms(x_ref[...], g_ref[...]).astype(BF16)
    own = slot if k_ref.shape[0] > 1 else 0
    for s in range(k_ref.shape[0]):
        if s != own:
            k_ref[s, 0] = jnp.zeros(k_ref.shape[2:], k_ref.dtype)
            v_ref[s, 0] = jnp.zeros(v_ref.shape[2:], v_ref.dtype)
    k = _dot(h, w_ref[:, d:2 * d])
    qkv_ref[:, d:2 * d] = k.astype(BF16)
    k_ref[own, 0] = _split_heads(k, hd)
    v = _dot(h, w_ref[:, 2 * d:])
    qkv_ref[:, 2 * d:] = v.astype(BF16)
    v_ref[own, 0] = _split_heads(v, hd)
    qkv_ref[:, :d] = (_dot(h, w_ref[:, :d]) * q_scale).astype(BF16)


def _sb_qkv(x, batch, gain, w, slot, n_slots, k_all=None, v_all=None):
    rows, d = x.shape
    seq = rows // batch
    tm = min(ROW_TILE, seq)
    nt = seq // tm
    hd = d // SB_HEADS
    assert 2 * hd == LANES
    q_scale = float(hd ** -0.5)
    ins = [x, gain.reshape(1, d), w]
    in_specs = [pl.BlockSpec((tm, d), lambda i: (i, 0)),
                pl.BlockSpec((1, d), lambda i: (0, 0)),
                pl.BlockSpec(w.shape, lambda i: (0, 0))]
    if k_all is None:
        aliases = {}
        slot_spec = pl.BlockSpec((n_slots, 1, tm, SB_HEADS, hd), lambda i: (0, i // nt, i % nt, 0, 0))
    else:
        ins += [k_all, v_all]
        in_specs += [pl.BlockSpec(memory_space=pl.ANY)] * 2
        aliases = {3: 0, 4: 1}
        slot_spec = pl.BlockSpec((1, 1, tm, SB_HEADS, hd), lambda i: (slot, i // nt, i % nt, 0, 0))
    kv_shape = jax.ShapeDtypeStruct((n_slots, batch, seq, SB_HEADS, hd), F32)
    return pl.pallas_call(
        functools.partial(_sb_qkv_kernel, q_scale=q_scale, slot=slot),
        grid=(rows // tm,),
        in_specs=in_specs,
        out_specs=[slot_spec, slot_spec, pl.BlockSpec((tm, 3 * d), lambda i: (i, 0))],
        out_shape=[kv_shape, kv_shape, jax.ShapeDtypeStruct((rows, 3 * d), BF16)],
        input_output_aliases=aliases,
        compiler_params=_params("arbitrary"),
        name="sb_qkv",
    )(*ins)


def _sb_step(q2s, kv_blocks, carry, acc, neg_lower2, mask, n_waves):
    m = q2s[0].shape[0]
    groups = len(q2s)
    per = max(groups // n_waves, 1)
    waves = [range(c, min(c + per, groups)) for c in range(0, groups, per)]
    rows = [slice(w_[0] * m, (w_[-1] + 1) * m) for w_ in waves]
    units = [(b, c) for b in range(len(kv_blocks)) for c in range(len(waves))]

    def scores(b, c):
        keys, _, keys_on_lanes = kv_blocks[b]
        dot = _dot if keys_on_lanes else _dot_nt
        return jnp.concatenate([dot(q2s[g], keys[g]) for g in waves[c]], axis=0)

    def softplus_pieces(c, z):
        sp = jnp.maximum(z, 0.0) + jnp.log(1.0 + jnp.exp2(jnp.abs(z) * NEG_LOG2_E))
        if mask is not None:
            sp = jnp.where(mask[rows[c]], sp, 0.0)
        return jnp.concatenate(_split(sp, 2), axis=1)

    def weights(c, z, from_s, later):
        w = jnp.exp(z + from_s + jnp.concatenate([later] * (z.shape[1] // LANES), axis=1))
        if mask is not None:
            w = jnp.where(mask[rows[c]], w, 0.0)
        return w.astype(BF16)

    def values(b, c, w):
        _, vals, keys_on_lanes = kv_blocks[b]
        dot = _dot_nt if keys_on_lanes else _dot
        return jnp.concatenate([dot(w[i * m:(i + 1) * m], vals[g]) for i, g in enumerate(waves[c])], axis=0)

    n = len(units)
    z, pieces, from_s, w = ([None] * n for _ in range(4))
    later = [carry[r] for r in rows]
    pv = [None] * len(waves)
    for tick in range(n + 4):
        for u, (b, c) in enumerate(units):
            stage = tick - u
            if stage == 0:
                z[u] = scores(b, c)
            elif stage == 1:
                pieces[u] = softplus_pieces(c, z[u])
            elif stage == 2:
                from_s[u] = _dot(pieces[u], neg_lower2)
            elif stage == 3:
                w[u] = weights(c, z[u], from_s[u], later[c])
                later[c] = later[c] + jnp.broadcast_to(from_s[u][:, 0:1], later[c].shape)
            elif stage == 4:
                out = values(b, c, w[u])
                pv[c] = out if pv[c] is None else pv[c] + out
    return jnp.concatenate(later, axis=0), acc + jnp.concatenate(pv, axis=0)


def _sb_consts(bk):
    r = lax.broadcasted_iota(jnp.int32, (2, bk, bk), 1).reshape(2 * bk, bk)
    c = lax.broadcasted_iota(jnp.int32, (2 * bk, bk), 1)
    return jnp.where(r >= c, -1.0, 0.0).astype(BF16)


def _sb_stack_heads(q_ref, q2_ref):
    tq = q_ref.shape[0]
    lane = lax.broadcasted_iota(jnp.int32, (tq, LANES), 1)
    for g in range(q2_ref.shape[0]):
        q = q_ref[:, g * LANES:(g + 1) * LANES]
        zero = jnp.zeros_like(q)
        q2_ref[g] = jnp.concatenate([jnp.where(lane < LANES // 2, q, zero),
                                     jnp.where(lane >= LANES // 2, q, zero)], axis=0)


def _sb_causal_mask(groups, tq, bk, t0=0):
    t = lax.broadcasted_iota(jnp.int32, (2 * groups, tq, bk), 1).reshape(2 * groups * tq, bk)
    s = lax.broadcasted_iota(jnp.int32, (2 * groups * tq, bk), 1)
    return s < t + t0


def _sb_write(o_ref, acc_ref):
    tq = o_ref.shape[0]
    lane = lax.broadcasted_iota(jnp.int32, (tq, LANES), 1)
    for g in range(acc_ref.shape[0] // (2 * tq)):
        o_ref[:, g * LANES:(g + 1) * LANES] = jnp.where(
            lane < LANES // 2, acc_ref[2 * g * tq:(2 * g + 1) * tq, :],
            acc_ref[(2 * g + 1) * tq:(2 * g + 2) * tq, :]).astype(o_ref.dtype)


def _sb_prompt_kernel(q_ref, k_ref, v_ref, o_ref, q2_ref, carry_ref, acc_ref):
    blk = q_ref.shape[0]
    groups = q2_ref.shape[0]
    i = pl.program_id(2)
    neg_lower2 = _sb_consts(blk)
    _sb_stack_heads(q_ref, q2_ref)
    lanes = [slice(g * LANES, (g + 1) * LANES) for g in range(groups)]

    def block(j):
        row0 = pl.multiple_of(j * blk, blk)
        return ([k_ref[pl.ds(row0, blk), sl] for sl in lanes], [v_ref[pl.ds(row0, blk), sl] for sl in lanes], False)

    def step(blocks, carry, acc, mask):
        return _sb_step([q2_ref[g] for g in range(groups)], blocks, carry, acc, neg_lower2, mask, SB_WAVES)

    half = blk // 2
    d0 = pl.multiple_of(i * blk, blk)
    zero = jnp.zeros((groups * blk, LANES), F32)
    halves = []
    for t0, keys, consts in ((half, blk, neg_lower2), (0, half, _sb_consts(half))):
        q2_half = [jnp.concatenate([q2_ref[g, t0:t0 + half, :], q2_ref[g, blk + t0:blk + t0 + half, :]], axis=0)
                   for g in range(groups)]
        own = ([k_ref[pl.ds(d0, keys), sl] for sl in lanes], [v_ref[pl.ds(d0, keys), sl] for sl in lanes], False)
        halves.append(_sb_step(q2_half, [own], zero, zero, consts, _sb_causal_mask(groups, half, keys, t0),
                               SB_WAVES))
    (carry_late, acc_late), (carry_early, acc_early) = halves
    for g in range(groups):
        for h in range(2):
            src = slice((2 * g + h) * half, (2 * g + h + 1) * half)
            dst = (2 * g + h) * blk
            for ref, early, late in ((carry_ref, carry_early, carry_late), (acc_ref, acc_early, acc_late)):
                ref[dst:dst + half, :] = early[src]
                ref[dst + half:dst + blk, :] = late[src]

    odd = i % SB_UNROLL

    def single(it, c):
        carry_ref[...], acc_ref[...] = step([block(i - 1 - it)], carry_ref[...], acc_ref[...], None)
        return c

    def multi(it, c):
        top = i - 1 - odd - it * SB_UNROLL
        carry_ref[...], acc_ref[...] = step([block(top - u) for u in range(SB_UNROLL)],
                                            carry_ref[...], acc_ref[...], None)
        return c

    lax.fori_loop(0, odd, single, 0)
    lax.fori_loop(0, i // SB_UNROLL, multi, 0)
    _sb_write(o_ref, acc_ref)


def _sb_attention_prompt(qkv, batch):
    rows, d3 = qkv.shape
    d = d3 // 3
    seq = rows // batch
    blk = min(SB_BLOCK, seq)
    nq = seq // blk
    width = SB_PAIRS * LANES
    ng = d // width
    return pl.pallas_call(
        _sb_prompt_kernel,
        grid=(batch, ng, nq),
        in_specs=[pl.BlockSpec((blk, width), lambda b, p, i: (b * nq + i, p)),
                  pl.BlockSpec((seq, width), lambda b, p, i: (b, ng + p)),
                  pl.BlockSpec((seq, width), lambda b, p, i: (b, 2 * ng + p))],
        out_specs=pl.BlockSpec((blk, width), lambda b, p, i: (b * nq + i, p)),
        out_shape=jax.ShapeDtypeStruct((rows, d), BF16),
        scratch_shapes=[pltpu.VMEM((SB_PAIRS, 2 * blk, LANES), BF16),
                        pltpu.VMEM((SB_PAIRS * 2 * blk, LANES), F32),
                        pltpu.VMEM((SB_PAIRS * 2 * blk, LANES), F32)],
        compiler_params=_params("arbitrary", "arbitrary", "arbitrary"),
        name="sb_attention_prompt",
    )(qkv, qkv, qkv)


def _sb_sample_kernel(q_ref, kn_ref, vn_ref, kp_ref, vp_ref, o_ref, q2_ref, carry_ref, acc_ref, *, blk):
    tq = q_ref.shape[0]
    heads, _, hd = q2_ref.shape
    c = pl.program_id(1)
    neg_lower2 = _sb_consts(blk)
    lanes = [slice(h * hd, (h + 1) * hd) for h in range(heads)]

    @pl.when(c == 0)
    def _():
        for h, sl in enumerate(lanes):
            q2_ref[h] = q_ref[:, sl]
        own = ([kn_ref[0, :, sl] for sl in lanes], [vn_ref[0, :, sl] for sl in lanes], False)
        carry_ref[...], acc_ref[...] = _sb_step(
            [q2_ref[h] for h in range(heads)], [own], jnp.zeros(carry_ref.shape, F32),
            jnp.zeros(acc_ref.shape, F32), neg_lower2, _sb_causal_mask(heads // 2, tq, blk), SB_DECODE_WAVES)

    blocks = []
    for sb in reversed(range(kp_ref.shape[-1] // blk)):
        cols = slice(sb * blk, (sb + 1) * blk)
        blocks.append(([kp_ref[0, 0, h, :, cols].astype(BF16) for h in range(heads)],
                       [vp_ref[0, 0, h, :, cols].astype(BF16) for h in range(heads)], True))
    carry_ref[...], acc_ref[...] = _sb_step([q2_ref[h] for h in range(heads)], blocks,
                                            carry_ref[...], acc_ref[...], neg_lower2, None, SB_DECODE_WAVES)

    @pl.when(c == pl.num_programs(1) - 1)
    def _():
        o_ref[...] = jnp.concatenate([acc_ref[h * tq:(h + 1) * tq, :] for h in range(heads)],
                                     axis=1).astype(o_ref.dtype)


def _sb_attention_sample(qkv, batch, past_k, past_v, layer):
    rows, d3 = qkv.shape
    d = d3 // 3
    seq = rows // batch
    blk = SB_BLOCK
    _, _, past_len, heads, hd = past_k.shape
    span = min(SB_PAST_SPAN, past_len)
    steps = past_len // span
    assert seq <= blk and past_len % span == 0 and span % blk == 0 and heads % 2 == 0
    new = qkv[:, d:].reshape(batch, seq, 2 * d)
    new = jnp.pad(new, ((0, 0), (0, blk - seq), (0, 0)))
    past_k = jnp.transpose(past_k, (0, 1, 3, 4, 2))
    past_v = jnp.transpose(past_v, (0, 1, 3, 4, 2))
    past_spec = pl.BlockSpec((1, 1, heads, hd, span), lambda b, c: (layer, b, 0, 0, steps - 1 - c))
    return pl.pallas_call(
        functools.partial(_sb_sample_kernel, blk=blk),
        grid=(batch, steps),
        in_specs=[pl.BlockSpec((seq, d), lambda b, c: (b, 0)),
                  pl.BlockSpec((1, blk, d), lambda b, c: (b, 0, 0)),
                  pl.BlockSpec((1, blk, d), lambda b, c: (b, 0, 1)),
                  past_spec, past_spec],
        out_specs=pl.BlockSpec((seq, d), lambda b, c: (b, 0)),
        out_shape=jax.ShapeDtypeStruct((rows, d), BF16),
        scratch_shapes=[pltpu.VMEM((heads, seq, hd), BF16),
                        pltpu.VMEM((heads * seq, LANES), F32),
                        pltpu.VMEM((heads * seq, hd), F32)],
        compiler_params=_params("arbitrary", "arbitrary"),
        name="sb_attention_sample",
    )(qkv, new, new, past_k, past_v)


def _xattn_kernel(a_ref, wa_ref, x_ref, g_ref, wq_ref, wo_ref, mk_ref, mv_ref, o_ref, *, heads):
    x = x_ref[...] + _dot(a_ref[...], wa_ref[...])
    d = x.shape[-1]
    dh = d // heads
    h = _rms(x, g_ref[...]).astype(BF16)
    q = _dot(h, wq_ref[0]).astype(BF16)
    nb = mk_ref.shape[1]
    seq = x.shape[0] // nb
    per_seq = []
    for b in range(nb):
        rows = slice(b * seq, (b + 1) * seq)
        outs = []
        for hd in range(heads):
            sl = slice(hd * dh, (hd + 1) * dh)
            s = _dot_nt(q[rows, sl], mk_ref[0, b, :, sl]) * (dh ** -0.5)
            e = jnp.exp(s - jnp.max(s, axis=-1, keepdims=True))
            p = e / jnp.sum(e, axis=-1, keepdims=True)
            outs.append(_dot(p.astype(BF16), mv_ref[0, b, :, sl]).astype(BF16))
        per_seq.append(jnp.concatenate(outs, axis=-1))
    o = per_seq[0] if nb == 1 else jnp.concatenate(per_seq, axis=0)
    o_ref[...] = x + _dot(o, wo_ref[0])


def _xattn(a, wa, x, gain, wq, wo, mem_k, mem_v, layer, in_place):
    rows, d = x.shape
    _, batch, n_mem, _ = mem_k.shape
    seq = rows // batch
    nb = min(batch, max(1, WIDE_ROW_TILE // seq))
    assert batch % nb == 0
    tm = nb * seq if nb > 1 else min(XATTN_ROW_TILE, seq)
    nt = 1 if nb > 1 else seq // tm
    row_spec = pl.BlockSpec((tm, d), lambda b, i: (b * nt + i, 0))
    weight_spec = pl.BlockSpec((1, d, d), lambda b, i: (layer, 0, 0))
    mem_spec = pl.BlockSpec((1, nb, n_mem, d), lambda b, i: (layer, b, 0, 0))
    return pl.pallas_call(
        functools.partial(_xattn_kernel, heads=XA_HEADS),
        grid=(batch // nb, nt),
        in_specs=[row_spec, pl.BlockSpec((d, d), lambda b, i: (0, 0)), row_spec,
                  pl.BlockSpec((1, d), lambda b, i: (0, 0)), weight_spec, weight_spec, mem_spec, mem_spec],
        out_specs=row_spec,
        out_shape=jax.ShapeDtypeStruct((rows, d), F32),
        input_output_aliases={2: 0} if in_place else {},
        compiler_params=_params("arbitrary", "arbitrary"),
        name="xattn",
    )(a, wa, x, gain.reshape(1, d), wq, wo, mem_k, mem_v)


def _ffn_kernel(x_ref, g_ref, wgu_ref, wd_ref, o_ref, *, final_gain):
    x = x_ref[...]
    hidden = wd_ref.shape[1]
    h = _rms(x, g_ref[0:1, :]).astype(BF16)
    acc = x
    for c0 in range(0, hidden, FFN_CHUNK):
        c1 = min(c0 + FFN_CHUNK, hidden)
        gate = _dot(h, wgu_ref[0, :, c0:c1])
        up = _dot(h, wgu_ref[0, :, hidden + c0:hidden + c1])
        act = (gate * jax.nn.sigmoid(gate) * up).astype(BF16)
        acc = acc + _dot(act, wd_ref[0, c0:c1, :])
    if final_gain:
        acc = _rms(acc, g_ref[1:2, :])
    o_ref[...] = acc


def _ffn(x, gain, w_gate_up, w_down, layer, final_gain=None):
    rows, d = x.shape
    tm = min(WIDE_ROW_TILE, rows)
    gains = jnp.stack([gain, gain if final_gain is None else final_gain])
    return pl.pallas_call(
        functools.partial(_ffn_kernel, final_gain=final_gain is not None),
        grid=(rows // tm,),
        in_specs=[pl.BlockSpec((tm, d), lambda i: (i, 0)),
                  pl.BlockSpec((2, d), lambda i: (0, 0)),
                  pl.BlockSpec((1,) + w_gate_up.shape[1:], lambda i: (layer, 0, 0)),
                  pl.BlockSpec((1,) + w_down.shape[1:], lambda i: (layer, 0, 0))],
        out_specs=pl.BlockSpec((tm, d), lambda i: (i, 0)),
        out_shape=jax.ShapeDtypeStruct((rows, d), F32),
        input_output_aliases={0: 0},
        compiler_params=_params("arbitrary"),
        name="ffn",
    )(x, gains, w_gate_up, w_down)


def _rw_proj_kernel(*refs, has_v_res):
    if has_v_res:
        (x_ref, gain_ref, shift0_ref, mu_ref, wr_ref, wk_ref, wv_ref, vec_ref,
         w1_ref, w2_ref, a1_ref, a2_ref, g1_ref, g2_ref, v1_ref, v2_ref, vfirst_ref,
         r_ref, lw_ref, k_ref, v_ref, kk_ref, a_ref, g_ref, shift_ref, prev_ref) = refs
    else:
        (x_ref, gain_ref, shift0_ref, mu_ref, wr_ref, wk_ref, wv_ref, vec_ref,
         w1_ref, w2_ref, a1_ref, a2_ref, g1_ref, g2_ref,
         r_ref, lw_ref, k_ref, v_ref, kk_ref, a_ref, g_ref, shift_ref, prev_ref) = refs
    i = pl.program_id(1)
    h = _rms(x_ref[...], gain_ref[...])
    tm = h.shape[0]

    @pl.when(i == 0)
    def _():
        prev_ref[...] = shift0_ref[0]

    row = lax.broadcasted_iota(jnp.int32, h.shape, 0)
    h_prev = jnp.where(row == 0, prev_ref[...], pltpu.roll(h, 1, 0))
    last = h[tm - 1:tm, :]
    prev_ref[...] = last
    shift_ref[0] = last

    xx = h_prev - h
    mixed = lambda j: (h + xx * mu_ref[j:j + 1, :]).astype(BF16)
    w0, a0, k_k, k_a, v0 = (vec_ref[j:j + 1, :] for j in range(5))

    r_ref[...] = _dot(mixed(0), wr_ref[...]).astype(r_ref.dtype)
    dec_in = _dot(mixed(1), w1_ref[...])
    k = _dot(mixed(2), wk_ref[...])
    dec = w0 + _dot(jnp.tanh(dec_in).astype(BF16), w2_ref[...])
    x_v = mixed(3)
    v = _dot(x_v, wv_ref[...])
    lw_ref[...] = -jnp.exp(-jax.nn.softplus(-dec) - 0.5)
    if has_v_res:
        mix = jax.nn.sigmoid(v0 + _dot(_dot(x_v, v1_ref[...]).astype(BF16), v2_ref[...]))
        v = v + (vfirst_ref[...] - v) * mix
    v_ref[...] = v
    a_in = _dot(mixed(4), a1_ref[...])
    g_in = _dot(mixed(5), g1_ref[...])
    a = jax.nn.sigmoid(a0 + _dot(a_in.astype(BF16), a2_ref[...]))
    kk_ref[...] = (k * k_k).astype(kk_ref.dtype)
    k_ref[...] = (k * (1.0 + (a - 1.0) * k_a)).astype(k_ref.dtype)
    a_ref[...] = a.astype(a_ref.dtype)
    g_ref[...] = _dot(jax.nn.sigmoid(g_in).astype(BF16), g2_ref[...]).astype(g_ref.dtype)


def _pad_cols(w, mult=LANES):
    pad = -w.shape[1] % mult
    return jnp.pad(w, ((0, 0), (0, pad))) if pad else w


def _pad_rows(w, mult=LANES):
    pad = -w.shape[0] % mult
    return jnp.pad(w, ((0, pad), (0, 0))) if pad else w


def _rw_proj(x, batch, gain, shift0, p, v_first):
    rows, d = x.shape
    seq = rows // batch
    tm = min(WIDE_ROW_TILE, seq)
    nt = seq // tm
    has_v_res = v_first is not None
    zeros = jnp.zeros((d,), F32)
    vec = jnp.stack([p["w0"], p["a0"], p["k_k"], p["k_a"], p["v0"] if has_v_res else zeros,
                     zeros, zeros, zeros])
    row_spec = pl.BlockSpec((tm, d), lambda b, i: (b * nt + i, 0))

    def full(arr):
        return pl.BlockSpec(arr.shape, lambda b, i: (0,) * arr.ndim, pipeline_mode=pl.Buffered(1))

    ins = [x, gain.reshape(1, d), shift0.reshape(batch, 1, d), p["mu"], p["w_r"], p["w_k"], p["w_v"], vec,
           p["w1"], p["w2"], p["a1"], p["a2"], p["g1"], p["g2"]]
    in_specs = [row_spec, full(ins[1]), pl.BlockSpec((1, 1, d), lambda b, i: (b, 0, 0))]
    in_specs += [full(a) for a in ins[3:]]
    if has_v_res:
        ins += [p["v1"], p["v2"], v_first]
        in_specs += [full(p["v1"]), full(p["v2"]), row_spec]
    tok = [jax.ShapeDtypeStruct((rows, d), dt) for dt in (BF16, F32, BF16, F32, BF16, BF16, BF16)]
    outs = pl.pallas_call(
        functools.partial(_rw_proj_kernel, has_v_res=has_v_res),
        grid=(batch, nt),
        in_specs=in_specs,
        out_specs=[row_spec] * 7 + [pl.BlockSpec((1, 1, d), lambda b, i: (b, 0, 0))],
        out_shape=tok + [jax.ShapeDtypeStruct((batch, 1, d), F32)],
        scratch_shapes=[pltpu.VMEM((1, d), F32)],
        compiler_params=_params("arbitrary", "arbitrary"),
        name="rw_proj",
    )(*ins)
    return outs[:7], outs[7].reshape(batch, d)


WKV_BASE_BLOCK = 4
WKV_DECAY_PIECES = 3
WKV_GROUP_PIECES = 2
WKV_STATE_PASSES = 1
WKV_SOLVE_PASSES = 1


def _wkv_prepare(units, c_):
    n = units[0][0].shape[0]
    sum_l = lambda x: _mm_exact_lhs(c_["tri_incl"], x, WKV_DECAY_PIECES)
    sum_g = lambda x: _mm_exact_rhs(x, c_["gsum"], WKV_GROUP_PIECES)
    solve_mm = lambda x, y, nt=False: _mm(x, y, WKV_SOLVE_PASSES, nt=nt)

    cs = [sum_l(u[1]) for u in units]
    head_sums = [sum_g(jnp.concatenate([u[4] * u[4], u[0] * u[2] * u[7][0:1, :]], axis=0)) for u in units]
    yield
    pre = []
    for (r, lw, k, v, kk, a, g, vec), cs_u, hs in zip(units, cs, head_sums):
        cs_last = cs_u[n - 1:n, :]
        d_inv = jnp.exp(-cs_u)
        d_rest = jnp.exp(cs_last - cs_u)
        kkn = kk * lax.rsqrt(jnp.maximum(hs[:n], 1e-24))
        b = kkn * a
        pre.append(dict(a_t=-kkn * jnp.exp(cs_u - lw), r_t=r * jnp.exp(cs_u), b_t=b * d_inv, k_t=k * d_inv,
                        bk_rest=jnp.concatenate([b * d_rest, k * d_rest], axis=0), decay=jnp.exp(cs_last),
                        v=v, g=g, vec=vec, bonus=hs[n:] * v))
    yield

    by_head = lambda x: jnp.concatenate([jnp.where(m, x, 0.0) for m in c_["head_masks"]], axis=0)
    block_diag = lambda w: jnp.where(c_["same_block"], jnp.concatenate([w, w], axis=0), 0.0)
    grams = [solve_mm(jnp.concatenate([q["a_t"], q["r_t"]], axis=0),
                      jnp.concatenate([by_head(q["b_t"]), by_head(q["k_t"])], axis=0), nt=True)
             for q in pre]
    yield
    lows = [jnp.where(c_["strict_w"], gm[:n, :2 * n], 0.0) for gm in grams]
    m_aks = [jnp.where(c_["strict_w"], gm[:n, 2 * n:], 0.0) for gm in grams]
    for q, gm in zip(pre, grams):
        q["m_r"] = jnp.where(c_["incl_w2"], gm[n:], 0.0)
        q["v_heads"] = by_head(q["v"])
    from_v = [solve_mm(mk, q["v_heads"]) for mk, q in zip(m_aks, pre)]
    yield
    diag = [jnp.where(c_["diag_block"], lo, 0.0) for lo in lows]
    inv = [c_["eye_w"] + dg for dg in diag]
    sq = [solve_mm(dg, block_diag(dg)) for dg in diag]
    yield
    inv = [t + solve_mm(t, block_diag(p2)) for t, p2 in zip(inv, sq)]
    yield
    for below in c_["below_blocks"]:
        half = [solve_mm(t, block_diag(jnp.where(below, lo, 0.0))) for t, lo in zip(inv, lows)]
        yield
        inv = [t + solve_mm(hf, block_diag(t)) for t, hf in zip(inv, half)]
        yield
    for q, t, fv in zip(pre, inv, from_v):
        q["inv"], q["from_v"] = t, fv
    return pre


def _wkv_advance(pre, states, c_):
    n = pre[0]["v"].shape[0]
    inv_hd = 1.0 / RW_HEAD_DIM
    sum_g = lambda x: _mm_exact_rhs(x, c_["gsum"], WKV_GROUP_PIECES)
    solve_mm = lambda x, y: _mm(x, y, WKV_SOLVE_PASSES)
    by_head = lambda x: jnp.concatenate([jnp.where(m, x, 0.0) for m in c_["head_masks"]], axis=0)

    from_state = [_mm(jnp.concatenate([q["a_t"], q["r_t"]], axis=0), s, WKV_STATE_PASSES, nt=True)
                  for q, s in zip(pre, states)]
    yield
    sig_p = [solve_mm(q["inv"], by_head(fs[:n] + q["from_v"])) for q, fs in zip(pre, from_state)]
    yield
    y_p = [fs[n:] + solve_mm(q["m_r"], jnp.concatenate([by_head(sg), q["v_heads"]], axis=0))
           for q, fs, sg in zip(pre, from_state, sig_p)]
    upd = [_mm(jnp.concatenate([s, q["v"]], axis=0).T, q["bk_rest"], WKV_STATE_PASSES)
           for s, q in zip(sig_p, pre)]
    new_states = [s * q["decay"] + jnp.where(c_["same_head"], u, 0.0) for s, q, u in zip(states, pre, upd)]

    yield
    mean = [sum_g(y) * inv_hd for y in y_p]
    dev = [y - m for y, m in zip(y_p, mean)]
    yield
    var = [sum_g(dv * dv) * inv_hd for dv in dev]
    yield
    outs = []
    for q, dv, vr in zip(pre, dev, var):
        o = dv * lax.rsqrt(vr + RW_GN_EPS) * q["vec"][1:2, :] + q["vec"][2:3, :]
        outs.append((o + q["bonus"]) * q["g"])
    return outs, new_states


def _wkv_advance_chunks(pre_chunks, states, c_):
    outs = []
    for pre in pre_chunks:
        out, states = yield from _wkv_advance(pre, states, c_)
        outs.append(jnp.concatenate(out, axis=1))
    return outs, states


def _in_lock_step(*staged):
    results = [None] * len(staged)
    live = list(enumerate(staged))
    while live:
        still = []
        for i, gen in live:
            try:
                next(gen)
                still.append((i, gen))
            except StopIteration as done:
                results[i] = done.value
        live = still
    return results


def _wkv_kernel(*refs, has_state0, n):
    if has_state0:
        (r_ref, lw_ref, k_ref, v_ref, kk_ref, a_ref, g_ref, vec_ref, s0_ref,
         o_ref, sout_ref, state_ref) = refs
    else:
        (r_ref, lw_ref, k_ref, v_ref, kk_ref, a_ref, g_ref, vec_ref,
         o_ref, sout_ref, state_ref) = refs
    c = pl.program_id(2)
    hd = RW_HEAD_DIM

    @pl.when(c == 0)
    def _():
        if has_state0:
            state_ref[...] = s0_ref[0]
        else:
            state_ref[...] = jnp.zeros_like(state_ref)

    tr = lax.broadcasted_iota(jnp.int32, (n, n), 0)
    tc = lax.broadcasted_iota(jnp.int32, (n, n), 1)
    tr2 = lax.broadcasted_iota(jnp.int32, (n, 2 * n), 0)
    tc2 = lax.rem(lax.broadcasted_iota(jnp.int32, (n, 2 * n), 1), n)
    tr4 = lax.broadcasted_iota(jnp.int32, (n, 4 * n), 0)
    tc4 = lax.rem(lax.broadcasted_iota(jnp.int32, (n, 4 * n), 1), n)
    br = lax.broadcasted_iota(jnp.int32, (2 * n, 2 * n), 0) // n
    bc = lax.broadcasted_iota(jnp.int32, (2 * n, 2 * n), 1) // n
    lane = lax.broadcasted_iota(jnp.int32, (n, LANES), 1)
    sr = lax.broadcasted_iota(jnp.int32, (LANES, LANES), 0) // hd
    sc = lax.broadcasted_iota(jnp.int32, (LANES, LANES), 1) // hd
    sizes = [WKV_BASE_BLOCK << i for i in range(n.bit_length()) if (WKV_BASE_BLOCK << i) < n]
    below_blocks = []
    for s in sizes:
        u, v = tr2 // s, tc2 // s
        below_blocks.append(jnp.where((u & 1) == 1, u - 1, -1) == v)
    consts = dict(
        diag_block=(tr2 // WKV_BASE_BLOCK) == (tc2 // WKV_BASE_BLOCK),
        below_blocks=below_blocks,
        gsum=jnp.concatenate([_group_ones(LANES, hd)] * WKV_GROUP_PIECES, axis=0),
        tri_incl=jnp.concatenate([jnp.where(tc <= tr, 1.0, 0.0).astype(BF16)] * WKV_DECAY_PIECES, axis=1),
        strict_w=tc2 < tr2,
        incl_w2=tc4 <= tr4,
        eye_w=jnp.where(tc2 == tr2, 1.0, 0.0).astype(F32),
        same_block=br == bc,
        head_masks=[lane < hd, lane >= hd],
        same_head=sr == sc,
    )
    n_pairs = state_ref.shape[0]
    n_chunks = r_ref.shape[0] // n

    def units(chunks):
        out = []
        for ci in chunks:
            rows = slice(ci * n, (ci + 1) * n)
            for p in range(n_pairs):
                sl = slice(p * LANES, (p + 1) * LANES)
                out.append(tuple(ref[rows, sl].astype(F32)
                                 for ref in (r_ref, lw_ref, k_ref, v_ref, kk_ref, a_ref, g_ref)) + (vec_ref[:, sl],))
        return out

    def per_chunk(pre):
        return [pre[i:i + n_pairs] for i in range(0, len(pre), n_pairs)]

    first = range((n_chunks + 1) // 2)
    second = range(len(first), n_chunks)
    states = [state_ref[p] for p in range(n_pairs)]
    (pre_first,) = _in_lock_step(_wkv_prepare(units(first), consts))
    if second:
        (outs, states), pre_second = _in_lock_step(
            _wkv_advance_chunks(per_chunk(pre_first), states, consts), _wkv_prepare(units(second), consts))
        ((outs_second, states),) = _in_lock_step(_wkv_advance_chunks(per_chunk(pre_second), states, consts))
        outs = outs + outs_second
    else:
        ((outs, states),) = _in_lock_step(_wkv_advance_chunks(per_chunk(pre_first), states, consts))
    o_ref[...] = jnp.concatenate(outs, axis=0).astype(o_ref.dtype)
    new_states = jnp.stack(states)
    state_ref[...] = new_states
    sout_ref[0] = new_states


def _wkv(tok, batch, vec, state0):
    rows, d = tok[0].shape
    seq = rows // batch
    n = min(WKV_CHUNK, seq)
    span = n * min(WKV_CHUNKS_PER_STEP, seq // n)
    nc = seq // span
    pairs = d // LANES
    gp = WKV_PAIRS
    width = gp * LANES
    has_state0 = state0 is not None
    tok_spec = pl.BlockSpec((span, width), lambda b, p, c: (b * nc + c, p))
    state_spec = pl.BlockSpec((1, gp, LANES, LANES), lambda b, p, c: (b, p, 0, 0))
    ins = list(tok) + [vec]
    in_specs = [tok_spec] * 7 + [pl.BlockSpec((8, width), lambda b, p, c: (0, p))]
    if has_state0:
        ins.append(state0)
        in_specs.append(state_spec)
    return pl.pallas_call(
        functools.partial(_wkv_kernel, has_state0=has_state0, n=n),
        grid=(batch, pairs // gp, nc),
        in_specs=in_specs,
        out_specs=[tok_spec, state_spec],
        out_shape=[jax.ShapeDtypeStruct((rows, d), BF16),
                   jax.ShapeDtypeStruct((batch, pairs, LANES, LANES), F32)],
        scratch_shapes=[pltpu.VMEM((gp, LANES, LANES), F32)],
        compiler_params=_params("arbitrary", "arbitrary", "arbitrary"),
        name="wkv",
    )(*ins)


def _state_to_blockdiag(state):
    b, h, n, _ = state.shape
    s = state.reshape(b, h // 2, 2, n, n)
    z = jnp.zeros_like(s[:, :, 0])
    top = jnp.concatenate([s[:, :, 0], z], axis=-1)
    bot = jnp.concatenate([z, s[:, :, 1]], axis=-1)
    return jnp.concatenate([top, bot], axis=-2)


def _state_from_blockdiag(bd):
    b, p, n2, _ = bd.shape
    n = n2 // 2
    return jnp.stack([bd[:, :, :n, :n], bd[:, :, n:, n:]], axis=2).reshape(b, 2 * p, n, n)


def _trunk(x, batch, mem_k, mem_v, past_k, past_v, rw_state0, rw_shift0, w):
    depth = w["ffn_gu"].shape[0]
    d = x.shape[-1]
    sb_k, sb_v, rw_states, rw_shifts = None, None, [], []
    n_sb = (depth + 1) // 2
    v_first = None
    for layer in range(depth):
        j = layer // 2
        if layer % 2 == 0:
            sb_k, sb_v, qkv = _sb_qkv(x, batch, w["norm_mix"][layer], w["sb_qkv"][j], j, n_sb, sb_k, sb_v)
            if past_k is None:
                o = _sb_attention_prompt(qkv, batch)
            else:
                o = _sb_attention_sample(qkv, batch, past_k, past_v, j)
            w_o = w["sb_o"][j]
        else:
            if rw_state0 is None:
                state0 = None
                shift0 = jnp.zeros((batch, d), F32)
            else:
                state0 = _state_to_blockdiag(rw_state0[j])
                shift0 = rw_shift0[j]
            tok, shift = _rw_proj(x, batch, w["norm_mix"][layer], shift0, w["rw"][j], v_first)
            if v_first is None:
                v_first = tok[3]
            o, state = _wkv(tok, batch, w["rw"][j]["out_vec"], state0)
            rw_states.append(_state_from_blockdiag(state))
            rw_shifts.append(shift)
            w_o = w["rw"][j]["w_o"]
        x = _xattn(o, w_o, x, w["norm_xattn"][layer], w["xa_q"], w["xa_o"], mem_k, mem_v, layer,
                   in_place=layer > 0)
        final = w["norm_final"] if layer == depth - 1 else None
        x = _ffn(x, w["norm_ffn"][layer], w["ffn_gu"], w["ffn_down"], layer, final)
    return x, sb_k, sb_v, jnp.stack(rw_states), jnp.stack(rw_shifts)


def kernel(x_prompt, x_sample, mem_prompt, cache_sb_k, cache_sb_v, state_rwkv_wkv, state_rwkv_shift,
           cache_mem_k, cache_mem_v, norm_mix, norm_xattn, norm_ffn, norm_mem, norm_final,
           sb_w_qkv, sb_w_o, rw_mu, rw_w_r, rw_w_k, rw_w_v, rw_w_o, rw_w0, rw_w1, rw_w2,
           rw_a0, rw_a1, rw_a2, rw_v0, rw_v1, rw_v2, rw_g1, rw_g2, rw_k_k, rw_k_a, rw_r_k,
           rw_lnx_w, rw_lnx_b, xa_w_q, xa_w_kv, xa_w_o, ffn_w_gate_up, ffn_w_down):
    batch, seq, d = x_prompt.shape
    dec_batch, dec_seq, _ = x_sample.shape
    depth = norm_mix.shape[0]
    n_rw = rw_w_r.shape[0]
    n_sb = sb_w_qkv.shape[0]
    n_mem = mem_prompt.shape[1]
    bf = lambda t: t.astype(BF16)
    per_layer = lambda t: [bf(t[i]) for i in range(t.shape[0])]

    rw = []
    for j in range(n_rw):
        pj = dict(mu=rw_mu[j], w_r=bf(rw_w_r[j]), w_k=bf(rw_w_k[j]), w_v=bf(rw_w_v[j]), w_o=bf(rw_w_o[j]),
                  w0=rw_w0[j], a0=rw_a0[j], k_k=rw_k_k[j], k_a=rw_k_a[j],
                  w1=bf(_pad_cols(rw_w1[j])), w2=bf(_pad_rows(rw_w2[j])),
                  a1=bf(_pad_cols(rw_a1[j])), a2=bf(_pad_rows(rw_a2[j])),
                  g1=bf(_pad_cols(rw_g1[j])), g2=bf(_pad_rows(rw_g2[j])))
        if j > 0:
            pj.update(v0=rw_v0[j - 1], v1=bf(_pad_cols(rw_v1[j - 1])), v2=bf(_pad_rows(rw_v2[j - 1])))
        zeros = jnp.zeros((d,), F32)
        pj["out_vec"] = jnp.stack([rw_r_k[j].reshape(d), rw_lnx_w[j], rw_lnx_b[j]] + [zeros] * 5)
        rw.append(pj)
    w = dict(norm_mix=norm_mix, norm_xattn=norm_xattn, norm_ffn=norm_ffn, norm_final=norm_final,
             sb_qkv=per_layer(sb_w_qkv), sb_o=per_layer(sb_w_o), rw=rw,
             xa_q=bf(xa_w_q), xa_o=bf(xa_w_o), ffn_gu=bf(ffn_w_gate_up), ffn_down=bf(ffn_w_down))

    mem_k, mem_v, mem_k_bf, mem_v_bf = _mem_kv(mem_prompt, norm_mem, bf(xa_w_kv))
    y_p, p_sb_k, p_sb_v, p_wkv, p_shift = _trunk(
        x_prompt.reshape(batch * seq, d), batch, mem_k_bf, mem_v_bf, None, None, None, None, w)

    y_s, s_sb_k, s_sb_v, s_wkv, s_shift = _trunk(
        x_sample.reshape(dec_batch * dec_seq, d), dec_batch,
        bf(cache_mem_k).reshape(depth, dec_batch, n_mem, d), bf(cache_mem_v).reshape(depth, dec_batch, n_mem, d),
        cache_sb_k, cache_sb_v,
        state_rwkv_wkv, state_rwkv_shift, w)

    hd = d // SB_HEADS
    return (y_p.reshape(batch, seq, d), y_s.reshape(dec_batch, dec_seq, d),
            p_sb_k.reshape(n_sb, batch, seq, SB_HEADS, hd), p_sb_v.reshape(n_sb, batch, seq, SB_HEADS, hd),
            p_wkv, p_shift,
            mem_k, mem_v,
            s_sb_k.reshape(n_sb, dec_batch, dec_seq, SB_HEADS, hd),
            s_sb_v.reshape(n_sb, dec_batch, dec_seq, SB_HEADS, hd),
            s_wkv, s_shift)
```

```python
import functools

import jax
import jax.numpy as jnp
from jax import lax
from jax.experimental import pallas as pl
from jax.experimental.pallas import tpu as pltpu

F32 = jnp.float32
BF16 = jnp.bfloat16

NORM_EPS = 1e-6
NEG_LOG2_E = -1.4426950408889634
RW_GN_EPS = 64e-5
SB_HEADS = 16
XA_HEADS = 4
RW_HEAD_DIM = 64
LANES = 128
ROW_TILE = 256
WIDE_ROW_TILE = 512
XATTN_ROW_TILE = 1024
SB_BLOCK = 256
SB_PAIRS = 8
SB_WAVES = 4
SB_DECODE_WAVES = 2
SB_UNROLL = 2
SB_PAST_SPAN = 2048
WKV_CHUNK = 64
WKV_CHUNKS_PER_STEP = 4
WKV_PAIRS = 8
FFN_CHUNK = 256
VMEM_LIMIT_BYTES = 56 * 1024 * 1024


def _params(*sem):
    return pltpu.CompilerParams(dimension_semantics=sem, vmem_limit_bytes=VMEM_LIMIT_BYTES)


def _dot(a, b):
    return jnp.dot(a, b, preferred_element_type=F32)


def _dot_nt(a, b):
    return lax.dot_general(a, b, (((1,), (1,)), ((), ())), preferred_element_type=F32)


def _split(x, n):
    parts = []
    for i in range(n):
        p = x.astype(BF16)
        parts.append(p)
        if i + 1 < n:
            x = x - p.astype(F32)
    return parts


def _mm(a, b, passes, nt=False):
    dot = _dot_nt if nt else _dot
    if passes == 1:
        return dot(a.astype(BF16), b.astype(BF16))
    a_hi, a_lo = _split(a, 2)
    b_hi, b_lo = _split(b, 2)
    return dot(a_hi, b_hi) + (dot(a_hi, b_lo) + dot(a_lo, b_hi))


def _mm_exact_rhs(a, b_stacked, n):
    return _dot(jnp.concatenate(_split(a, n), axis=1), b_stacked)


def _mm_exact_lhs(a_repeated, b, n):
    return _dot(a_repeated, jnp.concatenate(_split(b, n), axis=0))


def _rms(x, gain):
    ms = jnp.mean(x * x, axis=-1, keepdims=True)
    return x * lax.rsqrt(ms + NORM_EPS) * gain


def _group_ones(n, group):
    r = lax.broadcasted_iota(jnp.int32, (n, n), 0) // group
    c = lax.broadcasted_iota(jnp.int32, (n, n), 1) // group
    return jnp.where(r == c, 1.0, 0.0).astype(BF16)


def _mem_kv_kernel(x_ref, g_ref, w_ref, k_ref, v_ref, kb_ref, vb_ref):
    tm, d = x_ref.shape
    heads, dh = k_ref.shape[-2:]
    h = _rms(x_ref[...], g_ref[0]).astype(BF16)
    kv = _dot(h, w_ref[0])
    k, v = kv[:, :d], kv[:, d:]
    k_ref[0, 0] = k.reshape(tm, heads, dh)
    v_ref[0, 0] = v.reshape(tm, heads, dh)
    kb_ref[0, 0] = k.astype(BF16)
    vb_ref[0, 0] = v.astype(BF16)


def _mem_kv(mem, gains, w_kv):
    depth = w_kv.shape[0]
    batch, n_mem, d = mem.shape
    tm = min(ROW_TILE, n_mem)
    nt = n_mem // tm
    dh = d // XA_HEADS
    assert n_mem % tm == 0 and dh % LANES == 0
    out = jax.ShapeDtypeStruct((depth, batch, n_mem, XA_HEADS, dh), F32)
    out_b = jax.ShapeDtypeStruct((depth, batch, n_mem, d), BF16)
    spec = pl.BlockSpec((1, 1, tm, XA_HEADS, dh), lambda l, i: (l, i // nt, i % nt, 0, 0))
    spec_b = pl.BlockSpec((1, 1, tm, d), lambda l, i: (l, i // nt, i % nt, 0))
    return pl.pallas_call(
        _mem_kv_kernel,
        grid=(depth, batch * nt),
        in_specs=[pl.BlockSpec((tm, d), lambda l, i: (i, 0)),
                  pl.BlockSpec((1, 1, d), lambda l, i: (l, 0, 0)),
                  pl.BlockSpec((1, d, 2 * d), lambda l, i: (l, 0, 0))],
        out_specs=[spec, spec, spec_b, spec_b],
        out_shape=[out, out, out_b, out_b],
        compiler_params=_params("arbitrary", "arbitrary"),
        name="mem_kv",
    )(mem.reshape(batch * n_mem, d), gains.reshape(depth, 1, d), w_kv)


def _sb_qkv_kernel(x_ref, g_ref, w_ref, *rest, q_scale, slot):
    k_ref, v_ref, qkv_ref = rest[-3:]
    tm, d = x_ref.shape
    heads, hd = k_ref.shape[2:4]
    h = _rms(x_ref[...], g_ref[...]).astype(BF16)

    def per_head_transposed(t):
        return t.T.reshape(heads, hd, tm)

    own = slot if k_ref.shape[0] > 1 else 0
    for s in range(k_ref.shape[0]):
        if s != own:
            k_ref[s, 0] = jnp.zeros(k_ref.shape[2:], k_ref.dtype)
            v_ref[s, 0] = jnp.zeros(v_ref.shape[2:], v_ref.dtype)
    k = _dot(h, w_ref[:, d:2 * d])
    qkv_ref[:, d:2 * d] = k.astype(BF16)
    k_ref[own, 0] = per_head_transposed(k)
    v = _dot(h, w_ref[:, 2 * d:])
    qkv_ref[:, 2 * d:] = v.astype(BF16)
    v_ref[own, 0] = per_head_transposed(v)
    qkv_ref[:, :d] = (_dot(h, w_ref[:, :d]) * q_scale).astype(BF16)


def _sb_qkv(x, batch, gain, w, slot, n_slots, k_all=None, v_all=None):
    rows, d = x.shape
    seq = rows // batch
    tm = min(ROW_TILE, seq)
    nt = seq // tm
    hd = d // SB_HEADS
    q_scale = float(hd ** -0.5)
    ins = [x, gain.reshape(1, d), w]
    in_specs = [pl.BlockSpec((tm, d), lambda i: (i, 0)),
                pl.BlockSpec((1, d), lambda i: (0, 0)),
                pl.BlockSpec(w.shape, lambda i: (0, 0))]
    if k_all is None:
        aliases = {}
        slot_spec = pl.BlockSpec((n_slots, 1, SB_HEADS, hd, tm), lambda i: (0, i // nt, 0, 0, i % nt))
    else:
        ins += [k_all, v_all]
        in_specs += [pl.BlockSpec(memory_space=pl.ANY)] * 2
        aliases = {3: 0, 4: 1}
        slot_spec = pl.BlockSpec((1, 1, SB_HEADS, hd, tm), lambda i: (slot, i // nt, 0, 0, i % nt))
    kv_shape = jax.ShapeDtypeStruct((n_slots, batch, SB_HEADS, hd, seq), F32)
    return pl.pallas_call(
        functools.partial(_sb_qkv_kernel, q_scale=q_scale, slot=slot),
        grid=(rows // tm,),
        in_specs=in_specs,
        out_specs=[slot_spec, slot_spec, pl.BlockSpec((tm, 3 * d), lambda i: (i, 0))],
        out_shape=[kv_shape, kv_shape, jax.ShapeDtypeStruct((rows, 3 * d), BF16)],
        input_output_aliases=aliases,
        compiler_params=_params("arbitrary"),
        name="sb_qkv",
    )(*ins)


def _sb_step(q2s, kv_blocks, carry, acc, neg_lower2, mask, n_waves):
    m = q2s[0].shape[0]
    groups = len(q2s)
    per = max(groups // n_waves, 1)
    waves = [range(c, min(c + per, groups)) for c in range(0, groups, per)]
    rows = [slice(w_[0] * m, (w_[-1] + 1) * m) for w_ in waves]
    units = [(b, c) for b in range(len(kv_blocks)) for c in range(len(waves))]

    def scores(b, c):
        keys, _, keys_on_lanes = kv_blocks[b]
        dot = _dot if keys_on_lanes else _dot_nt
        return jnp.concatenate([dot(q2s[g], keys[g]) for g in waves[c]], axis=0)

    def softplus_pieces(c, z):
        sp = jnp.maximum(z, 0.0) + jnp.log(1.0 + jnp.exp2(jnp.abs(z) * NEG_LOG2_E))
        if mask is not None:
            sp = jnp.where(mask[rows[c]], sp, 0.0)
        return jnp.concatenate(_split(sp, 2), axis=1)

    def weights(c, z, from_s, later):
        w = jnp.exp(z + from_s + jnp.concatenate([later] * (z.shape[1] // LANES), axis=1))
        if mask is not None:
            w = jnp.where(mask[rows[c]], w, 0.0)
        return w.astype(BF16)

    def values(b, c, w):
        _, vals, keys_on_lanes = kv_blocks[b]
        dot = _dot_nt if keys_on_lanes else _dot
        return jnp.concatenate([dot(w[i * m:(i + 1) * m], vals[g]) for i, g in enumerate(waves[c])], axis=0)

    n = len(units)
    z, pieces, from_s, w = ([None] * n for _ in range(4))
    later = [carry[r] for r in rows]
    pv = [None] * len(waves)
    for tick in range(n + 4):
        for u, (b, c) in enumerate(units):
            stage = tick - u
            if stage == 0:
                z[u] = scores(b, c)
            elif stage == 1:
                pieces[u] = softplus_pieces(c, z[u])
            elif stage == 2:
                from_s[u] = _dot(pieces[u], neg_lower2)
            elif stage == 3:
                w[u] = weights(c, z[u], from_s[u], later[c])
                later[c] = later[c] + jnp.broadcast_to(from_s[u][:, 0:1], later[c].shape)
            elif stage == 4:
                out = values(b, c, w[u])
                pv[c] = out if pv[c] is None else pv[c] + out
    return jnp.concatenate(later, axis=0), acc + jnp.concatenate(pv, axis=0)


def _sb_consts(bk):
    r = lax.broadcasted_iota(jnp.int32, (2, bk, bk), 1).reshape(2 * bk, bk)
    c = lax.broadcasted_iota(jnp.int32, (2 * bk, bk), 1)
    return jnp.where(r >= c, -1.0, 0.0).astype(BF16)


def _sb_stack_heads(q_ref, q2_ref):
    tq = q_ref.shape[0]
    lane = lax.broadcasted_iota(jnp.int32, (tq, LANES), 1)
    for g in range(q2_ref.shape[0]):
        q = q_ref[:, g * LANES:(g + 1) * LANES]
        zero = jnp.zeros_like(q)
        q2_ref[g] = jnp.concatenate([jnp.where(lane < LANES // 2, q, zero),
                                     jnp.where(lane >= LANES // 2, q, zero)], axis=0)


def _sb_causal_mask(groups, tq, bk, t0=0):
    t = lax.broadcasted_iota(jnp.int32, (2 * groups, tq, bk), 1).reshape(2 * groups * tq, bk)
    s = lax.broadcasted_iota(jnp.int32, (2 * groups * tq, bk), 1)
    return s < t + t0


def _sb_write(o_ref, acc_ref):
    tq = o_ref.shape[0]
    lane = lax.broadcasted_iota(jnp.int32, (tq, LANES), 1)
    for g in range(acc_ref.shape[0] // (2 * tq)):
        o_ref[:, g * LANES:(g + 1) * LANES] = jnp.where(
            lane < LANES // 2, acc_ref[2 * g * tq:(2 * g + 1) * tq, :],
            acc_ref[(2 * g + 1) * tq:(2 * g + 2) * tq, :]).astype(o_ref.dtype)


def _sb_prompt_kernel(q_ref, k_ref, v_ref, o_ref, q2_ref, carry_ref, acc_ref):
    blk = q_ref.shape[0]
    groups = q2_ref.shape[0]
    i = pl.program_id(2)
    neg_lower2 = _sb_consts(blk)
    _sb_stack_heads(q_ref, q2_ref)
    lanes = [slice(g * LANES, (g + 1) * LANES) for g in range(groups)]

    def block(j):
        row0 = pl.multiple_of(j * blk, blk)
        return ([k_ref[pl.ds(row0, blk), sl] for sl in lanes], [v_ref[pl.ds(row0, blk), sl] for sl in lanes], False)

    def step(blocks, carry, acc, mask):
        return _sb_step([q2_ref[g] for g in range(groups)], blocks, carry, acc, neg_lower2, mask, SB_WAVES)

    half = blk // 2
    d0 = pl.multiple_of(i * blk, blk)
    zero = jnp.zeros((groups * blk, LANES), F32)
    halves = []
    for t0, keys, consts in ((half, blk, neg_lower2), (0, half, _sb_consts(half))):
        q2_half = [jnp.concatenate([q2_ref[g, t0:t0 + half, :], q2_ref[g, blk + t0:blk + t0 + half, :]], axis=0)
                   for g in range(groups)]
        own = ([k_ref[pl.ds(d0, keys), sl] for sl in lanes], [v_ref[pl.ds(d0, keys), sl] for sl in lanes], False)
        halves.append(_sb_step(q2_half, [own], zero, zero, consts, _sb_causal_mask(groups, half, keys, t0),
                               SB_WAVES))
    (carry_late, acc_late), (carry_early, acc_early) = halves
    for g in range(groups):
        for h in range(2):
            src = slice((2 * g + h) * half, (2 * g + h + 1) * half)
            dst = (2 * g + h) * blk
            for ref, early, late in ((carry_ref, carry_early, carry_late), (acc_ref, acc_early, acc_late)):
                ref[dst:dst + half, :] = early[src]
                ref[dst + half:dst + blk, :] = late[src]

    odd = i % SB_UNROLL

    def single(it, c):
        carry_ref[...], acc_ref[...] = step([block(i - 1 - it)], carry_ref[...], acc_ref[...], None)
        return c

    def multi(it, c):
        top = i - 1 - odd - it * SB_UNROLL
        carry_ref[...], acc_ref[...] = step([block(top - u) for u in range(SB_UNROLL)],
                                            carry_ref[...], acc_ref[...], None)
        return c

    lax.fori_loop(0, odd, single, 0)
    lax.fori_loop(0, i // SB_UNROLL, multi, 0)
    _sb_write(o_ref, acc_ref)


def _sb_attention_prompt(qkv, batch):
    rows, d3 = qkv.shape
    d = d3 // 3
    seq = rows // batch
    blk = min(SB_BLOCK, seq)
    nq = seq // blk
    width = SB_PAIRS * LANES
    ng = d // width
    return pl.pallas_call(
        _sb_prompt_kernel,
        grid=(batch, ng, nq),
        in_specs=[pl.BlockSpec((blk, width), lambda b, p, i: (b * nq + i, p)),
                  pl.BlockSpec((seq, width), lambda b, p, i: (b, ng + p)),
                  pl.BlockSpec((seq, width), lambda b, p, i: (b, 2 * ng + p))],
        out_specs=pl.BlockSpec((blk, width), lambda b, p, i: (b * nq + i, p)),
        out_shape=jax.ShapeDtypeStruct((rows, d), BF16),
        scratch_shapes=[pltpu.VMEM((SB_PAIRS, 2 * blk, LANES), BF16),
                        pltpu.VMEM((SB_PAIRS * 2 * blk, LANES), F32),
                        pltpu.VMEM((SB_PAIRS * 2 * blk, LANES), F32)],
        compiler_params=_params("arbitrary", "arbitrary", "arbitrary"),
        name="sb_attention_prompt",
    )(qkv, qkv, qkv)


def _sb_sample_kernel(q_ref, kn_ref, vn_ref, kp_ref, vp_ref, o_ref, q2_ref, carry_ref, acc_ref, *, blk):
    tq = q_ref.shape[0]
    heads, _, hd = q2_ref.shape
    c = pl.program_id(1)
    neg_lower2 = _sb_consts(blk)
    lanes = [slice(h * hd, (h + 1) * hd) for h in range(heads)]

    @pl.when(c == 0)
    def _():
        for h, sl in enumerate(lanes):
            q2_ref[h] = q_ref[:, sl]
        own = ([kn_ref[0, :, sl] for sl in lanes], [vn_ref[0, :, sl] for sl in lanes], False)
        carry_ref[...], acc_ref[...] = _sb_step(
            [q2_ref[h] for h in range(heads)], [own], jnp.zeros(carry_ref.shape, F32),
            jnp.zeros(acc_ref.shape, F32), neg_lower2, _sb_causal_mask(heads // 2, tq, blk), SB_DECODE_WAVES)

    blocks = []
    for sb in reversed(range(kp_ref.shape[-1] // blk)):
        cols = slice(sb * blk, (sb + 1) * blk)
        blocks.append(([kp_ref[0, 0, h, :, cols].astype(BF16) for h in range(heads)],
                       [vp_ref[0, 0, h, :, cols].astype(BF16) for h in range(heads)], True))
    carry_ref[...], acc_ref[...] = _sb_step([q2_ref[h] for h in range(heads)], blocks,
                                            carry_ref[...], acc_ref[...], neg_lower2, None, SB_DECODE_WAVES)

    @pl.when(c == pl.num_programs(1) - 1)
    def _():
        o_ref[...] = jnp.concatenate([acc_ref[h * tq:(h + 1) * tq, :] for h in range(heads)],
                                     axis=1).astype(o_ref.dtype)


def _sb_attention_sample(qkv, batch, past_k, past_v, layer):
    rows, d3 = qkv.shape
    d = d3 // 3
    seq = rows // batch
    blk = SB_BLOCK
    _, _, past_len, heads, hd = past_k.shape
    span = min(SB_PAST_SPAN, past_len)
    steps = past_len // span
    assert seq <= blk and past_len % span == 0 and span % blk == 0 and heads % 2 == 0
    new = qkv[:, d:].reshape(batch, seq, 2 * d)
    new = jnp.pad(new, ((0, 0), (0, blk - seq), (0, 0)))
    past_k = jnp.transpose(past_k, (0, 1, 3, 4, 2))
    past_v = jnp.transpose(past_v, (0, 1, 3, 4, 2))
    past_spec = pl.BlockSpec((1, 1, heads, hd, span), lambda b, c: (layer, b, 0, 0, steps - 1 - c))
    return pl.pallas_call(
        functools.partial(_sb_sample_kernel, blk=blk),
        grid=(batch, steps),
        in_specs=[pl.BlockSpec((seq, d), lambda b, c: (b, 0)),
                  pl.BlockSpec((1, blk, d), lambda b, c: (b, 0, 0)),
                  pl.BlockSpec((1, blk, d), lambda b, c: (b, 0, 1)),
                  past_spec, past_spec],
        out_specs=pl.BlockSpec((seq, d), lambda b, c: (b, 0)),
        out_shape=jax.ShapeDtypeStruct((rows, d), BF16),
        scratch_shapes=[pltpu.VMEM((heads, seq, hd), BF16),
                        pltpu.VMEM((heads * seq, LANES), F32),
                        pltpu.VMEM((heads * seq, hd), F32)],
        compiler_params=_params("arbitrary", "arbitrary"),
        name="sb_attention_sample",
    )(qkv, new, new, past_k, past_v)


def _xattn_kernel(a_ref, wa_ref, x_ref, g_ref, wq_ref, wo_ref, mk_ref, mv_ref, o_ref, *, heads):
    x = x_ref[...] + _dot(a_ref[...], wa_ref[...])
    d = x.shape[-1]
    dh = d // heads
    h = _rms(x, g_ref[...]).astype(BF16)
    q = _dot(h, wq_ref[0]).astype(BF16)
    nb = mk_ref.shape[1]
    seq = x.shape[0] // nb
    per_seq = []
    for b in range(nb):
        rows = slice(b * seq, (b + 1) * seq)
        outs = []
        for hd in range(heads):
            sl = slice(hd * dh, (hd + 1) * dh)
            s = _dot_nt(q[rows, sl], mk_ref[0, b, :, sl]) * (dh ** -0.5)
            e = jnp.exp(s - jnp.max(s, axis=-1, keepdims=True))
            p = e / jnp.sum(e, axis=-1, keepdims=True)
            outs.append(_dot(p.astype(BF16), mv_ref[0, b, :, sl]).astype(BF16))
        per_seq.append(jnp.concatenate(outs, axis=-1))
    o = per_seq[0] if nb == 1 else jnp.concatenate(per_seq, axis=0)
    o_ref[...] = x + _dot(o, wo_ref[0])


def _xattn(a, wa, x, gain, wq, wo, mem_k, mem_v, layer, in_place):
    rows, d = x.shape
    _, batch, n_mem, _ = mem_k.shape
    seq = rows // batch
    nb = min(batch, max(1, WIDE_ROW_TILE // seq))
    assert batch % nb == 0
    tm = nb * seq if nb > 1 else min(XATTN_ROW_TILE, seq)
    nt = 1 if nb > 1 else seq // tm
    row_spec = pl.BlockSpec((tm, d), lambda b, i: (b * nt + i, 0))
    weight_spec = pl.BlockSpec((1, d, d), lambda b, i: (layer, 0, 0))
    mem_spec = pl.BlockSpec((1, nb, n_mem, d), lambda b, i: (layer, b, 0, 0))
    return pl.pallas_call(
        functools.partial(_xattn_kernel, heads=XA_HEADS),
        grid=(batch // nb, nt),
        in_specs=[row_spec, pl.BlockSpec((d, d), lambda b, i: (0, 0)), row_spec,
                  pl.BlockSpec((1, d), lambda b, i: (0, 0)), weight_spec, weight_spec, mem_spec, mem_spec],
        out_specs=row_spec,
        out_shape=jax.ShapeDtypeStruct((rows, d), F32),
        input_output_aliases={2: 0} if in_place else {},
        compiler_params=_params("arbitrary", "arbitrary"),
        name="xattn",
    )(a, wa, x, gain.reshape(1, d), wq, wo, mem_k, mem_v)


def _ffn_kernel(x_ref, g_ref, wgu_ref, wd_ref, o_ref, *, final_gain):
    x = x_ref[...]
    hidden = wd_ref.shape[1]
    h = _rms(x, g_ref[0:1, :]).astype(BF16)
    acc = x
    for c0 in range(0, hidden, FFN_CHUNK):
        c1 = min(c0 + FFN_CHUNK, hidden)
        gate = _dot(h, wgu_ref[0, :, c0:c1])
        up = _dot(h, wgu_ref[0, :, hidden + c0:hidden + c1])
        act = (gate * jax.nn.sigmoid(gate) * up).astype(BF16)
        acc = acc + _dot(act, wd_ref[0, c0:c1, :])
    if final_gain:
        acc = _rms(acc, g_ref[1:2, :])
    o_ref[...] = acc


def _ffn(x, gain, w_gate_up, w_down, layer, final_gain=None):
    rows, d = x.shape
    tm = min(WIDE_ROW_TILE, rows)
    gains = jnp.stack([gain, gain if final_gain is None else final_gain])
    return pl.pallas_call(
        functools.partial(_ffn_kernel, final_gain=final_gain is not None),
        grid=(rows // tm,),
        in_specs=[pl.BlockSpec((tm, d), lambda i: (i, 0)),
                  pl.BlockSpec((2, d), lambda i: (0, 0)),
                  pl.BlockSpec((1,) + w_gate_up.shape[1:], lambda i: (layer, 0, 0)),
                  pl.BlockSpec((1,) + w_down.shape[1:], lambda i: (layer, 0, 0))],
        out_specs=pl.BlockSpec((tm, d), lambda i: (i, 0)),
        out_shape=jax.ShapeDtypeStruct((rows, d), F32),
        input_output_aliases={0: 0},
        compiler_params=_params("arbitrary"),
        name="ffn",
    )(x, gains, w_gate_up, w_down)


def _rw_proj_kernel(*refs, has_v_res):
    if has_v_res:
        (x_ref, gain_ref, shift0_ref, mu_ref, wr_ref, wk_ref, wv_ref, vec_ref,
         w1_ref, w2_ref, a1_ref, a2_ref, g1_ref, g2_ref, v1_ref, v2_ref, vfirst_ref,
         r_ref, lw_ref, k_ref, v_ref, kk_ref, a_ref, g_ref, shift_ref, prev_ref) = refs
    else:
        (x_ref, gain_ref, shift0_ref, mu_ref, wr_ref, wk_ref, wv_ref, vec_ref,
         w1_ref, w2_ref, a1_ref, a2_ref, g1_ref, g2_ref,
         r_ref, lw_ref, k_ref, v_ref, kk_ref, a_ref, g_ref, shift_ref, prev_ref) = refs
    i = pl.program_id(1)
    h = _rms(x_ref[...], gain_ref[...])
    tm = h.shape[0]

    @pl.when(i == 0)
    def _():
        prev_ref[...] = shift0_ref[0]

    row = lax.broadcasted_iota(jnp.int32, h.shape, 0)
    h_prev = jnp.where(row == 0, prev_ref[...], pltpu.roll(h, 1, 0))
    last = h[tm - 1:tm, :]
    prev_ref[...] = last
    shift_ref[0] = last

    xx = h_prev - h
    mixed = lambda j: (h + xx * mu_ref[j:j + 1, :]).astype(BF16)
    w0, a0, k_k, k_a, v0 = (vec_ref[j:j + 1, :] for j in range(5))

    r_ref[...] = _dot(mixed(0), wr_ref[...]).astype(r_ref.dtype)
    dec_in = _dot(mixed(1), w1_ref[...])
    k = _dot(mixed(2), wk_ref[...])
    dec = w0 + _dot(jnp.tanh(dec_in).astype(BF16), w2_ref[...])
    x_v = mixed(3)
    v = _dot(x_v, wv_ref[...])
    lw_ref[...] = -jnp.exp(-jax.nn.softplus(-dec) - 0.5)
    if has_v_res:
        mix = jax.nn.sigmoid(v0 + _dot(_dot(x_v, v1_ref[...]).astype(BF16), v2_ref[...]))
        v = v + (vfirst_ref[...] - v) * mix
    v_ref[...] = v
    a_in = _dot(mixed(4), a1_ref[...])
    g_in = _dot(mixed(5), g1_ref[...])
    a = jax.nn.sigmoid(a0 + _dot(a_in.astype(BF16), a2_ref[...]))
    kk_ref[...] = (k * k_k).astype(kk_ref.dtype)
    k_ref[...] = (k * (1.0 + (a - 1.0) * k_a)).astype(k_ref.dtype)
    a_ref[...] = a.astype(a_ref.dtype)
    g_ref[...] = _dot(jax.nn.sigmoid(g_in).astype(BF16), g2_ref[...]).astype(g_ref.dtype)


def _pad_cols(w, mult=LANES):
    pad = -w.shape[1] % mult
    return jnp.pad(w, ((0, 0), (0, pad))) if pad else w


def _pad_rows(w, mult=LANES):
    pad = -w.shape[0] % mult
    return jnp.pad(w, ((0, pad), (0, 0))) if pad else w


def _rw_proj(x, batch, gain, shift0, p, v_first):
    rows, d = x.shape
    seq = rows // batch
    tm = min(WIDE_ROW_TILE, seq)
    nt = seq // tm
    has_v_res = v_first is not None
    zeros = jnp.zeros((d,), F32)
    vec = jnp.stack([p["w0"], p["a0"], p["k_k"], p["k_a"], p["v0"] if has_v_res else zeros,
                     zeros, zeros, zeros])
    row_spec = pl.BlockSpec((tm, d), lambda b, i: (b * nt + i, 0))

    def full(arr):
        return pl.BlockSpec(arr.shape, lambda b, i: (0,) * arr.ndim, pipeline_mode=pl.Buffered(1))

    ins = [x, gain.reshape(1, d), shift0.reshape(batch, 1, d), p["mu"], p["w_r"], p["w_k"], p["w_v"], vec,
           p["w1"], p["w2"], p["a1"], p["a2"], p["g1"], p["g2"]]
    in_specs = [row_spec, full(ins[1]), pl.BlockSpec((1, 1, d), lambda b, i: (b, 0, 0))]
    in_specs += [full(a) for a in ins[3:]]
    if has_v_res:
        ins += [p["v1"], p["v2"], v_first]
        in_specs += [full(p["v1"]), full(p["v2"]), row_spec]
    tok = [jax.ShapeDtypeStruct((rows, d), dt) for dt in (BF16, F32, BF16, F32, BF16, BF16, BF16)]
    outs = pl.pallas_call(
        functools.partial(_rw_proj_kernel, has_v_res=has_v_res),
        grid=(batch, nt),
        in_specs=in_specs,
        out_specs=[row_spec] * 7 + [pl.BlockSpec((1, 1, d), lambda b, i: (b, 0, 0))],
        out_shape=tok + [jax.ShapeDtypeStruct((batch, 1, d), F32)],
        scratch_shapes=[pltpu.VMEM((1, d), F32)],
        compiler_params=_params("arbitrary", "arbitrary"),
        name="rw_proj",
    )(*ins)
    return outs[:7], outs[7].reshape(batch, d)


WKV_BASE_BLOCK = 4
WKV_DECAY_PIECES = 3
WKV_GROUP_PIECES = 2
WKV_STATE_PASSES = 1
WKV_SOLVE_PASSES = 1


def _wkv_prepare(units, c_):
    n = units[0][0].shape[0]
    sum_l = lambda x: _mm_exact_lhs(c_["tri_incl"], x, WKV_DECAY_PIECES)
    sum_g = lambda x: _mm_exact_rhs(x, c_["gsum"], WKV_GROUP_PIECES)
    solve_mm = lambda x, y, nt=False: _mm(x, y, WKV_SOLVE_PASSES, nt=nt)

    cs = [sum_l(u[1]) for u in units]
    head_sums = [sum_g(jnp.concatenate([u[4] * u[4], u[0] * u[2] * u[7][0:1, :]], axis=0)) for u in units]
    yield
    pre = []
    for (r, lw, k, v, kk, a, g, vec), cs_u, hs in zip(units, cs, head_sums):
        cs_last = cs_u[n - 1:n, :]
        d_inv = jnp.exp(-cs_u)
        d_rest = jnp.exp(cs_last - cs_u)
        kkn = kk * lax.rsqrt(jnp.maximum(hs[:n], 1e-24))
        b = kkn * a
        pre.append(dict(a_t=-kkn * jnp.exp(cs_u - lw), r_t=r * jnp.exp(cs_u), b_t=b * d_inv, k_t=k * d_inv,
                        bk_rest=jnp.concatenate([b * d_rest, k * d_rest], axis=0), decay=jnp.exp(cs_last),
                        v=v, g=g, vec=vec, bonus=hs[n:] * v))
    yield

    by_head = lambda x: jnp.concatenate([jnp.where(m, x, 0.0) for m in c_["head_masks"]], axis=0)
    block_diag = lambda w: jnp.where(c_["same_block"], jnp.concatenate([w, w], axis=0), 0.0)
    grams = [solve_mm(jnp.concatenate([q["a_t"], q["r_t"]], axis=0),
                      jnp.concatenate([by_head(q["b_t"]), by_head(q["k_t"])], axis=0), nt=True)
             for q in pre]
    yield
    lows = [jnp.where(c_["strict_w"], gm[:n, :2 * n], 0.0) for gm in grams]
    m_aks = [jnp.where(c_["strict_w"], gm[:n, 2 * n:], 0.0) for gm in grams]
    for q, gm in zip(pre, grams):
        q["m_r"] = jnp.where(c_["incl_w2"], gm[n:], 0.0)
        q["v_heads"] = by_head(q["v"])
    from_v = [solve_mm(mk, q["v_heads"]) for mk, q in zip(m_aks, pre)]
    yield
    diag = [jnp.where(c_["diag_block"], lo, 0.0) for lo in lows]
    inv = [c_["eye_w"] + dg for dg in diag]
    sq = [solve_mm(dg, block_diag(dg)) for dg in diag]
    yield
    inv = [t + solve_mm(t, block_diag(p2)) for t, p2 in zip(inv, sq)]
    yield
    for below in c_["below_blocks"]:
        half = [solve_mm(t, block_diag(jnp.where(below, lo, 0.0))) for t, lo in zip(inv, lows)]
        yield
        inv = [t + solve_mm(hf, block_diag(t)) for t, hf in zip(inv, half)]
        yield
    for q, t, fv in zip(pre, inv, from_v):
        q["inv"], q["from_v"] = t, fv
    return pre


def _wkv_advance(pre, states, c_):
    n = pre[0]["v"].shape[0]
    inv_hd = 1.0 / RW_HEAD_DIM
    sum_g = lambda x: _mm_exact_rhs(x, c_["gsum"], WKV_GROUP_PIECES)
    solve_mm = lambda x, y: _mm(x, y, WKV_SOLVE_PASSES)
    by_head = lambda x: jnp.concatenate([jnp.where(m, x, 0.0) for m in c_["head_masks"]], axis=0)

    from_state = [_mm(jnp.concatenate([q["a_t"], q["r_t"]], axis=0), s, WKV_STATE_PASSES, nt=True)
                  for q, s in zip(pre, states)]
    yield
    sig_p = [solve_mm(q["inv"], by_head(fs[:n] + q["from_v"])) for q, fs in zip(pre, from_state)]
    yield
    y_p = [fs[n:] + solve_mm(q["m_r"], jnp.concatenate([by_head(sg), q["v_heads"]], axis=0))
           for q, fs, sg in zip(pre, from_state, sig_p)]
    upd = [_mm(jnp.concatenate([s, q["v"]], axis=0).T, q["bk_rest"], WKV_STATE_PASSES)
           for s, q in zip(sig_p, pre)]
    new_states = [s * q["decay"] + jnp.where(c_["same_head"], u, 0.0) for s, q, u in zip(states, pre, upd)]

    yield
    mean = [sum_g(y) * inv_hd for y in y_p]
    dev = [y - m for y, m in zip(y_p, mean)]
    yield
    var = [sum_g(dv * dv) * inv_hd for dv in dev]
    yield
    outs = []
    for q, dv, vr in zip(pre, dev, var):
        o = dv * lax.rsqrt(vr + RW_GN_EPS) * q["vec"][1:2, :] + q["vec"][2:3, :]
        outs.append((o + q["bonus"]) * q["g"])
    return outs, new_states


def _wkv_advance_chunks(pre_chunks, states, c_):
    outs = []
    for pre in pre_chunks:
        out, states = yield from _wkv_advance(pre, states, c_)
        outs.append(jnp.concatenate(out, axis=1))
    return outs, states


def _in_lock_step(*staged):
    results = [None] * len(staged)
    live = list(enumerate(staged))
    while live:
        still = []
        for i, gen in live:
            try:
                next(gen)
                still.append((i, gen))
            except StopIteration as done:
                results[i] = done.value
        live = still
    return results


def _wkv_kernel(*refs, has_state0, n):
    if has_state0:
        (r_ref, lw_ref, k_ref, v_ref, kk_ref, a_ref, g_ref, vec_ref, s0_ref,
         o_ref, sout_ref, state_ref) = refs
    else:
        (r_ref, lw_ref, k_ref, v_ref, kk_ref, a_ref, g_ref, vec_ref,
         o_ref, sout_ref, state_ref) = refs
    c = pl.program_id(2)
    hd = RW_HEAD_DIM

    @pl.when(c == 0)
    def _():
        if has_state0:
            state_ref[...] = s0_ref[0]
        else:
            state_ref[...] = jnp.zeros_like(state_ref)

    tr = lax.broadcasted_iota(jnp.int32, (n, n), 0)
    tc = lax.broadcasted_iota(jnp.int32, (n, n), 1)
    tr2 = lax.broadcasted_iota(jnp.int32, (n, 2 * n), 0)
    tc2 = lax.rem(lax.broadcasted_iota(jnp.int32, (n, 2 * n), 1), n)
    tr4 = lax.broadcasted_iota(jnp.int32, (n, 4 * n), 0)
    tc4 = lax.rem(lax.broadcasted_iota(jnp.int32, (n, 4 * n), 1), n)
    br = lax.broadcasted_iota(jnp.int32, (2 * n, 2 * n), 0) // n
    bc = lax.broadcasted_iota(jnp.int32, (2 * n, 2 * n), 1) // n
    lane = lax.broadcasted_iota(jnp.int32, (n, LANES), 1)
    sr = lax.broadcasted_iota(jnp.int32, (LANES, LANES), 0) // hd
    sc = lax.broadcasted_iota(jnp.int32, (LANES, LANES), 1) // hd
    sizes = [WKV_BASE_BLOCK << i for i in range(n.bit_length()) if (WKV_BASE_BLOCK << i) < n]
    below_blocks = []
    for s in sizes:
        u, v = tr2 // s, tc2 // s
        below_blocks.append(jnp.where((u & 1) == 1, u - 1, -1) == v)
    consts = dict(
        diag_block=(tr2 // WKV_BASE_BLOCK) == (tc2 // WKV_BASE_BLOCK),
        below_blocks=below_blocks,
        gsum=jnp.concatenate([_group_ones(LANES, hd)] * WKV_GROUP_PIECES, axis=0),
        tri_incl=jnp.concatenate([jnp.where(tc <= tr, 1.0, 0.0).astype(BF16)] * WKV_DECAY_PIECES, axis=1),
        strict_w=tc2 < tr2,
        incl_w2=tc4 <= tr4,
        eye_w=jnp.where(tc2 == tr2, 1.0, 0.0).astype(F32),
        same_block=br == bc,
        head_masks=[lane < hd, lane >= hd],
        same_head=sr == sc,
    )
    n_pairs = state_ref.shape[0]
    n_chunks = r_ref.shape[0] // n

    def units(chunks):
        out = []
        for ci in chunks:
            rows = slice(ci * n, (ci + 1) * n)
            for p in range(n_pairs):
                sl = slice(p * LANES, (p + 1) * LANES)
                out.append(tuple(ref[rows, sl].astype(F32)
                                 for ref in (r_ref, lw_ref, k_ref, v_ref, kk_ref, a_ref, g_ref)) + (vec_ref[:, sl],))
        return out

    def per_chunk(pre):
        return [pre[i:i + n_pairs] for i in range(0, len(pre), n_pairs)]

    first = range((n_chunks + 1) // 2)
    second = range(len(first), n_chunks)
    states = [state_ref[p] for p in range(n_pairs)]
    (pre_first,) = _in_lock_step(_wkv_prepare(units(first), consts))
    if second:
        (outs, states), pre_second = _in_lock_step(
            _wkv_advance_chunks(per_chunk(pre_first), states, consts), _wkv_prepare(units(second), consts))
        ((outs_second, states),) = _in_lock_step(_wkv_advance_chunks(per_chunk(pre_second), states, consts))
        outs = outs + outs_second
    else:
        ((outs, states),) = _in_lock_step(_wkv_advance_chunks(per_chunk(pre_first), states, consts))
    o_ref[...] = jnp.concatenate(outs, axis=0).astype(o_ref.dtype)
    new_states = jnp.stack(states)
    state_ref[...] = new_states
    sout_ref[0] = new_states


def _wkv(tok, batch, vec, state0):
    rows, d = tok[0].shape
    seq = rows // batch
    n = min(WKV_CHUNK, seq)
    span = n * min(WKV_CHUNKS_PER_STEP, seq // n)
    nc = seq // span
    pairs = d // LANES
    gp = WKV_PAIRS
    width = gp * LANES
    has_state0 = state0 is not None
    tok_spec = pl.BlockSpec((span, width), lambda b, p, c: (b * nc + c, p))
    state_spec = pl.BlockSpec((1, gp, LANES, LANES), lambda b, p, c: (b, p, 0, 0))
    ins = list(tok) + [vec]
    in_specs = [tok_spec] * 7 + [pl.BlockSpec((8, width), lambda b, p, c: (0, p))]
    if has_state0:
        ins.append(state0)
        in_specs.append(state_spec)
    return pl.pallas_call(
        functools.partial(_wkv_kernel, has_state0=has_state0, n=n),
        grid=(batch, pairs // gp, nc),
        in_specs=in_specs,
        out_specs=[tok_spec, state_spec],
        out_shape=[jax.ShapeDtypeStruct((rows, d), BF16),
                   jax.ShapeDtypeStruct((batch, pairs, LANES, LANES), F32)],
        scratch_shapes=[pltpu.VMEM((gp, LANES, LANES), F32)],
        compiler_params=_params("arbitrary", "arbitrary", "arbitrary"),
        name="wkv",
    )(*ins)


def _state_to_blockdiag(state):
    b, h, n, _ = state.shape
    s = state.reshape(b, h // 2, 2, n, n)
    z = jnp.zeros_like(s[:, :, 0])
    top = jnp.concatenate([s[:, :, 0], z], axis=-1)
    bot = jnp.concatenate([z, s[:, :, 1]], axis=-1)
    return jnp.concatenate([top, bot], axis=-2)


def _state_from_blockdiag(bd):
    b, p, n2, _ = bd.shape
    n = n2 // 2
    return jnp.stack([bd[:, :, :n, :n], bd[:, :, n:, n:]], axis=2).reshape(b, 2 * p, n, n)


def _trunk(x, batch, mem_k, mem_v, past_k, past_v, rw_state0, rw_shift0, w):
    depth = w["ffn_gu"].shape[0]
    d = x.shape[-1]
    sb_k, sb_v, rw_states, rw_shifts = None, None, [], []
    n_sb = (depth + 1) // 2
    v_first = None
    for layer in range(depth):
        j = layer // 2
        if layer % 2 == 0:
            sb_k, sb_v, qkv = _sb_qkv(x, batch, w["norm_mix"][layer], w["sb_qkv"][j], j, n_sb, sb_k, sb_v)
            if past_k is None:
                o = _sb_attention_prompt(qkv, batch)
            else:
                o = _sb_attention_sample(qkv, batch, past_k, past_v, j)
            w_o = w["sb_o"][j]
        else:
            if rw_state0 is None:
                state0 = None
                shift0 = jnp.zeros((batch, d), F32)
            else:
                state0 = _state_to_blockdiag(rw_state0[j])
                shift0 = rw_shift0[j]
            tok, shift = _rw_proj(x, batch, w["norm_mix"][layer], shift0, w["rw"][j], v_first)
            if v_first is None:
                v_first = tok[3]
            o, state = _wkv(tok, batch, w["rw"][j]["out_vec"], state0)
            rw_states.append(_state_from_blockdiag(state))
            rw_shifts.append(shift)
            w_o = w["rw"][j]["w_o"]
        x = _xattn(o, w_o, x, w["norm_xattn"][layer], w["xa_q"], w["xa_o"], mem_k, mem_v, layer,
                   in_place=layer > 0)
        final = w["norm_final"] if layer == depth - 1 else None
        x = _ffn(x, w["norm_ffn"][layer], w["ffn_gu"], w["ffn_down"], layer, final)
    return x, sb_k, sb_v, jnp.stack(rw_states), jnp.stack(rw_shifts)


def kernel(x_prompt, x_sample, mem_prompt, cache_sb_k, cache_sb_v, state_rwkv_wkv, state_rwkv_shift,
           cache_mem_k, cache_mem_v, norm_mix, norm_xattn, norm_ffn, norm_mem, norm_final,
           sb_w_qkv, sb_w_o, rw_mu, rw_w_r, rw_w_k, rw_w_v, rw_w_o, rw_w0, rw_w1, rw_w2,
           rw_a0, rw_a1, rw_a2, rw_v0, rw_v1, rw_v2, rw_g1, rw_g2, rw_k_k, rw_k_a, rw_r_k,
           rw_lnx_w, rw_lnx_b, xa_w_q, xa_w_kv, xa_w_o, ffn_w_gate_up, ffn_w_down):
    batch, seq, d = x_prompt.shape
    dec_batch, dec_seq, _ = x_sample.shape
    depth = norm_mix.shape[0]
    n_rw = rw_w_r.shape[0]
    n_sb = sb_w_qkv.shape[0]
    n_mem = mem_prompt.shape[1]
    bf = lambda t: t.astype(BF16)
    per_layer = lambda t: [bf(t[i]) for i in range(t.shape[0])]

    rw = []
    for j in range(n_rw):
        pj = dict(mu=rw_mu[j], w_r=bf(rw_w_r[j]), w_k=bf(rw_w_k[j]), w_v=bf(rw_w_v[j]), w_o=bf(rw_w_o[j]),
                  w0=rw_w0[j], a0=rw_a0[j], k_k=rw_k_k[j], k_a=rw_k_a[j],
                  w1=bf(_pad_cols(rw_w1[j])), w2=bf(_pad_rows(rw_w2[j])),
                  a1=bf(_pad_cols(rw_a1[j])), a2=bf(_pad_rows(rw_a2[j])),
                  g1=bf(_pad_cols(rw_g1[j])), g2=bf(_pad_rows(rw_g2[j])))
        if j > 0:
            pj.update(v0=rw_v0[j - 1], v1=bf(_pad_cols(rw_v1[j - 1])), v2=bf(_pad_rows(rw_v2[j - 1])))
        zeros = jnp.zeros((d,), F32)
        pj["out_vec"] = jnp.stack([rw_r_k[j].reshape(d), rw_lnx_w[j], rw_lnx_b[j]] + [zeros] * 5)
        rw.append(pj)
    w = dict(norm_mix=norm_mix, norm_xattn=norm_xattn, norm_ffn=norm_ffn, norm_final=norm_final,
             sb_qkv=per_layer(sb_w_qkv), sb_o=per_layer(sb_w_o), rw=rw,
             xa_q=bf(xa_w_q), xa_o=bf(xa_w_o), ffn_gu=bf(ffn_w_gate_up), ffn_down=bf(ffn_w_down))

    mem_k, mem_v, mem_k_bf, mem_v_bf = _mem_kv(mem_prompt, norm_mem, bf(xa_w_kv))
    y_p, p_sb_k, p_sb_v, p_wkv, p_shift = _trunk(
        x_prompt.reshape(batch * seq, d), batch, mem_k_bf, mem_v_bf, None, None, None, None, w)

    y_s, s_sb_k, s_sb_v, s_wkv, s_shift = _trunk(
        x_sample.reshape(dec_batch * dec_seq, d), dec_batch,
        bf(cache_mem_k).reshape(depth, dec_batch, n_mem, d), bf(cache_mem_v).reshape(depth, dec_batch, n_mem, d),
        cache_sb_k, cache_sb_v,
        state_rwkv_wkv, state_rwkv_shift, w)

    tokens_first = lambda t: jnp.transpose(t, (0, 1, 4, 2, 3))
    return (y_p.reshape(batch, seq, d), y_s.reshape(dec_batch, dec_seq, d),
            tokens_first(p_sb_k), tokens_first(p_sb_v), p_wkv, p_shift, mem_k, mem_v,
            tokens_first(s_sb_k), tokens_first(s_sb_v), s_wkv, s_shift)
```

```python
import functools

import jax
import jax.numpy as jnp
from jax import lax
from jax.experimental import pallas as pl
from jax.experimental.pallas import tpu as pltpu

F32 = jnp.float32
BF16 = jnp.bfloat16

NORM_EPS = 1e-6
NEG_LOG2_E = -1.4426950408889634
RW_GN_EPS = 64e-5
SB_HEADS = 16
XA_HEADS = 4
RW_HEAD_DIM = 64
LANES = 128
ROW_TILE = 256
WIDE_ROW_TILE = 512
XATTN_ROW_TILE = 1024
SB_BLOCK = 256
SB_PAIRS = 8
SB_WAVES = 4
SB_DECODE_WAVES = 2
SB_UNROLL = 2
SB_PAST_SPAN = 2048
WKV_CHUNK = 64
WKV_CHUNKS_PER_STEP = 4
WKV_PAIRS = 8
FFN_CHUNK = 256
FFN_STREAM_SLABS = 2
VMEM_LIMIT_BYTES = 56 * 1024 * 1024


def _params(*sem):
    return pltpu.CompilerParams(dimension_semantics=sem, vmem_limit_bytes=VMEM_LIMIT_BYTES)


def _dot(a, b):
    return jnp.dot(a, b, preferred_element_type=F32)


def _dot_nt(a, b):
    return lax.dot_general(a, b, (((1,), (1,)), ((), ())), preferred_element_type=F32)


def _split(x, n):
    parts = []
    for i in range(n):
        p = x.astype(BF16)
        parts.append(p)
        if i + 1 < n:
            x = x - p.astype(F32)
    return parts


def _mm(a, b, passes, nt=False):
    dot = _dot_nt if nt else _dot
    if passes == 1:
        return dot(a.astype(BF16), b.astype(BF16))
    a_hi, a_lo = _split(a, 2)
    b_hi, b_lo = _split(b, 2)
    return dot(a_hi, b_hi) + (dot(a_hi, b_lo) + dot(a_lo, b_hi))


def _mm_exact_rhs(a, b_stacked, n):
    return _dot(jnp.concatenate(_split(a, n), axis=1), b_stacked)


def _mm_exact_lhs(a_repeated, b, n):
    return _dot(a_repeated, jnp.concatenate(_split(b, n), axis=0))


def _rms(x, gain):
    ms = jnp.mean(x * x, axis=-1, keepdims=True)
    return x * lax.rsqrt(ms + NORM_EPS) * gain


def _group_ones(n, group):
    r = lax.broadcasted_iota(jnp.int32, (n, n), 0) // group
    c = lax.broadcasted_iota(jnp.int32, (n, n), 1) // group
    return jnp.where(r == c, 1.0, 0.0).astype(BF16)


def _mem_kv_kernel(x_ref, g_ref, w_ref, k_ref, v_ref, kb_ref, vb_ref):
    tm, d = x_ref.shape
    heads, dh = k_ref.shape[-2:]
    h = _rms(x_ref[...], g_ref[0]).astype(BF16)
    kv = _dot(h, w_ref[0])
    k, v = kv[:, :d], kv[:, d:]
    k_ref[0, 0] = k.reshape(tm, heads, dh)
    v_ref[0, 0] = v.reshape(tm, heads, dh)
    kb_ref[0, 0] = k.astype(BF16)
    vb_ref[0, 0] = v.astype(BF16)


def _mem_kv(mem, gains, w_kv):
    depth = w_kv.shape[0]
    batch, n_mem, d = mem.shape
    tm = min(ROW_TILE, n_mem)
    nt = n_mem // tm
    dh = d // XA_HEADS
    assert n_mem % tm == 0 and dh % LANES == 0
    out = jax.ShapeDtypeStruct((depth, batch, n_mem, XA_HEADS, dh), F32)
    out_b = jax.ShapeDtypeStruct((depth, batch, n_mem, d), BF16)
    spec = pl.BlockSpec((1, 1, tm, XA_HEADS, dh), lambda l, i: (l, i // nt, i % nt, 0, 0))
    spec_b = pl.BlockSpec((1, 1, tm, d), lambda l, i: (l, i // nt, i % nt, 0))
    return pl.pallas_call(
        _mem_kv_kernel,
        grid=(depth, batch * nt),
        in_specs=[pl.BlockSpec((tm, d), lambda l, i: (i, 0)),
                  pl.BlockSpec((1, 1, d), lambda l, i: (l, 0, 0)),
                  pl.BlockSpec((1, d, 2 * d), lambda l, i: (l, 0, 0))],
        out_specs=[spec, spec, spec_b, spec_b],
        out_shape=[out, out, out_b, out_b],
        compiler_params=_params("arbitrary", "arbitrary"),
        name="mem_kv",
    )(mem.reshape(batch * n_mem, d), gains.reshape(depth, 1, d), w_kv)


def _sb_qkv_kernel(x_ref, g_ref, w_ref, *rest, q_scale, slot):
    k_ref, v_ref, qkv_ref = rest[-3:]
    tm, d = x_ref.shape
    heads, hd = k_ref.shape[2:4]
    h = _rms(x_ref[...], g_ref[...]).astype(BF16)

    def per_head_transposed(t):
        return t.T.reshape(heads, hd, tm)

    own = slot if k_ref.shape[0] > 1 else 0
    for s in range(k_ref.shape[0]):
        if s != own:
            k_ref[s, 0] = jnp.zeros(k_ref.shape[2:], k_ref.dtype)
            v_ref[s, 0] = jnp.zeros(v_ref.shape[2:], v_ref.dtype)
    k = _dot(h, w_ref[:, d:2 * d])
    qkv_ref[:, d:2 * d] = k.astype(BF16)
    k_ref[own, 0] = per_head_transposed(k)
    v = _dot(h, w_ref[:, 2 * d:])
    qkv_ref[:, 2 * d:] = v.astype(BF16)
    v_ref[own, 0] = per_head_transposed(v)
    qkv_ref[:, :d] = (_dot(h, w_ref[:, :d]) * q_scale).astype(BF16)


def _sb_qkv(x, batch, gain, w, slot, n_slots, k_all=None, v_all=None):
    rows, d = x.shape
    seq = rows // batch
    tm = min(ROW_TILE, seq)
    nt = seq // tm
    hd = d // SB_HEADS
    q_scale = float(hd ** -0.5)
    ins = [x, gain.reshape(1, d), w]
    in_specs = [pl.BlockSpec((tm, d), lambda i: (i, 0)),
                pl.BlockSpec((1, d), lambda i: (0, 0)),
                pl.BlockSpec(w.shape, lambda i: (0, 0))]
    if k_all is None:
        aliases = {}
        slot_spec = pl.BlockSpec((n_slots, 1, SB_HEADS, hd, tm), lambda i: (0, i // nt, 0, 0, i % nt))
    else:
        ins += [k_all, v_all]
        in_specs += [pl.BlockSpec(memory_space=pl.ANY)] * 2
        aliases = {3: 0, 4: 1}
        slot_spec = pl.BlockSpec((1, 1, SB_HEADS, hd, tm), lambda i: (slot, i // nt, 0, 0, i % nt))
    kv_shape = jax.ShapeDtypeStruct((n_slots, batch, SB_HEADS, hd, seq), F32)
    return pl.pallas_call(
        functools.partial(_sb_qkv_kernel, q_scale=q_scale, slot=slot),
        grid=(rows // tm,),
        in_specs=in_specs,
        out_specs=[slot_spec, slot_spec, pl.BlockSpec((tm, 3 * d), lambda i: (i, 0))],
        out_shape=[kv_shape, kv_shape, jax.ShapeDtypeStruct((rows, 3 * d), BF16)],
        input_output_aliases=aliases,
        compiler_params=_params("arbitrary"),
        name="sb_qkv",
    )(*ins)


def _sb_step(q2s, kv_blocks, carry, acc, neg_lower2, mask, n_waves):
    m = q2s[0].shape[0]
    groups = len(q2s)
    per = max(groups // n_waves, 1)
    waves = [range(c, min(c + per, groups)) for c in range(0, groups, per)]
    rows = [slice(w_[0] * m, (w_[-1] + 1) * m) for w_ in waves]
    units = [(b, c) for b in range(len(kv_blocks)) for c in range(len(waves))]

    def scores(b, c):
        keys, _, keys_on_lanes = kv_blocks[b]
        dot = _dot if keys_on_lanes else _dot_nt
        return jnp.concatenate([dot(q2s[g], keys[g]) for g in waves[c]], axis=0)

    def softplus_pieces(c, z):
        sp = jnp.maximum(z, 0.0) + jnp.log(1.0 + jnp.exp2(jnp.abs(z) * NEG_LOG2_E))
        if mask is not None:
            sp = jnp.where(mask[rows[c]], sp, 0.0)
        return jnp.concatenate(_split(sp, 2), axis=1)

    def weights(c, z, from_s, later):
        w = jnp.exp(z + from_s + jnp.concatenate([later] * (z.shape[1] // LANES), axis=1))
        if mask is not None:
            w = jnp.where(mask[rows[c]], w, 0.0)
        return w.astype(BF16)

    def values(b, c, w):
        _, vals, keys_on_lanes = kv_blocks[b]
        dot = _dot_nt if keys_on_lanes else _dot
        return jnp.concatenate([dot(w[i * m:(i + 1) * m], vals[g]) for i, g in enumerate(waves[c])], axis=0)

    n = len(units)
    z, pieces, from_s, w = ([None] * n for _ in range(4))
    later = [carry[r] for r in rows]
    pv = [None] * len(waves)
    for tick in range(n + 4):
        for u, (b, c) in enumerate(units):
            stage = tick - u
            if stage == 0:
                z[u] = scores(b, c)
            elif stage == 1:
                pieces[u] = softplus_pieces(c, z[u])
            elif stage == 2:
                from_s[u] = _dot(pieces[u], neg_lower2)
            elif stage == 3:
                w[u] = weights(c, z[u], from_s[u], later[c])
                later[c] = later[c] + jnp.broadcast_to(from_s[u][:, 0:1], later[c].shape)
            elif stage == 4:
                out = values(b, c, w[u])
                pv[c] = out if pv[c] is None else pv[c] + out
    return jnp.concatenate(later, axis=0), acc + jnp.concatenate(pv, axis=0)


def _sb_consts(bk):
    r = lax.broadcasted_iota(jnp.int32, (2, bk, bk), 1).reshape(2 * bk, bk)
    c = lax.broadcasted_iota(jnp.int32, (2 * bk, bk), 1)
    return jnp.where(r >= c, -1.0, 0.0).astype(BF16)


def _sb_stack_heads(q_ref, q2_ref):
    tq = q_ref.shape[0]
    lane = lax.broadcasted_iota(jnp.int32, (tq, LANES), 1)
    for g in range(q2_ref.shape[0]):
        q = q_ref[:, g * LANES:(g + 1) * LANES]
        zero = jnp.zeros_like(q)
        q2_ref[g] = jnp.concatenate([jnp.where(lane < LANES // 2, q, zero),
                                     jnp.where(lane >= LANES // 2, q, zero)], axis=0)


def _sb_causal_mask(groups, tq, bk, t0=0):
    t = lax.broadcasted_iota(jnp.int32, (2 * groups, tq, bk), 1).reshape(2 * groups * tq, bk)
    s = lax.broadcasted_iota(jnp.int32, (2 * groups * tq, bk), 1)
    return s < t + t0


def _sb_write(o_ref, acc_ref):
    tq = o_ref.shape[0]
    lane = lax.broadcasted_iota(jnp.int32, (tq, LANES), 1)
    for g in range(acc_ref.shape[0] // (2 * tq)):
        o_ref[:, g * LANES:(g + 1) * LANES] = jnp.where(
            lane < LANES // 2, acc_ref[2 * g * tq:(2 * g + 1) * tq, :],
            acc_ref[(2 * g + 1) * tq:(2 * g + 2) * tq, :]).astype(o_ref.dtype)


def _sb_prompt_kernel(q_ref, k_ref, v_ref, o_ref, q2_ref, carry_ref, acc_ref):
    blk = q_ref.shape[0]
    groups = q2_ref.shape[0]
    i = pl.program_id(2)
    neg_lower2 = _sb_consts(blk)
    _sb_stack_heads(q_ref, q2_ref)
    lanes = [slice(g * LANES, (g + 1) * LANES) for g in range(groups)]

    def block(j):
        row0 = pl.multiple_of(j * blk, blk)
        return ([k_ref[pl.ds(row0, blk), sl] for sl in lanes], [v_ref[pl.ds(row0, blk), sl] for sl in lanes], False)

    def step(blocks, carry, acc, mask):
        return _sb_step([q2_ref[g] for g in range(groups)], blocks, carry, acc, neg_lower2, mask, SB_WAVES)

    half = blk // 2
    d0 = pl.multiple_of(i * blk, blk)
    zero = jnp.zeros((groups * blk, LANES), F32)
    halves = []
    for t0, keys, consts in ((half, blk, neg_lower2), (0, half, _sb_consts(half))):
        q2_half = [jnp.concatenate([q2_ref[g, t0:t0 + half, :], q2_ref[g, blk + t0:blk + t0 + half, :]], axis=0)
                   for g in range(groups)]
        own = ([k_ref[pl.ds(d0, keys), sl] for sl in lanes], [v_ref[pl.ds(d0, keys), sl] for sl in lanes], False)
        halves.append(_sb_step(q2_half, [own], zero, zero, consts, _sb_causal_mask(groups, half, keys, t0),
                               SB_WAVES))
    (carry_late, acc_late), (carry_early, acc_early) = halves
    for g in range(groups):
        for h in range(2):
            src = slice((2 * g + h) * half, (2 * g + h + 1) * half)
            dst = (2 * g + h) * blk
            for ref, early, late in ((carry_ref, carry_early, carry_late), (acc_ref, acc_early, acc_late)):
                ref[dst:dst + half, :] = early[src]
                ref[dst + half:dst + blk, :] = late[src]

    odd = i % SB_UNROLL

    def single(it, c):
        carry_ref[...], acc_ref[...] = step([block(i - 1 - it)], carry_ref[...], acc_ref[...], None)
        return c

    def multi(it, c):
        top = i - 1 - odd - it * SB_UNROLL
        carry_ref[...], acc_ref[...] = step([block(top - u) for u in range(SB_UNROLL)],
                                            carry_ref[...], acc_ref[...], None)
        return c

    lax.fori_loop(0, odd, single, 0)
    lax.fori_loop(0, i // SB_UNROLL, multi, 0)
    _sb_write(o_ref, acc_ref)


def _sb_attention_prompt(qkv, batch):
    rows, d3 = qkv.shape
    d = d3 // 3
    seq = rows // batch
    blk = min(SB_BLOCK, seq)
    nq = seq // blk
    width = SB_PAIRS * LANES
    ng = d // width
    return pl.pallas_call(
        _sb_prompt_kernel,
        grid=(batch, ng, nq),
        in_specs=[pl.BlockSpec((blk, width), lambda b, p, i: (b * nq + i, p)),
                  pl.BlockSpec((seq, width), lambda b, p, i: (b, ng + p)),
                  pl.BlockSpec((seq, width), lambda b, p, i: (b, 2 * ng + p))],
        out_specs=pl.BlockSpec((blk, width), lambda b, p, i: (b * nq + i, p)),
        out_shape=jax.ShapeDtypeStruct((rows, d), BF16),
        scratch_shapes=[pltpu.VMEM((SB_PAIRS, 2 * blk, LANES), BF16),
                        pltpu.VMEM((SB_PAIRS * 2 * blk, LANES), F32),
                        pltpu.VMEM((SB_PAIRS * 2 * blk, LANES), F32)],
        compiler_params=_params("arbitrary", "arbitrary", "arbitrary"),
        name="sb_attention_prompt",
    )(qkv, qkv, qkv)


def _sb_sample_kernel(q_ref, kn_ref, vn_ref, kp_ref, vp_ref, o_ref, q2_ref, carry_ref, acc_ref, *, blk):
    tq = q_ref.shape[0]
    heads, _, hd = q2_ref.shape
    c = pl.program_id(1)
    neg_lower2 = _sb_consts(blk)
    lanes = [slice(h * hd, (h + 1) * hd) for h in range(heads)]

    @pl.when(c == 0)
    def _():
        for h, sl in enumerate(lanes):
            q2_ref[h] = q_ref[:, sl]
        own = ([kn_ref[0, :, sl] for sl in lanes], [vn_ref[0, :, sl] for sl in lanes], False)
        carry_ref[...], acc_ref[...] = _sb_step(
            [q2_ref[h] for h in range(heads)], [own], jnp.zeros(carry_ref.shape, F32),
            jnp.zeros(acc_ref.shape, F32), neg_lower2, _sb_causal_mask(heads // 2, tq, blk), SB_DECODE_WAVES)

    blocks = []
    for sb in reversed(range(kp_ref.shape[-1] // blk)):
        cols = slice(sb * blk, (sb + 1) * blk)
        blocks.append(([kp_ref[0, 0, h, :, cols].astype(BF16) for h in range(heads)],
                       [vp_ref[0, 0, h, :, cols].astype(BF16) for h in range(heads)], True))
    carry_ref[...], acc_ref[...] = _sb_step([q2_ref[h] for h in range(heads)], blocks,
                                            carry_ref[...], acc_ref[...], neg_lower2, None, SB_DECODE_WAVES)

    @pl.when(c == pl.num_programs(1) - 1)
    def _():
        o_ref[...] = jnp.concatenate([acc_ref[h * tq:(h + 1) * tq, :] for h in range(heads)],
                                     axis=1).astype(o_ref.dtype)


def _sb_attention_sample(qkv, batch, past_k, past_v, layer):
    rows, d3 = qkv.shape
    d = d3 // 3
    seq = rows // batch
    blk = SB_BLOCK
    _, _, past_len, heads, hd = past_k.shape
    span = min(SB_PAST_SPAN, past_len)
    steps = past_len // span
    assert seq <= blk and past_len % span == 0 and span % blk == 0 and heads % 2 == 0
    new = qkv[:, d:].reshape(batch, seq, 2 * d)
    new = jnp.pad(new, ((0, 0), (0, blk - seq), (0, 0)))
    past_k = jnp.transpose(past_k, (0, 1, 3, 4, 2))
    past_v = jnp.transpose(past_v, (0, 1, 3, 4, 2))
    past_spec = pl.BlockSpec((1, 1, heads, hd, span), lambda b, c: (layer, b, 0, 0, steps - 1 - c))
    return pl.pallas_call(
        functools.partial(_sb_sample_kernel, blk=blk),
        grid=(batch, steps),
        in_specs=[pl.BlockSpec((seq, d), lambda b, c: (b, 0)),
                  pl.BlockSpec((1, blk, d), lambda b, c: (b, 0, 0)),
                  pl.BlockSpec((1, blk, d), lambda b, c: (b, 0, 1)),
                  past_spec, past_spec],
        out_specs=pl.BlockSpec((seq, d), lambda b, c: (b, 0)),
        out_shape=jax.ShapeDtypeStruct((rows, d), BF16),
        scratch_shapes=[pltpu.VMEM((heads, seq, hd), BF16),
                        pltpu.VMEM((heads * seq, LANES), F32),
                        pltpu.VMEM((heads * seq, hd), F32)],
        compiler_params=_params("arbitrary", "arbitrary"),
        name="sb_attention_sample",
    )(qkv, new, new, past_k, past_v)


def _xattn_kernel(a_ref, wa_ref, x_ref, g_ref, wq_ref, wo_ref, mk_ref, mv_ref, o_ref, *, heads):
    x = x_ref[...] + _dot(a_ref[...], wa_ref[...])
    d = x.shape[-1]
    dh = d // heads
    h = _rms(x, g_ref[...]).astype(BF16)
    q = _dot(h, wq_ref[0]).astype(BF16)
    nb = mk_ref.shape[1]
    seq = x.shape[0] // nb
    per_seq = []
    for b in range(nb):
        rows = slice(b * seq, (b + 1) * seq)
        outs = []
        for hd in range(heads):
            sl = slice(hd * dh, (hd + 1) * dh)
            s = _dot_nt(q[rows, sl], mk_ref[0, b, :, sl]) * (dh ** -0.5)
            e = jnp.exp(s - jnp.max(s, axis=-1, keepdims=True))
            p = e / jnp.sum(e, axis=-1, keepdims=True)
            outs.append(_dot(p.astype(BF16), mv_ref[0, b, :, sl]).astype(BF16))
        per_seq.append(jnp.concatenate(outs, axis=-1))
    o = per_seq[0] if nb == 1 else jnp.concatenate(per_seq, axis=0)
    o_ref[...] = x + _dot(o, wo_ref[0])


def _xattn(a, wa, x, gain, wq, wo, mem_k, mem_v, layer, in_place):
    rows, d = x.shape
    _, batch, n_mem, _ = mem_k.shape
    seq = rows // batch
    nb = min(batch, max(1, WIDE_ROW_TILE // seq))
    assert batch % nb == 0
    tm = nb * seq if nb > 1 else min(XATTN_ROW_TILE, seq)
    nt = 1 if nb > 1 else seq // tm
    row_spec = pl.BlockSpec((tm, d), lambda b, i: (b * nt + i, 0))
    weight_spec = pl.BlockSpec((1, d, d), lambda b, i: (layer, 0, 0))
    mem_spec = pl.BlockSpec((1, nb, n_mem, d), lambda b, i: (layer, b, 0, 0))
    return pl.pallas_call(
        functools.partial(_xattn_kernel, heads=XA_HEADS),
        grid=(batch // nb, nt),
        in_specs=[row_spec, pl.BlockSpec((d, d), lambda b, i: (0, 0)), row_spec,
                  pl.BlockSpec((1, d), lambda b, i: (0, 0)), weight_spec, weight_spec, mem_spec, mem_spec],
        out_specs=row_spec,
        out_shape=jax.ShapeDtypeStruct((rows, d), F32),
        input_output_aliases={2: 0} if in_place else {},
        compiler_params=_params("arbitrary", "arbitrary"),
        name="xattn",
    )(a, wa, x, gain.reshape(1, d), wq, wo, mem_k, mem_v)


def _ffn_kernel(x_ref, g_ref, wgu_ref, wd_ref, o_ref, *, final_gain):
    x = x_ref[...]
    hidden = wd_ref.shape[1]
    h = _rms(x, g_ref[0:1, :]).astype(BF16)
    acc = x
    for c0 in range(0, hidden, FFN_CHUNK):
        c1 = min(c0 + FFN_CHUNK, hidden)
        gate = _dot(h, wgu_ref[0, :, c0:c1])
        up = _dot(h, wgu_ref[0, :, hidden + c0:hidden + c1])
        act = (gate * jax.nn.sigmoid(gate) * up).astype(BF16)
        acc = acc + _dot(act, wd_ref[0, c0:c1, :])
    if final_gain:
        acc = _rms(acc, g_ref[1:2, :])
    o_ref[...] = acc


def _ffn_streamed_kernel(x_ref, g_ref, wg_ref, wu_ref, wd_ref, o_ref, h_ref, *, final_gain):
    c = pl.program_id(0)
    span = wd_ref.shape[1]

    @pl.when(c == 0)
    def _():
        x = x_ref[...]
        h_ref[...] = _rms(x, g_ref[0:1, :]).astype(BF16)
        o_ref[...] = x

    h = h_ref[...]
    acc = o_ref[...]
    for c0 in range(0, span, FFN_CHUNK):
        c1 = min(c0 + FFN_CHUNK, span)
        gate = _dot(h, wg_ref[0, :, c0:c1])
        up = _dot(h, wu_ref[0, :, c0:c1])
        act = (gate * jax.nn.sigmoid(gate) * up).astype(BF16)
        acc = acc + _dot(act, wd_ref[0, c0:c1, :])
    o_ref[...] = acc

    if final_gain:
        @pl.when(c == pl.num_programs(0) - 1)
        def _():
            o_ref[...] = _rms(o_ref[...], g_ref[1:2, :])


def _ffn_streamed(x, gains, w_gate_up, w_down, layer, final_gain):
    rows, d = x.shape
    hidden = w_down.shape[1]
    n_slabs = FFN_STREAM_SLABS
    span = hidden // n_slabs
    assert hidden % n_slabs == 0 and span % LANES == 0
    return pl.pallas_call(
        functools.partial(_ffn_streamed_kernel, final_gain=final_gain),
        grid=(n_slabs,),
        in_specs=[pl.BlockSpec((rows, d), lambda c: (0, 0)),
                  pl.BlockSpec((2, d), lambda c: (0, 0)),
                  pl.BlockSpec((1, d, span), lambda c: (layer, 0, c)),
                  pl.BlockSpec((1, d, span), lambda c: (layer, 0, n_slabs + c)),
                  pl.BlockSpec((1, span, d), lambda c: (layer, c, 0))],
        out_specs=pl.BlockSpec((rows, d), lambda c: (0, 0)),
        out_shape=jax.ShapeDtypeStruct((rows, d), F32),
        scratch_shapes=[pltpu.VMEM((rows, d), BF16)],
        input_output_aliases={0: 0},
        compiler_params=_params("arbitrary"),
        name="ffn_streamed",
    )(x, gains, w_gate_up, w_gate_up, w_down)


def _ffn(x, gain, w_gate_up, w_down, layer, final_gain=None):
    rows, d = x.shape
    tm = min(WIDE_ROW_TILE, rows)
    gains = jnp.stack([gain, gain if final_gain is None else final_gain])
    if rows == tm:
        return _ffn_streamed(x, gains, w_gate_up, w_down, layer, final_gain is not None)
    return pl.pallas_call(
        functools.partial(_ffn_kernel, final_gain=final_gain is not None),
        grid=(rows // tm,),
        in_specs=[pl.BlockSpec((tm, d), lambda i: (i, 0)),
                  pl.BlockSpec((2, d), lambda i: (0, 0)),
                  pl.BlockSpec((1,) + w_gate_up.shape[1:], lambda i: (layer, 0, 0)),
                  pl.BlockSpec((1,) + w_down.shape[1:], lambda i: (layer, 0, 0))],
        out_specs=pl.BlockSpec((tm, d), lambda i: (i, 0)),
        out_shape=jax.ShapeDtypeStruct((rows, d), F32),
        input_output_aliases={0: 0},
        compiler_params=_params("arbitrary"),
        name="ffn",
    )(x, gains, w_gate_up, w_down)


def _rw_proj_kernel(*refs, has_v_res):
    if has_v_res:
        (x_ref, gain_ref, shift0_ref, mu_ref, wr_ref, wk_ref, wv_ref, vec_ref,
         w1_ref, w2_ref, a1_ref, a2_ref, g1_ref, g2_ref, v1_ref, v2_ref, vfirst_ref,
         r_ref, lw_ref, k_ref, v_ref, kk_ref, a_ref, g_ref, shift_ref, prev_ref) = refs
    else:
        (x_ref, gain_ref, shift0_ref, mu_ref, wr_ref, wk_ref, wv_ref, vec_ref,
         w1_ref, w2_ref, a1_ref, a2_ref, g1_ref, g2_ref,
         r_ref, lw_ref, k_ref, v_ref, kk_ref, a_ref, g_ref, shift_ref, prev_ref) = refs
    i = pl.program_id(1)
    h = _rms(x_ref[...], gain_ref[...])
    tm = h.shape[0]

    @pl.when(i == 0)
    def _():
        prev_ref[...] = shift0_ref[0]

    row = lax.broadcasted_iota(jnp.int32, h.shape, 0)
    h_prev = jnp.where(row == 0, prev_ref[...], pltpu.roll(h, 1, 0))
    last = h[tm - 1:tm, :]
    prev_ref[...] = last
    shift_ref[0] = last

    xx = h_prev - h
    mixed = lambda j: (h + xx * mu_ref[j:j + 1, :]).astype(BF16)
    w0, a0, k_k, k_a, v0 = (vec_ref[j:j + 1, :] for j in range(5))

    r_ref[...] = _dot(mixed(0), wr_ref[...]).astype(r_ref.dtype)
    dec_in = _dot(mixed(1), w1_ref[...])
    k = _dot(mixed(2), wk_ref[...])
    dec = w0 + _dot(jnp.tanh(dec_in).astype(BF16), w2_ref[...])
    x_v = mixed(3)
    v = _dot(x_v, wv_ref[...])
    lw_ref[...] = -jnp.exp(-jax.nn.softplus(-dec) - 0.5)
    if has_v_res:
        mix = jax.nn.sigmoid(v0 + _dot(_dot(x_v, v1_ref[...]).astype(BF16), v2_ref[...]))
        v = v + (vfirst_ref[...] - v) * mix
    v_ref[...] = v
    a_in = _dot(mixed(4), a1_ref[...])
    g_in = _dot(mixed(5), g1_ref[...])
    a = jax.nn.sigmoid(a0 + _dot(a_in.astype(BF16), a2_ref[...]))
    kk_ref[...] = (k * k_k).astype(kk_ref.dtype)
    k_ref[...] = (k * (1.0 + (a - 1.0) * k_a)).astype(k_ref.dtype)
    a_ref[...] = a.astype(a_ref.dtype)
    g_ref[...] = _dot(jax.nn.sigmoid(g_in).astype(BF16), g2_ref[...]).astype(g_ref.dtype)


def _pad_cols(w, mult=LANES):
    pad = -w.shape[1] % mult
    return jnp.pad(w, ((0, 0), (0, pad))) if pad else w


def _pad_rows(w, mult=LANES):
    pad = -w.shape[0] % mult
    return jnp.pad(w, ((0, pad), (0, 0))) if pad else w


def _rw_proj(x, batch, gain, shift0, p, v_first):
    rows, d = x.shape
    seq = rows // batch
    tm = min(WIDE_ROW_TILE, seq)
    nt = seq // tm
    has_v_res = v_first is not None
    zeros = jnp.zeros((d,), F32)
    vec = jnp.stack([p["w0"], p["a0"], p["k_k"], p["k_a"], p["v0"] if has_v_res else zeros,
                     zeros, zeros, zeros])
    row_spec = pl.BlockSpec((tm, d), lambda b, i: (b * nt + i, 0))

    def full(arr):
        return pl.BlockSpec(arr.shape, lambda b, i: (0,) * arr.ndim, pipeline_mode=pl.Buffered(1))

    ins = [x, gain.reshape(1, d), shift0.reshape(batch, 1, d), p["mu"], p["w_r"], p["w_k"], p["w_v"], vec,
           p["w1"], p["w2"], p["a1"], p["a2"], p["g1"], p["g2"]]
    in_specs = [row_spec, full(ins[1]), pl.BlockSpec((1, 1, d), lambda b, i: (b, 0, 0))]
    in_specs += [full(a) for a in ins[3:]]
    if has_v_res:
        ins += [p["v1"], p["v2"], v_first]
        in_specs += [full(p["v1"]), full(p["v2"]), row_spec]
    tok = [jax.ShapeDtypeStruct((rows, d), dt) for dt in (BF16, F32, BF16, F32, BF16, BF16, BF16)]
    outs = pl.pallas_call(
        functools.partial(_rw_proj_kernel, has_v_res=has_v_res),
        grid=(batch, nt),
        in_specs=in_specs,
        out_specs=[row_spec] * 7 + [pl.BlockSpec((1, 1, d), lambda b, i: (b, 0, 0))],
        out_shape=tok + [jax.ShapeDtypeStruct((batch, 1, d), F32)],
        scratch_shapes=[pltpu.VMEM((1, d), F32)],
        compiler_params=_params("arbitrary", "arbitrary"),
        name="rw_proj",
    )(*ins)
    return outs[:7], outs[7].reshape(batch, d)


WKV_BASE_BLOCK = 4
WKV_DECAY_PIECES = 3
WKV_GROUP_PIECES = 2
WKV_STATE_PASSES = 1
WKV_SOLVE_PASSES = 1


def _wkv_prepare(units, c_):
    n = units[0][0].shape[0]
    sum_l = lambda x: _mm_exact_lhs(c_["tri_incl"], x, WKV_DECAY_PIECES)
    sum_g = lambda x: _mm_exact_rhs(x, c_["gsum"], WKV_GROUP_PIECES)
    solve_mm = lambda x, y, nt=False: _mm(x, y, WKV_SOLVE_PASSES, nt=nt)

    cs = [sum_l(u[1]) for u in units]
    head_sums = [sum_g(jnp.concatenate([u[4] * u[4], u[0] * u[2] * u[7][0:1, :]], axis=0)) for u in units]
    yield
    pre = []
    for (r, lw, k, v, kk, a, g, vec), cs_u, hs in zip(units, cs, head_sums):
        cs_last = cs_u[n - 1:n, :]
        d_inv = jnp.exp(-cs_u)
        d_rest = jnp.exp(cs_last - cs_u)
        kkn = kk * lax.rsqrt(jnp.maximum(hs[:n], 1e-24))
        b = kkn * a
        pre.append(dict(a_t=-kkn * jnp.exp(cs_u - lw), r_t=r * jnp.exp(cs_u), b_t=b * d_inv, k_t=k * d_inv,
                        bk_rest=jnp.concatenate([b * d_rest, k * d_rest], axis=0), decay=jnp.exp(cs_last),
                        v=v, g=g, vec=vec, bonus=hs[n:] * v))
    yield

    by_head = lambda x: jnp.concatenate([jnp.where(m, x, 0.0) for m in c_["head_masks"]], axis=0)
    block_diag = lambda w: jnp.where(c_["same_block"], jnp.concatenate([w, w], axis=0), 0.0)
    grams = [solve_mm(jnp.concatenate([q["a_t"], q["r_t"]], axis=0),
                      jnp.concatenate([by_head(q["b_t"]), by_head(q["k_t"])], axis=0), nt=True)
             for q in pre]
    yield
    lows = [jnp.where(c_["strict_w"], gm[:n, :2 * n], 0.0) for gm in grams]
    m_aks = [jnp.where(c_["strict_w"], gm[:n, 2 * n:], 0.0) for gm in grams]
    for q, gm in zip(pre, grams):
        q["m_r"] = jnp.where(c_["incl_w2"], gm[n:], 0.0)
        q["v_heads"] = by_head(q["v"])
    from_v = [solve_mm(mk, q["v_heads"]) for mk, q in zip(m_aks, pre)]
    yield
    diag = [jnp.where(c_["diag_block"], lo, 0.0) for lo in lows]
    inv = [c_["eye_w"] + dg for dg in diag]
    sq = [solve_mm(dg, block_diag(dg)) for dg in diag]
    yield
    inv = [t + solve_mm(t, block_diag(p2)) for t, p2 in zip(inv, sq)]
    yield
    for below in c_["below_blocks"]:
        half = [solve_mm(t, block_diag(jnp.where(below, lo, 0.0))) for t, lo in zip(inv, lows)]
        yield
        inv = [t + solve_mm(hf, block_diag(t)) for t, hf in zip(inv, half)]
        yield
    for q, t, fv in zip(pre, inv, from_v):
        q["inv"], q["from_v"] = t, fv
    return pre


def _wkv_advance(pre, states, c_):
    n = pre[0]["v"].shape[0]
    inv_hd = 1.0 / RW_HEAD_DIM
    sum_g = lambda x: _mm_exact_rhs(x, c_["gsum"], WKV_GROUP_PIECES)
    solve_mm = lambda x, y: _mm(x, y, WKV_SOLVE_PASSES)
    by_head = lambda x: jnp.concatenate([jnp.where(m, x, 0.0) for m in c_["head_masks"]], axis=0)

    from_state = [_mm(jnp.concatenate([q["a_t"], q["r_t"]], axis=0), s, WKV_STATE_PASSES, nt=True)
                  for q, s in zip(pre, states)]
    yield
    sig_p = [solve_mm(q["inv"], by_head(fs[:n] + q["from_v"])) for q, fs in zip(pre, from_state)]
    yield
    y_p = [fs[n:] + solve_mm(q["m_r"], jnp.concatenate([by_head(sg), q["v_heads"]], axis=0))
           for q, fs, sg in zip(pre, from_state, sig_p)]
    upd = [_mm(jnp.concatenate([s, q["v"]], axis=0).T, q["bk_rest"], WKV_STATE_PASSES)
           for s, q in zip(sig_p, pre)]
    new_states = [s * q["decay"] + jnp.where(c_["same_head"], u, 0.0) for s, q, u in zip(states, pre, upd)]

    yield
    mean = [sum_g(y) * inv_hd for y in y_p]
    dev = [y - m for y, m in zip(y_p, mean)]
    yield
    var = [sum_g(dv * dv) * inv_hd for dv in dev]
    yield
    outs = []
    for q, dv, vr in zip(pre, dev, var):
        o = dv * lax.rsqrt(vr + RW_GN_EPS) * q["vec"][1:2, :] + q["vec"][2:3, :]
        outs.append((o + q["bonus"]) * q["g"])
    return outs, new_states


def _wkv_advance_chunks(pre_chunks, states, c_):
    outs = []
    for pre in pre_chunks:
        out, states = yield from _wkv_advance(pre, states, c_)
        outs.append(jnp.concatenate(out, axis=1))
    return outs, states


def _in_lock_step(*staged):
    results = [None] * len(staged)
    live = list(enumerate(staged))
    while live:
        still = []
        for i, gen in live:
            try:
                next(gen)
                still.append((i, gen))
            except StopIteration as done:
                results[i] = done.value
        live = still
    return results


def _wkv_kernel(*refs, has_state0, n):
    if has_state0:
        (r_ref, lw_ref, k_ref, v_ref, kk_ref, a_ref, g_ref, vec_ref, s0_ref,
         o_ref, sout_ref, state_ref) = refs
    else:
        (r_ref, lw_ref, k_ref, v_ref, kk_ref, a_ref, g_ref, vec_ref,
         o_ref, sout_ref, state_ref) = refs
    c = pl.program_id(2)
    hd = RW_HEAD_DIM

    @pl.when(c == 0)
    def _():
        if has_state0:
            state_ref[...] = s0_ref[0]
        else:
            state_ref[...] = jnp.zeros_like(state_ref)

    tr = lax.broadcasted_iota(jnp.int32, (n, n), 0)
    tc = lax.broadcasted_iota(jnp.int32, (n, n), 1)
    tr2 = lax.broadcasted_iota(jnp.int32, (n, 2 * n), 0)
    tc2 = lax.rem(lax.broadcasted_iota(jnp.int32, (n, 2 * n), 1), n)
    tr4 = lax.broadcasted_iota(jnp.int32, (n, 4 * n), 0)
    tc4 = lax.rem(lax.broadcasted_iota(jnp.int32, (n, 4 * n), 1), n)
    br = lax.broadcasted_iota(jnp.int32, (2 * n, 2 * n), 0) // n
    bc = lax.broadcasted_iota(jnp.int32, (2 * n, 2 * n), 1) // n
    lane = lax.broadcasted_iota(jnp.int32, (n, LANES), 1)
    sr = lax.broadcasted_iota(jnp.int32, (LANES, LANES), 0) // hd
    sc = lax.broadcasted_iota(jnp.int32, (LANES, LANES), 1) // hd
    sizes = [WKV_BASE_BLOCK << i for i in range(n.bit_length()) if (WKV_BASE_BLOCK << i) < n]
    below_blocks = []
    for s in sizes:
        u, v = tr2 // s, tc2 // s
        below_blocks.append(jnp.where((u & 1) == 1, u - 1, -1) == v)
    consts = dict(
        diag_block=(tr2 // WKV_BASE_BLOCK) == (tc2 // WKV_BASE_BLOCK),
        below_blocks=below_blocks,
        gsum=jnp.concatenate([_group_ones(LANES, hd)] * WKV_GROUP_PIECES, axis=0),
        tri_incl=jnp.concatenate([jnp.where(tc <= tr, 1.0, 0.0).astype(BF16)] * WKV_DECAY_PIECES, axis=1),
        strict_w=tc2 < tr2,
        incl_w2=tc4 <= tr4,
        eye_w=jnp.where(tc2 == tr2, 1.0, 0.0).astype(F32),
        same_block=br == bc,
        head_masks=[lane < hd, lane >= hd],
        same_head=sr == sc,
    )
    n_pairs = state_ref.shape[0]
    n_chunks = r_ref.shape[0] // n

    def units(chunks):
        out = []
        for ci in chunks:
            rows = slice(ci * n, (ci + 1) * n)
            for p in range(n_pairs):
                sl = slice(p * LANES, (p + 1) * LANES)
                out.append(tuple(ref[rows, sl].astype(F32)
                                 for ref in (r_ref, lw_ref, k_ref, v_ref, kk_ref, a_ref, g_ref)) + (vec_ref[:, sl],))
        return out

    def per_chunk(pre):
        return [pre[i:i + n_pairs] for i in range(0, len(pre), n_pairs)]

    first = range((n_chunks + 1) // 2)
    second = range(len(first), n_chunks)
    states = [state_ref[p] for p in range(n_pairs)]
    (pre_first,) = _in_lock_step(_wkv_prepare(units(first), consts))
    if second:
        (outs, states), pre_second = _in_lock_step(
            _wkv_advance_chunks(per_chunk(pre_first), states, consts), _wkv_prepare(units(second), consts))
        ((outs_second, states),) = _in_lock_step(_wkv_advance_chunks(per_chunk(pre_second), states, consts))
        outs = outs + outs_second
    else:
        ((outs, states),) = _in_lock_step(_wkv_advance_chunks(per_chunk(pre_first), states, consts))
    o_ref[...] = jnp.concatenate(outs, axis=0).astype(o_ref.dtype)
    new_states = jnp.stack(states)
    state_ref[...] = new_states
    sout_ref[0] = new_states


def _wkv(tok, batch, vec, state0):
    rows, d = tok[0].shape
    seq = rows // batch
    n = min(WKV_CHUNK, seq)
    span = n * min(WKV_CHUNKS_PER_STEP, seq // n)
    nc = seq // span
    pairs = d // LANES
    gp = WKV_PAIRS
    width = gp * LANES
    has_state0 = state0 is not None
    tok_spec = pl.BlockSpec((span, width), lambda b, p, c: (b * nc + c, p))
    state_spec = pl.BlockSpec((1, gp, LANES, LANES), lambda b, p, c: (b, p, 0, 0))
    ins = list(tok) + [vec]
    in_specs = [tok_spec] * 7 + [pl.BlockSpec((8, width), lambda b, p, c: (0, p))]
    if has_state0:
        ins.append(state0)
        in_specs.append(state_spec)
    return pl.pallas_call(
        functools.partial(_wkv_kernel, has_state0=has_state0, n=n),
        grid=(batch, pairs // gp, nc),
        in_specs=in_specs,
        out_specs=[tok_spec, state_spec],
        out_shape=[jax.ShapeDtypeStruct((rows, d), BF16),
                   jax.ShapeDtypeStruct((batch, pairs, LANES, LANES), F32)],
        scratch_shapes=[pltpu.VMEM((gp, LANES, LANES), F32)],
        compiler_params=_params("arbitrary", "arbitrary", "arbitrary"),
        name="wkv",
    )(*ins)


def _state_to_blockdiag(state):
    b, h, n, _ = state.shape
    s = state.reshape(b, h // 2, 2, n, n)
    z = jnp.zeros_like(s[:, :, 0])
    top = jnp.concatenate([s[:, :, 0], z], axis=-1)
    bot = jnp.concatenate([z, s[:, :, 1]], axis=-1)
    return jnp.concatenate([top, bot], axis=-2)


def _state_from_blockdiag(bd):
    b, p, n2, _ = bd.shape
    n = n2 // 2
    return jnp.stack([bd[:, :, :n, :n], bd[:, :, n:, n:]], axis=2).reshape(b, 2 * p, n, n)


def _trunk(x, batch, mem_k, mem_v, past_k, past_v, rw_state0, rw_shift0, w):
    depth = w["ffn_gu"].shape[0]
    d = x.shape[-1]
    sb_k, sb_v, rw_states, rw_shifts = None, None, [], []
    n_sb = (depth + 1) // 2
    v_first = None
    for layer in range(depth):
        j = layer // 2
        if layer % 2 == 0:
            sb_k, sb_v, qkv = _sb_qkv(x, batch, w["norm_mix"][layer], w["sb_qkv"][j], j, n_sb, sb_k, sb_v)
            if past_k is None:
                o = _sb_attention_prompt(qkv, batch)
            else:
                o = _sb_attention_sample(qkv, batch, past_k, past_v, j)
            w_o = w["sb_o"][j]
        else:
            if rw_state0 is None:
                state0 = None
                shift0 = jnp.zeros((batch, d), F32)
            else:
                state0 = _state_to_blockdiag(rw_state0[j])
                shift0 = rw_shift0[j]
            tok, shift = _rw_proj(x, batch, w["norm_mix"][layer], shift0, w["rw"][j], v_first)
            if v_first is None:
                v_first = tok[3]
            o, state = _wkv(tok, batch, w["rw"][j]["out_vec"], state0)
            rw_states.append(_state_from_blockdiag(state))
            rw_shifts.append(shift)
            w_o = w["rw"][j]["w_o"]
        x = _xattn(o, w_o, x, w["norm_xattn"][layer], w["xa_q"], w["xa_o"], mem_k, mem_v, layer,
                   in_place=layer > 0)
        final = w["norm_final"] if layer == depth - 1 else None
        x = _ffn(x, w["norm_ffn"][layer], w["ffn_gu"], w["ffn_down"], layer, final)
    return x, sb_k, sb_v, jnp.stack(rw_states), jnp.stack(rw_shifts)


def kernel(x_prompt, x_sample, mem_prompt, cache_sb_k, cache_sb_v, state_rwkv_wkv, state_rwkv_shift,
           cache_mem_k, cache_mem_v, norm_mix, norm_xattn, norm_ffn, norm_mem, norm_final,
           sb_w_qkv, sb_w_o, rw_mu, rw_w_r, rw_w_k, rw_w_v, rw_w_o, rw_w0, rw_w1, rw_w2,
           rw_a0, rw_a1, rw_a2, rw_v0, rw_v1, rw_v2, rw_g1, rw_g2, rw_k_k, rw_k_a, rw_r_k,
           rw_lnx_w, rw_lnx_b, xa_w_q, xa_w_kv, xa_w_o, ffn_w_gate_up, ffn_w_down):
    batch, seq, d = x_prompt.shape
    dec_batch, dec_seq, _ = x_sample.shape
    depth = norm_mix.shape[0]
    n_rw = rw_w_r.shape[0]
    n_sb = sb_w_qkv.shape[0]
    n_mem = mem_prompt.shape[1]
    bf = lambda t: t.astype(BF16)
    per_layer = lambda t: [bf(t[i]) for i in range(t.shape[0])]

    rw = []
    for j in range(n_rw):
        pj = dict(mu=rw_mu[j], w_r=bf(rw_w_r[j]), w_k=bf(rw_w_k[j]), w_v=bf(rw_w_v[j]), w_o=bf(rw_w_o[j]),
                  w0=rw_w0[j], a0=rw_a0[j], k_k=rw_k_k[j], k_a=rw_k_a[j],
                  w1=bf(_pad_cols(rw_w1[j])), w2=bf(_pad_rows(rw_w2[j])),
                  a1=bf(_pad_cols(rw_a1[j])), a2=bf(_pad_rows(rw_a2[j])),
                  g1=bf(_pad_cols(rw_g1[j])), g2=bf(_pad_rows(rw_g2[j])))
        if j > 0:
            pj.update(v0=rw_v0[j - 1], v1=bf(_pad_cols(rw_v1[j - 1])), v2=bf(_pad_rows(rw_v2[j - 1])))
        zeros = jnp.zeros((d,), F32)
        pj["out_vec"] = jnp.stack([rw_r_k[j].reshape(d), rw_lnx_w[j], rw_lnx_b[j]] + [zeros] * 5)
        rw.append(pj)
    w = dict(norm_mix=norm_mix, norm_xattn=norm_xattn, norm_ffn=norm_ffn, norm_final=norm_final,
             sb_qkv=per_layer(sb_w_qkv), sb_o=per_layer(sb_w_o), rw=rw,
             xa_q=bf(xa_w_q), xa_o=bf(xa_w_o), ffn_gu=bf(ffn_w_gate_up), ffn_down=bf(ffn_w_down))

    mem_k, mem_v, mem_k_bf, mem_v_bf = _mem_kv(mem_prompt, norm_mem, bf(xa_w_kv))
    y_p, p_sb_k, p_sb_v, p_wkv, p_shift = _trunk(
        x_prompt.reshape(batch * seq, d), batch, mem_k_bf, mem_v_bf, None, None, None, None, w)

    y_s, s_sb_k, s_sb_v, s_wkv, s_shift = _trunk(
        x_sample.reshape(dec_batch * dec_seq, d), dec_batch,
        bf(cache_mem_k).reshape(depth, dec_batch, n_mem, d), bf(cache_mem_v).reshape(depth, dec_batch, n_mem, d),
        cache_sb_k, cache_sb_v,
        state_rwkv_wkv, state_rwkv_shift, w)

    tokens_first = lambda t: jnp.transpose(t, (0, 1, 4, 2, 3))
    return (y_p.reshape(batch, seq, d), y_s.reshape(dec_batch, dec_seq, d),
            tokens_first(p_sb_k), tokens_first(p_sb_v), p_wkv, p_shift, mem_k, mem_v,
            tokens_first(s_sb_k), tokens_first(s_sb_v), s_wkv, s_shift)
```

```python
import functools

import jax
import jax.numpy as jnp
from jax import lax
from jax.experimental import pallas as pl
from jax.experimental.pallas import tpu as pltpu

F32 = jnp.float32
BF16 = jnp.bfloat16

NORM_EPS = 1e-6
NEG_LOG2_E = -1.4426950408889634
RW_GN_EPS = 64e-5
SB_HEADS = 16
XA_HEADS = 4
RW_HEAD_DIM = 64
LANES = 128
ROW_TILE = 256
WIDE_ROW_TILE = 512
XATTN_ROW_TILE = 1024
SB_BLOCK = 256
SB_PAIRS = 8
SB_WAVES = 4
SB_DECODE_WAVES = 2
SB_UNROLL = 2
SB_PAST_SPAN = 2048
WKV_CHUNK = 64
WKV_CHUNKS_PER_STEP = 4
WKV_PAIRS = 8
FFN_CHUNK = 256
VMEM_LIMIT_BYTES = 56 * 1024 * 1024


def _params(*sem):
    return pltpu.CompilerParams(dimension_semantics=sem, vmem_limit_bytes=VMEM_LIMIT_BYTES)


def _dot(a, b):
    return jnp.dot(a, b, preferred_element_type=F32)


def _dot_nt(a, b):
    return lax.dot_general(a, b, (((1,), (1,)), ((), ())), preferred_element_type=F32)


def _split(x, n):
    parts = []
    for i in range(n):
        p = x.astype(BF16)
        parts.append(p)
        if i + 1 < n:
            x = x - p.astype(F32)
    return parts


def _mm(a, b, passes, nt=False):
    dot = _dot_nt if nt else _dot
    if passes == 1:
        return dot(a.astype(BF16), b.astype(BF16))
    a_hi, a_lo = _split(a, 2)
    b_hi, b_lo = _split(b, 2)
    return dot(a_hi, b_hi) + (dot(a_hi, b_lo) + dot(a_lo, b_hi))


def _mm_exact_rhs(a, b_stacked, n):
    return _dot(jnp.concatenate(_split(a, n), axis=1), b_stacked)


def _mm_exact_lhs(a_repeated, b, n):
    return _dot(a_repeated, jnp.concatenate(_split(b, n), axis=0))


def _rms(x, gain):
    ms = jnp.mean(x * x, axis=-1, keepdims=True)
    return x * lax.rsqrt(ms + NORM_EPS) * gain


def _group_ones(n, group):
    r = lax.broadcasted_iota(jnp.int32, (n, n), 0) // group
    c = lax.broadcasted_iota(jnp.int32, (n, n), 1) // group
    return jnp.where(r == c, 1.0, 0.0).astype(BF16)


def _mem_kv_kernel(x_ref, g_ref, w_ref, k_ref, v_ref, kb_ref, vb_ref):
    tm, d = x_ref.shape
    heads, dh = k_ref.shape[-2:]
    h = _rms(x_ref[...], g_ref[0]).astype(BF16)
    kv = _dot(h, w_ref[0])
    k, v = kv[:, :d], kv[:, d:]
    k_ref[0, 0] = k.reshape(tm, heads, dh)
    v_ref[0, 0] = v.reshape(tm, heads, dh)
    kb_ref[0, 0] = k.astype(BF16)
    vb_ref[0, 0] = v.astype(BF16)


def _mem_kv(mem, gains, w_kv):
    depth = w_kv.shape[0]
    batch, n_mem, d = mem.shape
    tm = min(ROW_TILE, n_mem)
    nt = n_mem // tm
    dh = d // XA_HEADS
    assert n_mem % tm == 0 and dh % LANES == 0
    out = jax.ShapeDtypeStruct((depth, batch, n_mem, XA_HEADS, dh), F32)
    out_b = jax.ShapeDtypeStruct((depth, batch, n_mem, d), BF16)
    spec = pl.BlockSpec((1, 1, tm, XA_HEADS, dh), lambda l, i: (l, i // nt, i % nt, 0, 0))
    spec_b = pl.BlockSpec((1, 1, tm, d), lambda l, i: (l, i // nt, i % nt, 0))
    return pl.pallas_call(
        _mem_kv_kernel,
        grid=(depth, batch * nt),
        in_specs=[pl.BlockSpec((tm, d), lambda l, i: (i, 0)),
                  pl.BlockSpec((1, 1, d), lambda l, i: (l, 0, 0)),
                  pl.BlockSpec((1, d, 2 * d), lambda l, i: (l, 0, 0))],
        out_specs=[spec, spec, spec_b, spec_b],
        out_shape=[out, out, out_b, out_b],
        compiler_params=_params("arbitrary", "arbitrary"),
        name="mem_kv",
    )(mem.reshape(batch * n_mem, d), gains.reshape(depth, 1, d), w_kv)


def _sb_qkv_kernel(x_ref, g_ref, w_ref, *rest, q_scale, slot):
    k_ref, v_ref, qkv_ref = rest[-3:]
    tm, d = x_ref.shape
    heads, hd = k_ref.shape[2:4]
    h = _rms(x_ref[...], g_ref[...]).astype(BF16)

    def per_head_transposed(t):
        return t.T.reshape(heads, hd, tm)

    own = slot if k_ref.shape[0] > 1 else 0
    for s in range(k_ref.shape[0]):
        if s != own:
            k_ref[s, 0] = jnp.zeros(k_ref.shape[2:], k_ref.dtype)
            v_ref[s, 0] = jnp.zeros(v_ref.shape[2:], v_ref.dtype)
    k = _dot(h, w_ref[:, d:2 * d])
    qkv_ref[:, d:2 * d] = k.astype(BF16)
    k_ref[own, 0] = per_head_transposed(k)
    v = _dot(h, w_ref[:, 2 * d:])
    qkv_ref[:, 2 * d:] = v.astype(BF16)
    v_ref[own, 0] = per_head_transposed(v)
    qkv_ref[:, :d] = (_dot(h, w_ref[:, :d]) * q_scale).astype(BF16)


def _sb_qkv(x, batch, gain, w, slot, n_slots, k_all=None, v_all=None):
    rows, d = x.shape
    seq = rows // batch
    tm = min(WIDE_ROW_TILE, seq)
    nt = seq // tm
    hd = d // SB_HEADS
    q_scale = float(hd ** -0.5)
    ins = [x, gain.reshape(1, d), w]
    in_specs = [pl.BlockSpec((tm, d), lambda i: (i, 0)),
                pl.BlockSpec((1, d), lambda i: (0, 0)),
                pl.BlockSpec(w.shape, lambda i: (0, 0))]
    if k_all is None:
        aliases = {}
        slot_spec = pl.BlockSpec((n_slots, 1, SB_HEADS, hd, tm), lambda i: (0, i // nt, 0, 0, i % nt))
    else:
        ins += [k_all, v_all]
        in_specs += [pl.BlockSpec(memory_space=pl.ANY)] * 2
        aliases = {3: 0, 4: 1}
        slot_spec = pl.BlockSpec((1, 1, SB_HEADS, hd, tm), lambda i: (slot, i // nt, 0, 0, i % nt))
    kv_shape = jax.ShapeDtypeStruct((n_slots, batch, SB_HEADS, hd, seq), F32)
    return pl.pallas_call(
        functools.partial(_sb_qkv_kernel, q_scale=q_scale, slot=slot),
        grid=(rows // tm,),
        in_specs=in_specs,
        out_specs=[slot_spec, slot_spec, pl.BlockSpec((tm, 3 * d), lambda i: (i, 0))],
        out_shape=[kv_shape, kv_shape, jax.ShapeDtypeStruct((rows, 3 * d), BF16)],
        input_output_aliases=aliases,
        compiler_params=_params("arbitrary"),
        name="sb_qkv",
    )(*ins)


def _sb_step(q2s, kv_blocks, carry, acc, neg_lower2, mask, n_waves):
    m = q2s[0].shape[0]
    groups = len(q2s)
    per = max(groups // n_waves, 1)
    waves = [range(c, min(c + per, groups)) for c in range(0, groups, per)]
    rows = [slice(w_[0] * m, (w_[-1] + 1) * m) for w_ in waves]
    units = [(b, c) for b in range(len(kv_blocks)) for c in range(len(waves))]

    def scores(b, c):
        keys, _, keys_on_lanes = kv_blocks[b]
        dot = _dot if keys_on_lanes else _dot_nt
        return jnp.concatenate([dot(q2s[g], keys[g]) for g in waves[c]], axis=0)

    def softplus_pieces(c, z):
        sp = jnp.maximum(z, 0.0) + jnp.log(1.0 + jnp.exp2(jnp.abs(z) * NEG_LOG2_E))
        if mask is not None:
            sp = jnp.where(mask[rows[c]], sp, 0.0)
        return jnp.concatenate(_split(sp, 2), axis=1)

    def weights(c, z, from_s, later):
        w = jnp.exp(z + from_s + jnp.concatenate([later] * (z.shape[1] // LANES), axis=1))
        if mask is not None:
            w = jnp.where(mask[rows[c]], w, 0.0)
        return w.astype(BF16)

    def values(b, c, w):
        _, vals, keys_on_lanes = kv_blocks[b]
        dot = _dot_nt if keys_on_lanes else _dot
        return jnp.concatenate([dot(w[i * m:(i + 1) * m], vals[g]) for i, g in enumerate(waves[c])], axis=0)

    n = len(units)
    z, pieces, from_s, w = ([None] * n for _ in range(4))
    later = [carry[r] for r in rows]
    pv = [None] * len(waves)
    for tick in range(n + 4):
        for u, (b, c) in enumerate(units):
            stage = tick - u
            if stage == 0:
                z[u] = scores(b, c)
            elif stage == 1:
                pieces[u] = softplus_pieces(c, z[u])
            elif stage == 2:
                from_s[u] = _dot(pieces[u], neg_lower2)
            elif stage == 3:
                w[u] = weights(c, z[u], from_s[u], later[c])
                later[c] = later[c] + jnp.broadcast_to(from_s[u][:, 0:1], later[c].shape)
            elif stage == 4:
                out = values(b, c, w[u])
                pv[c] = out if pv[c] is None else pv[c] + out
    return jnp.concatenate(later, axis=0), acc + jnp.concatenate(pv, axis=0)


def _sb_consts(bk):
    r = lax.broadcasted_iota(jnp.int32, (2, bk, bk), 1).reshape(2 * bk, bk)
    c = lax.broadcasted_iota(jnp.int32, (2 * bk, bk), 1)
    return jnp.where(r >= c, -1.0, 0.0).astype(BF16)


def _sb_stack_heads(q_ref, q2_ref):
    tq = q_ref.shape[0]
    lane = lax.broadcasted_iota(jnp.int32, (tq, LANES), 1)
    for g in range(q2_ref.shape[0]):
        q = q_ref[:, g * LANES:(g + 1) * LANES]
        zero = jnp.zeros_like(q)
        q2_ref[g] = jnp.concatenate([jnp.where(lane < LANES // 2, q, zero),
                                     jnp.where(lane >= LANES // 2, q, zero)], axis=0)


def _sb_causal_mask(groups, tq, bk, t0=0):
    t = lax.broadcasted_iota(jnp.int32, (2 * groups, tq, bk), 1).reshape(2 * groups * tq, bk)
    s = lax.broadcasted_iota(jnp.int32, (2 * groups * tq, bk), 1)
    return s < t + t0


def _sb_write(o_ref, acc_ref):
    tq = o_ref.shape[0]
    lane = lax.broadcasted_iota(jnp.int32, (tq, LANES), 1)
    for g in range(acc_ref.shape[0] // (2 * tq)):
        o_ref[:, g * LANES:(g + 1) * LANES] = jnp.where(
            lane < LANES // 2, acc_ref[2 * g * tq:(2 * g + 1) * tq, :],
            acc_ref[(2 * g + 1) * tq:(2 * g + 2) * tq, :]).astype(o_ref.dtype)


def _sb_prompt_kernel(q_ref, k_ref, v_ref, o_ref, q2_ref, carry_ref, acc_ref):
    blk = q_ref.shape[0]
    groups = q2_ref.shape[0]
    i = pl.program_id(2)
    neg_lower2 = _sb_consts(blk)
    _sb_stack_heads(q_ref, q2_ref)
    lanes = [slice(g * LANES, (g + 1) * LANES) for g in range(groups)]

    def block(j):
        row0 = pl.multiple_of(j * blk, blk)
        return ([k_ref[pl.ds(row0, blk), sl] for sl in lanes], [v_ref[pl.ds(row0, blk), sl] for sl in lanes], False)

    def step(blocks, carry, acc, mask):
        return _sb_step([q2_ref[g] for g in range(groups)], blocks, carry, acc, neg_lower2, mask, SB_WAVES)

    half = blk // 2
    d0 = pl.multiple_of(i * blk, blk)
    zero = jnp.zeros((groups * blk, LANES), F32)
    halves = []
    for t0, keys, consts in ((half, blk, neg_lower2), (0, half, _sb_consts(half))):
        q2_half = [jnp.concatenate([q2_ref[g, t0:t0 + half, :], q2_ref[g, blk + t0:blk + t0 + half, :]], axis=0)
                   for g in range(groups)]
        own = ([k_ref[pl.ds(d0, keys), sl] for sl in lanes], [v_ref[pl.ds(d0, keys), sl] for sl in lanes], False)
        halves.append(_sb_step(q2_half, [own], zero, zero, consts, _sb_causal_mask(groups, half, keys, t0),
                               SB_WAVES))
    (carry_late, acc_late), (carry_early, acc_early) = halves
    for g in range(groups):
        for h in range(2):
            src = slice((2 * g + h) * half, (2 * g + h + 1) * half)
            dst = (2 * g + h) * blk
            for ref, early, late in ((carry_ref, carry_early, carry_late), (acc_ref, acc_early, acc_late)):
                ref[dst:dst + half, :] = early[src]
                ref[dst + half:dst + blk, :] = late[src]

    odd = i % SB_UNROLL

    def single(it, c):
        carry_ref[...], acc_ref[...] = step([block(i - 1 - it)], carry_ref[...], acc_ref[...], None)
        return c

    def multi(it, c):
        top = i - 1 - odd - it * SB_UNROLL
        carry_ref[...], acc_ref[...] = step([block(top - u) for u in range(SB_UNROLL)],
                                            carry_ref[...], acc_ref[...], None)
        return c

    lax.fori_loop(0, odd, single, 0)
    lax.fori_loop(0, i // SB_UNROLL, multi, 0)
    _sb_write(o_ref, acc_ref)


def _sb_attention_prompt(qkv, batch):
    rows, d3 = qkv.shape
    d = d3 // 3
    seq = rows // batch
    blk = min(SB_BLOCK, seq)
    nq = seq // blk
    width = SB_PAIRS * LANES
    ng = d // width
    return pl.pallas_call(
        _sb_prompt_kernel,
        grid=(batch, ng, nq),
        in_specs=[pl.BlockSpec((blk, width), lambda b, p, i: (b * nq + i, p)),
                  pl.BlockSpec((seq, width), lambda b, p, i: (b, ng + p)),
                  pl.BlockSpec((seq, width), lambda b, p, i: (b, 2 * ng + p))],
        out_specs=pl.BlockSpec((blk, width), lambda b, p, i: (b * nq + i, p)),
        out_shape=jax.ShapeDtypeStruct((rows, d), BF16),
        scratch_shapes=[pltpu.VMEM((SB_PAIRS, 2 * blk, LANES), BF16),
                        pltpu.VMEM((SB_PAIRS * 2 * blk, LANES), F32),
                        pltpu.VMEM((SB_PAIRS * 2 * blk, LANES), F32)],
        compiler_params=_params("arbitrary", "arbitrary", "arbitrary"),
        name="sb_attention_prompt",
    )(qkv, qkv, qkv)


def _sb_sample_kernel(q_ref, kn_ref, vn_ref, kp_ref, vp_ref, o_ref, q2_ref, carry_ref, acc_ref, *, blk):
    tq = q_ref.shape[0]
    heads, _, hd = q2_ref.shape
    c = pl.program_id(1)
    neg_lower2 = _sb_consts(blk)
    lanes = [slice(h * hd, (h + 1) * hd) for h in range(heads)]

    @pl.when(c == 0)
    def _():
        for h, sl in enumerate(lanes):
            q2_ref[h] = q_ref[:, sl]
        own = ([kn_ref[0, :, sl] for sl in lanes], [vn_ref[0, :, sl] for sl in lanes], False)
        carry_ref[...], acc_ref[...] = _sb_step(
            [q2_ref[h] for h in range(heads)], [own], jnp.zeros(carry_ref.shape, F32),
            jnp.zeros(acc_ref.shape, F32), neg_lower2, _sb_causal_mask(heads // 2, tq, blk), SB_DECODE_WAVES)

    blocks = []
    for sb in reversed(range(kp_ref.shape[-1] // blk)):
        cols = slice(sb * blk, (sb + 1) * blk)
        blocks.append(([kp_ref[0, 0, h, :, cols].astype(BF16) for h in range(heads)],
                       [vp_ref[0, 0, h, :, cols].astype(BF16) for h in range(heads)], True))
    carry_ref[...], acc_ref[...] = _sb_step([q2_ref[h] for h in range(heads)], blocks,
                                            carry_ref[...], acc_ref[...], neg_lower2, None, SB_DECODE_WAVES)

    @pl.when(c == pl.num_programs(1) - 1)
    def _():
        o_ref[...] = jnp.concatenate([acc_ref[h * tq:(h + 1) * tq, :] for h in range(heads)],
                                     axis=1).astype(o_ref.dtype)


def _sb_attention_sample(qkv, batch, past_k, past_v, layer):
    rows, d3 = qkv.shape
    d = d3 // 3
    seq = rows // batch
    blk = SB_BLOCK
    _, _, past_len, heads, hd = past_k.shape
    span = min(SB_PAST_SPAN, past_len)
    steps = past_len // span
    assert seq <= blk and past_len % span == 0 and span % blk == 0 and heads % 2 == 0
    new = qkv[:, d:].reshape(batch, seq, 2 * d)
    new = jnp.pad(new, ((0, 0), (0, blk - seq), (0, 0)))
    past_k = jnp.transpose(past_k, (0, 1, 3, 4, 2))
    past_v = jnp.transpose(past_v, (0, 1, 3, 4, 2))
    past_spec = pl.BlockSpec((1, 1, heads, hd, span), lambda b, c: (layer, b, 0, 0, steps - 1 - c))
    return pl.pallas_call(
        functools.partial(_sb_sample_kernel, blk=blk),
        grid=(batch, steps),
        in_specs=[pl.BlockSpec((seq, d), lambda b, c: (b, 0)),
                  pl.BlockSpec((1, blk, d), lambda b, c: (b, 0, 0)),
                  pl.BlockSpec((1, blk, d), lambda b, c: (b, 0, 1)),
                  past_spec, past_spec],
        out_specs=pl.BlockSpec((seq, d), lambda b, c: (b, 0)),
        out_shape=jax.ShapeDtypeStruct((rows, d), BF16),
        scratch_shapes=[pltpu.VMEM((heads, seq, hd), BF16),
                        pltpu.VMEM((heads * seq, LANES), F32),
                        pltpu.VMEM((heads * seq, hd), F32)],
        compiler_params=_params("arbitrary", "arbitrary"),
        name="sb_attention_sample",
    )(qkv, new, new, past_k, past_v)


def _xattn_kernel(a_ref, wa_ref, x_ref, g_ref, wq_ref, wo_ref, mk_ref, mv_ref, o_ref, *, heads):
    x = x_ref[...] + _dot(a_ref[...], wa_ref[...])
    d = x.shape[-1]
    dh = d // heads
    h = _rms(x, g_ref[...]).astype(BF16)
    q = _dot(h, wq_ref[0]).astype(BF16)
    nb = mk_ref.shape[1]
    seq = x.shape[0] // nb
    per_seq = []
    for b in range(nb):
        rows = slice(b * seq, (b + 1) * seq)
        outs = []
        for hd in range(heads):
            sl = slice(hd * dh, (hd + 1) * dh)
            s = _dot_nt(q[rows, sl], mk_ref[0, b, :, sl]) * (dh ** -0.5)
            e = jnp.exp(s - jnp.max(s, axis=-1, keepdims=True))
            p = e / jnp.sum(e, axis=-1, keepdims=True)
            outs.append(_dot(p.astype(BF16), mv_ref[0, b, :, sl]).astype(BF16))
        per_seq.append(jnp.concatenate(outs, axis=-1))
    o = per_seq[0] if nb == 1 else jnp.concatenate(per_seq, axis=0)
    o_ref[...] = x + _dot(o, wo_ref[0])


def _xattn(a, wa, x, gain, wq, wo, mem_k, mem_v, layer, in_place):
    rows, d = x.shape
    _, batch, n_mem, _ = mem_k.shape
    seq = rows // batch
    nb = min(batch, max(1, WIDE_ROW_TILE // seq))
    assert batch % nb == 0
    tm = nb * seq if nb > 1 else min(XATTN_ROW_TILE, seq)
    nt = 1 if nb > 1 else seq // tm
    row_spec = pl.BlockSpec((tm, d), lambda b, i: (b * nt + i, 0))
    weight_spec = pl.BlockSpec((1, d, d), lambda b, i: (layer, 0, 0))
    mem_spec = pl.BlockSpec((1, nb, n_mem, d), lambda b, i: (layer, b, 0, 0))
    return pl.pallas_call(
        functools.partial(_xattn_kernel, heads=XA_HEADS),
        grid=(batch // nb, nt),
        in_specs=[row_spec, pl.BlockSpec((d, d), lambda b, i: (0, 0)), row_spec,
                  pl.BlockSpec((1, d), lambda b, i: (0, 0)), weight_spec, weight_spec, mem_spec, mem_spec],
        out_specs=row_spec,
        out_shape=jax.ShapeDtypeStruct((rows, d), F32),
        input_output_aliases={2: 0} if in_place else {},
        compiler_params=_params("arbitrary", "arbitrary"),
        name="xattn",
    )(a, wa, x, gain.reshape(1, d), wq, wo, mem_k, mem_v)


def _ffn_kernel(x_ref, g_ref, wgu_ref, wd_ref, o_ref, *, final_gain):
    x = x_ref[...]
    hidden = wd_ref.shape[1]
    h = _rms(x, g_ref[0:1, :]).astype(BF16)
    acc = x
    for c0 in range(0, hidden, FFN_CHUNK):
        c1 = min(c0 + FFN_CHUNK, hidden)
        gate = _dot(h, wgu_ref[0, :, c0:c1])
        up = _dot(h, wgu_ref[0, :, hidden + c0:hidden + c1])
        act = (gate * jax.nn.sigmoid(gate) * up).astype(BF16)
        acc = acc + _dot(act, wd_ref[0, c0:c1, :])
    if final_gain:
        acc = _rms(acc, g_ref[1:2, :])
    o_ref[...] = acc


def _ffn(x, gain, w_gate_up, w_down, layer, final_gain=None):
    rows, d = x.shape
    tm = min(WIDE_ROW_TILE, rows)
    gains = jnp.stack([gain, gain if final_gain is None else final_gain])
    return pl.pallas_call(
        functools.partial(_ffn_kernel, final_gain=final_gain is not None),
        grid=(rows // tm,),
        in_specs=[pl.BlockSpec((tm, d), lambda i: (i, 0)),
                  pl.BlockSpec((2, d), lambda i: (0, 0)),
                  pl.BlockSpec((1,) + w_gate_up.shape[1:], lambda i: (layer, 0, 0)),
                  pl.BlockSpec((1,) + w_down.shape[1:], lambda i: (layer, 0, 0))],
        out_specs=pl.BlockSpec((tm, d), lambda i: (i, 0)),
        out_shape=jax.ShapeDtypeStruct((rows, d), F32),
        input_output_aliases={0: 0},
        compiler_params=_params("arbitrary"),
        name="ffn",
    )(x, gains, w_gate_up, w_down)


def _rw_proj_kernel(*refs, has_v_res):
    if has_v_res:
        (x_ref, gain_ref, shift0_ref, mu_ref, wr_ref, wk_ref, wv_ref, vec_ref,
         w1_ref, w2_ref, a1_ref, a2_ref, g1_ref, g2_ref, v1_ref, v2_ref, vfirst_ref,
         r_ref, lw_ref, k_ref, v_ref, kk_ref, a_ref, g_ref, shift_ref, prev_ref) = refs
    else:
        (x_ref, gain_ref, shift0_ref, mu_ref, wr_ref, wk_ref, wv_ref, vec_ref,
         w1_ref, w2_ref, a1_ref, a2_ref, g1_ref, g2_ref,
         r_ref, lw_ref, k_ref, v_ref, kk_ref, a_ref, g_ref, shift_ref, prev_ref) = refs
    i = pl.program_id(1)
    h = _rms(x_ref[...], gain_ref[...])
    tm = h.shape[0]

    @pl.when(i == 0)
    def _():
        prev_ref[...] = shift0_ref[0]

    row = lax.broadcasted_iota(jnp.int32, h.shape, 0)
    h_prev = jnp.where(row == 0, prev_ref[...], pltpu.roll(h, 1, 0))
    last = h[tm - 1:tm, :]
    prev_ref[...] = last
    shift_ref[0] = last

    xx = h_prev - h
    mixed = lambda j: (h + xx * mu_ref[j:j + 1, :]).astype(BF16)
    w0, a0, k_k, k_a, v0 = (vec_ref[j:j + 1, :] for j in range(5))

    r_ref[...] = _dot(mixed(0), wr_ref[...]).astype(r_ref.dtype)
    dec_in = _dot(mixed(1), w1_ref[...])
    k = _dot(mixed(2), wk_ref[...])
    dec = w0 + _dot(jnp.tanh(dec_in).astype(BF16), w2_ref[...])
    x_v = mixed(3)
    v = _dot(x_v, wv_ref[...])
    lw_ref[...] = -jnp.exp(-jax.nn.softplus(-dec) - 0.5)
    if has_v_res:
        mix = jax.nn.sigmoid(v0 + _dot(_dot(x_v, v1_ref[...]).astype(BF16), v2_ref[...]))
        v = v + (vfirst_ref[...] - v) * mix
    v_ref[...] = v
    a_in = _dot(mixed(4), a1_ref[...])
    g_in = _dot(mixed(5), g1_ref[...])
    a = jax.nn.sigmoid(a0 + _dot(a_in.astype(BF16), a2_ref[...]))
    kk_ref[...] = (k * k_k).astype(kk_ref.dtype)
    k_ref[...] = (k * (1.0 + (a - 1.0) * k_a)).astype(k_ref.dtype)
    a_ref[...] = a.astype(a_ref.dtype)
    g_ref[...] = _dot(jax.nn.sigmoid(g_in).astype(BF16), g2_ref[...]).astype(g_ref.dtype)


def _pad_cols(w, mult=LANES):
    pad = -w.shape[1] % mult
    return jnp.pad(w, ((0, 0), (0, pad))) if pad else w


def _pad_rows(w, mult=LANES):
    pad = -w.shape[0] % mult
    return jnp.pad(w, ((0, pad), (0, 0))) if pad else w


def _rw_proj(x, batch, gain, shift0, p, v_first):
    rows, d = x.shape
    seq = rows // batch
    tm = min(WIDE_ROW_TILE, seq)
    nt = seq // tm
    has_v_res = v_first is not None
    zeros = jnp.zeros((d,), F32)
    vec = jnp.stack([p["w0"], p["a0"], p["k_k"], p["k_a"], p["v0"] if has_v_res else zeros,
                     zeros, zeros, zeros])
    row_spec = pl.BlockSpec((tm, d), lambda b, i: (b * nt + i, 0))

    def full(arr):
        return pl.BlockSpec(arr.shape, lambda b, i: (0,) * arr.ndim, pipeline_mode=pl.Buffered(1))

    ins = [x, gain.reshape(1, d), shift0.reshape(batch, 1, d), p["mu"], p["w_r"], p["w_k"], p["w_v"], vec,
           p["w1"], p["w2"], p["a1"], p["a2"], p["g1"], p["g2"]]
    in_specs = [row_spec, full(ins[1]), pl.BlockSpec((1, 1, d), lambda b, i: (b, 0, 0))]
    in_specs += [full(a) for a in ins[3:]]
    if has_v_res:
        ins += [p["v1"], p["v2"], v_first]
        in_specs += [full(p["v1"]), full(p["v2"]), row_spec]
    tok = [jax.ShapeDtypeStruct((rows, d), dt) for dt in (BF16, F32, BF16, F32, BF16, BF16, BF16)]
    outs = pl.pallas_call(
        functools.partial(_rw_proj_kernel, has_v_res=has_v_res),
        grid=(batch, nt),
        in_specs=in_specs,
        out_specs=[row_spec] * 7 + [pl.BlockSpec((1, 1, d), lambda b, i: (b, 0, 0))],
        out_shape=tok + [jax.ShapeDtypeStruct((batch, 1, d), F32)],
        scratch_shapes=[pltpu.VMEM((1, d), F32)],
        compiler_params=_params("arbitrary", "arbitrary"),
        name="rw_proj",
    )(*ins)
    return outs[:7], outs[7].reshape(batch, d)


WKV_BASE_BLOCK = 4
WKV_DECAY_PIECES = 3
WKV_GROUP_PIECES = 2
WKV_STATE_PASSES = 1
WKV_SOLVE_PASSES = 1


def _wkv_prepare(units, c_):
    n = units[0][0].shape[0]
    sum_l = lambda x: _mm_exact_lhs(c_["tri_incl"], x, WKV_DECAY_PIECES)
    sum_g = lambda x: _mm_exact_rhs(x, c_["gsum"], WKV_GROUP_PIECES)
    solve_mm = lambda x, y, nt=False: _mm(x, y, WKV_SOLVE_PASSES, nt=nt)

    cs = [sum_l(u[1]) for u in units]
    head_sums = [sum_g(jnp.concatenate([u[4] * u[4], u[0] * u[2] * u[7][0:1, :]], axis=0)) for u in units]
    yield
    pre = []
    for (r, lw, k, v, kk, a, g, vec), cs_u, hs in zip(units, cs, head_sums):
        cs_last = cs_u[n - 1:n, :]
        d_inv = jnp.exp(-cs_u)
        d_rest = jnp.exp(cs_last - cs_u)
        kkn = kk * lax.rsqrt(jnp.maximum(hs[:n], 1e-24))
        b = kkn * a
        pre.append(dict(a_t=-kkn * jnp.exp(cs_u - lw), r_t=r * jnp.exp(cs_u), b_t=b * d_inv, k_t=k * d_inv,
                        bk_rest=jnp.concatenate([b * d_rest, k * d_rest], axis=0), decay=jnp.exp(cs_last),
                        v=v, g=g, vec=vec, bonus=hs[n:] * v))
    yield

    by_head = lambda x: jnp.concatenate([jnp.where(m, x, 0.0) for m in c_["head_masks"]], axis=0)
    block_diag = lambda w: jnp.where(c_["same_block"], jnp.concatenate([w, w], axis=0), 0.0)
    grams = [solve_mm(jnp.concatenate([q["a_t"], q["r_t"]], axis=0),
                      jnp.concatenate([by_head(q["b_t"]), by_head(q["k_t"])], axis=0), nt=True)
             for q in pre]
    yield
    lows = [jnp.where(c_["strict_w"], gm[:n, :2 * n], 0.0) for gm in grams]
    m_aks = [jnp.where(c_["strict_w"], gm[:n, 2 * n:], 0.0) for gm in grams]
    for q, gm in zip(pre, grams):
        q["m_r"] = jnp.where(c_["incl_w2"], gm[n:], 0.0)
        q["v_heads"] = by_head(q["v"])
    from_v = [solve_mm(mk, q["v_heads"]) for mk, q in zip(m_aks, pre)]
    yield
    diag = [jnp.where(c_["diag_block"], lo, 0.0) for lo in lows]
    inv = [c_["eye_w"] + dg for dg in diag]
    sq = [solve_mm(dg, block_diag(dg)) for dg in diag]
    yield
    inv = [t + solve_mm(t, block_diag(p2)) for t, p2 in zip(inv, sq)]
    yield
    for below in c_["below_blocks"]:
        half = [solve_mm(t, block_diag(jnp.where(below, lo, 0.0))) for t, lo in zip(inv, lows)]
        yield
        inv = [t + solve_mm(hf, block_diag(t)) for t, hf in zip(inv, half)]
        yield
    for q, t, fv in zip(pre, inv, from_v):
        q["inv"], q["from_v"] = t, fv
    return pre


def _wkv_advance(pre, states, c_):
    n = pre[0]["v"].shape[0]
    inv_hd = 1.0 / RW_HEAD_DIM
    sum_g = lambda x: _mm_exact_rhs(x, c_["gsum"], WKV_GROUP_PIECES)
    solve_mm = lambda x, y: _mm(x, y, WKV_SOLVE_PASSES)
    by_head = lambda x: jnp.concatenate([jnp.where(m, x, 0.0) for m in c_["head_masks"]], axis=0)

    from_state = [_mm(jnp.concatenate([q["a_t"], q["r_t"]], axis=0), s, WKV_STATE_PASSES, nt=True)
                  for q, s in zip(pre, states)]
    yield
    sig_p = [solve_mm(q["inv"], by_head(fs[:n] + q["from_v"])) for q, fs in zip(pre, from_state)]
    yield
    y_p = [fs[n:] + solve_mm(q["m_r"], jnp.concatenate([by_head(sg), q["v_heads"]], axis=0))
           for q, fs, sg in zip(pre, from_state, sig_p)]
    upd = [_mm(jnp.concatenate([s, q["v"]], axis=0).T, q["bk_rest"], WKV_STATE_PASSES)
           for s, q in zip(sig_p, pre)]
    new_states = [s * q["decay"] + jnp.where(c_["same_head"], u, 0.0) for s, q, u in zip(states, pre, upd)]

    yield
    mean = [sum_g(y) * inv_hd for y in y_p]
    dev = [y - m for y, m in zip(y_p, mean)]
    yield
    var = [sum_g(dv * dv) * inv_hd for dv in dev]
    yield
    outs = []
    for q, dv, vr in zip(pre, dev, var):
        o = dv * lax.rsqrt(vr + RW_GN_EPS) * q["vec"][1:2, :] + q["vec"][2:3, :]
        outs.append((o + q["bonus"]) * q["g"])
    return outs, new_states


def _wkv_advance_chunks(pre_chunks, states, c_):
    outs = []
    for pre in pre_chunks:
        out, states = yield from _wkv_advance(pre, states, c_)
        outs.append(jnp.concatenate(out, axis=1))
    return outs, states


def _in_lock_step(*staged):
    results = [None] * len(staged)
    live = list(enumerate(staged))
    while live:
        still = []
        for i, gen in live:
            try:
                next(gen)
                still.append((i, gen))
            except StopIteration as done:
                results[i] = done.value
        live = still
    return results


def _wkv_kernel(*refs, has_state0, n):
    if has_state0:
        (r_ref, lw_ref, k_ref, v_ref, kk_ref, a_ref, g_ref, vec_ref, s0_ref,
         o_ref, sout_ref, state_ref) = refs
    else:
        (r_ref, lw_ref, k_ref, v_ref, kk_ref, a_ref, g_ref, vec_ref,
         o_ref, sout_ref, state_ref) = refs
    c = pl.program_id(2)
    hd = RW_HEAD_DIM

    @pl.when(c == 0)
    def _():
        if has_state0:
            state_ref[...] = s0_ref[0]
        else:
            state_ref[...] = jnp.zeros_like(state_ref)

    tr = lax.broadcasted_iota(jnp.int32, (n, n), 0)
    tc = lax.broadcasted_iota(jnp.int32, (n, n), 1)
    tr2 = lax.broadcasted_iota(jnp.int32, (n, 2 * n), 0)
    tc2 = lax.rem(lax.broadcasted_iota(jnp.int32, (n, 2 * n), 1), n)
    tr4 = lax.broadcasted_iota(jnp.int32, (n, 4 * n), 0)
    tc4 = lax.rem(lax.broadcasted_iota(jnp.int32, (n, 4 * n), 1), n)
    br = lax.broadcasted_iota(jnp.int32, (2 * n, 2 * n), 0) // n
    bc = lax.broadcasted_iota(jnp.int32, (2 * n, 2 * n), 1) // n
    lane = lax.broadcasted_iota(jnp.int32, (n, LANES), 1)
    sr = lax.broadcasted_iota(jnp.int32, (LANES, LANES), 0) // hd
    sc = lax.broadcasted_iota(jnp.int32, (LANES, LANES), 1) // hd
    sizes = [WKV_BASE_BLOCK << i for i in range(n.bit_length()) if (WKV_BASE_BLOCK << i) < n]
    below_blocks = []
    for s in sizes:
        u, v = tr2 // s, tc2 // s
        below_blocks.append(jnp.where((u & 1) == 1, u - 1, -1) == v)
    consts = dict(
        diag_block=(tr2 // WKV_BASE_BLOCK) == (tc2 // WKV_BASE_BLOCK),
        below_blocks=below_blocks,
        gsum=jnp.concatenate([_group_ones(LANES, hd)] * WKV_GROUP_PIECES, axis=0),
        tri_incl=jnp.concatenate([jnp.where(tc <= tr, 1.0, 0.0).astype(BF16)] * WKV_DECAY_PIECES, axis=1),
        strict_w=tc2 < tr2,
        incl_w2=tc4 <= tr4,
        eye_w=jnp.where(tc2 == tr2, 1.0, 0.0).astype(F32),
        same_block=br == bc,
        head_masks=[lane < hd, lane >= hd],
        same_head=sr == sc,
    )
    n_pairs = state_ref.shape[0]
    n_chunks = r_ref.shape[0] // n

    def units(chunks):
        out = []
        for ci in chunks:
            rows = slice(ci * n, (ci + 1) * n)
            for p in range(n_pairs):
                sl = slice(p * LANES, (p + 1) * LANES)
                out.append(tuple(ref[rows, sl].astype(F32)
                                 for ref in (r_ref, lw_ref, k_ref, v_ref, kk_ref, a_ref, g_ref)) + (vec_ref[:, sl],))
        return out

    def per_chunk(pre):
        return [pre[i:i + n_pairs] for i in range(0, len(pre), n_pairs)]

    first = range((n_chunks + 1) // 2)
    second = range(len(first), n_chunks)
    states = [state_ref[p] for p in range(n_pairs)]
    (pre_first,) = _in_lock_step(_wkv_prepare(units(first), consts))
    if second:
        (outs, states), pre_second = _in_lock_step(
            _wkv_advance_chunks(per_chunk(pre_first), states, consts), _wkv_prepare(units(second), consts))
        ((outs_second, states),) = _in_lock_step(_wkv_advance_chunks(per_chunk(pre_second), states, consts))
        outs = outs + outs_second
    else:
        ((outs, states),) = _in_lock_step(_wkv_advance_chunks(per_chunk(pre_first), states, consts))
    o_ref[...] = jnp.concatenate(outs, axis=0).astype(o_ref.dtype)
    new_states = jnp.stack(states)
    state_ref[...] = new_states
    sout_ref[0] = new_states


def _wkv(tok, batch, vec, state0):
    rows, d = tok[0].shape
    seq = rows // batch
    n = min(WKV_CHUNK, seq)
    span = n * min(WKV_CHUNKS_PER_STEP, seq // n)
    nc = seq // span
    pairs = d // LANES
    gp = WKV_PAIRS
    width = gp * LANES
    has_state0 = state0 is not None
    tok_spec = pl.BlockSpec((span, width), lambda b, p, c: (b * nc + c, p))
    state_spec = pl.BlockSpec((1, gp, LANES, LANES), lambda b, p, c: (b, p, 0, 0))
    ins = list(tok) + [vec]
    in_specs = [tok_spec] * 7 + [pl.BlockSpec((8, width), lambda b, p, c: (0, p))]
    if has_state0:
        ins.append(state0)
        in_specs.append(state_spec)
    return pl.pallas_call(
        functools.partial(_wkv_kernel, has_state0=has_state0, n=n),
        grid=(batch, pairs // gp, nc),
        in_specs=in_specs,
        out_specs=[tok_spec, state_spec],
        out_shape=[jax.ShapeDtypeStruct((rows, d), BF16),
                   jax.ShapeDtypeStruct((batch, pairs, LANES, LANES), F32)],
        scratch_shapes=[pltpu.VMEM((gp, LANES, LANES), F32)],
        compiler_params=_params("arbitrary", "arbitrary", "arbitrary"),
        name="wkv",
    )(*ins)


def _state_to_blockdiag(state):
    b, h, n, _ = state.shape
    s = state.reshape(b, h // 2, 2, n, n)
    z = jnp.zeros_like(s[:, :, 0])
    top = jnp.concatenate([s[:, :, 0], z], axis=-1)
    bot = jnp.concatenate([z, s[:, :, 1]], axis=-1)
    return jnp.concatenate([top, bot], axis=-2)


def _state_from_blockdiag(bd):
    b, p, n2, _ = bd.shape
    n = n2 // 2
    return jnp.stack([bd[:, :, :n, :n], bd[:, :, n:, n:]], axis=2).reshape(b, 2 * p, n, n)


def _trunk(x, batch, mem_k, mem_v, past_k, past_v, rw_state0, rw_shift0, w):
    depth = w["ffn_gu"].shape[0]
    d = x.shape[-1]
    sb_k, sb_v, rw_states, rw_shifts = None, None, [], []
    n_sb = (depth + 1) // 2
    v_first = None
    for layer in range(depth):
        j = layer // 2
        if layer % 2 == 0:
            sb_k, sb_v, qkv = _sb_qkv(x, batch, w["norm_mix"][layer], w["sb_qkv"][j], j, n_sb, sb_k, sb_v)
            if past_k is None:
                o = _sb_attention_prompt(qkv, batch)
            else:
                o = _sb_attention_sample(qkv, batch, past_k, past_v, j)
            w_o = w["sb_o"][j]
        else:
            if rw_state0 is None:
                state0 = None
                shift0 = jnp.zeros((batch, d), F32)
            else:
                state0 = _state_to_blockdiag(rw_state0[j])
                shift0 = rw_shift0[j]
            tok, shift = _rw_proj(x, batch, w["norm_mix"][layer], shift0, w["rw"][j], v_first)
            if v_first is None:
                v_first = tok[3]
            o, state = _wkv(tok, batch, w["rw"][j]["out_vec"], state0)
            rw_states.append(_state_from_blockdiag(state))
            rw_shifts.append(shift)
            w_o = w["rw"][j]["w_o"]
        x = _xattn(o, w_o, x, w["norm_xattn"][layer], w["xa_q"], w["xa_o"], mem_k, mem_v, layer,
                   in_place=layer > 0)
        final = w["norm_final"] if layer == depth - 1 else None
        x = _ffn(x, w["norm_ffn"][layer], w["ffn_gu"], w["ffn_down"], layer, final)
    return x, sb_k, sb_v, jnp.stack(rw_states), jnp.stack(rw_shifts)


def kernel(x_prompt, x_sample, mem_prompt, cache_sb_k, cache_sb_v, state_rwkv_wkv, state_rwkv_shift,
           cache_mem_k, cache_mem_v, norm_mix, norm_xattn, norm_ffn, norm_mem, norm_final,
           sb_w_qkv, sb_w_o, rw_mu, rw_w_r, rw_w_k, rw_w_v, rw_w_o, rw_w0, rw_w1, rw_w2,
           rw_a0, rw_a1, rw_a2, rw_v0, rw_v1, rw_v2, rw_g1, rw_g2, rw_k_k, rw_k_a, rw_r_k,
           rw_lnx_w, rw_lnx_b, xa_w_q, xa_w_kv, xa_w_o, ffn_w_gate_up, ffn_w_down):
    batch, seq, d = x_prompt.shape
    dec_batch, dec_seq, _ = x_sample.shape
    depth = norm_mix.shape[0]
    n_rw = rw_w_r.shape[0]
    n_sb = sb_w_qkv.shape[0]
    n_mem = mem_prompt.shape[1]
    bf = lambda t: t.astype(BF16)
    per_layer = lambda t: [bf(t[i]) for i in range(t.shape[0])]

    rw = []
    for j in range(n_rw):
        pj = dict(mu=rw_mu[j], w_r=bf(rw_w_r[j]), w_k=bf(rw_w_k[j]), w_v=bf(rw_w_v[j]), w_o=bf(rw_w_o[j]),
                  w0=rw_w0[j], a0=rw_a0[j], k_k=rw_k_k[j], k_a=rw_k_a[j],
                  w1=bf(_pad_cols(rw_w1[j])), w2=bf(_pad_rows(rw_w2[j])),
                  a1=bf(_pad_cols(rw_a1[j])), a2=bf(_pad_rows(rw_a2[j])),
                  g1=bf(_pad_cols(rw_g1[j])), g2=bf(_pad_rows(rw_g2[j])))
        if j > 0:
            pj.update(v0=rw_v0[j - 1], v1=bf(_pad_cols(rw_v1[j - 1])), v2=bf(_pad_rows(rw_v2[j - 1])))
        zeros = jnp.zeros((d,), F32)
        pj["out_vec"] = jnp.stack([rw_r_k[j].reshape(d), rw_lnx_w[j], rw_lnx_b[j]] + [zeros] * 5)
        rw.append(pj)
    w = dict(norm_mix=norm_mix, norm_xattn=norm_xattn, norm_ffn=norm_ffn, norm_final=norm_final,
             sb_qkv=per_layer(sb_w_qkv), sb_o=per_layer(sb_w_o), rw=rw,
             xa_q=bf(xa_w_q), xa_o=bf(xa_w_o), ffn_gu=bf(ffn_w_gate_up), ffn_down=bf(ffn_w_down))

    mem_k, mem_v, mem_k_bf, mem_v_bf = _mem_kv(mem_prompt, norm_mem, bf(xa_w_kv))
    y_p, p_sb_k, p_sb_v, p_wkv, p_shift = _trunk(
        x_prompt.reshape(batch * seq, d), batch, mem_k_bf, mem_v_bf, None, None, None, None, w)

    y_s, s_sb_k, s_sb_v, s_wkv, s_shift = _trunk(
        x_sample.reshape(dec_batch * dec_seq, d), dec_batch,
        bf(cache_mem_k).reshape(depth, dec_batch, n_mem, d), bf(cache_mem_v).reshape(depth, dec_batch, n_mem, d),
        cache_sb_k, cache_sb_v,
        state_rwkv_wkv, state_rwkv_shift, w)

    tokens_first = lambda t: jnp.transpose(t, (0, 1, 4, 2, 3))
    return (y_p.reshape(batch, seq, d), y_s.reshape(dec_batch, dec_seq, d),
            tokens_first(p_sb_k), tokens_first(p_sb_v), p_wkv, p_shift, mem_k, mem_v,
            tokens_first(s_sb_k), tokens_first(s_sb_v), s_wkv, s_shift)
```
